```python
import math
import jax, jax.numpy as jnp
from jax import lax
import numpy as np

D_MODEL = 1024
BATCH = 8
SEQ = 8192
DEPTH = 1

MIX_WIDTH = D_MODEL
GLA_WIDTH = MIX_WIDTH // 2
GLA_HEADS = 4
GLA_DV = GLA_WIDTH // GLA_HEADS
GLA_DK = GLA_DV // 2
GLA_GATE_RANK = 16
GLA_GATE_NORMALIZER = 16.0
GLA_CHUNK = 64
SB_WIDTH = MIX_WIDTH - GLA_WIDTH
SB_HEAD_DIM = 64
SB_HEADS = SB_WIDTH // SB_HEAD_DIM
SB_BLOCK = 128
MEM_LEN = 256
MEM_HEADS = 4
MEM_HEAD_DIM = D_MODEL // MEM_HEADS
D_FF = -(-8 * D_MODEL // (3 * 256)) * 256
RMS_EPS = 1e-6
GLA_QK_W = GLA_HEADS * GLA_DK
GLA_V_W = GLA_HEADS * GLA_DV
IN_SIZES = (GLA_QK_W, GLA_QK_W, GLA_V_W, GLA_V_W, GLA_GATE_RANK, SB_WIDTH, SB_WIDTH, SB_WIDTH)
D_IN = sum(IN_SIZES)

kernel_name = "hybrid_gla_stickbreaking_memxattn_swiglu"


def rms_norm(x, w):
    xf = x.astype(jnp.float32)
    y = xf * lax.rsqrt(jnp.mean(xf * xf, axis=-1, keepdims=True) + RMS_EPS)
    return (y * w.astype(jnp.float32)).astype(x.dtype)


def split_points():
    return [int(v) for v in np.cumsum(np.array(IN_SIZES))[:-1]]


def gla_chunked(q, k, v, gk):
    B, T, H, dk = q.shape
    dv = v.shape[-1]
    C = GLA_CHUNK
    N = T // C
    out_dtype = v.dtype

    def to_chunks(a):
        d = a.shape[-1]
        return a.astype(jnp.float32).reshape(B, N, C, H, d).transpose(1, 0, 3, 2, 4)

    qc, kc, vc, gc = to_chunks(q), to_chunks(k), to_chunks(v), to_chunks(gk)
    causal = jnp.tril(jnp.ones((C, C), dtype=bool))[:, :, None]

    def step(S, inp):
        qi, ki, vi, gi = inp
        b = jnp.cumsum(gi, axis=2)
        o_inter = jnp.einsum('bhcd,bhde->bhce', qi * jnp.exp(b), S)
        diff = b[:, :, :, None, :] - b[:, :, None, :, :]
        decay = jnp.where(causal, jnp.exp(jnp.where(causal, diff, 0.0)), 0.0)
        A = jnp.einsum('bhid,bhjd,bhijd->bhij', qi, ki, decay)
        o = o_inter + jnp.einsum('bhij,bhje->bhie', A, vi)
        b_last = b[:, :, -1]
        k_dec = ki * jnp.exp(b_last[:, :, None, :] - b)
        S = jnp.exp(b_last)[..., None] * S + jnp.einsum('bhcd,bhce->bhde', k_dec, vi)
        return S, o

    S0 = jnp.zeros((B, H, dk, dv), jnp.float32)
    _, o = lax.scan(step, S0, (qc, kc, vc, gc))
    return o.transpose(1, 0, 3, 2, 4).reshape(B, T, H, dv).astype(out_dtype)


def stick_breaking_attention(q, k, v):
    B, H, T, d = q.shape
    scale = 1.0 / math.sqrt(d)
    outs = []
    for i in range(T // SB_BLOCK):
        end = (i + 1) * SB_BLOCK
        q_blk = q[:, :, i * SB_BLOCK:end]
        k_blk = k[:, :, :end]
        v_blk = v[:, :, :end]
        z = jnp.einsum('bhqd,bhkd->bhqk', q_blk, k_blk).astype(jnp.float32) * scale
        t_idx = i * SB_BLOCK + jnp.arange(SB_BLOCK)[:, None]
        s_idx = jnp.arange(end)[None, :]
        strict = s_idx < t_idx
        log_beta = jax.nn.log_sigmoid(z)
        log_1m = jnp.where(strict, jax.nn.log_sigmoid(-z), 0.0)
        rev = lax.cumsum(log_1m, axis=3, reverse=True)
        log_A = log_beta + rev - log_1m
        A = jnp.where(strict, jnp.exp(log_A), 0.0)
        outs.append(jnp.einsum('bhqk,bhkd->bhqd', A, v_blk.astype(jnp.float32)).astype(v.dtype))
    return jnp.concatenate(outs, axis=2)


def parallel_mixer(h, w_in, w_gk_up, b_gk, gla_norm_w, sb_norm_w, w_out):
    B, T, _ = h.shape
    proj = h @ w_in
    q_g, k_g, v_g, g_g, gk_lr, q_s, k_s, v_s = jnp.split(proj, split_points(), axis=-1)
    q_g = q_g.reshape(B, T, GLA_HEADS, GLA_DK) * (GLA_DK ** -0.5)
    k_g = k_g.reshape(B, T, GLA_HEADS, GLA_DK)
    v_g = v_g.reshape(B, T, GLA_HEADS, GLA_DV)
    gk = (jax.nn.log_sigmoid((gk_lr @ w_gk_up + b_gk).astype(jnp.float32)) / GLA_GATE_NORMALIZER)
    gk = gk.reshape(B, T, GLA_HEADS, GLA_DK)
    o_g = gla_chunked(q_g, k_g, v_g, gk)
    o_g = rms_norm(o_g, gla_norm_w) * jax.nn.silu(g_g.reshape(B, T, GLA_HEADS, GLA_DV))
    o_g = o_g.reshape(B, T, GLA_V_W)
    def heads(a):
        return a.reshape(B, T, SB_HEADS, SB_HEAD_DIM).transpose(0, 2, 1, 3)
    o_s = stick_breaking_attention(heads(q_s), heads(k_s), heads(v_s)).transpose(0, 2, 1, 3)
    o_s = rms_norm(o_s, sb_norm_w).reshape(B, T, SB_WIDTH)
    return jnp.concatenate([o_g, o_s], axis=-1) @ w_out


def memory_cross_attention(h, mem_n, w_mq, w_mkv, mq_norm_w, mk_norm_w, w_mo):
    B, T, _ = h.shape
    M = mem_n.shape[1]
    q = rms_norm((h @ w_mq).reshape(B, T, MEM_HEADS, MEM_HEAD_DIM), mq_norm_w)
    kv = mem_n @ w_mkv
    k, v = jnp.split(kv, 2, axis=-1)
    k = rms_norm(k.reshape(B, M, MEM_HEADS, MEM_HEAD_DIM), mk_norm_w)
    v = v.reshape(B, M, MEM_HEADS, MEM_HEAD_DIM)
    s = jnp.einsum('bthd,bmhd->bhtm', q, k).astype(jnp.float32) / math.sqrt(MEM_HEAD_DIM)
    p = jax.nn.softmax(s, axis=-1).astype(v.dtype)
    o = jnp.einsum('bhtm,bmhd->bthd', p, v).reshape(B, T, D_MODEL)
    return o @ w_mo


def swiglu(h, w_gate_up, w_down):
    gate, up = jnp.split(h @ w_gate_up, 2, axis=-1)
    return (jax.nn.silu(gate) * up) @ w_down


def _fwd_setup_inputs(seed: int = 0) -> dict:
    key = jax.random.key(seed)
    ks = jax.random.split(key, 24)
    f32 = jnp.float32

    def w(k, shape, fan_in):
        return jax.random.normal(k, (DEPTH,) + shape, f32) * (fan_in ** -0.5)

    def gain(k, n):
        return 1.0 + 0.01 * jax.random.normal(k, (DEPTH, n), f32)

    return {
        "x": jax.random.normal(ks[0], (BATCH, SEQ, D_MODEL), f32),
        "mem": jax.random.normal(ks[1], (BATCH, MEM_LEN, D_MODEL), f32),
        "mix_norm_w": gain(ks[2], D_MODEL),
        "w_in": w(ks[3], (D_MODEL, D_IN), D_MODEL),
        "w_gk_up": w(ks[4], (GLA_GATE_RANK, GLA_QK_W), GLA_GATE_RANK),
        "b_gk": 0.1 * jax.random.normal(ks[5], (DEPTH, GLA_QK_W), f32),
        "gla_norm_w": gain(ks[6], GLA_DV),
        "sb_norm_w": gain(ks[7], SB_HEAD_DIM),
        "w_out": w(ks[8], (MIX_WIDTH, D_MODEL), MIX_WIDTH),
        "xattn_norm_w": gain(ks[9], D_MODEL),
        "mem_norm_w": gain(ks[10], D_MODEL),
        "w_mq": w(ks[11], (D_MODEL, D_MODEL), D_MODEL),
        "w_mkv": w(ks[12], (D_MODEL, 2 * D_MODEL), D_MODEL),
        "mq_norm_w": gain(ks[13], MEM_HEAD_DIM),
        "mk_norm_w": gain(ks[14], MEM_HEAD_DIM),
        "w_mo": w(ks[15], (D_MODEL, D_MODEL), D_MODEL),
        "ffn_norm_w": gain(ks[16], D_MODEL),
        "w_gate_up": w(ks[17], (D_MODEL, 2 * D_FF), D_MODEL),
        "w_down": w(ks[18], (D_FF, D_MODEL), D_FF),
    }


def _fwd_reference(x, mem, mix_norm_w, w_in, w_gk_up, b_gk, gla_norm_w, sb_norm_w, w_out,
              xattn_norm_w, mem_norm_w, w_mq, w_mkv, mq_norm_w, mk_norm_w, w_mo,
              ffn_norm_w, w_gate_up, w_down):
    for l in range(DEPTH):
        h = rms_norm(x, mix_norm_w[l])
        x = x + parallel_mixer(h, w_in[l], w_gk_up[l], b_gk[l], gla_norm_w[l], sb_norm_w[l], w_out[l])
        h = rms_norm(x, xattn_norm_w[l])
        mem_n = rms_norm(mem, mem_norm_w[l])
        x = x + memory_cross_attention(h, mem_n, w_mq[l], w_mkv[l], mq_norm_w[l], mk_norm_w[l], w_mo[l])
        h = rms_norm(x, ffn_norm_w[l])
        x = x + swiglu(h, w_gate_up[l], w_down[l])
    return x


import jax as _jax
import jax.numpy as _jnp

TWIN_FORMAT = 'train_step'
FWD_PARAMS = ['x', 'mem', 'mix_norm_w', 'w_in', 'w_gk_up', 'b_gk', 'gla_norm_w', 'sb_norm_w', 'w_out', 'xattn_norm_w', 'mem_norm_w', 'w_mq', 'w_mkv', 'mq_norm_w', 'mk_norm_w', 'w_mo', 'ffn_norm_w', 'w_gate_up', 'w_down']
TWIN_WEIGHTS = ['mix_norm_w', 'w_in', 'w_gk_up', 'b_gk', 'gla_norm_w', 'sb_norm_w', 'w_out', 'xattn_norm_w', 'mem_norm_w', 'w_mq', 'w_mkv', 'mq_norm_w', 'mk_norm_w', 'w_mo', 'ffn_norm_w', 'w_gate_up', 'w_down']
TWIN_DIFF_INPUT = 'x'
TWIN_INPUTS = ['x', 'mem', 'mix_norm_w', 'w_in', 'w_gk_up', 'b_gk', 'gla_norm_w', 'sb_norm_w', 'w_out', 'xattn_norm_w', 'mem_norm_w', 'w_mq', 'w_mkv', 'mq_norm_w', 'mk_norm_w', 'w_mo', 'ffn_norm_w', 'w_gate_up', 'w_down', 'loss_target', 'm_mix_norm_w', 'm_w_in', 'm_w_gk_up', 'm_b_gk', 'm_gla_norm_w', 'm_sb_norm_w', 'm_w_out', 'm_xattn_norm_w', 'm_mem_norm_w', 'm_w_mq', 'm_w_mkv', 'm_mq_norm_w', 'm_mk_norm_w', 'm_w_mo', 'm_ffn_norm_w', 'm_w_gate_up', 'm_w_down', 'v_mix_norm_w', 'v_w_in', 'v_w_gk_up', 'v_b_gk', 'v_gla_norm_w', 'v_sb_norm_w', 'v_w_out', 'v_xattn_norm_w', 'v_mem_norm_w', 'v_w_mq', 'v_w_mkv', 'v_mq_norm_w', 'v_mk_norm_w', 'v_w_mo', 'v_ffn_norm_w', 'v_w_gate_up', 'v_w_down']
TWIN_OUTPUTS = ['loss', 'grad_x', 'grad_mix_norm_w', 'grad_w_in', 'grad_w_gk_up', 'grad_b_gk', 'grad_gla_norm_w', 'grad_sb_norm_w', 'grad_w_out', 'grad_xattn_norm_w', 'grad_mem_norm_w', 'grad_w_mq', 'grad_w_mkv', 'grad_mq_norm_w', 'grad_mk_norm_w', 'grad_w_mo', 'grad_ffn_norm_w', 'grad_w_gate_up', 'grad_w_down', 'delta_mix_norm_w', 'delta_w_in', 'delta_w_gk_up', 'delta_b_gk', 'delta_gla_norm_w', 'delta_sb_norm_w', 'delta_w_out', 'delta_xattn_norm_w', 'delta_mem_norm_w', 'delta_w_mq', 'delta_w_mkv', 'delta_mq_norm_w', 'delta_mk_norm_w', 'delta_w_mo', 'delta_ffn_norm_w', 'delta_w_gate_up', 'delta_w_down', 'new_m_mix_norm_w', 'new_m_w_in', 'new_m_w_gk_up', 'new_m_b_gk', 'new_m_gla_norm_w', 'new_m_sb_norm_w', 'new_m_w_out', 'new_m_xattn_norm_w', 'new_m_mem_norm_w', 'new_m_w_mq', 'new_m_w_mkv', 'new_m_mq_norm_w', 'new_m_mk_norm_w', 'new_m_w_mo', 'new_m_ffn_norm_w', 'new_m_w_gate_up', 'new_m_w_down', 'new_v_mix_norm_w', 'new_v_w_in', 'new_v_w_gk_up', 'new_v_b_gk', 'new_v_gla_norm_w', 'new_v_sb_norm_w', 'new_v_w_out', 'new_v_xattn_norm_w', 'new_v_mem_norm_w', 'new_v_w_mq', 'new_v_w_mkv', 'new_v_mq_norm_w', 'new_v_mk_norm_w', 'new_v_w_mo', 'new_v_ffn_norm_w', 'new_v_w_gate_up', 'new_v_w_down']
TWIN_LEAF_KINDS = {'loss': 'loss', 'grad_x': 'grad_x', 'grad_mix_norm_w': 'grad_w', 'grad_w_in': 'grad_w', 'grad_w_gk_up': 'grad_w', 'grad_b_gk': 'grad_w', 'grad_gla_norm_w': 'grad_w', 'grad_sb_norm_w': 'grad_w', 'grad_w_out': 'grad_w', 'grad_xattn_norm_w': 'grad_w', 'grad_mem_norm_w': 'grad_w', 'grad_w_mq': 'grad_w', 'grad_w_mkv': 'grad_w', 'grad_mq_norm_w': 'grad_w', 'grad_mk_norm_w': 'grad_w', 'grad_w_mo': 'grad_w', 'grad_ffn_norm_w': 'grad_w', 'grad_w_gate_up': 'grad_w', 'grad_w_down': 'grad_w', 'delta_mix_norm_w': 'delta_w', 'delta_w_in': 'delta_w', 'delta_w_gk_up': 'delta_w', 'delta_b_gk': 'delta_w', 'delta_gla_norm_w': 'delta_w', 'delta_sb_norm_w': 'delta_w', 'delta_w_out': 'delta_w', 'delta_xattn_norm_w': 'delta_w', 'delta_mem_norm_w': 'delta_w', 'delta_w_mq': 'delta_w', 'delta_w_mkv': 'delta_w', 'delta_mq_norm_w': 'delta_w', 'delta_mk_norm_w': 'delta_w', 'delta_w_mo': 'delta_w', 'delta_ffn_norm_w': 'delta_w', 'delta_w_gate_up': 'delta_w', 'delta_w_down': 'delta_w', 'new_m_mix_norm_w': 'new_m', 'new_m_w_in': 'new_m', 'new_m_w_gk_up': 'new_m', 'new_m_b_gk': 'new_m', 'new_m_gla_norm_w': 'new_m', 'new_m_sb_norm_w': 'new_m', 'new_m_w_out': 'new_m', 'new_m_xattn_norm_w': 'new_m', 'new_m_mem_norm_w': 'new_m', 'new_m_w_mq': 'new_m', 'new_m_w_mkv': 'new_m', 'new_m_mq_norm_w': 'new_m', 'new_m_mk_norm_w': 'new_m', 'new_m_w_mo': 'new_m', 'new_m_ffn_norm_w': 'new_m', 'new_m_w_gate_up': 'new_m', 'new_m_w_down': 'new_m', 'new_v_mix_norm_w': 'new_v', 'new_v_w_in': 'new_v', 'new_v_w_gk_up': 'new_v', 'new_v_b_gk': 'new_v', 'new_v_gla_norm_w': 'new_v', 'new_v_sb_norm_w': 'new_v', 'new_v_w_out': 'new_v', 'new_v_xattn_norm_w': 'new_v', 'new_v_mem_norm_w': 'new_v', 'new_v_w_mq': 'new_v', 'new_v_w_mkv': 'new_v', 'new_v_mq_norm_w': 'new_v', 'new_v_mk_norm_w': 'new_v', 'new_v_w_mo': 'new_v', 'new_v_ffn_norm_w': 'new_v', 'new_v_w_gate_up': 'new_v', 'new_v_w_down': 'new_v'}


def _forward(args):
    return _fwd_reference(*[args[k] for k in FWD_PARAMS])


def _output_shape():
    def fwd():
        inp = _fwd_setup_inputs(0)
        return _fwd_reference(*[inp[k] for k in FWD_PARAMS])
    out = _jax.eval_shape(fwd)
    return out.shape, out.dtype

N_MICROBATCH = 1
ADAM_LR = 0.001
ADAM_B1 = 0.9
ADAM_B2 = 0.999
ADAM_EPS = 1e-08
ADAM_WD = 0.01
ADAM_STEP = 10
PER_EXAMPLE_BATCH_AXIS = {'x': 0, 'mem': 0, 'loss_target': 0}
SHARED_INPUTS = []
_WEIGHT_DTYPES = {'mix_norm_w': _jnp.float32, 'w_in': _jnp.float32, 'w_gk_up': _jnp.float32, 'b_gk': _jnp.float32, 'gla_norm_w': _jnp.float32, 'sb_norm_w': _jnp.float32, 'w_out': _jnp.float32, 'xattn_norm_w': _jnp.float32, 'mem_norm_w': _jnp.float32, 'w_mq': _jnp.float32, 'w_mkv': _jnp.float32, 'mq_norm_w': _jnp.float32, 'mk_norm_w': _jnp.float32, 'w_mo': _jnp.float32, 'ffn_norm_w': _jnp.float32, 'w_gate_up': _jnp.float32, 'w_down': _jnp.float32}
MOMENT_SCALE = {'mix_norm_w': 1.290339e+01, 'w_in': 5.474635e-01, 'w_gk_up': 6.562329e-02, 'b_gk': 2.681478e-01, 'gla_norm_w': 9.130861e+01, 'sb_norm_w': 5.103313e+02, 'w_out': 1.311016e+00, 'xattn_norm_w': 7.734702e-02, 'mem_norm_w': 6.951747e-01, 'w_mq': 8.892504e-02, 'w_mkv': 1.392025e-01, 'mq_norm_w': 2.606267e+00, 'mk_norm_w': 2.604617e+00, 'w_mo': 1.733193e-01, 'ffn_norm_w': 4.952910e+01, 'w_gate_up': 3.099322e-01, 'w_down': 5.352867e-01}


def _to_microbatches(a, axis):
    t = _jnp.moveaxis(a, axis, 0)
    t = t.reshape((N_MICROBATCH, t.shape[0] // N_MICROBATCH) + t.shape[1:])
    return _jnp.moveaxis(t, 1, axis + 1)


def setup_inputs(seed: int = 0) -> dict:
    inp = _fwd_setup_inputs(seed)
    key = _jax.random.fold_in(_jax.random.key(seed), 7919)
    shape, _ = _output_shape()
    out = dict(inp)
    out["loss_target"] = _jax.random.normal(_jax.random.fold_in(key, 0), shape, _jnp.float32)
    for i, name in enumerate(TWIN_WEIGHTS):
        w = inp[name].astype(_jnp.float32)
        if MOMENT_SCALE is None:
            s = _jnp.sqrt(_jnp.mean(_jnp.square(w)) + 1e-30)
        else:
            s = MOMENT_SCALE[name]
        km, kv = _jax.random.split(_jax.random.fold_in(key, i + 1))
        out[name] = w
        out["m_" + name] = s * _jax.random.normal(km, w.shape, _jnp.float32)
        out["v_" + name] = (s * s) * _jax.random.uniform(kv, w.shape, _jnp.float32, 0.5, 1.5)
    if N_MICROBATCH > 1:
        for name, axis in PER_EXAMPLE_BATCH_AXIS.items():
            out[name] = _to_microbatches(out[name], axis)
    return {'x': out['x'], 'mem': out['mem'], 'mix_norm_w': out['mix_norm_w'], 'w_in': out['w_in'], 'w_gk_up': out['w_gk_up'], 'b_gk': out['b_gk'], 'gla_norm_w': out['gla_norm_w'], 'sb_norm_w': out['sb_norm_w'], 'w_out': out['w_out'], 'xattn_norm_w': out['xattn_norm_w'], 'mem_norm_w': out['mem_norm_w'], 'w_mq': out['w_mq'], 'w_mkv': out['w_mkv'], 'mq_norm_w': out['mq_norm_w'], 'mk_norm_w': out['mk_norm_w'], 'w_mo': out['w_mo'], 'ffn_norm_w': out['ffn_norm_w'], 'w_gate_up': out['w_gate_up'], 'w_down': out['w_down'], 'loss_target': out['loss_target'], 'm_mix_norm_w': out['m_mix_norm_w'], 'm_w_in': out['m_w_in'], 'm_w_gk_up': out['m_w_gk_up'], 'm_b_gk': out['m_b_gk'], 'm_gla_norm_w': out['m_gla_norm_w'], 'm_sb_norm_w': out['m_sb_norm_w'], 'm_w_out': out['m_w_out'], 'm_xattn_norm_w': out['m_xattn_norm_w'], 'm_mem_norm_w': out['m_mem_norm_w'], 'm_w_mq': out['m_w_mq'], 'm_w_mkv': out['m_w_mkv'], 'm_mq_norm_w': out['m_mq_norm_w'], 'm_mk_norm_w': out['m_mk_norm_w'], 'm_w_mo': out['m_w_mo'], 'm_ffn_norm_w': out['m_ffn_norm_w'], 'm_w_gate_up': out['m_w_gate_up'], 'm_w_down': out['m_w_down'], 'v_mix_norm_w': out['v_mix_norm_w'], 'v_w_in': out['v_w_in'], 'v_w_gk_up': out['v_w_gk_up'], 'v_b_gk': out['v_b_gk'], 'v_gla_norm_w': out['v_gla_norm_w'], 'v_sb_norm_w': out['v_sb_norm_w'], 'v_w_out': out['v_w_out'], 'v_xattn_norm_w': out['v_xattn_norm_w'], 'v_mem_norm_w': out['v_mem_norm_w'], 'v_w_mq': out['v_w_mq'], 'v_w_mkv': out['v_w_mkv'], 'v_mq_norm_w': out['v_mq_norm_w'], 'v_mk_norm_w': out['v_mk_norm_w'], 'v_w_mo': out['v_w_mo'], 'v_ffn_norm_w': out['v_ffn_norm_w'], 'v_w_gate_up': out['v_w_gate_up'], 'v_w_down': out['v_w_down']}


def _loss(weights, diff, rest, loss_target):
    with _jax.named_scope("forward"):
        args = {**rest, TWIN_DIFF_INPUT: diff, **{k: w.astype(_WEIGHT_DTYPES[k]) for k, w in weights.items()}}
        y = _forward(args)
    with _jax.named_scope("loss_head"):
        err = _jnp.square(y.astype(_jnp.float32) - loss_target)
        return 0.5 * _jnp.sum(_jnp.mean(err, axis=-1)) if err.ndim else 0.5 * err


def _adamw(w, g, m, v):
    m = ADAM_B1 * m + (1.0 - ADAM_B1) * g
    v = ADAM_B2 * v + (1.0 - ADAM_B2) * _jnp.square(g)
    m_hat = m / (1.0 - ADAM_B1 ** ADAM_STEP)
    v_hat = v / (1.0 - ADAM_B2 ** ADAM_STEP)
    delta = -ADAM_LR * (m_hat / (_jnp.sqrt(v_hat) + ADAM_EPS) + ADAM_WD * w)
    return delta, m, v


def reference(x, mem, mix_norm_w, w_in, w_gk_up, b_gk, gla_norm_w, sb_norm_w, w_out, xattn_norm_w, mem_norm_w, w_mq, w_mkv, mq_norm_w, mk_norm_w, w_mo, ffn_norm_w, w_gate_up, w_down, loss_target, m_mix_norm_w, m_w_in, m_w_gk_up, m_b_gk, m_gla_norm_w, m_sb_norm_w, m_w_out, m_xattn_norm_w, m_mem_norm_w, m_w_mq, m_w_mkv, m_mq_norm_w, m_mk_norm_w, m_w_mo, m_ffn_norm_w, m_w_gate_up, m_w_down, v_mix_norm_w, v_w_in, v_w_gk_up, v_b_gk, v_gla_norm_w, v_sb_norm_w, v_w_out, v_xattn_norm_w, v_mem_norm_w, v_w_mq, v_w_mkv, v_mq_norm_w, v_mk_norm_w, v_w_mo, v_ffn_norm_w, v_w_gate_up, v_w_down):
    given = dict(x=x, mem=mem, mix_norm_w=mix_norm_w, w_in=w_in, w_gk_up=w_gk_up, b_gk=b_gk, gla_norm_w=gla_norm_w, sb_norm_w=sb_norm_w, w_out=w_out, xattn_norm_w=xattn_norm_w, mem_norm_w=mem_norm_w, w_mq=w_mq, w_mkv=w_mkv, mq_norm_w=mq_norm_w, mk_norm_w=mk_norm_w, w_mo=w_mo, ffn_norm_w=ffn_norm_w, w_gate_up=w_gate_up, w_down=w_down, loss_target=loss_target, m_mix_norm_w=m_mix_norm_w, m_w_in=m_w_in, m_w_gk_up=m_w_gk_up, m_b_gk=m_b_gk, m_gla_norm_w=m_gla_norm_w, m_sb_norm_w=m_sb_norm_w, m_w_out=m_w_out, m_xattn_norm_w=m_xattn_norm_w, m_mem_norm_w=m_mem_norm_w, m_w_mq=m_w_mq, m_w_mkv=m_w_mkv, m_mq_norm_w=m_mq_norm_w, m_mk_norm_w=m_mk_norm_w, m_w_mo=m_w_mo, m_ffn_norm_w=m_ffn_norm_w, m_w_gate_up=m_w_gate_up, m_w_down=m_w_down, v_mix_norm_w=v_mix_norm_w, v_w_in=v_w_in, v_w_gk_up=v_w_gk_up, v_b_gk=v_b_gk, v_gla_norm_w=v_gla_norm_w, v_sb_norm_w=v_sb_norm_w, v_w_out=v_w_out, v_xattn_norm_w=v_xattn_norm_w, v_mem_norm_w=v_mem_norm_w, v_w_mq=v_w_mq, v_w_mkv=v_w_mkv, v_mq_norm_w=v_mq_norm_w, v_mk_norm_w=v_mk_norm_w, v_w_mo=v_w_mo, v_ffn_norm_w=v_ffn_norm_w, v_w_gate_up=v_w_gate_up, v_w_down=v_w_down)
    weights = {n: given[n] for n in TWIN_WEIGHTS}
    shared = {n: given[n] for n in SHARED_INPUTS}
    per_example = {n: given[n] for n in ['x', 'mem']}
    grad_fn = _jax.value_and_grad(_loss, argnums=(0, 1))

    def one_microbatch(ex, loss_target):
        ex = dict(ex)
        diff = ex.pop(TWIN_DIFF_INPUT)
        return grad_fn(weights, diff, {**shared, **ex}, loss_target)

    if N_MICROBATCH == 1:
        loss, (grad_w, grad_x) = one_microbatch(per_example, given["loss_target"])
    else:
        def body(carry, xs):
            loss_sum, grad_sum = carry
            l_k, (gw_k, gx_k) = one_microbatch(xs[0], xs[1])
            with _jax.named_scope("update"):
                return (loss_sum + l_k, _jax.tree.map(_jnp.add, grad_sum, gw_k)), gx_k

        init = (_jnp.zeros((), _jnp.float32), _jax.tree.map(_jnp.zeros_like, weights))
        (loss, grad_w), grad_x = _jax.lax.scan(body, init, (per_example, given["loss_target"]))
    with _jax.named_scope("update"):
        delta_w, new_m, new_v = {}, {}, {}
        for n in TWIN_WEIGHTS:
            delta_w[n], new_m[n], new_v[n] = _adamw(weights[n], grad_w[n], given["m_" + n], given["v_" + n])
    return (loss, grad_x, *[grad_w[n] for n in TWIN_WEIGHTS], *[delta_w[n] for n in TWIN_WEIGHTS],
            *[new_m[n] for n in TWIN_WEIGHTS], *[new_v[n] for n in TWIN_WEIGHTS])
```

```python
import functools
import math

import jax
import jax.numpy as jnp
from jax import lax
from jax.experimental import pallas as pl
from jax.experimental.pallas import tpu as pltpu

F32 = jnp.float32
BF16 = jnp.bfloat16

N_DEV = 8
D_MODEL = 1024
GLA_HEADS = 4
GLA_DK = 64
GLA_DV = 128
GLA_CHUNK = 64
GLA_SUB = 16
GLA_QK_W = GLA_HEADS * GLA_DK
GLA_V_W = GLA_HEADS * GLA_DV
GATE_RANK = 16
SB_HEADS = 8
SB_DH = 64
SB_W = SB_HEADS * SB_DH
SB_BLK = 128
MEM_LEN = 256
MEM_HEADS = 4
MEM_DH = 256
D_FF = 2816
D_IN = 3088
RMS_EPS = 1e-6
LANES = 128

PROJ_W = 3200
C_QG, C_KG, C_VG, C_GG, C_QS, C_KS, C_VS, C_LR = 0, 256, 512, 1024, 1536, 2048, 2560, 3072

ADAM_LR, ADAM_B1, ADAM_B2, ADAM_EPS, ADAM_WD, ADAM_STEP = 0.001, 0.9, 0.999, 1e-08, 0.01, 10

SHARDED = ("w_in", "w_out", "w_mq", "w_mkv", "w_mo", "w_gate_up", "w_down", "w_gk_up")
SHARD_SHAPE = {"w_in": (1024, 386), "w_out": (128, 1024), "w_mq": (128, 1024), "w_mkv": (1024, 256),
               "w_mo": (128, 1024), "w_gate_up": (1024, 704), "w_down": (352, 1024), "w_gk_up": (16, 32)}
SHARD_ROWS = {n: -(-(s[0] * s[1]) // 1024) for n, s in SHARD_SHAPE.items()}
REPL = ("mix_norm_w", "b_gk", "gla_norm_w", "sb_norm_w", "xattn_norm_w", "mem_norm_w", "mq_norm_w",
        "mk_norm_w", "ffn_norm_w")
WEIGHTS = ("mix_norm_w", "w_in", "w_gk_up", "b_gk", "gla_norm_w", "sb_norm_w", "w_out", "xattn_norm_w",
           "mem_norm_w", "w_mq", "w_mkv", "mq_norm_w", "mk_norm_w", "w_mo", "ffn_norm_w", "w_gate_up", "w_down")
AG_ROWS = 2096
RS_BLOCK = 192
RS_ROWS = 11 * RS_BLOCK
VMEM_LIMIT = 56 * 1024 * 1024


def _cparams(*sem):
    return pltpu.CompilerParams(dimension_semantics=sem if sem else None, vmem_limit_bytes=VMEM_LIMIT)


def _dot(a, b, ca=1, cb=0):
    return lax.dot_general(a.astype(BF16), b.astype(BF16), (((ca,), (cb,)), ((), ())),
                           preferred_element_type=F32)


def _split(x, parts):
    out = []
    for _ in range(parts - 1):
        hi = x.astype(BF16)
        out.append(hi)
        x = x - hi.astype(F32)
    out.append(x.astype(BF16))
    return out


def _dot_lhs_exact(x, m, ca=1, cb=0, parts=3):
    acc = None
    for p in _split(x, parts):
        t = _dot(p, m, ca, cb)
        acc = t if acc is None else acc + t
    return acc


def _dot_rhs_exact(m, x, ca=1, cb=0, parts=3):
    acc = None
    for p in _split(x, parts):
        t = _dot(m, p, ca, cb)
        acc = t if acc is None else acc + t
    return acc


def _dot3(a, b, ca=1, cb=0):
    a_hi, a_lo = _split(a, 2)
    b_hi, b_lo = _split(b, 2)
    return _dot(a_hi, b_hi, ca, cb) + (_dot(a_hi, b_lo, ca, cb) + _dot(a_lo, b_hi, ca, cb))


def _log_sigmoid(z):
    return jnp.minimum(z, 0.0) - jnp.log(1.0 + jnp.exp(-jnp.abs(z)))


def _sigmoid(z):
    e = jnp.exp(-jnp.abs(z))
    return jnp.where(z >= 0, 1.0, e) / (1.0 + e)


def _iota2(shape, dim):
    return lax.broadcasted_iota(jnp.int32, shape, dim)


def _rowcall(name, fn, row_ins, full_ins, row_outs, acc_outs, bm, rows):
    n_in = len(row_ins) + len(full_ins)
    n_row = len(row_outs)

    def body(*refs):
        ins, outs = refs[:n_in], refs[n_in:]
        res = fn(*[r[...] for r in ins])
        for r, v in zip(outs[:n_row], res[:n_row]):
            r[...] = v.astype(r.dtype)
        first = pl.program_id(0) == 0
        for r, v in zip(outs[n_row:], res[n_row:]):
            def init(r=r):
                r[...] = jnp.zeros(r.shape, r.dtype)
            pl.when(first)(init)
            r[...] += v

    in_specs = [pl.BlockSpec((bm, w), functools.partial(lambda i, c: (i, c), c=c)) for _, w, c in row_ins]
    in_specs += [pl.BlockSpec(a.shape, lambda i: (0, 0)) for a in full_ins]
    out_specs = [pl.BlockSpec((bm, w), lambda i: (i, 0)) for w, _ in row_outs]
    out_specs += [pl.BlockSpec(s, lambda i: (0, 0)) for s in acc_outs]
    out_shape = [jax.ShapeDtypeStruct((rows, w), dt) for w, dt in row_outs]
    out_shape += [jax.ShapeDtypeStruct(s, F32) for s in acc_outs]
    return pl.pallas_call(
        body, name=name, grid=(rows // bm,), in_specs=in_specs, out_specs=out_specs, out_shape=out_shape,
        compiler_params=_cparams("arbitrary"),
    )(*[a for a, _, _ in row_ins], *full_ins)


def _matmul(name, a, b, mode, m, n, k, out_dtype, bm, bn, bk, residual=None, a_spec=None, b_spec=None):
    bm, bn, bk = min(bm, m), min(bn, n), min(bk, k)
    nk = k // bk
    ca, cb = {"nn": (1, 0), "nt": (1, 1), "tn": (0, 0)}[mode]
    if a_spec is None:
        a_spec = (pl.BlockSpec((bk, bm), lambda i, j, kk: (kk, i)) if mode == "tn"
                  else pl.BlockSpec((bm, bk), lambda i, j, kk: (i, kk)))
    if b_spec is None:
        b_spec = (pl.BlockSpec((bn, bk), lambda i, j, kk: (j, kk)) if mode == "nt"
                  else pl.BlockSpec((bk, bn), lambda i, j, kk: (kk, j)))
    has_res = residual is not None

    def body(*refs):
        a_ref, b_ref = refs[0], refs[1]
        res_ref = refs[2] if has_res else None
        o_ref = refs[2 + has_res]
        part = _dot(a_ref[...], b_ref[...], ca, cb)

        def finish(total):
            if has_res:
                total = total + res_ref[...]
            o_ref[...] = total.astype(o_ref.dtype)

        if nk == 1:
            finish(part)
        else:
            acc_ref = refs[3 + has_res]
            kk = pl.program_id(2)

            @pl.when(kk == 0)
            def _():
                acc_ref[...] = part

            @pl.when(kk > 0)
            def _():
                acc_ref[...] += part

            @pl.when(kk == nk - 1)
            def _():
                finish(acc_ref[...])

    in_specs = [a_spec, b_spec]
    args = [a, b]
    if has_res:
        in_specs.append(pl.BlockSpec((bm, bn), lambda i, j, kk: (i, j)))
        args.append(residual)
    return pl.pallas_call(
        body, name=name, grid=(m // bm, n // bn, nk), in_specs=in_specs,
        out_specs=pl.BlockSpec((bm, bn), lambda i, j, kk: (i, j)),
        out_shape=jax.ShapeDtypeStruct((m, n), out_dtype),
        scratch_shapes=[pltpu.VMEM((bm, bn), F32)] if nk > 1 else [],
        compiler_params=_cparams("parallel", "parallel", "arbitrary"),
    )(*args)


def _peer(mask):
    x, y, c = lax.axis_index("x"), lax.axis_index("y"), lax.axis_index("c")
    mx, my, mc = (mask >> 2) & 1, (mask >> 1) & 1, mask & 1
    px, py, pc = (1 - x if mx else x), (1 - y if my else y), (1 - c if mc else c)
    return (px, py, pc), 4 * px + 2 * py + pc


def _my_index():
    return 4 * lax.axis_index("x") + 2 * lax.axis_index("y") + lax.axis_index("c")


def _all_gather(pack):
    rows = pack.shape[0]

    def body(src_ref, out_ref, send_sems, recv_sems, local_sem):
        me = _my_index()
        mine = pltpu.make_async_copy(src_ref, out_ref.at[me], local_sem)
        mine.start()
        copies = []
        for mask in range(1, N_DEV):
            peer, _ = _peer(mask)
            cp = pltpu.make_async_remote_copy(
                src_ref=src_ref, dst_ref=out_ref.at[me], send_sem=send_sems.at[mask - 1],
                recv_sem=recv_sems.at[mask - 1], device_id=peer, device_id_type=pl.DeviceIdType.MESH)
            cp.start()
            copies.append(cp)
        for cp in copies:
            cp.wait()
        mine.wait()

    return pl.pallas_call(
        body, name="all_gather_weights",
        in_specs=[pl.BlockSpec(memory_space=pl.ANY)], out_specs=pl.BlockSpec(memory_space=pl.ANY),
        out_shape=jax.ShapeDtypeStruct((N_DEV, rows, 1024), pack.dtype),
        scratch_shapes=[pltpu.SemaphoreType.DMA((N_DEV - 1,)), pltpu.SemaphoreType.DMA((N_DEV - 1,)),
                        pltpu.SemaphoreType.DMA(())],
        compiler_params=pltpu.CompilerParams(has_side_effects=True),
    )(pack)


def _scatter_to_owners(gpack):
    rows = gpack.shape[1]

    def body(src_ref, out_ref, send_sems, recv_sems, local_sem):
        me = _my_index()
        mine = pltpu.make_async_copy(src_ref.at[me], out_ref.at[me], local_sem)
        mine.start()
        copies = []
        for mask in range(1, N_DEV):
            peer, peer_index = _peer(mask)
            cp = pltpu.make_async_remote_copy(
                src_ref=src_ref.at[peer_index], dst_ref=out_ref.at[me], send_sem=send_sems.at[mask - 1],
                recv_sem=recv_sems.at[mask - 1], device_id=peer, device_id_type=pl.DeviceIdType.MESH)
            cp.start()
            copies.append(cp)
        for cp in copies:
            cp.wait()
        mine.wait()

    return pl.pallas_call(
        body, name="scatter_gradients",
        in_specs=[pl.BlockSpec(memory_space=pl.ANY)], out_specs=pl.BlockSpec(memory_space=pl.ANY),
        out_shape=jax.ShapeDtypeStruct((N_DEV, rows, 1024), gpack.dtype),
        scratch_shapes=[pltpu.SemaphoreType.DMA((N_DEV - 1,)), pltpu.SemaphoreType.DMA((N_DEV - 1,)),
                        pltpu.SemaphoreType.DMA(())],
        compiler_params=pltpu.CompilerParams(has_side_effects=True),
    )(gpack)


def _adamw(recv, w, m, v):
    rows = w.shape[0]
    c1 = 1.0 - ADAM_B1 ** ADAM_STEP
    c2 = 1.0 - ADAM_B2 ** ADAM_STEP

    def body(r_ref, w_ref, m_ref, v_ref, g_out, d_out, m_out, v_out):
        g = r_ref[0]
        for s in range(1, N_DEV):
            g = g + r_ref[s]
        m_new = ADAM_B1 * m_ref[...] + (1.0 - ADAM_B1) * g
        v_new = ADAM_B2 * v_ref[...] + (1.0 - ADAM_B2) * (g * g)
        m_hat = m_new / c1
        v_hat = v_new / c2
        g_out[...] = g
        d_out[...] = -ADAM_LR * (m_hat / (jnp.sqrt(v_hat) + ADAM_EPS) + ADAM_WD * w_ref[...])
        m_out[...] = m_new
        v_out[...] = v_new

    blk = pl.BlockSpec((RS_BLOCK, 1024), lambda i: (i, 0))
    return pl.pallas_call(
        body, name="sum_adamw", grid=(rows // RS_BLOCK,),
        in_specs=[pl.BlockSpec((N_DEV, RS_BLOCK, 1024), lambda i: (0, i, 0)), blk, blk, blk],
        out_specs=[blk] * 4, out_shape=[jax.ShapeDtypeStruct((rows, 1024), F32)] * 4,
        compiler_params=_cparams("parallel"),
    )(recv, w, m, v)


def _rms_fwd(name, x, w, bm):
    def fn(xb, wb):
        r = lax.rsqrt(jnp.mean(xb * xb, axis=-1, keepdims=True) + RMS_EPS)
        return (xb * r * wb,)
    return _rowcall(name, fn, [(x, D_MODEL, 0)], [w], [(D_MODEL, BF16)], [], bm, x.shape[0])[0]


def _rms_bwd(name, x, w, dh, dres, bm):
    def fn(xb, dhb, drb, wb):
        r = lax.rsqrt(jnp.mean(xb * xb, axis=-1, keepdims=True) + RMS_EPS)
        xh = xb * r
        dxh = dhb.astype(F32) * wb
        dx = drb + r * (dxh - xh * jnp.mean(dxh * xh, axis=-1, keepdims=True))
        return dx, jnp.sum(dhb.astype(F32) * xh, axis=0, keepdims=True)
    return _rowcall(name, fn, [(x, D_MODEL, 0), (dh, D_MODEL, 0), (dres, D_MODEL, 0)], [w],
                    [(D_MODEL, F32)], [(1, D_MODEL)], bm, x.shape[0])


def _gate_fwd(proj, wgk_pad, b_gk, bm):
    def fn(lr, wg, bg):
        z = _dot(lr, wg) + bg
        return (_log_sigmoid(z) * (1.0 / 16.0),)
    return _rowcall("gla_gate_fwd", fn, [(proj, LANES, C_LR // LANES)], [wgk_pad, b_gk],
                    [(GLA_QK_W, F32)], [], bm, proj.shape[0])[0]


def _gate_bwd(proj, wgk_pad, b_gk, dgk, bm):
    def fn(lr, dg, wg, bg):
        z = _dot(lr, wg) + bg
        dz = dg * _sigmoid(-z) * (1.0 / 16.0)
        return _dot(dz, wg, 1, 1), _dot(lr, dz, 0, 0), jnp.sum(dz, axis=0, keepdims=True)
    return _rowcall("gla_gate_bwd", fn, [(proj, LANES, C_LR // LANES), (dgk, GLA_QK_W, 0)], [wgk_pad, b_gk],
                    [(LANES, BF16)], [(LANES, GLA_QK_W), (1, GLA_QK_W)], bm, proj.shape[0])


def _group_mean(x, g, size):
    return _dot_lhs_exact(x, g, parts=2) * (1.0 / size)


def _mix_out_fwd(o_g, proj, o_s, wg_t, ws_t, grp_g, grp_s, bm):
    def fn(og, gg, os_, wg, ws, gmat, smat):
        rg = lax.rsqrt(_group_mean(og * og, gmat, GLA_DV) + RMS_EPS)
        yg = og * rg * wg * (gg * _sigmoid(gg))
        rs = lax.rsqrt(_group_mean(os_ * os_, smat, SB_DH) + RMS_EPS)
        ys = os_ * rs * ws
        return (jnp.concatenate([yg, ys], axis=1),)
    return _rowcall("mix_out_fwd", fn, [(o_g, GLA_V_W, 0), (proj, GLA_V_W, C_GG // GLA_V_W), (o_s, SB_W, 0)],
                    [wg_t, ws_t, grp_g, grp_s], [(D_MODEL, BF16)], [], bm, o_g.shape[0])[0]


def _mix_out_bwd(dcat, o_g, proj, o_s, wg_t, ws_t, grp_g, grp_s, bm):
    def fn(dyg, dys, og, gg, os_, wg, ws, gmat, smat):
        dyg = dyg.astype(F32)
        dys = dys.astype(F32)
        rg = lax.rsqrt(_group_mean(og * og, gmat, GLA_DV) + RMS_EPS)
        xh = og * rg
        sg = _sigmoid(gg)
        silu = gg * sg
        dxh = dyg * wg * silu
        dgg = dyg * xh * wg * (sg * (1.0 + gg * (1.0 - sg)))
        dwg = jnp.sum(dyg * xh * silu, axis=0, keepdims=True)
        dog = rg * (dxh - xh * _group_mean(dxh * xh, gmat, GLA_DV))
        rs = lax.rsqrt(_group_mean(os_ * os_, smat, SB_DH) + RMS_EPS)
        xs = os_ * rs
        dxs = dys * ws
        dws = jnp.sum(dys * xs, axis=0, keepdims=True)
        dos = rs * (dxs - xs * _group_mean(dxs * xs, smat, SB_DH))
        return dog, dgg, dos, dwg, dws
    return _rowcall("mix_out_bwd", fn,
                    [(dcat, GLA_V_W, 0), (dcat, SB_W, 1), (o_g, GLA_V_W, 0), (proj, GLA_V_W, C_GG // GLA_V_W),
                     (o_s, SB_W, 0)],
                    [wg_t, ws_t, grp_g, grp_s], [(GLA_V_W, F32), (GLA_V_W, BF16), (SB_W, F32)],
                    [(1, GLA_V_W), (1, SB_W)], bm, o_g.shape[0])


def _loss_kernel(y, tgt, bm):
    def fn(yb, tb):
        err = yb - tb
        return err * (1.0 / D_MODEL), jnp.sum(err * err, axis=0, keepdims=True)
    return _rowcall("loss_head", fn, [(y, D_MODEL, 0), (tgt, D_MODEL, 0)], [], [(D_MODEL, F32)], [(1, D_MODEL)],
                    bm, y.shape[0])


def _sb_tile(q, k, v, c, m_excl, strict):
    z = _dot(q, k, 1, 1)
    lb = _log_sigmoid(z)
    l1 = lb - z
    if strict is not None:
        l1 = jnp.where(strict, l1, 0.0)
    a = jnp.exp(lb + _dot_lhs_exact(l1, m_excl, parts=2) + c)
    if strict is not None:
        a = jnp.where(strict, a, 0.0)
    return lb, l1, a


def _sb_fwd(proj, rows):
    nq = rows // SB_BLK

    def body(q_ref, k_ref, v_ref, o_ref):
        i = pl.program_id(1)
        r_i, c_i = _iota2((SB_BLK, SB_BLK), 0), _iota2((SB_BLK, SB_BLK), 1)
        m_excl = (r_i > c_i).astype(BF16)
        strict = c_i < r_i
        for hh in range(2):
            sl = slice(SB_DH * hh, SB_DH * (hh + 1))
            q = (q_ref[:, sl] * 0.125).astype(BF16)

            def tile(j, c, mask):
                start = pl.multiple_of(j * SB_BLK, SB_BLK)
                k = k_ref[pl.ds(start, SB_BLK), sl].astype(BF16)
                v = v_ref[pl.ds(start, SB_BLK), sl].astype(BF16)
                _, l1, a = _sb_tile(q, k, v, c, m_excl, mask)
                return _dot(a, v), jnp.sum(l1, axis=1, keepdims=True)

            o0, c0 = tile(i, jnp.zeros((SB_BLK, 1), F32), strict)

            def step(jj, carry):
                o, c = carry
                do, dc = tile(i - 1 - jj, c, None)
                return o + do, c + dc

            o, _ = lax.fori_loop(0, i, step, (o0, c0))
            o_ref[:, sl] = o

    return pl.pallas_call(
        body, name="sb_attention_fwd", grid=(SB_HEADS // 2, nq),
        in_specs=[pl.BlockSpec((SB_BLK, LANES), lambda h, i: (i, C_QS // LANES + h)),
                  pl.BlockSpec((rows, LANES), lambda h, i: (0, C_KS // LANES + h)),
                  pl.BlockSpec((rows, LANES), lambda h, i: (0, C_VS // LANES + h))],
        out_specs=pl.BlockSpec((SB_BLK, LANES), lambda h, i: (i, h)),
        out_shape=jax.ShapeDtypeStruct((rows, SB_W), F32),
        compiler_params=_cparams("parallel", "arbitrary"),
    )(proj, proj, proj)


def _sb_bwd(proj, o_s, do_s, rows):
    nq = rows // SB_BLK

    def body(q_ref, k_ref, v_ref, o_ref, do_ref, dq_ref, dk_ref, dv_ref):
        i = pl.program_id(1)

        @pl.when(i == 0)
        def _():
            dk_ref[...] = jnp.zeros(dk_ref.shape, F32)
            dv_ref[...] = jnp.zeros(dv_ref.shape, F32)

        r_i, c_i = _iota2((SB_BLK, SB_BLK), 0), _iota2((SB_BLK, SB_BLK), 1)
        m_excl = (r_i > c_i).astype(BF16)
        m_incl = (r_i >= c_i).astype(BF16)
        strict = c_i < r_i
        for hh in range(2):
            sl = slice(SB_DH * hh, SB_DH * (hh + 1))
            q = (q_ref[:, sl] * 0.125).astype(BF16)
            do = do_ref[:, sl]
            dob = do.astype(BF16)
            dsum = jnp.sum(dob.astype(F32) * o_ref[:, sl], axis=1, keepdims=True)

            def tile(j, c, cp, mask):
                start = pl.multiple_of(j * SB_BLK, SB_BLK)
                k = k_ref[pl.ds(start, SB_BLK), sl].astype(BF16)
                v = v_ref[pl.ds(start, SB_BLK), sl].astype(BF16)
                lb, l1, a = _sb_tile(q, k, v, c, m_excl, mask)
                ab = a.astype(BF16)
                p = ab.astype(F32) * _dot(dob, v, 1, 1)
                left = dsum - (_dot_lhs_exact(p, m_incl, parts=2) + cp)
                beta = jnp.exp(lb)
                dz = p * (1.0 - beta) - left * beta
                if mask is not None:
                    dz = jnp.where(mask, dz, 0.0)
                dzb = dz.astype(BF16)
                dk_ref[pl.ds(start, SB_BLK), sl] += _dot(dzb, q, 0, 0)
                dv_ref[pl.ds(start, SB_BLK), sl] += _dot(ab, dob, 0, 0)
                return (_dot(dzb, k), jnp.sum(l1, axis=1, keepdims=True), jnp.sum(p, axis=1, keepdims=True))

            zero = jnp.zeros((SB_BLK, 1), F32)
            dq0, c0, cp0 = tile(i, zero, zero, strict)

            def step(jj, carry):
                dq, c, cp = carry
                ddq, dc, dcp = tile(i - 1 - jj, c, cp, None)
                return dq + ddq, c + dc, cp + dcp

            dq, _, _ = lax.fori_loop(0, i, step, (dq0, c0, cp0))
            dq_ref[:, sl] = dq * 0.125

    whole = lambda base: pl.BlockSpec((rows, LANES), functools.partial(lambda h, i, b: (0, b + h), b=base))
    blk = lambda base: pl.BlockSpec((SB_BLK, LANES), functools.partial(lambda h, i, b: (i, b + h), b=base))
    return pl.pallas_call(
        body, name="sb_attention_bwd", grid=(SB_HEADS // 2, nq),
        in_specs=[blk(C_QS // LANES), whole(C_KS // LANES), whole(C_VS // LANES), blk(0), blk(0)],
        out_specs=[blk(0), whole(0), whole(0)],
        out_shape=[jax.ShapeDtypeStruct((rows, SB_W), F32)] * 3,
        compiler_params=_cparams("parallel", "arbitrary"),
    )(proj, proj, proj, o_s, do_s)


def _gla_chunk_common(g_all):
    r_i, c_i = _iota2((GLA_CHUNK, GLA_CHUNK), 0), _iota2((GLA_CHUNK, GLA_CHUNK), 1)
    tri = (c_i <= r_i).astype(BF16)
    return _dot_rhs_exact(tri, g_all), r_i, c_i


def _gla_sub(qh, kh, bh, sub):
    rs, nc = GLA_SUB * sub, GLA_SUB * (sub + 1)
    ref = bh[rs:rs + 1, :]
    eq = jnp.exp(bh[rs:rs + GLA_SUB] - ref)
    ek = jnp.exp(jnp.where(_iota2((GLA_CHUNK, GLA_DK), 0) < nc, ref - bh, 0.0))
    mask = _iota2((GLA_SUB, GLA_CHUNK), 1) <= _iota2((GLA_SUB, GLA_CHUNK), 0) + rs
    return rs, eq, ek, qh[rs:rs + GLA_SUB] * eq, kh * ek, mask


def _gla_fwd(proj, gk, rows):
    n_chunks = rows // GLA_CHUNK

    def body(q_ref, k_ref, v_ref, g_ref, o_ref, sall_ref, s_scr):
        @pl.when(pl.program_id(0) == 0)
        def _():
            s_scr[...] = jnp.zeros(s_scr.shape, F32)

        g_all = g_ref[...]
        b_all, _, _ = _gla_chunk_common(g_all)
        ones = jnp.ones((GLA_CHUNK, GLA_DV), BF16)
        for h in range(GLA_HEADS):
            sl = slice(GLA_DK * h, GLA_DK * (h + 1))
            vs = slice(GLA_DV * h, GLA_DV * (h + 1))
            qh, kh, vh = q_ref[:, sl] * 0.125, k_ref[:, sl], v_ref[:, vs]
            bh, gh = b_all[:, sl], g_all[:, sl]
            s = s_scr[h]
            sall_ref[0, h] = s
            o = _dot3(qh * jnp.exp(bh), s)
            parts = []
            for sub in range(GLA_CHUNK // GLA_SUB):
                _, _, _, qs, ks, mask = _gla_sub(qh, kh, bh, sub)
                a = jnp.where(mask, _dot3(qs, ks, 1, 1), 0.0)
                parts.append(_dot3(a, vh))
            o_ref[:, vs] = o + jnp.concatenate(parts, axis=0)
            bl_col = _dot_lhs_exact(gh, ones, 0, 0)
            kd = kh * jnp.exp(bh[GLA_CHUNK - 1:GLA_CHUNK, :] - bh)
            s_scr[h] = jnp.exp(bl_col) * s + _dot3(kd, vh, 0, 0)

    c64 = lambda w, base: pl.BlockSpec((GLA_CHUNK, w), functools.partial(lambda n, b: (n, b), b=base))
    return pl.pallas_call(
        body, name="gla_fwd", grid=(n_chunks,),
        in_specs=[c64(GLA_QK_W, C_QG // GLA_QK_W), c64(GLA_QK_W, C_KG // GLA_QK_W), c64(GLA_V_W, C_VG // GLA_V_W),
                  c64(GLA_QK_W, 0)],
        out_specs=[c64(GLA_V_W, 0), pl.BlockSpec((1, GLA_HEADS, GLA_DK, GLA_DV), lambda n: (n, 0, 0, 0))],
        out_shape=[jax.ShapeDtypeStruct((rows, GLA_V_W), F32),
                   jax.ShapeDtypeStruct((n_chunks, GLA_HEADS, GLA_DK, GLA_DV), F32)],
        scratch_shapes=[pltpu.VMEM((GLA_HEADS, GLA_DK, GLA_DV), F32)],
        compiler_params=_cparams("arbitrary"),
    )(proj, proj, proj, gk)


def _gla_bwd(proj, gk, do_g, s_all, rows):
    n_chunks = rows // GLA_CHUNK

    def body(q_ref, k_ref, v_ref, g_ref, do_ref, sall_ref, dq_ref, dk_ref, dv_ref, dg_ref, ds_scr):
        @pl.when(pl.program_id(0) == 0)
        def _():
            ds_scr[...] = jnp.zeros(ds_scr.shape, F32)

        g_all = g_ref[...]
        b_all, r_i, c_i = _gla_chunk_common(g_all)
        triu = (c_i >= r_i).astype(BF16)
        ones = jnp.ones((GLA_CHUNK, GLA_DV), BF16)
        ones8 = jnp.ones((8, GLA_DV), F32)
        last_row = _iota2((GLA_CHUNK, GLA_DK), 0) == GLA_CHUNK - 1
        for h in range(GLA_HEADS):
            sl = slice(GLA_DK * h, GLA_DK * (h + 1))
            vs = slice(GLA_DV * h, GLA_DV * (h + 1))
            qh, kh, vh = q_ref[:, sl] * 0.125, k_ref[:, sl], v_ref[:, vs]
            bh, gh = b_all[:, sl], g_all[:, sl]
            doh = do_ref[:, vs]
            s, ds = sall_ref[0, h], ds_scr[h]
            eb = jnp.exp(bh)
            ekd = jnp.exp(bh[GLA_CHUNK - 1:GLA_CHUNK, :] - bh)
            ebl = jnp.exp(_dot_lhs_exact(gh, ones, 0, 0))
            qb, kd = qh * eb, kh * ekd
            dq = _dot3(doh, s, 1, 1) * eb
            dk = _dot3(vh, ds, 1, 1) * ekd
            dv = _dot3(kd, ds)
            dbl = jnp.sum(dk * kh, axis=0, keepdims=True) + _dot3(ones8, ebl * s * ds, 1, 1)[0:1, :]
            dq_parts = []
            for sub in range(GLA_CHUNK // GLA_SUB):
                rs, eq, ek, qs, ks, mask = _gla_sub(qh, kh, bh, sub)
                do_sub = doh[rs:rs + GLA_SUB]
                a = jnp.where(mask, _dot3(qs, ks, 1, 1), 0.0)
                da = jnp.where(mask, _dot3(do_sub, vh, 1, 1), 0.0)
                dq_parts.append(_dot3(da, ks) * eq)
                dk = dk + _dot3(da, qs, 0, 0) * ek
                dv = dv + _dot3(a, do_sub, 0, 0)
            dq = dq + jnp.concatenate(dq_parts, axis=0)
            db = qh * dq - kh * dk + jnp.where(last_row, dbl, 0.0)
            dq_ref[:, sl] = dq * 0.125
            dk_ref[:, sl] = dk
            dv_ref[:, vs] = dv
            dg_ref[:, sl] = _dot_rhs_exact(triu, db)
            ds_scr[h] = _dot3(qb, doh, 0, 0) + ebl * ds

    last = n_chunks - 1
    c64 = lambda w, base: pl.BlockSpec((GLA_CHUNK, w), functools.partial(lambda n, b: (last - n, b), b=base))
    return pl.pallas_call(
        body, name="gla_bwd", grid=(n_chunks,),
        in_specs=[c64(GLA_QK_W, C_QG // GLA_QK_W), c64(GLA_QK_W, C_KG // GLA_QK_W), c64(GLA_V_W, C_VG // GLA_V_W),
                  c64(GLA_QK_W, 0), c64(GLA_V_W, 0),
                  pl.BlockSpec((1, GLA_HEADS, GLA_DK, GLA_DV), lambda n: (last - n, 0, 0, 0))],
        out_specs=[c64(GLA_QK_W, 0), c64(GLA_QK_W, 0), c64(GLA_V_W, 0), c64(GLA_QK_W, 0)],
        out_shape=[jax.ShapeDtypeStruct((rows, GLA_QK_W), F32), jax.ShapeDtypeStruct((rows, GLA_QK_W), F32),
                   jax.ShapeDtypeStruct((rows, GLA_V_W), F32), jax.ShapeDtypeStruct((rows, GLA_QK_W), F32)],
        scratch_shapes=[pltpu.VMEM((GLA_HEADS, GLA_DK, GLA_DV), F32)],
        compiler_params=_cparams("arbitrary"),
    )(proj, proj, proj, gk, do_g, s_all)


def _mem_kv_fwd(mem, mem_norm_w, w_mkv, mk_norm_w):
    def body(mem_ref, mw_ref, w_ref, kw_ref, memn_ref, kpre_ref, kn_ref, v_ref):
        xb = mem_ref[...]
        r = lax.rsqrt(jnp.mean(xb * xb, axis=-1, keepdims=True) + RMS_EPS)
        mem_n = (xb * r * mw_ref[...]).astype(BF16)
        memn_ref[...] = mem_n
        kv = _dot(mem_n, w_ref[...])
        kpre_ref[...] = kv[:, :D_MODEL]
        v_ref[...] = kv[:, D_MODEL:].astype(BF16)
        for h in range(MEM_HEADS):
            sl = slice(MEM_DH * h, MEM_DH * (h + 1))
            kh = kv[:, sl]
            rk = lax.rsqrt(jnp.mean(kh * kh, axis=-1, keepdims=True) + RMS_EPS)
            kn_ref[:, sl] = (kh * rk * kw_ref[...]).astype(BF16)

    return pl.pallas_call(
        body, name="mem_kv_fwd",
        out_shape=[jax.ShapeDtypeStruct((MEM_LEN, D_MODEL), BF16), jax.ShapeDtypeStruct((MEM_LEN, D_MODEL), F32),
                   jax.ShapeDtypeStruct((MEM_LEN, D_MODEL), BF16), jax.ShapeDtypeStruct((MEM_LEN, D_MODEL), BF16)],
        compiler_params=_cparams(),
    )(mem, mem_norm_w, w_mkv, mk_norm_w)


def _mem_kv_bwd(mem, mem_norm_w, w_mkv, mk_norm_w, mem_n, kpre, dkn, dv):
    def body(mem_ref, mw_ref, w_ref, kw_ref, memn_ref, kpre_ref, dkn_ref, dv_ref, dw_ref, dkw_ref, dmw_ref):
        dkw = jnp.zeros((1, MEM_DH), F32)
        dk_parts = []
        for h in range(MEM_HEADS):
            sl = slice(MEM_DH * h, MEM_DH * (h + 1))
            kh, dkh = kpre_ref[:, sl], dkn_ref[:, sl]
            rk = lax.rsqrt(jnp.mean(kh * kh, axis=-1, keepdims=True) + RMS_EPS)
            xh = kh * rk
            dxh = dkh * kw_ref[...]
            dkw = dkw + jnp.sum(dkh * xh, axis=0, keepdims=True)
            dk_parts.append(rk * (dxh - xh * jnp.mean(dxh * xh, axis=-1, keepdims=True)))
        dkw_ref[...] = dkw
        dkv = jnp.concatenate(dk_parts + [dv_ref[...]], axis=1).astype(BF16)
        dw_ref[...] = _dot(memn_ref[...], dkv, 0, 0)
        dmem_n = _dot(dkv, w_ref[...], 1, 1)
        xb = mem_ref[...]
        r = lax.rsqrt(jnp.mean(xb * xb, axis=-1, keepdims=True) + RMS_EPS)
        dmw_ref[...] = jnp.sum(dmem_n * (xb * r), axis=0, keepdims=True)

    return pl.pallas_call(
        body, name="mem_kv_bwd",
        out_shape=[jax.ShapeDtypeStruct((D_MODEL, 2 * D_MODEL), F32), jax.ShapeDtypeStruct((1, MEM_DH), F32),
                   jax.ShapeDtypeStruct((1, D_MODEL), F32)],
        compiler_params=_cparams(),
    )(mem, mem_norm_w, w_mkv, mk_norm_w, mem_n, kpre, dkn, dv)


def _xattn_head(qh, kn_h, qw):
    rq = lax.rsqrt(jnp.mean(qh * qh, axis=-1, keepdims=True) + RMS_EPS)
    xh = qh * rq
    qn = (xh * qw).astype(BF16)
    s = _dot(qn, kn_h, 1, 1) * (1.0 / 16.0)
    e = jnp.exp(s - jnp.max(s, axis=-1, keepdims=True))
    p = e / jnp.sum(e, axis=-1, keepdims=True)
    return rq, xh, qn, p


def _xattn_fwd(qm, kn, v, mq_norm_w, bm):
    def fn(qb, knb, vb, qw):
        outs = []
        for h in range(MEM_HEADS):
            sl = slice(MEM_DH * h, MEM_DH * (h + 1))
            _, _, _, p = _xattn_head(qb[:, sl], knb[:, sl], qw)
            outs.append(_dot(p, vb[:, sl]))
        return (jnp.concatenate(outs, axis=1),)
    return _rowcall("xattn_fwd", fn, [(qm, D_MODEL, 0)], [kn, v, mq_norm_w], [(D_MODEL, BF16)], [], bm,
                    qm.shape[0])[0]


def _xattn_bwd(qm, kn, v, mq_norm_w, do, bm):
    def fn(qb, dob, knb, vb, qw):
        dq_parts, dkn_parts, dv_parts = [], [], []
        dqw = jnp.zeros((1, MEM_DH), F32)
        for h in range(MEM_HEADS):
            sl = slice(MEM_DH * h, MEM_DH * (h + 1))
            rq, xh, qn, p = _xattn_head(qb[:, sl], knb[:, sl], qw)
            doh = dob[:, sl].astype(BF16)
            dp = _dot(doh, vb[:, sl], 1, 1)
            ds = (p * (dp - jnp.sum(dp * p, axis=-1, keepdims=True)) * (1.0 / 16.0)).astype(BF16)
            dqn = _dot(ds, knb[:, sl])
            dkn_parts.append(_dot(ds, qn, 0, 0))
            dv_parts.append(_dot(p, doh, 0, 0))
            dqw = dqw + jnp.sum(dqn * xh, axis=0, keepdims=True)
            dxh = dqn * qw
            dq_parts.append(rq * (dxh - xh * jnp.mean(dxh * xh, axis=-1, keepdims=True)))
        return (jnp.concatenate(dq_parts, axis=1), jnp.concatenate(dkn_parts, axis=1),
                jnp.concatenate(dv_parts, axis=1), dqw)
    return _rowcall("xattn_bwd", fn, [(qm, D_MODEL, 0), (do, D_MODEL, 0)], [kn, v, mq_norm_w],
                    [(D_MODEL, BF16)], [(MEM_LEN, D_MODEL), (MEM_LEN, D_MODEL), (1, MEM_DH)], bm, qm.shape[0])


FF_BN = 1408
FF_NB = D_FF // FF_BN


def _ffn_up(h3, w_gate_up, rows, bm):
    def body(h_ref, wg_ref, wu_ref, gate_ref, up_ref, act_ref):
        hb = h_ref[...]
        gate = _dot(hb, wg_ref[...])
        up = _dot(hb, wu_ref[...])
        gate_ref[...] = gate
        up_ref[...] = up
        act_ref[...] = (gate * _sigmoid(gate) * up).astype(BF16)

    out_blk = pl.BlockSpec((bm, FF_BN), lambda i, j: (i, j))
    return pl.pallas_call(
        body, name="ffn_up", grid=(rows // bm, FF_NB),
        in_specs=[pl.BlockSpec((bm, D_MODEL), lambda i, j: (i, 0)),
                  pl.BlockSpec((D_MODEL, FF_BN), lambda i, j: (0, j)),
                  pl.BlockSpec((D_MODEL, FF_BN), lambda i, j: (0, FF_NB + j))],
        out_specs=[out_blk, out_blk, out_blk],
        out_shape=[jax.ShapeDtypeStruct((rows, D_FF), F32), jax.ShapeDtypeStruct((rows, D_FF), F32),
                   jax.ShapeDtypeStruct((rows, D_FF), BF16)],
        compiler_params=_cparams("parallel", "arbitrary"),
    )(h3, w_gate_up, w_gate_up)


def _ffn_act_bwd(dy, w_down, gate, up, rows, bm):
    def body(dy_ref, wd_ref, gate_ref, up_ref, o_ref):
        dact = _dot(dy_ref[...], wd_ref[...], 1, 1)
        g, u = gate_ref[...], up_ref[...]
        sg = _sigmoid(g)
        o_ref[0] = (dact * u * (sg * (1.0 + g * (1.0 - sg)))).astype(BF16)
        o_ref[1] = (dact * (g * sg)).astype(BF16)

    blk = pl.BlockSpec((bm, FF_BN), lambda i, j: (i, j))
    return pl.pallas_call(
        body, name="ffn_act_bwd", grid=(rows // bm, FF_NB),
        in_specs=[pl.BlockSpec((bm, D_MODEL), lambda i, j: (i, 0)),
                  pl.BlockSpec((FF_BN, D_MODEL), lambda i, j: (j, 0)), blk, blk],
        out_specs=pl.BlockSpec((2, bm, FF_BN), lambda i, j: (0, i, j)),
        out_shape=jax.ShapeDtypeStruct((2, rows, D_FF), BF16),
        compiler_params=_cparams("parallel", "arbitrary"),
    )(dy, w_down, gate, up)


def _pack_rows(a):
    flat = a.reshape(-1)
    pad = (-flat.shape[0]) % 1024
    if pad:
        flat = jnp.concatenate([flat, jnp.zeros((pad,), flat.dtype)])
    return flat.reshape(-1, 1024)


def _pack_shards(shards, extra_rows, total_rows, dtype):
    parts = [_pack_rows(shards[n].astype(dtype)) for n in SHARDED] + [r.astype(dtype) for r in extra_rows]
    used = sum(p.shape[0] for p in parts)
    parts.append(jnp.zeros((total_rows - used, 1024), dtype))
    return jnp.concatenate(parts, axis=0)


def _shard_offsets():
    offs, o = {}, 0
    for n in SHARDED:
        offs[n] = o
        o += SHARD_ROWS[n]
    return offs, o


def _unpack_shard(pack, name):
    offs, _ = _shard_offsets()
    r, c = SHARD_SHAPE[name]
    seg = pack[offs[name]:offs[name] + SHARD_ROWS[name]].reshape(-1)[:r * c]
    return seg.reshape(1, r, c)


def _gathered_weight(gathered, name):
    offs, _ = _shard_offsets()
    r, c = SHARD_SHAPE[name]
    seg = gathered[:, offs[name]:offs[name] + SHARD_ROWS[name]].reshape(N_DEV, -1)[:, :r * c].reshape(N_DEV, r, c)
    if name in ("w_out", "w_mq", "w_mo", "w_down"):
        return seg.reshape(N_DEV * r, c)
    return seg.transpose(1, 0, 2).reshape(r, N_DEV * c)


def _split_for_owners(name, full):
    r, c = SHARD_SHAPE[name]
    if name in ("w_out", "w_mq", "w_mo", "w_down"):
        return full.reshape(N_DEV, r, c)
    return full.reshape(r, N_DEV, c).transpose(1, 0, 2)


def _repl_row(a):
    flat = a.reshape(-1)
    return jnp.concatenate([flat, jnp.zeros((1024 - flat.shape[0],), flat.dtype)]).reshape(1, 1024)


def _local_step(x, mem, tgt, wf, rp):
    rows = x.shape[0]
    bm = min(512, rows)
    bmx = min(256, rows)
    w_cat, wgk_pad = wf["w_cat"], wf["wgk_pad"]
    wg_t = jnp.tile(rp["gla_norm_w"], (1, GLA_HEADS))
    ws_t = jnp.tile(rp["sb_norm_w"], (1, SB_HEADS))
    lane = jnp.arange(GLA_V_W)
    grp_g = (lane[:, None] // GLA_DV == lane[None, :] // GLA_DV).astype(BF16)
    grp_s = (lane[:, None] // SB_DH == lane[None, :] // SB_DH).astype(BF16)

    h1 = _rms_fwd("mix_norm_fwd", x, rp["mix_norm_w"], bm)
    proj = _matmul("in_proj", h1, w_cat, "nn", rows, PROJ_W, D_MODEL, F32, bm, 640, D_MODEL)
    gk = _gate_fwd(proj, wgk_pad, rp["b_gk"], bm)
    o_g, s_all = _gla_fwd(proj, gk, rows)
    o_s = _sb_fwd(proj, rows)
    cat = _mix_out_fwd(o_g, proj, o_s, wg_t, ws_t, grp_g, grp_s, bm)
    x1 = _matmul("out_proj", cat, wf["w_out"], "nn", rows, D_MODEL, D_MODEL, F32, bm, 512, D_MODEL, residual=x)
    h2 = _rms_fwd("xattn_norm_fwd", x1, rp["xattn_norm_w"], bm)
    qm = _matmul("mq_proj", h2, wf["w_mq"], "nn", rows, D_MODEL, D_MODEL, F32, bm, 512, D_MODEL)
    mem_n, kpre, kn, v_m = _mem_kv_fwd(mem, rp["mem_norm_w"], wf["w_mkv"], rp["mk_norm_w"])
    o_m = _xattn_fwd(qm, kn, v_m, rp["mq_norm_w"], bmx)
    x2 = _matmul("mo_proj", o_m, wf["w_mo"], "nn", rows, D_MODEL, D_MODEL, F32, bm, 512, D_MODEL, residual=x1)
    h3 = _rms_fwd("ffn_norm_fwd", x2, rp["ffn_norm_w"], bm)
    gate, up, act = _ffn_up(h3, wf["w_gate_up"], rows, bm)
    y = _matmul("ffn_down", act, wf["w_down"], "nn", rows, D_MODEL, D_FF, F32, bm, 512, D_FF // 2, residual=x2)
    dy, sq = _loss_kernel(y, tgt, bm)

    g = {}
    dgu = _ffn_act_bwd(dy, wf["w_down"], gate, up, rows, bm)
    g["w_down"] = _matmul("grad_w_down", act, dy, "tn", D_FF, D_MODEL, rows, F32, FF_BN, 512, 512)
    nkb = FF_NB
    bkr = min(512, rows)
    dh3 = _matmul("ffn_up_bwd", dgu, wf["w_gate_up"], "nt", rows, D_MODEL, 2 * D_FF, F32, bm, 512, FF_BN,
                  a_spec=pl.BlockSpec((None, bm, FF_BN), lambda i, j, kk: (kk // nkb, i, kk % nkb)))
    g["w_gate_up"] = _matmul(
        "grad_w_gate_up", h3, dgu, "tn", D_MODEL, 2 * D_FF, rows, F32, 512, FF_BN, 512,
        b_spec=pl.BlockSpec((None, bkr, FF_BN), lambda i, j, kk: (j // nkb, kk, j % nkb)))
    dx2, g["ffn_norm_w"] = _rms_bwd("ffn_norm_bwd", x2, rp["ffn_norm_w"], dh3, dy, bm)

    do_m = _matmul("mo_proj_bwd", dx2, wf["w_mo"], "nt", rows, D_MODEL, D_MODEL, BF16, bm, 512, D_MODEL)
    g["w_mo"] = _matmul("grad_w_mo", o_m, dx2, "tn", D_MODEL, D_MODEL, rows, F32, 512, 512, 512)
    dqm, dkn, dv_m, g["mq_norm_w"] = _xattn_bwd(qm, kn, v_m, rp["mq_norm_w"], do_m, bmx)
    g["w_mkv"], g["mk_norm_w"], g["mem_norm_w"] = _mem_kv_bwd(
        mem, rp["mem_norm_w"], wf["w_mkv"], rp["mk_norm_w"], mem_n, kpre, dkn, dv_m)
    dh2 = _matmul("mq_proj_bwd", dqm, wf["w_mq"], "nt", rows, D_MODEL, D_MODEL, F32, bm, 512, D_MODEL)
    g["w_mq"] = _matmul("grad_w_mq", h2, dqm, "tn", D_MODEL, D_MODEL, rows, F32, 512, 512, 512)
    dx1, g["xattn_norm_w"] = _rms_bwd("xattn_norm_bwd", x1, rp["xattn_norm_w"], dh2, dx2, bm)

    dcat = _matmul("out_proj_bwd", dx1, wf["w_out"], "nt", rows, D_MODEL, D_MODEL, F32, bm, 512, D_MODEL)
    g["w_out"] = _matmul("grad_w_out", cat, dx1, "tn", D_MODEL, D_MODEL, rows, F32, 512, 512, 512)
    do_g, dgg, do_s, dwg, dws = _mix_out_bwd(dcat, o_g, proj, o_s, wg_t, ws_t, grp_g, grp_s, bm)
    g["gla_norm_w"] = dwg.reshape(GLA_HEADS, GLA_DV).sum(axis=0, keepdims=True)
    g["sb_norm_w"] = dws.reshape(SB_HEADS, SB_DH).sum(axis=0, keepdims=True)
    dq_s, dk_s, dv_s = _sb_bwd(proj, o_s, do_s, rows)
    dq_g, dk_g, dv_g, dgk = _gla_bwd(proj, gk, do_g, s_all, rows)
    dlr, dwgk, g["b_gk"] = _gate_bwd(proj, wgk_pad, rp["b_gk"], dgk, bm)
    g["w_gk_up"] = dwgk[:GATE_RANK]
    dproj = jnp.concatenate([dq_g.astype(BF16), dk_g.astype(BF16), dv_g.astype(BF16), dgg, dq_s.astype(BF16),
                             dk_s.astype(BF16), dv_s.astype(BF16), dlr], axis=1)
    dh1 = _matmul("in_proj_bwd", dproj, w_cat, "nt", rows, D_MODEL, PROJ_W, F32, bm, 512, 640)
    dw_cat = _matmul("grad_w_in", h1, dproj, "tn", D_MODEL, PROJ_W, rows, F32, 512, 640, 512)
    g["w_in"] = jnp.concatenate([dw_cat[:, :C_QS], dw_cat[:, C_LR:C_LR + GATE_RANK], dw_cat[:, C_QS:C_LR]], axis=1)
    dx, g["mix_norm_w"] = _rms_bwd("mix_norm_bwd", x, rp["mix_norm_w"], dh1, dx1, bm)
    return sq, dx, g


def _full_weights(gathered):
    w_in = _gathered_weight(gathered, "w_in")
    lr_end = C_QS + GATE_RANK
    w_cat = jnp.concatenate([w_in[:, :C_QS], w_in[:, lr_end:], w_in[:, C_QS:lr_end],
                             jnp.zeros((D_MODEL, PROJ_W - D_IN), BF16)], axis=1)
    wgk = _gathered_weight(gathered, "w_gk_up")
    wf = {n: _gathered_weight(gathered, n) for n in ("w_out", "w_mq", "w_mkv", "w_mo", "w_gate_up", "w_down")}
    wf["w_cat"] = w_cat
    wf["wgk_pad"] = jnp.concatenate([wgk, jnp.zeros((LANES - GATE_RANK, GLA_QK_W), BF16)], axis=0)
    return wf


def kernel(x, mem, mix_norm_w, w_in, w_gk_up, b_gk, gla_norm_w, sb_norm_w, w_out, xattn_norm_w, mem_norm_w, w_mq, w_mkv, mq_norm_w, mk_norm_w, w_mo, ffn_norm_w, w_gate_up, w_down, loss_target, m_mix_norm_w, m_w_in, m_w_gk_up, m_b_gk, m_gla_norm_w, m_sb_norm_w, m_w_out, m_xattn_norm_w, m_mem_norm_w, m_w_mq, m_w_mkv, m_mq_norm_w, m_mk_norm_w, m_w_mo, m_ffn_norm_w, m_w_gate_up, m_w_down, v_mix_norm_w, v_w_in, v_w_gk_up, v_b_gk, v_gla_norm_w, v_sb_norm_w, v_w_out, v_xattn_norm_w, v_mem_norm_w, v_w_mq, v_w_mkv, v_mq_norm_w, v_mk_norm_w, v_w_mo, v_ffn_norm_w, v_w_gate_up, v_w_down):
    given = dict(locals())
    w = {n: given[n][0] for n in WEIGHTS}
    m = {n: given["m_" + n][0] for n in WEIGHTS}
    v = {n: given["v_" + n][0] for n in WEIGHTS}

    gathered = _all_gather(_pack_shards(w, [], AG_ROWS, BF16))
    wf = _full_weights(gathered)
    rp = {n: w[n].reshape(1, -1) for n in REPL}
    sq, dx, g = _local_step(x[0], mem[0], loss_target[0], wf, rp)

    loss_row = _repl_row(jnp.sum(sq).reshape(1) * (0.5 / D_MODEL))
    repl_rows = [_repl_row(g[n]) for n in REPL] + [loss_row]
    per_owner = {n: _split_for_owners(n, g[n]) for n in SHARDED}
    gpack = jnp.stack([
        _pack_shards({n: per_owner[n][d] for n in SHARDED}, repl_rows, RS_ROWS, F32) for d in range(N_DEV)])
    recv = _scatter_to_owners(gpack)

    zero_row = jnp.zeros((1, 1024), F32)
    packs = [_pack_shards(t, [_repl_row(t[n]) for n in REPL] + [zero_row], RS_ROWS, F32) for t in (w, m, v)]
    out_packs = _adamw(recv, *packs)

    _, n_shard_rows = _shard_offsets()

    def unpack(pack, name):
        if name in SHARDED:
            return _unpack_shard(pack, name)
        row = pack[n_shard_rows + REPL.index(name)]
        return row[:w[name].shape[-1]].reshape(1, -1)

    loss = out_packs[0][n_shard_rows + len(REPL), 0]
    outs = [loss, dx[None]]
    for pack in out_packs:
        outs += [unpack(pack, n) for n in WEIGHTS]
    return tuple(outs)
```

```python
import functools
import math

import jax
import jax.numpy as jnp
from jax import lax
from jax.experimental import pallas as pl
from jax.experimental.pallas import tpu as pltpu

F32 = jnp.float32
BF16 = jnp.bfloat16

N_DEV = 8
D_MODEL = 1024
GLA_HEADS = 4
GLA_DK = 64
GLA_DV = 128
GLA_CHUNK = 64
GLA_SUB = 16
GLA_QK_W = GLA_HEADS * GLA_DK
GLA_V_W = GLA_HEADS * GLA_DV
GATE_RANK = 16
SB_HEADS = 8
SB_DH = 64
SB_W = SB_HEADS * SB_DH
SB_BLK = 128
SB_QT = 1024
MEM_LEN = 256
MEM_HEADS = 4
MEM_DH = 256
D_FF = 2816
D_IN = 3088
RMS_EPS = 1e-6
LANES = 128

PROJ_W = 3200
C_QG, C_KG, C_VG, C_GG, C_QS, C_KS, C_VS, C_LR = 0, 256, 512, 1024, 1536, 2048, 2560, 3072

ADAM_LR, ADAM_B1, ADAM_B2, ADAM_EPS, ADAM_WD, ADAM_STEP = 0.001, 0.9, 0.999, 1e-08, 0.01, 10

SHARDED = ("w_in", "w_out", "w_mq", "w_mkv", "w_mo", "w_gate_up", "w_down", "w_gk_up")
SHARD_SHAPE = {"w_in": (1024, 386), "w_out": (128, 1024), "w_mq": (128, 1024), "w_mkv": (1024, 256),
               "w_mo": (128, 1024), "w_gate_up": (1024, 704), "w_down": (352, 1024), "w_gk_up": (16, 32)}
SHARD_ROWS = {n: -(-(s[0] * s[1]) // 1024) for n, s in SHARD_SHAPE.items()}
REPL = ("mix_norm_w", "b_gk", "gla_norm_w", "sb_norm_w", "xattn_norm_w", "mem_norm_w", "mq_norm_w",
        "mk_norm_w", "ffn_norm_w")
WEIGHTS = ("mix_norm_w", "w_in", "w_gk_up", "b_gk", "gla_norm_w", "sb_norm_w", "w_out", "xattn_norm_w",
           "mem_norm_w", "w_mq", "w_mkv", "mq_norm_w", "mk_norm_w", "w_mo", "ffn_norm_w", "w_gate_up", "w_down")
AG_ROWS = 2096
RS_BLOCK = 192
RS_ROWS = 11 * RS_BLOCK
VMEM_LIMIT = 56 * 1024 * 1024


def _cparams(*sem):
    return pltpu.CompilerParams(dimension_semantics=sem if sem else None, vmem_limit_bytes=VMEM_LIMIT)


def _dot(a, b, ca=1, cb=0):
    return lax.dot_general(a.astype(BF16), b.astype(BF16), (((ca,), (cb,)), ((), ())),
                           preferred_element_type=F32)


def _split(x, parts):
    out = []
    for _ in range(parts - 1):
        hi = x.astype(BF16)
        out.append(hi)
        x = x - hi.astype(F32)
    out.append(x.astype(BF16))
    return out


def _dot_lhs_exact(x, m, ca=1, cb=0, parts=3):
    acc = None
    for p in _split(x, parts):
        t = _dot(p, m, ca, cb)
        acc = t if acc is None else acc + t
    return acc


def _dot_rhs_exact(m, x, ca=1, cb=0, parts=3):
    acc = None
    for p in _split(x, parts):
        t = _dot(m, p, ca, cb)
        acc = t if acc is None else acc + t
    return acc


def _dot3(a, b, ca=1, cb=0):
    a_hi, a_lo = _split(a, 2)
    b_hi, b_lo = _split(b, 2)
    return _dot(a_hi, b_hi, ca, cb) + (_dot(a_hi, b_lo, ca, cb) + _dot(a_lo, b_hi, ca, cb))


def _log_sigmoid(z):
    return jnp.minimum(z, 0.0) - jnp.log(1.0 + jnp.exp(-jnp.abs(z)))


def _sigmoid(z):
    e = jnp.exp(-jnp.abs(z))
    return jnp.where(z >= 0, 1.0, e) / (1.0 + e)


def _iota2(shape, dim):
    return lax.broadcasted_iota(jnp.int32, shape, dim)


def _rowcall(name, fn, row_ins, full_ins, row_outs, acc_outs, bm, rows):
    n_in = len(row_ins) + len(full_ins)
    n_row = len(row_outs)

    def body(*refs):
        ins, outs = refs[:n_in], refs[n_in:]
        res = fn(*[r[...] for r in ins])
        for r, v in zip(outs[:n_row], res[:n_row]):
            r[...] = v.astype(r.dtype)
        first = pl.program_id(0) == 0
        for r, v in zip(outs[n_row:], res[n_row:]):
            def init(r=r):
                r[...] = jnp.zeros(r.shape, r.dtype)
            pl.when(first)(init)
            r[...] += v

    in_specs = [pl.BlockSpec((bm, w), functools.partial(lambda i, c: (i, c), c=c)) for _, w, c in row_ins]
    in_specs += [pl.BlockSpec(a.shape, lambda i: (0, 0)) for a in full_ins]
    out_specs = [pl.BlockSpec((bm, w), lambda i: (i, 0)) for w, _ in row_outs]
    out_specs += [pl.BlockSpec(s, lambda i: (0, 0)) for s in acc_outs]
    out_shape = [jax.ShapeDtypeStruct((rows, w), dt) for w, dt in row_outs]
    out_shape += [jax.ShapeDtypeStruct(s, F32) for s in acc_outs]
    return pl.pallas_call(
        body, name=name, grid=(rows // bm,), in_specs=in_specs, out_specs=out_specs, out_shape=out_shape,
        compiler_params=_cparams("arbitrary"),
    )(*[a for a, _, _ in row_ins], *full_ins)


def _matmul(name, a, b, mode, m, n, k, out_dtype, bm, bn, bk, residual=None, a_spec=None, b_spec=None):
    bm, bn, bk = min(bm, m), min(bn, n), min(bk, k)
    nk = k // bk
    ca, cb = {"nn": (1, 0), "nt": (1, 1), "tn": (0, 0)}[mode]
    if a_spec is None:
        a_spec = (pl.BlockSpec((bk, bm), lambda i, j, kk: (kk, i)) if mode == "tn"
                  else pl.BlockSpec((bm, bk), lambda i, j, kk: (i, kk)))
    if b_spec is None:
        b_spec = (pl.BlockSpec((bn, bk), lambda i, j, kk: (j, kk)) if mode == "nt"
                  else pl.BlockSpec((bk, bn), lambda i, j, kk: (kk, j)))
    has_res = residual is not None

    def body(*refs):
        a_ref, b_ref = refs[0], refs[1]
        res_ref = refs[2] if has_res else None
        o_ref = refs[2 + has_res]
        part = _dot(a_ref[...], b_ref[...], ca, cb)

        def finish(total):
            if has_res:
                total = total + res_ref[...]
            o_ref[...] = total.astype(o_ref.dtype)

        if nk == 1:
            finish(part)
        else:
            acc_ref = refs[3 + has_res]
            kk = pl.program_id(2)

            @pl.when(kk == 0)
            def _():
                acc_ref[...] = part

            @pl.when(kk > 0)
            def _():
                acc_ref[...] += part

            @pl.when(kk == nk - 1)
            def _():
                finish(acc_ref[...])

    in_specs = [a_spec, b_spec]
    args = [a, b]
    if has_res:
        in_specs.append(pl.BlockSpec((bm, bn), lambda i, j, kk: (i, j)))
        args.append(residual)
    return pl.pallas_call(
        body, name=name, grid=(m // bm, n // bn, nk), in_specs=in_specs,
        out_specs=pl.BlockSpec((bm, bn), lambda i, j, kk: (i, j)),
        out_shape=jax.ShapeDtypeStruct((m, n), out_dtype),
        scratch_shapes=[pltpu.VMEM((bm, bn), F32)] if nk > 1 else [],
        compiler_params=_cparams("parallel", "parallel", "arbitrary"),
    )(*args)


def _peer(mask):
    x, y, c = lax.axis_index("x"), lax.axis_index("y"), lax.axis_index("c")
    mx, my, mc = (mask >> 2) & 1, (mask >> 1) & 1, mask & 1
    px, py, pc = (1 - x if mx else x), (1 - y if my else y), (1 - c if mc else c)
    return (px, py, pc), 4 * px + 2 * py + pc


def _my_index():
    return 4 * lax.axis_index("x") + 2 * lax.axis_index("y") + lax.axis_index("c")


def _all_gather(pack):
    rows = pack.shape[0]

    def body(src_ref, out_ref, send_sems, recv_sems, local_sem):
        me = _my_index()
        mine = pltpu.make_async_copy(src_ref, out_ref.at[me], local_sem)
        mine.start()
        copies = []
        for mask in range(1, N_DEV):
            peer, _ = _peer(mask)
            cp = pltpu.make_async_remote_copy(
                src_ref=src_ref, dst_ref=out_ref.at[me], send_sem=send_sems.at[mask - 1],
                recv_sem=recv_sems.at[mask - 1], device_id=peer, device_id_type=pl.DeviceIdType.MESH)
            cp.start()
            copies.append(cp)
        for cp in copies:
            cp.wait()
        mine.wait()

    return pl.pallas_call(
        body, name="all_gather_weights",
        in_specs=[pl.BlockSpec(memory_space=pl.ANY)], out_specs=pl.BlockSpec(memory_space=pl.ANY),
        out_shape=jax.ShapeDtypeStruct((N_DEV, rows, 1024), pack.dtype),
        scratch_shapes=[pltpu.SemaphoreType.DMA((N_DEV - 1,)), pltpu.SemaphoreType.DMA((N_DEV - 1,)),
                        pltpu.SemaphoreType.DMA(())],
        compiler_params=pltpu.CompilerParams(has_side_effects=True),
    )(pack)


def _scatter_to_owners(gpack):
    rows = gpack.shape[1]

    def body(src_ref, out_ref, send_sems, recv_sems, local_sem):
        me = _my_index()
        mine = pltpu.make_async_copy(src_ref.at[me], out_ref.at[me], local_sem)
        mine.start()
        copies = []
        for mask in range(1, N_DEV):
            peer, peer_index = _peer(mask)
            cp = pltpu.make_async_remote_copy(
                src_ref=src_ref.at[peer_index], dst_ref=out_ref.at[me], send_sem=send_sems.at[mask - 1],
                recv_sem=recv_sems.at[mask - 1], device_id=peer, device_id_type=pl.DeviceIdType.MESH)
            cp.start()
            copies.append(cp)
        for cp in copies:
            cp.wait()
        mine.wait()

    return pl.pallas_call(
        body, name="scatter_gradients",
        in_specs=[pl.BlockSpec(memory_space=pl.ANY)], out_specs=pl.BlockSpec(memory_space=pl.ANY),
        out_shape=jax.ShapeDtypeStruct((N_DEV, rows, 1024), gpack.dtype),
        scratch_shapes=[pltpu.SemaphoreType.DMA((N_DEV - 1,)), pltpu.SemaphoreType.DMA((N_DEV - 1,)),
                        pltpu.SemaphoreType.DMA(())],
        compiler_params=pltpu.CompilerParams(has_side_effects=True),
    )(gpack)


def _adamw(recv, w, m, v):
    rows = w.shape[0]
    c1 = 1.0 - ADAM_B1 ** ADAM_STEP
    c2 = 1.0 - ADAM_B2 ** ADAM_STEP

    def body(r_ref, w_ref, m_ref, v_ref, g_out, d_out, m_out, v_out):
        g = r_ref[0]
        for s in range(1, N_DEV):
            g = g + r_ref[s]
        m_new = ADAM_B1 * m_ref[...] + (1.0 - ADAM_B1) * g
        v_new = ADAM_B2 * v_ref[...] + (1.0 - ADAM_B2) * (g * g)
        m_hat = m_new / c1
        v_hat = v_new / c2
        g_out[...] = g
        d_out[...] = -ADAM_LR * (m_hat / (jnp.sqrt(v_hat) + ADAM_EPS) + ADAM_WD * w_ref[...])
        m_out[...] = m_new
        v_out[...] = v_new

    blk = pl.BlockSpec((RS_BLOCK, 1024), lambda i: (i, 0))
    return pl.pallas_call(
        body, name="sum_adamw", grid=(rows // RS_BLOCK,),
        in_specs=[pl.BlockSpec((N_DEV, RS_BLOCK, 1024), lambda i: (0, i, 0)), blk, blk, blk],
        out_specs=[blk] * 4, out_shape=[jax.ShapeDtypeStruct((rows, 1024), F32)] * 4,
        compiler_params=_cparams("parallel"),
    )(recv, w, m, v)


def _rms_fwd(name, x, w, bm):
    def fn(xb, wb):
        r = lax.rsqrt(jnp.mean(xb * xb, axis=-1, keepdims=True) + RMS_EPS)
        return (xb * r * wb,)
    return _rowcall(name, fn, [(x, D_MODEL, 0)], [w], [(D_MODEL, BF16)], [], bm, x.shape[0])[0]


def _rms_bwd(name, x, w, dh, dres, bm):
    def fn(xb, dhb, drb, wb):
        r = lax.rsqrt(jnp.mean(xb * xb, axis=-1, keepdims=True) + RMS_EPS)
        xh = xb * r
        dxh = dhb.astype(F32) * wb
        dx = drb + r * (dxh - xh * jnp.mean(dxh * xh, axis=-1, keepdims=True))
        return dx, jnp.sum(dhb.astype(F32) * xh, axis=0, keepdims=True)
    return _rowcall(name, fn, [(x, D_MODEL, 0), (dh, D_MODEL, 0), (dres, D_MODEL, 0)], [w],
                    [(D_MODEL, F32)], [(1, D_MODEL)], bm, x.shape[0])


def _gate_fwd(proj, wgk_pad, b_gk, bm):
    def fn(lr, wg, bg):
        z = _dot(lr, wg) + bg
        return (_log_sigmoid(z) * (1.0 / 16.0),)
    return _rowcall("gla_gate_fwd", fn, [(proj, LANES, C_LR // LANES)], [wgk_pad, b_gk],
                    [(GLA_QK_W, F32)], [], bm, proj.shape[0])[0]


def _gate_bwd(proj, wgk_pad, b_gk, dgk, bm):
    def fn(lr, dg, wg, bg):
        z = _dot(lr, wg) + bg
        dz = dg * _sigmoid(-z) * (1.0 / 16.0)
        return _dot(dz, wg, 1, 1), _dot(lr, dz, 0, 0), jnp.sum(dz, axis=0, keepdims=True)
    return _rowcall("gla_gate_bwd", fn, [(proj, LANES, C_LR // LANES), (dgk, GLA_QK_W, 0)], [wgk_pad, b_gk],
                    [(LANES, BF16)], [(LANES, GLA_QK_W), (1, GLA_QK_W)], bm, proj.shape[0])


def _group_mean(x, g, size):
    return _dot_lhs_exact(x, g, parts=2) * (1.0 / size)


def _mix_out_fwd(o_g, proj, o_s, wg_t, ws_t, grp_g, grp_s, bm):
    def fn(og, gg, os_, wg, ws, gmat, smat):
        rg = lax.rsqrt(_group_mean(og * og, gmat, GLA_DV) + RMS_EPS)
        yg = og * rg * wg * (gg * _sigmoid(gg))
        rs = lax.rsqrt(_group_mean(os_ * os_, smat, SB_DH) + RMS_EPS)
        ys = os_ * rs * ws
        return (jnp.concatenate([yg, ys], axis=1),)
    return _rowcall("mix_out_fwd", fn, [(o_g, GLA_V_W, 0), (proj, GLA_V_W, C_GG // GLA_V_W), (o_s, SB_W, 0)],
                    [wg_t, ws_t, grp_g, grp_s], [(D_MODEL, BF16)], [], bm, o_g.shape[0])[0]


def _mix_out_bwd(dcat, o_g, proj, o_s, wg_t, ws_t, grp_g, grp_s, bm):
    def fn(dyg, dys, og, gg, os_, wg, ws, gmat, smat):
        dyg = dyg.astype(F32)
        dys = dys.astype(F32)
        rg = lax.rsqrt(_group_mean(og * og, gmat, GLA_DV) + RMS_EPS)
        xh = og * rg
        sg = _sigmoid(gg)
        silu = gg * sg
        dxh = dyg * wg * silu
        dgg = dyg * xh * wg * (sg * (1.0 + gg * (1.0 - sg)))
        dwg = jnp.sum(dyg * xh * silu, axis=0, keepdims=True)
        dog = rg * (dxh - xh * _group_mean(dxh * xh, gmat, GLA_DV))
        rs = lax.rsqrt(_group_mean(os_ * os_, smat, SB_DH) + RMS_EPS)
        xs = os_ * rs
        dxs = dys * ws
        dws = jnp.sum(dys * xs, axis=0, keepdims=True)
        dos = rs * (dxs - xs * _group_mean(dxs * xs, smat, SB_DH))
        return dog, dgg, dos, dwg, dws
    return _rowcall("mix_out_bwd", fn,
                    [(dcat, GLA_V_W, 0), (dcat, SB_W, 1), (o_g, GLA_V_W, 0), (proj, GLA_V_W, C_GG // GLA_V_W),
                     (o_s, SB_W, 0)],
                    [wg_t, ws_t, grp_g, grp_s], [(GLA_V_W, F32), (GLA_V_W, BF16), (SB_W, F32)],
                    [(1, GLA_V_W), (1, SB_W)], bm, o_g.shape[0])


def _loss_kernel(y, tgt, bm):
    def fn(yb, tb):
        err = yb - tb
        return err * (1.0 / D_MODEL), jnp.sum(err * err, axis=0, keepdims=True)
    return _rowcall("loss_head", fn, [(y, D_MODEL, 0), (tgt, D_MODEL, 0)], [], [(D_MODEL, F32)], [(1, D_MODEL)],
                    bm, y.shape[0])


def _sb_tri(inclusive):
    j, s = _iota2((SB_BLK, 2 * SB_BLK), 0), _iota2((SB_BLK, 2 * SB_BLK), 1)
    keep = (j >= s) if inclusive else (j > s)
    return ((s >= SB_BLK) | keep).astype(BF16)


def _sb_mask(qt, sub):
    return _iota2((qt, SB_BLK), 1) + SB_BLK * sub < _iota2((qt, SB_BLK), 0)


def _sb_tile(q, k, c, tri_excl, mask):
    z = _dot(q, k, 1, 1)
    lb = _log_sigmoid(z)
    l1 = lb - z
    if mask is not None:
        l1 = jnp.where(mask, l1, 0.0)
    sums = _dot_lhs_exact(l1, tri_excl, parts=2)
    a = jnp.exp(lb + sums[:, :SB_BLK] + c)
    if mask is not None:
        a = jnp.where(mask, a, 0.0)
    return lb, a, sums[:, SB_BLK:]


def _sb_fwd(proj, rows):
    qt = min(SB_QT, rows)
    subs = qt // SB_BLK

    def body(q_ref, k_ref, v_ref, o_ref):
        i = pl.program_id(1)
        tri_excl = _sb_tri(False)
        heads = [slice(SB_DH * hh, SB_DH * (hh + 1)) for hh in range(2)]
        qs = [(q_ref[:, sl] * 0.125).astype(BF16) for sl in heads]

        def tiles(start, carry, mask):
            out = []
            for hh, sl in enumerate(heads):
                o, c = carry[2 * hh], carry[2 * hh + 1]
                k = k_ref[pl.ds(start, SB_BLK), sl].astype(BF16)
                v = v_ref[pl.ds(start, SB_BLK), sl].astype(BF16)
                _, a, dc = _sb_tile(qs[hh], k, c, tri_excl, mask)
                out += [o + _dot(a, v), c + dc]
            return tuple(out)

        carry = (jnp.zeros((qt, SB_DH), F32), jnp.zeros((qt, SB_BLK), F32)) * 2
        for sub in reversed(range(subs)):
            start = pl.multiple_of(i * qt + sub * SB_BLK, SB_BLK)
            carry = tiles(start, carry, _sb_mask(qt, sub))

        def step(jj, carry):
            start = pl.multiple_of((i * subs - 1 - jj) * SB_BLK, SB_BLK)
            return tiles(start, carry, None)

        carry = lax.fori_loop(0, i * subs, step, carry)
        for hh, sl in enumerate(heads):
            o_ref[:, sl] = carry[2 * hh]

    return pl.pallas_call(
        body, name="sb_attention_fwd", grid=(SB_HEADS // 2, rows // qt),
        in_specs=[pl.BlockSpec((qt, LANES), lambda h, i: (i, C_QS // LANES + h)),
                  pl.BlockSpec((rows, LANES), lambda h, i: (0, C_KS // LANES + h)),
                  pl.BlockSpec((rows, LANES), lambda h, i: (0, C_VS // LANES + h))],
        out_specs=pl.BlockSpec((qt, LANES), lambda h, i: (i, h)),
        out_shape=jax.ShapeDtypeStruct((rows, SB_W), F32),
        compiler_params=_cparams("parallel", "arbitrary"),
    )(proj, proj, proj)


def _sb_bwd(proj, o_s, do_s, rows):
    qt = min(SB_QT, rows)
    subs = qt // SB_BLK

    def body(q_ref, k_ref, v_ref, o_ref, do_ref, dq_ref, dk_ref, dv_ref):
        i = pl.program_id(1)

        @pl.when(i == 0)
        def _():
            dk_ref[...] = jnp.zeros(dk_ref.shape, F32)
            dv_ref[...] = jnp.zeros(dv_ref.shape, F32)

        tri_excl, tri_incl = _sb_tri(False), _sb_tri(True)
        heads = [slice(SB_DH * hh, SB_DH * (hh + 1)) for hh in range(2)]
        qs = [(q_ref[:, sl] * 0.125).astype(BF16) for sl in heads]
        dobs = [do_ref[:, sl].astype(BF16) for sl in heads]
        dsums = [jnp.broadcast_to(jnp.sum(dob.astype(F32) * o_ref[:, sl], axis=1, keepdims=True), (qt, SB_BLK))
                 for dob, sl in zip(dobs, heads)]

        def tiles(start, carry, mask):
            out = []
            for hh, sl in enumerate(heads):
                dq, c, cp = carry[3 * hh:3 * hh + 3]
                q, dob = qs[hh], dobs[hh]
                k = k_ref[pl.ds(start, SB_BLK), sl].astype(BF16)
                v = v_ref[pl.ds(start, SB_BLK), sl].astype(BF16)
                lb, a, dc = _sb_tile(q, k, c, tri_excl, mask)
                ab = a.astype(BF16)
                p = ab.astype(F32) * _dot(dob, v, 1, 1)
                psums = _dot_lhs_exact(p, tri_incl, parts=2)
                left = dsums[hh] - (psums[:, :SB_BLK] + cp)
                dz = p - jnp.exp(lb) * (p + left)
                if mask is not None:
                    dz = jnp.where(mask, dz, 0.0)
                dzb = dz.astype(BF16)
                dk_ref[pl.ds(start, SB_BLK), sl] += _dot(dzb, q, 0, 0)
                dv_ref[pl.ds(start, SB_BLK), sl] += _dot(ab, dob, 0, 0)
                out += [dq + _dot(dzb, k), c + dc, cp + psums[:, SB_BLK:]]
            return tuple(out)

        zero = jnp.zeros((qt, SB_BLK), F32)
        carry = (jnp.zeros((qt, SB_DH), F32), zero, zero) * 2
        for sub in reversed(range(subs)):
            start = pl.multiple_of(i * qt + sub * SB_BLK, SB_BLK)
            carry = tiles(start, carry, _sb_mask(qt, sub))

        def step(jj, carry):
            start = pl.multiple_of((i * subs - 1 - jj) * SB_BLK, SB_BLK)
            return tiles(start, carry, None)

        carry = lax.fori_loop(0, i * subs, step, carry)
        for hh, sl in enumerate(heads):
            dq_ref[:, sl] = carry[3 * hh] * 0.125

    whole = lambda base: pl.BlockSpec((rows, LANES), functools.partial(lambda h, i, b: (0, b + h), b=base))
    blk = lambda base: pl.BlockSpec((qt, LANES), functools.partial(lambda h, i, b: (i, b + h), b=base))
    return pl.pallas_call(
        body, name="sb_attention_bwd", grid=(SB_HEADS // 2, rows // qt),
        in_specs=[blk(C_QS // LANES), whole(C_KS // LANES), whole(C_VS // LANES), blk(0), blk(0)],
        out_specs=[blk(0), whole(0), whole(0)],
        out_shape=[jax.ShapeDtypeStruct((rows, SB_W), F32)] * 3,
        compiler_params=_cparams("parallel", "arbitrary"),
    )(proj, proj, proj, o_s, do_s)


def _gla_chunk_common(g_all):
    r_i, c_i = _iota2((GLA_CHUNK, GLA_CHUNK), 0), _iota2((GLA_CHUNK, GLA_CHUNK), 1)
    tri = (c_i <= r_i).astype(BF16)
    return _dot_rhs_exact(tri, g_all), r_i, c_i


def _gla_sub(qh, kh, bh, sub):
    rs, nc = GLA_SUB * sub, GLA_SUB * (sub + 1)
    ref = bh[rs:rs + 1, :]
    eq = jnp.exp(bh[rs:rs + GLA_SUB] - ref)
    ek = jnp.exp(jnp.where(_iota2((GLA_CHUNK, GLA_DK), 0) < nc, ref - bh, 0.0))
    mask = _iota2((GLA_SUB, GLA_CHUNK), 1) <= _iota2((GLA_SUB, GLA_CHUNK), 0) + rs
    return rs, eq, ek, qh[rs:rs + GLA_SUB] * eq, kh * ek, mask


def _gla_fwd(proj, gk, rows):
    n_chunks = rows // GLA_CHUNK

    def body(q_ref, k_ref, v_ref, g_ref, o_ref, sall_ref, s_scr):
        @pl.when(pl.program_id(0) == 0)
        def _():
            s_scr[...] = jnp.zeros(s_scr.shape, F32)

        g_all = g_ref[...]
        b_all, _, _ = _gla_chunk_common(g_all)
        ones = jnp.ones((GLA_CHUNK, GLA_DV), BF16)
        for h in range(GLA_HEADS):
            sl = slice(GLA_DK * h, GLA_DK * (h + 1))
            vs = slice(GLA_DV * h, GLA_DV * (h + 1))
            qh, kh, vh = q_ref[:, sl] * 0.125, k_ref[:, sl], v_ref[:, vs]
            bh, gh = b_all[:, sl], g_all[:, sl]
            s = s_scr[h]
            sall_ref[0, h] = s
            o = _dot3(qh * jnp.exp(bh), s)
            parts = []
            for sub in range(GLA_CHUNK // GLA_SUB):
                _, _, _, qs, ks, mask = _gla_sub(qh, kh, bh, sub)
                a = jnp.where(mask, _dot3(qs, ks, 1, 1), 0.0)
                parts.append(_dot3(a, vh))
            o_ref[:, vs] = o + jnp.concatenate(parts, axis=0)
            bl_col = _dot_lhs_exact(gh, ones, 0, 0)
            kd = kh * jnp.exp(bh[GLA_CHUNK - 1:GLA_CHUNK, :] - bh)
            s_scr[h] = jnp.exp(bl_col) * s + _dot3(kd, vh, 0, 0)

    c64 = lambda w, base: pl.BlockSpec((GLA_CHUNK, w), functools.partial(lambda n, b: (n, b), b=base))
    return pl.pallas_call(
        body, name="gla_fwd", grid=(n_chunks,),
        in_specs=[c64(GLA_QK_W, C_QG // GLA_QK_W), c64(GLA_QK_W, C_KG // GLA_QK_W), c64(GLA_V_W, C_VG // GLA_V_W),
                  c64(GLA_QK_W, 0)],
        out_specs=[c64(GLA_V_W, 0), pl.BlockSpec((1, GLA_HEADS, GLA_DK, GLA_DV), lambda n: (n, 0, 0, 0))],
        out_shape=[jax.ShapeDtypeStruct((rows, GLA_V_W), F32),
                   jax.ShapeDtypeStruct((n_chunks, GLA_HEADS, GLA_DK, GLA_DV), F32)],
        scratch_shapes=[pltpu.VMEM((GLA_HEADS, GLA_DK, GLA_DV), F32)],
        compiler_params=_cparams("arbitrary"),
    )(proj, proj, proj, gk)


def _gla_bwd(proj, gk, do_g, s_all, rows):
    n_chunks = rows // GLA_CHUNK

    def body(q_ref, k_ref, v_ref, g_ref, do_ref, sall_ref, dq_ref, dk_ref, dv_ref, dg_ref, ds_scr):
        @pl.when(pl.program_id(0) == 0)
        def _():
            ds_scr[...] = jnp.zeros(ds_scr.shape, F32)

        g_all = g_ref[...]
        b_all, r_i, c_i = _gla_chunk_common(g_all)
        triu = (c_i >= r_i).astype(BF16)
        ones = jnp.ones((GLA_CHUNK, GLA_DV), BF16)
        ones8 = jnp.ones((8, GLA_DV), F32)
        last_row = _iota2((GLA_CHUNK, GLA_DK), 0) == GLA_CHUNK - 1
        for h in range(GLA_HEADS):
            sl = slice(GLA_DK * h, GLA_DK * (h + 1))
            vs = slice(GLA_DV * h, GLA_DV * (h + 1))
            qh, kh, vh = q_ref[:, sl] * 0.125, k_ref[:, sl], v_ref[:, vs]
            bh, gh = b_all[:, sl], g_all[:, sl]
            doh = do_ref[:, vs]
            s, ds = sall_ref[0, h], ds_scr[h]
            eb = jnp.exp(bh)
            ekd = jnp.exp(bh[GLA_CHUNK - 1:GLA_CHUNK, :] - bh)
            ebl = jnp.exp(_dot_lhs_exact(gh, ones, 0, 0))
            qb, kd = qh * eb, kh * ekd
            dq = _dot3(doh, s, 1, 1) * eb
            dk = _dot3(vh, ds, 1, 1) * ekd
            dv = _dot3(kd, ds)
            dbl = jnp.sum(dk * kh, axis=0, keepdims=True) + _dot3(ones8, ebl * s * ds, 1, 1)[0:1, :]
            dq_parts = []
            for sub in range(GLA_CHUNK // GLA_SUB):
                rs, eq, ek, qs, ks, mask = _gla_sub(qh, kh, bh, sub)
                do_sub = doh[rs:rs + GLA_SUB]
                a = jnp.where(mask, _dot3(qs, ks, 1, 1), 0.0)
                da = jnp.where(mask, _dot3(do_sub, vh, 1, 1), 0.0)
                dq_parts.append(_dot3(da, ks) * eq)
                dk = dk + _dot3(da, qs, 0, 0) * ek
                dv = dv + _dot3(a, do_sub, 0, 0)
            dq = dq + jnp.concatenate(dq_parts, axis=0)
            db = qh * dq - kh * dk + jnp.where(last_row, dbl, 0.0)
            dq_ref[:, sl] = dq * 0.125
            dk_ref[:, sl] = dk
            dv_ref[:, vs] = dv
            dg_ref[:, sl] = _dot_rhs_exact(triu, db)
            ds_scr[h] = _dot3(qb, doh, 0, 0) + ebl * ds

    last = n_chunks - 1
    c64 = lambda w, base: pl.BlockSpec((GLA_CHUNK, w), functools.partial(lambda n, b: (last - n, b), b=base))
    return pl.pallas_call(
        body, name="gla_bwd", grid=(n_chunks,),
        in_specs=[c64(GLA_QK_W, C_QG // GLA_QK_W), c64(GLA_QK_W, C_KG // GLA_QK_W), c64(GLA_V_W, C_VG // GLA_V_W),
                  c64(GLA_QK_W, 0), c64(GLA_V_W, 0),
                  pl.BlockSpec((1, GLA_HEADS, GLA_DK, GLA_DV), lambda n: (last - n, 0, 0, 0))],
        out_specs=[c64(GLA_QK_W, 0), c64(GLA_QK_W, 0), c64(GLA_V_W, 0), c64(GLA_QK_W, 0)],
        out_shape=[jax.ShapeDtypeStruct((rows, GLA_QK_W), F32), jax.ShapeDtypeStruct((rows, GLA_QK_W), F32),
                   jax.ShapeDtypeStruct((rows, GLA_V_W), F32), jax.ShapeDtypeStruct((rows, GLA_QK_W), F32)],
        scratch_shapes=[pltpu.VMEM((GLA_HEADS, GLA_DK, GLA_DV), F32)],
        compiler_params=_cparams("arbitrary"),
    )(proj, proj, proj, gk, do_g, s_all)


def _mem_kv_fwd(mem, mem_norm_w, w_mkv, mk_norm_w):
    def body(mem_ref, mw_ref, w_ref, kw_ref, memn_ref, kpre_ref, kn_ref, v_ref):
        xb = mem_ref[...]
        r = lax.rsqrt(jnp.mean(xb * xb, axis=-1, keepdims=True) + RMS_EPS)
        mem_n = (xb * r * mw_ref[...]).astype(BF16)
        memn_ref[...] = mem_n
        kv = _dot(mem_n, w_ref[...])
        kpre_ref[...] = kv[:, :D_MODEL]
        v_ref[...] = kv[:, D_MODEL:].astype(BF16)
        for h in range(MEM_HEADS):
            sl = slice(MEM_DH * h, MEM_DH * (h + 1))
            kh = kv[:, sl]
            rk = lax.rsqrt(jnp.mean(kh * kh, axis=-1, keepdims=True) + RMS_EPS)
            kn_ref[:, sl] = (kh * rk * kw_ref[...]).astype(BF16)

    return pl.pallas_call(
        body, name="mem_kv_fwd",
        out_shape=[jax.ShapeDtypeStruct((MEM_LEN, D_MODEL), BF16), jax.ShapeDtypeStruct((MEM_LEN, D_MODEL), F32),
                   jax.ShapeDtypeStruct((MEM_LEN, D_MODEL), BF16), jax.ShapeDtypeStruct((MEM_LEN, D_MODEL), BF16)],
        compiler_params=_cparams(),
    )(mem, mem_norm_w, w_mkv, mk_norm_w)


def _mem_kv_bwd(mem, mem_norm_w, w_mkv, mk_norm_w, mem_n, kpre, dkn, dv):
    def body(mem_ref, mw_ref, w_ref, kw_ref, memn_ref, kpre_ref, dkn_ref, dv_ref, dw_ref, dkw_ref, dmw_ref):
        dkw = jnp.zeros((1, MEM_DH), F32)
        dk_parts = []
        for h in range(MEM_HEADS):
            sl = slice(MEM_DH * h, MEM_DH * (h + 1))
            kh, dkh = kpre_ref[:, sl], dkn_ref[:, sl]
            rk = lax.rsqrt(jnp.mean(kh * kh, axis=-1, keepdims=True) + RMS_EPS)
            xh = kh * rk
            dxh = dkh * kw_ref[...]
            dkw = dkw + jnp.sum(dkh * xh, axis=0, keepdims=True)
            dk_parts.append(rk * (dxh - xh * jnp.mean(dxh * xh, axis=-1, keepdims=True)))
        dkw_ref[...] = dkw
        dkv = jnp.concatenate(dk_parts + [dv_ref[...]], axis=1).astype(BF16)
        dw_ref[...] = _dot(memn_ref[...], dkv, 0, 0)
        dmem_n = _dot(dkv, w_ref[...], 1, 1)
        xb = mem_ref[...]
        r = lax.rsqrt(jnp.mean(xb * xb, axis=-1, keepdims=True) + RMS_EPS)
        dmw_ref[...] = jnp.sum(dmem_n * (xb * r), axis=0, keepdims=True)

    return pl.pallas_call(
        body, name="mem_kv_bwd",
        out_shape=[jax.ShapeDtypeStruct((D_MODEL, 2 * D_MODEL), F32), jax.ShapeDtypeStruct((1, MEM_DH), F32),
                   jax.ShapeDtypeStruct((1, D_MODEL), F32)],
        compiler_params=_cparams(),
    )(mem, mem_norm_w, w_mkv, mk_norm_w, mem_n, kpre, dkn, dv)


def _xattn_head(qh, kn_h, qw):
    rq = lax.rsqrt(jnp.mean(qh * qh, axis=-1, keepdims=True) + RMS_EPS)
    xh = qh * rq
    qn = (xh * qw).astype(BF16)
    s = _dot(qn, kn_h, 1, 1) * (1.0 / 16.0)
    e = jnp.exp(s - jnp.max(s, axis=-1, keepdims=True))
    p = e / jnp.sum(e, axis=-1, keepdims=True)
    return rq, xh, qn, p


def _xattn_fwd(qm, kn, v, mq_norm_w, bm):
    def fn(qb, knb, vb, qw):
        outs = []
        for h in range(MEM_HEADS):
            sl = slice(MEM_DH * h, MEM_DH * (h + 1))
            _, _, _, p = _xattn_head(qb[:, sl], knb[:, sl], qw)
            outs.append(_dot(p, vb[:, sl]))
        return (jnp.concatenate(outs, axis=1),)
    return _rowcall("xattn_fwd", fn, [(qm, D_MODEL, 0)], [kn, v, mq_norm_w], [(D_MODEL, BF16)], [], bm,
                    qm.shape[0])[0]


def _xattn_bwd(qm, kn, v, mq_norm_w, do, bm):
    def fn(qb, dob, knb, vb, qw):
        dq_parts, dkn_parts, dv_parts = [], [], []
        dqw = jnp.zeros((1, MEM_DH), F32)
        for h in range(MEM_HEADS):
            sl = slice(MEM_DH * h, MEM_DH * (h + 1))
            rq, xh, qn, p = _xattn_head(qb[:, sl], knb[:, sl], qw)
            doh = dob[:, sl].astype(BF16)
            dp = _dot(doh, vb[:, sl], 1, 1)
            ds = (p * (dp - jnp.sum(dp * p, axis=-1, keepdims=True)) * (1.0 / 16.0)).astype(BF16)
            dqn = _dot(ds, knb[:, sl])
            dkn_parts.append(_dot(ds, qn, 0, 0))
            dv_parts.append(_dot(p, doh, 0, 0))
            dqw = dqw + jnp.sum(dqn * xh, axis=0, keepdims=True)
            dxh = dqn * qw
            dq_parts.append(rq * (dxh - xh * jnp.mean(dxh * xh, axis=-1, keepdims=True)))
        return (jnp.concatenate(dq_parts, axis=1), jnp.concatenate(dkn_parts, axis=1),
                jnp.concatenate(dv_parts, axis=1), dqw)
    return _rowcall("xattn_bwd", fn, [(qm, D_MODEL, 0), (do, D_MODEL, 0)], [kn, v, mq_norm_w],
                    [(D_MODEL, BF16)], [(MEM_LEN, D_MODEL), (MEM_LEN, D_MODEL), (1, MEM_DH)], bm, qm.shape[0])


FF_BN = 1408
FF_NB = D_FF // FF_BN


def _ffn_up(h3, w_gate_up, rows, bm):
    def body(h_ref, wg_ref, wu_ref, gate_ref, up_ref, act_ref):
        hb = h_ref[...]
        gate = _dot(hb, wg_ref[...])
        up = _dot(hb, wu_ref[...])
        gate_ref[...] = gate
        up_ref[...] = up
        act_ref[...] = (gate * _sigmoid(gate) * up).astype(BF16)

    out_blk = pl.BlockSpec((bm, FF_BN), lambda i, j: (i, j))
    return pl.pallas_call(
        body, name="ffn_up", grid=(rows // bm, FF_NB),
        in_specs=[pl.BlockSpec((bm, D_MODEL), lambda i, j: (i, 0)),
                  pl.BlockSpec((D_MODEL, FF_BN), lambda i, j: (0, j)),
                  pl.BlockSpec((D_MODEL, FF_BN), lambda i, j: (0, FF_NB + j))],
        out_specs=[out_blk, out_blk, out_blk],
        out_shape=[jax.ShapeDtypeStruct((rows, D_FF), F32), jax.ShapeDtypeStruct((rows, D_FF), F32),
                   jax.ShapeDtypeStruct((rows, D_FF), BF16)],
        compiler_params=_cparams("parallel", "arbitrary"),
    )(h3, w_gate_up, w_gate_up)


def _ffn_act_bwd(dy, w_down, gate, up, rows, bm):
    def body(dy_ref, wd_ref, gate_ref, up_ref, o_ref):
        dact = _dot(dy_ref[...], wd_ref[...], 1, 1)
        g, u = gate_ref[...], up_ref[...]
        sg = _sigmoid(g)
        o_ref[0] = (dact * u * (sg * (1.0 + g * (1.0 - sg)))).astype(BF16)
        o_ref[1] = (dact * (g * sg)).astype(BF16)

    blk = pl.BlockSpec((bm, FF_BN), lambda i, j: (i, j))
    return pl.pallas_call(
        body, name="ffn_act_bwd", grid=(rows // bm, FF_NB),
        in_specs=[pl.BlockSpec((bm, D_MODEL), lambda i, j: (i, 0)),
                  pl.BlockSpec((FF_BN, D_MODEL), lambda i, j: (j, 0)), blk, blk],
        out_specs=pl.BlockSpec((2, bm, FF_BN), lambda i, j: (0, i, j)),
        out_shape=jax.ShapeDtypeStruct((2, rows, D_FF), BF16),
        compiler_params=_cparams("parallel", "arbitrary"),
    )(dy, w_down, gate, up)


def _pack_rows(a):
    flat = a.reshape(-1)
    pad = (-flat.shape[0]) % 1024
    if pad:
        flat = jnp.concatenate([flat, jnp.zeros((pad,), flat.dtype)])
    return flat.reshape(-1, 1024)


def _pack_shards(shards, extra_rows, total_rows, dtype):
    parts = [_pack_rows(shards[n].astype(dtype)) for n in SHARDED] + [r.astype(dtype) for r in extra_rows]
    used = sum(p.shape[0] for p in parts)
    parts.append(jnp.zeros((total_rows - used, 1024), dtype))
    return jnp.concatenate(parts, axis=0)


def _shard_offsets():
    offs, o = {}, 0
    for n in SHARDED:
        offs[n] = o
        o += SHARD_ROWS[n]
    return offs, o


def _unpack_shard(pack, name):
    offs, _ = _shard_offsets()
    r, c = SHARD_SHAPE[name]
    seg = pack[offs[name]:offs[name] + SHARD_ROWS[name]].reshape(-1)[:r * c]
    return seg.reshape(1, r, c)


def _gathered_weight(gathered, name):
    offs, _ = _shard_offsets()
    r, c = SHARD_SHAPE[name]
    seg = gathered[:, offs[name]:offs[name] + SHARD_ROWS[name]].reshape(N_DEV, -1)[:, :r * c].reshape(N_DEV, r, c)
    if name in ("w_out", "w_mq", "w_mo", "w_down"):
        return seg.reshape(N_DEV * r, c)
    return seg.transpose(1, 0, 2).reshape(r, N_DEV * c)


def _split_for_owners(name, full):
    r, c = SHARD_SHAPE[name]
    if name in ("w_out", "w_mq", "w_mo", "w_down"):
        return full.reshape(N_DEV, r, c)
    return full.reshape(r, N_DEV, c).transpose(1, 0, 2)


def _repl_row(a):
    flat = a.reshape(-1)
    return jnp.concatenate([flat, jnp.zeros((1024 - flat.shape[0],), flat.dtype)]).reshape(1, 1024)


def _local_step(x, mem, tgt, wf, rp):
    rows = x.shape[0]
    bm = min(512, rows)
    bmx = min(256, rows)
    w_cat, wgk_pad = wf["w_cat"], wf["wgk_pad"]
    wg_t = jnp.tile(rp["gla_norm_w"], (1, GLA_HEADS))
    ws_t = jnp.tile(rp["sb_norm_w"], (1, SB_HEADS))
    lane = jnp.arange(GLA_V_W)
    grp_g = (lane[:, None] // GLA_DV == lane[None, :] // GLA_DV).astype(BF16)
    grp_s = (lane[:, None] // SB_DH == lane[None, :] // SB_DH).astype(BF16)

    h1 = _rms_fwd("mix_norm_fwd", x, rp["mix_norm_w"], bm)
    proj = _matmul("in_proj", h1, w_cat, "nn", rows, PROJ_W, D_MODEL, F32, bm, 640, D_MODEL)
    gk = _gate_fwd(proj, wgk_pad, rp["b_gk"], bm)
    o_g, s_all = _gla_fwd(proj, gk, rows)
    o_s = _sb_fwd(proj, rows)
    cat = _mix_out_fwd(o_g, proj, o_s, wg_t, ws_t, grp_g, grp_s, bm)
    x1 = _matmul("out_proj", cat, wf["w_out"], "nn", rows, D_MODEL, D_MODEL, F32, bm, 512, D_MODEL, residual=x)
    h2 = _rms_fwd("xattn_norm_fwd", x1, rp["xattn_norm_w"], bm)
    qm = _matmul("mq_proj", h2, wf["w_mq"], "nn", rows, D_MODEL, D_MODEL, F32, bm, 512, D_MODEL)
    mem_n, kpre, kn, v_m = _mem_kv_fwd(mem, rp["mem_norm_w"], wf["w_mkv"], rp["mk_norm_w"])
    o_m = _xattn_fwd(qm, kn, v_m, rp["mq_norm_w"], bmx)
    x2 = _matmul("mo_proj", o_m, wf["w_mo"], "nn", rows, D_MODEL, D_MODEL, F32, bm, 512, D_MODEL, residual=x1)
    h3 = _rms_fwd("ffn_norm_fwd", x2, rp["ffn_norm_w"], bm)
    gate, up, act = _ffn_up(h3, wf["w_gate_up"], rows, bm)
    y = _matmul("ffn_down", act, wf["w_down"], "nn", rows, D_MODEL, D_FF, F32, bm, 512, D_FF // 2, residual=x2)
    dy, sq = _loss_kernel(y, tgt, bm)

    g = {}
    dgu = _ffn_act_bwd(dy, wf["w_down"], gate, up, rows, bm)
    g["w_down"] = _matmul("grad_w_down", act, dy, "tn", D_FF, D_MODEL, rows, F32, FF_BN, 512, 512)
    nkb = FF_NB
    bkr = min(512, rows)
    dh3 = _matmul("ffn_up_bwd", dgu, wf["w_gate_up"], "nt", rows, D_MODEL, 2 * D_FF, F32, bm, 512, FF_BN,
                  a_spec=pl.BlockSpec((None, bm, FF_BN), lambda i, j, kk: (kk // nkb, i, kk % nkb)))
    g["w_gate_up"] = _matmul(
        "grad_w_gate_up", h3, dgu, "tn", D_MODEL, 2 * D_FF, rows, F32, 512, FF_BN, 512,
        b_spec=pl.BlockSpec((None, bkr, FF_BN), lambda i, j, kk: (j // nkb, kk, j % nkb)))
    dx2, g["ffn_norm_w"] = _rms_bwd("ffn_norm_bwd", x2, rp["ffn_norm_w"], dh3, dy, bm)

    do_m = _matmul("mo_proj_bwd", dx2, wf["w_mo"], "nt", rows, D_MODEL, D_MODEL, BF16, bm, 512, D_MODEL)
    g["w_mo"] = _matmul("grad_w_mo", o_m, dx2, "tn", D_MODEL, D_MODEL, rows, F32, 512, 512, 512)
    dqm, dkn, dv_m, g["mq_norm_w"] = _xattn_bwd(qm, kn, v_m, rp["mq_norm_w"], do_m, bmx)
    g["w_mkv"], g["mk_norm_w"], g["mem_norm_w"] = _mem_kv_bwd(
        mem, rp["mem_norm_w"], wf["w_mkv"], rp["mk_norm_w"], mem_n, kpre, dkn, dv_m)
    dh2 = _matmul("mq_proj_bwd", dqm, wf["w_mq"], "nt", rows, D_MODEL, D_MODEL, F32, bm, 512, D_MODEL)
    g["w_mq"] = _matmul("grad_w_mq", h2, dqm, "tn", D_MODEL, D_MODEL, rows, F32, 512, 512, 512)
    dx1, g["xattn_norm_w"] = _rms_bwd("xattn_norm_bwd", x1, rp["xattn_norm_w"], dh2, dx2, bm)

    dcat = _matmul("out_proj_bwd", dx1, wf["w_out"], "nt", rows, D_MODEL, D_MODEL, F32, bm, 512, D_MODEL)
    g["w_out"] = _matmul("grad_w_out", cat, dx1, "tn", D_MODEL, D_MODEL, rows, F32, 512, 512, 512)
    do_g, dgg, do_s, dwg, dws = _mix_out_bwd(dcat, o_g, proj, o_s, wg_t, ws_t, grp_g, grp_s, bm)
    g["gla_norm_w"] = dwg.reshape(GLA_HEADS, GLA_DV).sum(axis=0, keepdims=True)
    g["sb_norm_w"] = dws.reshape(SB_HEADS, SB_DH).sum(axis=0, keepdims=True)
    dq_s, dk_s, dv_s = _sb_bwd(proj, o_s, do_s, rows)
    dq_g, dk_g, dv_g, dgk = _gla_bwd(proj, gk, do_g, s_all, rows)
    dlr, dwgk, g["b_gk"] = _gate_bwd(proj, wgk_pad, rp["b_gk"], dgk, bm)
    g["w_gk_up"] = dwgk[:GATE_RANK]
    dproj = jnp.concatenate([dq_g.astype(BF16), dk_g.astype(BF16), dv_g.astype(BF16), dgg, dq_s.astype(BF16),
                             dk_s.astype(BF16), dv_s.astype(BF16), dlr], axis=1)
    dh1 = _matmul("in_proj_bwd", dproj, w_cat, "nt", rows, D_MODEL, PROJ_W, F32, bm, 512, 640)
    dw_cat = _matmul("grad_w_in", h1, dproj, "tn", D_MODEL, PROJ_W, rows, F32, 512, 640, 512)
    g["w_in"] = jnp.concatenate([dw_cat[:, :C_QS], dw_cat[:, C_LR:C_LR + GATE_RANK], dw_cat[:, C_QS:C_LR]], axis=1)
    dx, g["mix_norm_w"] = _rms_bwd("mix_norm_bwd", x, rp["mix_norm_w"], dh1, dx1, bm)
    return sq, dx, g


def _full_weights(gathered):
    w_in = _gathered_weight(gathered, "w_in")
    lr_end = C_QS + GATE_RANK
    w_cat = jnp.concatenate([w_in[:, :C_QS], w_in[:, lr_end:], w_in[:, C_QS:lr_end],
                             jnp.zeros((D_MODEL, PROJ_W - D_IN), BF16)], axis=1)
    wgk = _gathered_weight(gathered, "w_gk_up")
    wf = {n: _gathered_weight(gathered, n) for n in ("w_out", "w_mq", "w_mkv", "w_mo", "w_gate_up", "w_down")}
    wf["w_cat"] = w_cat
    wf["wgk_pad"] = jnp.concatenate([wgk, jnp.zeros((LANES - GATE_RANK, GLA_QK_W), BF16)], axis=0)
    return wf


def kernel(x, mem, mix_norm_w, w_in, w_gk_up, b_gk, gla_norm_w, sb_norm_w, w_out, xattn_norm_w, mem_norm_w, w_mq, w_mkv, mq_norm_w, mk_norm_w, w_mo, ffn_norm_w, w_gate_up, w_down, loss_target, m_mix_norm_w, m_w_in, m_w_gk_up, m_b_gk, m_gla_norm_w, m_sb_norm_w, m_w_out, m_xattn_norm_w, m_mem_norm_w, m_w_mq, m_w_mkv, m_mq_norm_w, m_mk_norm_w, m_w_mo, m_ffn_norm_w, m_w_gate_up, m_w_down, v_mix_norm_w, v_w_in, v_w_gk_up, v_b_gk, v_gla_norm_w, v_sb_norm_w, v_w_out, v_xattn_norm_w, v_mem_norm_w, v_w_mq, v_w_mkv, v_mq_norm_w, v_mk_norm_w, v_w_mo, v_ffn_norm_w, v_w_gate_up, v_w_down):
    given = dict(locals())
    w = {n: given[n][0] for n in WEIGHTS}
    m = {n: given["m_" + n][0] for n in WEIGHTS}
    v = {n: given["v_" + n][0] for n in WEIGHTS}

    gathered = _all_gather(_pack_shards(w, [], AG_ROWS, BF16))
    wf = _full_weights(gathered)
    rp = {n: w[n].reshape(1, -1) for n in REPL}
    sq, dx, g = _local_step(x[0], mem[0], loss_target[0], wf, rp)

    loss_row = _repl_row(jnp.sum(sq).reshape(1) * (0.5 / D_MODEL))
    repl_rows = [_repl_row(g[n]) for n in REPL] + [loss_row]
    per_owner = {n: _split_for_owners(n, g[n]) for n in SHARDED}
    gpack = jnp.stack([
        _pack_shards({n: per_owner[n][d] for n in SHARDED}, repl_rows, RS_ROWS, F32) for d in range(N_DEV)])
    recv = _scatter_to_owners(gpack)

    zero_row = jnp.zeros((1, 1024), F32)
    packs = [_pack_shards(t, [_repl_row(t[n]) for n in REPL] + [zero_row], RS_ROWS, F32) for t in (w, m, v)]
    out_packs = _adamw(recv, *packs)

    _, n_shard_rows = _shard_offsets()

    def unpack(pack, name):
        if name in SHARDED:
            return _unpack_shard(pack, name)
        row = pack[n_shard_rows + REPL.index(name)]
        return row[:w[name].shape[-1]].reshape(1, -1)

    loss = out_packs[0][n_shard_rows + len(REPL), 0]
    outs = [loss, dx[None]]
    for pack in out_packs:
        outs += [unpack(pack, n) for n in WEIGHTS]
    return tuple(outs)
```

```python
import functools
import math

import jax
import jax.numpy as jnp
from jax import lax
from jax.experimental import pallas as pl
from jax.experimental.pallas import tpu as pltpu

F32 = jnp.float32
BF16 = jnp.bfloat16

N_DEV = 8
D_MODEL = 1024
GLA_HEADS = 4
GLA_DK = 64
GLA_DV = 128
GLA_CHUNK = 64
GLA_SUB = 16
GLA_QK_W = GLA_HEADS * GLA_DK
GLA_V_W = GLA_HEADS * GLA_DV
GATE_RANK = 16
SB_HEADS = 8
SB_DH = 64
SB_W = SB_HEADS * SB_DH
SB_BLK = 128
SB_QT = 1024
MEM_LEN = 256
MEM_HEADS = 4
MEM_DH = 256
D_FF = 2816
D_IN = 3088
RMS_EPS = 1e-6
LANES = 128

PROJ_W = 3200
C_QG, C_KG, C_VG, C_GG, C_QS, C_KS, C_VS, C_LR = 0, 256, 512, 1024, 1536, 2048, 2560, 3072

ADAM_LR, ADAM_B1, ADAM_B2, ADAM_EPS, ADAM_WD, ADAM_STEP = 0.001, 0.9, 0.999, 1e-08, 0.01, 10

SHARDED = ("w_in", "w_out", "w_mq", "w_mkv", "w_mo", "w_gate_up", "w_down", "w_gk_up")
SHARD_SHAPE = {"w_in": (1024, 386), "w_out": (128, 1024), "w_mq": (128, 1024), "w_mkv": (1024, 256),
               "w_mo": (128, 1024), "w_gate_up": (1024, 704), "w_down": (352, 1024), "w_gk_up": (16, 32)}
SHARD_ROWS = {n: -(-(s[0] * s[1]) // 1024) for n, s in SHARD_SHAPE.items()}
REPL = ("mix_norm_w", "b_gk", "gla_norm_w", "sb_norm_w", "xattn_norm_w", "mem_norm_w", "mq_norm_w",
        "mk_norm_w", "ffn_norm_w")
WEIGHTS = ("mix_norm_w", "w_in", "w_gk_up", "b_gk", "gla_norm_w", "sb_norm_w", "w_out", "xattn_norm_w",
           "mem_norm_w", "w_mq", "w_mkv", "mq_norm_w", "mk_norm_w", "w_mo", "ffn_norm_w", "w_gate_up", "w_down")
AG_ROWS = 2096
RS_BLOCK = 192
RS_ROWS = 11 * RS_BLOCK
VMEM_LIMIT = 56 * 1024 * 1024
MM_TILE = 1024


def _cparams(*sem):
    return pltpu.CompilerParams(dimension_semantics=sem if sem else None, vmem_limit_bytes=VMEM_LIMIT)


def _dot(a, b, ca=1, cb=0):
    return lax.dot_general(a.astype(BF16), b.astype(BF16), (((ca,), (cb,)), ((), ())),
                           preferred_element_type=F32)


def _split(x, parts):
    out = []
    for _ in range(parts - 1):
        hi = x.astype(BF16)
        out.append(hi)
        x = x - hi.astype(F32)
    out.append(x.astype(BF16))
    return out


def _dot_lhs_exact(x, m, ca=1, cb=0, parts=3):
    acc = None
    for p in _split(x, parts):
        t = _dot(p, m, ca, cb)
        acc = t if acc is None else acc + t
    return acc


def _dot_rhs_exact(m, x, ca=1, cb=0, parts=3):
    acc = None
    for p in _split(x, parts):
        t = _dot(m, p, ca, cb)
        acc = t if acc is None else acc + t
    return acc


def _dot3(a, b, ca=1, cb=0):
    a_hi, a_lo = _split(a, 2)
    b_hi, b_lo = _split(b, 2)
    return _dot(a_hi, b_hi, ca, cb) + (_dot(a_hi, b_lo, ca, cb) + _dot(a_lo, b_hi, ca, cb))


def _log_sigmoid(z):
    return jnp.minimum(z, 0.0) - jnp.log(1.0 + jnp.exp(-jnp.abs(z)))


def _sigmoid(z):
    e = jnp.exp(-jnp.abs(z))
    return jnp.where(z >= 0, 1.0, e) / (1.0 + e)


def _iota2(shape, dim):
    return lax.broadcasted_iota(jnp.int32, shape, dim)


def _rowcall(name, fn, row_ins, full_ins, row_outs, acc_outs, bm, rows):
    n_in = len(row_ins) + len(full_ins)
    n_row = len(row_outs)

    def body(*refs):
        ins, outs = refs[:n_in], refs[n_in:]
        res = fn(*[r[...] for r in ins])
        for r, v in zip(outs[:n_row], res[:n_row]):
            r[...] = v.astype(r.dtype)
        first = pl.program_id(0) == 0
        for r, v in zip(outs[n_row:], res[n_row:]):
            def init(r=r):
                r[...] = jnp.zeros(r.shape, r.dtype)
            pl.when(first)(init)
            r[...] += v

    in_specs = [pl.BlockSpec((bm, w), functools.partial(lambda i, c: (i, c), c=c)) for _, w, c in row_ins]
    in_specs += [pl.BlockSpec(a.shape, lambda i: (0, 0)) for a in full_ins]
    out_specs = [pl.BlockSpec((bm, w), lambda i: (i, 0)) for w, _ in row_outs]
    out_specs += [pl.BlockSpec(s, lambda i: (0, 0)) for s in acc_outs]
    out_shape = [jax.ShapeDtypeStruct((rows, w), dt) for w, dt in row_outs]
    out_shape += [jax.ShapeDtypeStruct(s, F32) for s in acc_outs]
    return pl.pallas_call(
        body, name=name, grid=(rows // bm,), in_specs=in_specs, out_specs=out_specs, out_shape=out_shape,
        compiler_params=_cparams("arbitrary"),
    )(*[a for a, _, _ in row_ins], *full_ins)


def _matmul(name, a, b, mode, m, n, k, out_dtype, bm, bn, bk, residual=None, a_spec=None, b_spec=None):
    bm, bn, bk = min(bm, m), min(bn, n), min(bk, k)
    nk = k // bk
    ca, cb = {"nn": (1, 0), "nt": (1, 1), "tn": (0, 0)}[mode]
    if a_spec is None:
        a_spec = (pl.BlockSpec((bk, bm), lambda i, j, kk: (kk, i)) if mode == "tn"
                  else pl.BlockSpec((bm, bk), lambda i, j, kk: (i, kk)))
    if b_spec is None:
        b_spec = (pl.BlockSpec((bn, bk), lambda i, j, kk: (j, kk)) if mode == "nt"
                  else pl.BlockSpec((bk, bn), lambda i, j, kk: (kk, j)))
    has_res = residual is not None

    def body(*refs):
        a_ref, b_ref = refs[0], refs[1]
        res_ref = refs[2] if has_res else None
        o_ref = refs[2 + has_res]
        part = _dot(a_ref[...], b_ref[...], ca, cb)

        def finish(total):
            if has_res:
                total = total + res_ref[...]
            o_ref[...] = total.astype(o_ref.dtype)

        if nk == 1:
            finish(part)
        else:
            acc_ref = refs[3 + has_res]
            kk = pl.program_id(2)

            @pl.when(kk == 0)
            def _():
                acc_ref[...] = part

            @pl.when(kk > 0)
            def _():
                acc_ref[...] += part

            @pl.when(kk == nk - 1)
            def _():
                finish(acc_ref[...])

    in_specs = [a_spec, b_spec]
    args = [a, b]
    if has_res:
        in_specs.append(pl.BlockSpec((bm, bn), lambda i, j, kk: (i, j)))
        args.append(residual)
    return pl.pallas_call(
        body, name=name, grid=(m // bm, n // bn, nk), in_specs=in_specs,
        out_specs=pl.BlockSpec((bm, bn), lambda i, j, kk: (i, j)),
        out_shape=jax.ShapeDtypeStruct((m, n), out_dtype),
        scratch_shapes=[pltpu.VMEM((bm, bn), F32)] if nk > 1 else [],
        compiler_params=_cparams("parallel", "parallel", "arbitrary"),
    )(*args)


def _peer(mask):
    x, y, c = lax.axis_index("x"), lax.axis_index("y"), lax.axis_index("c")
    mx, my, mc = (mask >> 2) & 1, (mask >> 1) & 1, mask & 1
    px, py, pc = (1 - x if mx else x), (1 - y if my else y), (1 - c if mc else c)
    return (px, py, pc), 4 * px + 2 * py + pc


def _my_index():
    return 4 * lax.axis_index("x") + 2 * lax.axis_index("y") + lax.axis_index("c")


def _all_gather(pack):
    rows = pack.shape[0]

    def body(src_ref, out_ref, send_sems, recv_sems, local_sem):
        me = _my_index()
        mine = pltpu.make_async_copy(src_ref, out_ref.at[me], local_sem)
        mine.start()
        copies = []
        for mask in range(1, N_DEV):
            peer, _ = _peer(mask)
            cp = pltpu.make_async_remote_copy(
                src_ref=src_ref, dst_ref=out_ref.at[me], send_sem=send_sems.at[mask - 1],
                recv_sem=recv_sems.at[mask - 1], device_id=peer, device_id_type=pl.DeviceIdType.MESH)
            cp.start()
            copies.append(cp)
        for cp in copies:
            cp.wait()
        mine.wait()

    return pl.pallas_call(
        body, name="all_gather_weights",
        in_specs=[pl.BlockSpec(memory_space=pl.ANY)], out_specs=pl.BlockSpec(memory_space=pl.ANY),
        out_shape=jax.ShapeDtypeStruct((N_DEV, rows, 1024), pack.dtype),
        scratch_shapes=[pltpu.SemaphoreType.DMA((N_DEV - 1,)), pltpu.SemaphoreType.DMA((N_DEV - 1,)),
                        pltpu.SemaphoreType.DMA(())],
        compiler_params=pltpu.CompilerParams(has_side_effects=True),
    )(pack)


def _scatter_to_owners(gpack):
    rows = gpack.shape[1]

    def body(src_ref, out_ref, send_sems, recv_sems, local_sem):
        me = _my_index()
        mine = pltpu.make_async_copy(src_ref.at[me], out_ref.at[me], local_sem)
        mine.start()
        copies = []
        for mask in range(1, N_DEV):
            peer, peer_index = _peer(mask)
            cp = pltpu.make_async_remote_copy(
                src_ref=src_ref.at[peer_index], dst_ref=out_ref.at[me], send_sem=send_sems.at[mask - 1],
                recv_sem=recv_sems.at[mask - 1], device_id=peer, device_id_type=pl.DeviceIdType.MESH)
            cp.start()
            copies.append(cp)
        for cp in copies:
            cp.wait()
        mine.wait()

    return pl.pallas_call(
        body, name="scatter_gradients",
        in_specs=[pl.BlockSpec(memory_space=pl.ANY)], out_specs=pl.BlockSpec(memory_space=pl.ANY),
        out_shape=jax.ShapeDtypeStruct((N_DEV, rows, 1024), gpack.dtype),
        scratch_shapes=[pltpu.SemaphoreType.DMA((N_DEV - 1,)), pltpu.SemaphoreType.DMA((N_DEV - 1,)),
                        pltpu.SemaphoreType.DMA(())],
        compiler_params=pltpu.CompilerParams(has_side_effects=True),
    )(gpack)


def _adamw(recv, w, m, v):
    rows = w.shape[0]
    c1 = 1.0 - ADAM_B1 ** ADAM_STEP
    c2 = 1.0 - ADAM_B2 ** ADAM_STEP

    def body(r_ref, w_ref, m_ref, v_ref, g_out, d_out, m_out, v_out):
        g = r_ref[0]
        for s in range(1, N_DEV):
            g = g + r_ref[s]
        m_new = ADAM_B1 * m_ref[...] + (1.0 - ADAM_B1) * g
        v_new = ADAM_B2 * v_ref[...] + (1.0 - ADAM_B2) * (g * g)
        m_hat = m_new / c1
        v_hat = v_new / c2
        g_out[...] = g
        d_out[...] = -ADAM_LR * (m_hat / (jnp.sqrt(v_hat) + ADAM_EPS) + ADAM_WD * w_ref[...])
        m_out[...] = m_new
        v_out[...] = v_new

    blk = pl.BlockSpec((RS_BLOCK, 1024), lambda i: (i, 0))
    return pl.pallas_call(
        body, name="sum_adamw", grid=(rows // RS_BLOCK,),
        in_specs=[pl.BlockSpec((N_DEV, RS_BLOCK, 1024), lambda i: (0, i, 0)), blk, blk, blk],
        out_specs=[blk] * 4, out_shape=[jax.ShapeDtypeStruct((rows, 1024), F32)] * 4,
        compiler_params=_cparams("parallel"),
    )(recv, w, m, v)


def _rms_fwd(name, x, w, bm):
    def fn(xb, wb):
        r = lax.rsqrt(jnp.mean(xb * xb, axis=-1, keepdims=True) + RMS_EPS)
        return (xb * r * wb,)
    return _rowcall(name, fn, [(x, D_MODEL, 0)], [w], [(D_MODEL, BF16)], [], bm, x.shape[0])[0]


def _rms_bwd(name, x, w, dh, dres, bm):
    def fn(xb, dhb, drb, wb):
        r = lax.rsqrt(jnp.mean(xb * xb, axis=-1, keepdims=True) + RMS_EPS)
        xh = xb * r
        dxh = dhb.astype(F32) * wb
        dx = drb + r * (dxh - xh * jnp.mean(dxh * xh, axis=-1, keepdims=True))
        return dx, jnp.sum(dhb.astype(F32) * xh, axis=0, keepdims=True)
    return _rowcall(name, fn, [(x, D_MODEL, 0), (dh, D_MODEL, 0), (dres, D_MODEL, 0)], [w],
                    [(D_MODEL, F32)], [(1, D_MODEL)], bm, x.shape[0])


def _gate_fwd(proj, wgk_pad, b_gk, bm):
    def fn(lr, wg, bg):
        z = _dot(lr, wg) + bg
        return (_log_sigmoid(z) * (1.0 / 16.0),)
    return _rowcall("gla_gate_fwd", fn, [(proj, LANES, C_LR // LANES)], [wgk_pad, b_gk],
                    [(GLA_QK_W, F32)], [], bm, proj.shape[0])[0]


def _gate_bwd(proj, wgk_pad, b_gk, dgk, bm):
    def fn(lr, dg, wg, bg):
        z = _dot(lr, wg) + bg
        dz = dg * _sigmoid(-z) * (1.0 / 16.0)
        return _dot(dz, wg, 1, 1), _dot(lr, dz, 0, 0), jnp.sum(dz, axis=0, keepdims=True)
    return _rowcall("gla_gate_bwd", fn, [(proj, LANES, C_LR // LANES), (dgk, GLA_QK_W, 0)], [wgk_pad, b_gk],
                    [(LANES, BF16)], [(LANES, GLA_QK_W), (1, GLA_QK_W)], bm, proj.shape[0])


def _group_mean(x, g, size):
    return _dot_lhs_exact(x, g, parts=2) * (1.0 / size)


def _mix_out_fwd(o_g, proj, o_s, wg_t, ws_t, grp_g, grp_s, bm):
    def fn(og, gg, os_, wg, ws, gmat, smat):
        rg = lax.rsqrt(_group_mean(og * og, gmat, GLA_DV) + RMS_EPS)
        yg = og * rg * wg * (gg * _sigmoid(gg))
        rs = lax.rsqrt(_group_mean(os_ * os_, smat, SB_DH) + RMS_EPS)
        ys = os_ * rs * ws
        return (jnp.concatenate([yg, ys], axis=1),)
    return _rowcall("mix_out_fwd", fn, [(o_g, GLA_V_W, 0), (proj, GLA_V_W, C_GG // GLA_V_W), (o_s, SB_W, 0)],
                    [wg_t, ws_t, grp_g, grp_s], [(D_MODEL, BF16)], [], bm, o_g.shape[0])[0]


def _mix_out_bwd(dcat, o_g, proj, o_s, wg_t, ws_t, grp_g, grp_s, bm):
    def fn(dyg, dys, og, gg, os_, wg, ws, gmat, smat):
        dyg = dyg.astype(F32)
        dys = dys.astype(F32)
        rg = lax.rsqrt(_group_mean(og * og, gmat, GLA_DV) + RMS_EPS)
        xh = og * rg
        sg = _sigmoid(gg)
        silu = gg * sg
        dxh = dyg * wg * silu
        dgg = dyg * xh * wg * (sg * (1.0 + gg * (1.0 - sg)))
        dwg = jnp.sum(dyg * xh * silu, axis=0, keepdims=True)
        dog = rg * (dxh - xh * _group_mean(dxh * xh, gmat, GLA_DV))
        rs = lax.rsqrt(_group_mean(os_ * os_, smat, SB_DH) + RMS_EPS)
        xs = os_ * rs
        dxs = dys * ws
        dws = jnp.sum(dys * xs, axis=0, keepdims=True)
        dos = rs * (dxs - xs * _group_mean(dxs * xs, smat, SB_DH))
        return dog, dgg, dos, dwg, dws
    return _rowcall("mix_out_bwd", fn,
                    [(dcat, GLA_V_W, 0), (dcat, SB_W, 1), (o_g, GLA_V_W, 0), (proj, GLA_V_W, C_GG // GLA_V_W),
                     (o_s, SB_W, 0)],
                    [wg_t, ws_t, grp_g, grp_s], [(GLA_V_W, F32), (GLA_V_W, BF16), (SB_W, F32)],
                    [(1, GLA_V_W), (1, SB_W)], bm, o_g.shape[0])


def _loss_kernel(y, tgt, bm):
    def fn(yb, tb):
        err = yb - tb
        return err * (1.0 / D_MODEL), jnp.sum(err * err, axis=0, keepdims=True)
    return _rowcall("loss_head", fn, [(y, D_MODEL, 0), (tgt, D_MODEL, 0)], [], [(D_MODEL, F32)], [(1, D_MODEL)],
                    bm, y.shape[0])


def _sb_tri(inclusive):
    j, s = _iota2((2 * SB_BLK, 2 * SB_BLK), 0), _iota2((2 * SB_BLK, 2 * SB_BLK), 1)
    j = jnp.where(j >= SB_BLK, j - SB_BLK, j)
    keep = (j >= s) if inclusive else (j > s)
    return ((s >= SB_BLK) | keep).astype(BF16)


def _dot_hilo(x, m2):
    hi, lo = _split(x, 2)
    return _dot(jnp.concatenate([hi, lo], axis=1), m2)


def _sb_mask(n):
    return _iota2((n, SB_BLK), 1) < _iota2((n, SB_BLK), 0)


def _add_rows(full, part, row0):
    if row0 == 0:
        return full + part
    return jnp.concatenate([full[:row0], full[row0:] + part], axis=0)


def _row_blocks(n):
    return [slice(r, r + SB_BLK) for r in range(0, n, SB_BLK)]


def _hilo(x):
    hi, lo = _split(x, 2)
    return jnp.concatenate([hi, lo], axis=1)


def _sb_tile(q, k, c, tri_excl, diag):
    blocks = _row_blocks(q.shape[0])
    strict = _sb_mask(SB_BLK) if diag else None
    z = _dot(q, k, 1, 1)
    lbs, pieces = [], []
    for r, rs in enumerate(blocks):
        lb = _log_sigmoid(z[rs])
        l1 = lb - z[rs]
        if diag and r == 0:
            l1 = jnp.where(strict, l1, 0.0)
        lbs.append(lb)
        pieces.append(_hilo(l1))
    sums = _dot(jnp.concatenate(pieces, axis=0), tri_excl)
    a = []
    for r, rs in enumerate(blocks):
        ar = jnp.exp(lbs[r] + sums[rs, :SB_BLK] + c[rs])
        if diag and r == 0:
            ar = jnp.where(strict, ar, 0.0)
        a.append(ar.astype(BF16))
    return lbs, a, sums[:, SB_BLK:]


def _sb_fwd(proj, rows):
    qt = min(SB_QT, rows)
    subs = qt // SB_BLK

    def body(q_ref, k_ref, v_ref, o_ref):
        i = pl.program_id(1)
        tri_excl = _sb_tri(False)
        heads = [slice(SB_DH * hh, SB_DH * (hh + 1)) for hh in range(2)]
        qs = [(q_ref[:, sl] * 0.125).astype(BF16) for sl in heads]

        def tiles(start, carry, diag, row0=0):
            out = []
            for hh, sl in enumerate(heads):
                o, c = carry[2 * hh], carry[2 * hh + 1]
                k = k_ref[pl.ds(start, SB_BLK), sl].astype(BF16)
                v = v_ref[pl.ds(start, SB_BLK), sl].astype(BF16)
                _, a, dc = _sb_tile(qs[hh][row0:], k, c[row0:], tri_excl, diag)
                out += [_add_rows(o, _dot(jnp.concatenate(a, axis=0), v), row0), _add_rows(c, dc, row0)]
            return tuple(out)

        carry = (jnp.zeros((qt, SB_DH), F32), jnp.zeros((qt, SB_BLK), F32)) * 2
        for sub in reversed(range(subs)):
            start = pl.multiple_of(i * qt + sub * SB_BLK, SB_BLK)
            carry = tiles(start, carry, True, sub * SB_BLK)

        def step(jj, carry):
            start = pl.multiple_of((i * subs - 1 - jj) * SB_BLK, SB_BLK)
            return tiles(start, carry, False)

        carry = lax.fori_loop(0, i * subs, step, carry)
        for hh, sl in enumerate(heads):
            o_ref[:, sl] = carry[2 * hh]

    return pl.pallas_call(
        body, name="sb_attention_fwd", grid=(SB_HEADS // 2, rows // qt),
        in_specs=[pl.BlockSpec((qt, LANES), lambda h, i: (i, C_QS // LANES + h)),
                  pl.BlockSpec((rows, LANES), lambda h, i: (0, C_KS // LANES + h)),
                  pl.BlockSpec((rows, LANES), lambda h, i: (0, C_VS // LANES + h))],
        out_specs=pl.BlockSpec((qt, LANES), lambda h, i: (i, h)),
        out_shape=jax.ShapeDtypeStruct((rows, SB_W), F32),
        compiler_params=_cparams("parallel", "arbitrary"),
    )(proj, proj, proj)


def _sb_bwd(proj, o_s, do_s, rows):
    qt = min(SB_QT, rows)
    subs = qt // SB_BLK

    def body(q_ref, k_ref, v_ref, o_ref, do_ref, dq_ref, dk_ref, dv_ref):
        i = pl.program_id(1)

        @pl.when(i == 0)
        def _():
            dk_ref[...] = jnp.zeros(dk_ref.shape, F32)
            dv_ref[...] = jnp.zeros(dv_ref.shape, F32)

        tri_excl, tri_incl = _sb_tri(False), _sb_tri(True)
        heads = [slice(SB_DH * hh, SB_DH * (hh + 1)) for hh in range(2)]
        qs = [(q_ref[:, sl] * 0.125).astype(BF16) for sl in heads]
        dobs = [do_ref[:, sl].astype(BF16) for sl in heads]
        dsums = [jnp.broadcast_to(jnp.sum(dob.astype(F32) * o_ref[:, sl], axis=1, keepdims=True), (qt, SB_BLK))
                 for dob, sl in zip(dobs, heads)]

        def tiles(start, carry, diag, row0=0):
            out = []
            strict = _sb_mask(SB_BLK) if diag else None
            for hh, sl in enumerate(heads):
                dq, c, cp = carry[3 * hh:3 * hh + 3]
                q, dob = qs[hh][row0:], dobs[hh][row0:]
                dsum, cpr = dsums[hh][row0:], cp[row0:]
                blocks = _row_blocks(q.shape[0])
                k = k_ref[pl.ds(start, SB_BLK), sl].astype(BF16)
                v = v_ref[pl.ds(start, SB_BLK), sl].astype(BF16)
                lbs, a, dc = _sb_tile(q, k, c[row0:], tri_excl, diag)
                da = _dot(dob, v, 1, 1)
                ps = [a[r].astype(F32) * da[rs] for r, rs in enumerate(blocks)]
                psums = _dot(jnp.concatenate([_hilo(p) for p in ps], axis=0), tri_incl)
                dzs = []
                for r, rs in enumerate(blocks):
                    left = dsum[rs] - (psums[rs, :SB_BLK] + cpr[rs])
                    dz = ps[r] - jnp.exp(lbs[r]) * (ps[r] + left)
                    if diag and r == 0:
                        dz = jnp.where(strict, dz, 0.0)
                    dzs.append(dz.astype(BF16))
                dzb, ab = jnp.concatenate(dzs, axis=0), jnp.concatenate(a, axis=0)
                dk_ref[pl.ds(start, SB_BLK), sl] += _dot(dzb, q, 0, 0)
                dv_ref[pl.ds(start, SB_BLK), sl] += _dot(ab, dob, 0, 0)
                out += [_add_rows(dq, _dot(dzb, k), row0), _add_rows(c, dc, row0),
                        _add_rows(cp, psums[:, SB_BLK:], row0)]
            return tuple(out)

        zero = jnp.zeros((qt, SB_BLK), F32)
        carry = (jnp.zeros((qt, SB_DH), F32), zero, zero) * 2
        for sub in reversed(range(subs)):
            start = pl.multiple_of(i * qt + sub * SB_BLK, SB_BLK)
            carry = tiles(start, carry, True, sub * SB_BLK)

        def step(jj, carry):
            start = pl.multiple_of((i * subs - 1 - jj) * SB_BLK, SB_BLK)
            return tiles(start, carry, False)

        carry = lax.fori_loop(0, i * subs, step, carry)
        for hh, sl in enumerate(heads):
            dq_ref[:, sl] = carry[3 * hh] * 0.125

    whole = lambda base: pl.BlockSpec((rows, LANES), functools.partial(lambda h, i, b: (0, b + h), b=base))
    blk = lambda base: pl.BlockSpec((qt, LANES), functools.partial(lambda h, i, b: (i, b + h), b=base))
    return pl.pallas_call(
        body, name="sb_attention_bwd", grid=(SB_HEADS // 2, rows // qt),
        in_specs=[blk(C_QS // LANES), whole(C_KS // LANES), whole(C_VS // LANES), blk(0), blk(0)],
        out_specs=[blk(0), whole(0), whole(0)],
        out_shape=[jax.ShapeDtypeStruct((rows, SB_W), F32)] * 3,
        compiler_params=_cparams("parallel", "arbitrary"),
    )(proj, proj, proj, o_s, do_s)


def _gla_chunk_common(g_all):
    r_i, c_i = _iota2((GLA_CHUNK, GLA_CHUNK), 0), _iota2((GLA_CHUNK, GLA_CHUNK), 1)
    tri = (c_i <= r_i).astype(BF16)
    return _dot_rhs_exact(tri, g_all), r_i, c_i


def _gla_sub(qh, kh, bh, sub):
    rs, nc = GLA_SUB * sub, GLA_SUB * (sub + 1)
    ref = bh[rs:rs + 1, :]
    eq = jnp.exp(bh[rs:rs + GLA_SUB] - ref)
    ek = jnp.exp(jnp.where(_iota2((GLA_CHUNK, GLA_DK), 0) < nc, ref - bh, 0.0))
    mask = _iota2((GLA_SUB, GLA_CHUNK), 1) <= _iota2((GLA_SUB, GLA_CHUNK), 0) + rs
    return rs, eq, ek, qh[rs:rs + GLA_SUB] * eq, kh * ek, mask


def _gla_fwd(proj, gk, rows):
    n_chunks = rows // GLA_CHUNK

    def body(q_ref, k_ref, v_ref, g_ref, o_ref, sall_ref, s_scr):
        @pl.when(pl.program_id(0) == 0)
        def _():
            s_scr[...] = jnp.zeros(s_scr.shape, F32)

        g_all = g_ref[...]
        b_all, _, _ = _gla_chunk_common(g_all)
        ones = jnp.ones((GLA_CHUNK, GLA_DV), BF16)
        for h in range(GLA_HEADS):
            sl = slice(GLA_DK * h, GLA_DK * (h + 1))
            vs = slice(GLA_DV * h, GLA_DV * (h + 1))
            qh, kh, vh = q_ref[:, sl] * 0.125, k_ref[:, sl], v_ref[:, vs]
            bh, gh = b_all[:, sl], g_all[:, sl]
            s = s_scr[h]
            sall_ref[0, h] = s
            o = _dot(qh * jnp.exp(bh), s)
            parts = []
            for sub in range(GLA_CHUNK // GLA_SUB):
                _, _, _, qs, ks, mask = _gla_sub(qh, kh, bh, sub)
                a = jnp.where(mask, _dot(qs, ks, 1, 1), 0.0)
                parts.append(_dot(a, vh))
            o_ref[:, vs] = o + jnp.concatenate(parts, axis=0)
            bl_col = _dot_lhs_exact(gh, ones, 0, 0)
            kd = kh * jnp.exp(bh[GLA_CHUNK - 1:GLA_CHUNK, :] - bh)
            s_scr[h] = jnp.exp(bl_col) * s + _dot(kd, vh, 0, 0)

    c64 = lambda w, base: pl.BlockSpec((GLA_CHUNK, w), functools.partial(lambda n, b: (n, b), b=base))
    return pl.pallas_call(
        body, name="gla_fwd", grid=(n_chunks,),
        in_specs=[c64(GLA_QK_W, C_QG // GLA_QK_W), c64(GLA_QK_W, C_KG // GLA_QK_W), c64(GLA_V_W, C_VG // GLA_V_W),
                  c64(GLA_QK_W, 0)],
        out_specs=[c64(GLA_V_W, 0), pl.BlockSpec((1, GLA_HEADS, GLA_DK, GLA_DV), lambda n: (n, 0, 0, 0))],
        out_shape=[jax.ShapeDtypeStruct((rows, GLA_V_W), F32),
                   jax.ShapeDtypeStruct((n_chunks, GLA_HEADS, GLA_DK, GLA_DV), F32)],
        scratch_shapes=[pltpu.VMEM((GLA_HEADS, GLA_DK, GLA_DV), F32)],
        compiler_params=_cparams("arbitrary"),
    )(proj, proj, proj, gk)


def _gla_bwd(proj, gk, do_g, s_all, rows):
    n_chunks = rows // GLA_CHUNK

    def body(q_ref, k_ref, v_ref, g_ref, do_ref, sall_ref, dq_ref, dk_ref, dv_ref, dg_ref, ds_scr):
        @pl.when(pl.program_id(0) == 0)
        def _():
            ds_scr[...] = jnp.zeros(ds_scr.shape, F32)

        g_all = g_ref[...]
        b_all, r_i, c_i = _gla_chunk_common(g_all)
        triu = (c_i >= r_i).astype(BF16)
        ones = jnp.ones((GLA_CHUNK, GLA_DV), BF16)
        ones8 = jnp.ones((8, GLA_DV), F32)
        last_row = _iota2((GLA_CHUNK, GLA_DK), 0) == GLA_CHUNK - 1
        for h in range(GLA_HEADS):
            sl = slice(GLA_DK * h, GLA_DK * (h + 1))
            vs = slice(GLA_DV * h, GLA_DV * (h + 1))
            qh, kh, vh = q_ref[:, sl] * 0.125, k_ref[:, sl], v_ref[:, vs]
            bh, gh = b_all[:, sl], g_all[:, sl]
            doh = do_ref[:, vs]
            s, ds = sall_ref[0, h], ds_scr[h]
            eb = jnp.exp(bh)
            ekd = jnp.exp(bh[GLA_CHUNK - 1:GLA_CHUNK, :] - bh)
            ebl = jnp.exp(_dot_lhs_exact(gh, ones, 0, 0))
            qb, kd = qh * eb, kh * ekd
            dq = _dot(doh, s, 1, 1) * eb
            dk = _dot(vh, ds, 1, 1) * ekd
            dv = _dot(kd, ds)
            dbl = jnp.sum(dk * kh, axis=0, keepdims=True) + _dot3(ones8, ebl * s * ds, 1, 1)[0:1, :]
            dq_parts = []
            for sub in range(GLA_CHUNK // GLA_SUB):
                rs, eq, ek, qs, ks, mask = _gla_sub(qh, kh, bh, sub)
                do_sub = doh[rs:rs + GLA_SUB]
                a = jnp.where(mask, _dot(qs, ks, 1, 1), 0.0)
                da = jnp.where(mask, _dot(do_sub, vh, 1, 1), 0.0)
                dq_parts.append(_dot(da, ks) * eq)
                dk = dk + _dot(da, qs, 0, 0) * ek
                dv = dv + _dot(a, do_sub, 0, 0)
            dq = dq + jnp.concatenate(dq_parts, axis=0)
            db = qh * dq - kh * dk + jnp.where(last_row, dbl, 0.0)
            dq_ref[:, sl] = dq * 0.125
            dk_ref[:, sl] = dk
            dv_ref[:, vs] = dv
            dg_ref[:, sl] = _dot_rhs_exact(triu, db)
            ds_scr[h] = _dot(qb, doh, 0, 0) + ebl * ds

    last = n_chunks - 1
    c64 = lambda w, base: pl.BlockSpec((GLA_CHUNK, w), functools.partial(lambda n, b: (last - n, b), b=base))
    return pl.pallas_call(
        body, name="gla_bwd", grid=(n_chunks,),
        in_specs=[c64(GLA_QK_W, C_QG // GLA_QK_W), c64(GLA_QK_W, C_KG // GLA_QK_W), c64(GLA_V_W, C_VG // GLA_V_W),
                  c64(GLA_QK_W, 0), c64(GLA_V_W, 0),
                  pl.BlockSpec((1, GLA_HEADS, GLA_DK, GLA_DV), lambda n: (last - n, 0, 0, 0))],
        out_specs=[c64(GLA_QK_W, 0), c64(GLA_QK_W, 0), c64(GLA_V_W, 0), c64(GLA_QK_W, 0)],
        out_shape=[jax.ShapeDtypeStruct((rows, GLA_QK_W), F32), jax.ShapeDtypeStruct((rows, GLA_QK_W), F32),
                   jax.ShapeDtypeStruct((rows, GLA_V_W), F32), jax.ShapeDtypeStruct((rows, GLA_QK_W), F32)],
        scratch_shapes=[pltpu.VMEM((GLA_HEADS, GLA_DK, GLA_DV), F32)],
        compiler_params=_cparams("arbitrary"),
    )(proj, proj, proj, gk, do_g, s_all)


def _mem_kv_fwd(mem, mem_norm_w, w_mkv, mk_norm_w):
    def body(mem_ref, mw_ref, w_ref, kw_ref, memn_ref, kpre_ref, kn_ref, v_ref):
        xb = mem_ref[...]
        r = lax.rsqrt(jnp.mean(xb * xb, axis=-1, keepdims=True) + RMS_EPS)
        mem_n = (xb * r * mw_ref[...]).astype(BF16)
        memn_ref[...] = mem_n
        kv = _dot(mem_n, w_ref[...])
        kpre_ref[...] = kv[:, :D_MODEL]
        v_ref[...] = kv[:, D_MODEL:].astype(BF16)
        for h in range(MEM_HEADS):
            sl = slice(MEM_DH * h, MEM_DH * (h + 1))
            kh = kv[:, sl]
            rk = lax.rsqrt(jnp.mean(kh * kh, axis=-1, keepdims=True) + RMS_EPS)
            kn_ref[:, sl] = (kh * rk * kw_ref[...]).astype(BF16)

    return pl.pallas_call(
        body, name="mem_kv_fwd",
        out_shape=[jax.ShapeDtypeStruct((MEM_LEN, D_MODEL), BF16), jax.ShapeDtypeStruct((MEM_LEN, D_MODEL), F32),
                   jax.ShapeDtypeStruct((MEM_LEN, D_MODEL), BF16), jax.ShapeDtypeStruct((MEM_LEN, D_MODEL), BF16)],
        compiler_params=_cparams(),
    )(mem, mem_norm_w, w_mkv, mk_norm_w)


def _mem_kv_bwd(mem, mem_norm_w, w_mkv, mk_norm_w, mem_n, kpre, dkn, dv):
    def body(mem_ref, mw_ref, w_ref, kw_ref, memn_ref, kpre_ref, dkn_ref, dv_ref, dw_ref, dkw_ref, dmw_ref):
        dkw = jnp.zeros((1, MEM_DH), F32)
        dk_parts = []
        for h in range(MEM_HEADS):
            sl = slice(MEM_DH * h, MEM_DH * (h + 1))
            kh, dkh = kpre_ref[:, sl], dkn_ref[:, sl]
            rk = lax.rsqrt(jnp.mean(kh * kh, axis=-1, keepdims=True) + RMS_EPS)
            xh = kh * rk
            dxh = dkh * kw_ref[...]
            dkw = dkw + jnp.sum(dkh * xh, axis=0, keepdims=True)
            dk_parts.append(rk * (dxh - xh * jnp.mean(dxh * xh, axis=-1, keepdims=True)))
        dkw_ref[...] = dkw
        dkv = jnp.concatenate(dk_parts + [dv_ref[...]], axis=1).astype(BF16)
        dw_ref[...] = _dot(memn_ref[...], dkv, 0, 0)
        dmem_n = _dot(dkv, w_ref[...], 1, 1)
        xb = mem_ref[...]
        r = lax.rsqrt(jnp.mean(xb * xb, axis=-1, keepdims=True) + RMS_EPS)
        dmw_ref[...] = jnp.sum(dmem_n * (xb * r), axis=0, keepdims=True)

    return pl.pallas_call(
        body, name="mem_kv_bwd",
        out_shape=[jax.ShapeDtypeStruct((D_MODEL, 2 * D_MODEL), F32), jax.ShapeDtypeStruct((1, MEM_DH), F32),
                   jax.ShapeDtypeStruct((1, D_MODEL), F32)],
        compiler_params=_cparams(),
    )(mem, mem_norm_w, w_mkv, mk_norm_w, mem_n, kpre, dkn, dv)


def _xattn_head(qh, kn_h, qw):
    rq = lax.rsqrt(jnp.mean(qh * qh, axis=-1, keepdims=True) + RMS_EPS)
    xh = qh * rq
    qn = (xh * qw).astype(BF16)
    s = _dot(qn, kn_h, 1, 1) * (1.0 / 16.0)
    e = jnp.exp(s - jnp.max(s, axis=-1, keepdims=True))
    p = e / jnp.sum(e, axis=-1, keepdims=True)
    return rq, xh, qn, p


def _xattn_fwd(qm, kn, v, mq_norm_w, bm):
    def fn(qb, knb, vb, qw):
        outs = []
        for h in range(MEM_HEADS):
            sl = slice(MEM_DH * h, MEM_DH * (h + 1))
            _, _, _, p = _xattn_head(qb[:, sl], knb[:, sl], qw)
            outs.append(_dot(p, vb[:, sl]))
        return (jnp.concatenate(outs, axis=1),)
    return _rowcall("xattn_fwd", fn, [(qm, D_MODEL, 0)], [kn, v, mq_norm_w], [(D_MODEL, BF16)], [], bm,
                    qm.shape[0])[0]


def _xattn_bwd(qm, kn, v, mq_norm_w, do, bm):
    def fn(qb, dob, knb, vb, qw):
        dq_parts, dkn_parts, dv_parts = [], [], []
        dqw = jnp.zeros((1, MEM_DH), F32)
        for h in range(MEM_HEADS):
            sl = slice(MEM_DH * h, MEM_DH * (h + 1))
            rq, xh, qn, p = _xattn_head(qb[:, sl], knb[:, sl], qw)
            doh = dob[:, sl].astype(BF16)
            dp = _dot(doh, vb[:, sl], 1, 1)
            ds = (p * (dp - jnp.sum(dp * p, axis=-1, keepdims=True)) * (1.0 / 16.0)).astype(BF16)
            dqn = _dot(ds, knb[:, sl])
            dkn_parts.append(_dot(ds, qn, 0, 0))
            dv_parts.append(_dot(p, doh, 0, 0))
            dqw = dqw + jnp.sum(dqn * xh, axis=0, keepdims=True)
            dxh = dqn * qw
            dq_parts.append(rq * (dxh - xh * jnp.mean(dxh * xh, axis=-1, keepdims=True)))
        return (jnp.concatenate(dq_parts, axis=1), jnp.concatenate(dkn_parts, axis=1),
                jnp.concatenate(dv_parts, axis=1), dqw)
    return _rowcall("xattn_bwd", fn, [(qm, D_MODEL, 0), (do, D_MODEL, 0)], [kn, v, mq_norm_w],
                    [(D_MODEL, BF16)], [(MEM_LEN, D_MODEL), (MEM_LEN, D_MODEL), (1, MEM_DH)], bm, qm.shape[0])


FF_BN = 1408
FF_NB = D_FF // FF_BN


def _ffn_up(h3, w_gate_up, rows, bm):
    def body(h_ref, wg_ref, wu_ref, gate_ref, up_ref, act_ref):
        hb = h_ref[...]
        gate = _dot(hb, wg_ref[...])
        up = _dot(hb, wu_ref[...])
        gate_ref[...] = gate
        up_ref[...] = up
        act_ref[...] = (gate * _sigmoid(gate) * up).astype(BF16)

    out_blk = pl.BlockSpec((bm, FF_BN), lambda i, j: (i, j))
    return pl.pallas_call(
        body, name="ffn_up", grid=(rows // bm, FF_NB),
        in_specs=[pl.BlockSpec((bm, D_MODEL), lambda i, j: (i, 0)),
                  pl.BlockSpec((D_MODEL, FF_BN), lambda i, j: (0, j)),
                  pl.BlockSpec((D_MODEL, FF_BN), lambda i, j: (0, FF_NB + j))],
        out_specs=[out_blk, out_blk, out_blk],
        out_shape=[jax.ShapeDtypeStruct((rows, D_FF), F32), jax.ShapeDtypeStruct((rows, D_FF), F32),
                   jax.ShapeDtypeStruct((rows, D_FF), BF16)],
        compiler_params=_cparams("parallel", "arbitrary"),
    )(h3, w_gate_up, w_gate_up)


def _ffn_act_bwd(dy, w_down, gate, up, rows, bm):
    def body(dy_ref, wd_ref, gate_ref, up_ref, o_ref):
        dact = _dot(dy_ref[...], wd_ref[...], 1, 1)
        g, u = gate_ref[...], up_ref[...]
        sg = _sigmoid(g)
        o_ref[0] = (dact * u * (sg * (1.0 + g * (1.0 - sg)))).astype(BF16)
        o_ref[1] = (dact * (g * sg)).astype(BF16)

    blk = pl.BlockSpec((bm, FF_BN), lambda i, j: (i, j))
    return pl.pallas_call(
        body, name="ffn_act_bwd", grid=(rows // bm, FF_NB),
        in_specs=[pl.BlockSpec((bm, D_MODEL), lambda i, j: (i, 0)),
                  pl.BlockSpec((FF_BN, D_MODEL), lambda i, j: (j, 0)), blk, blk],
        out_specs=pl.BlockSpec((2, bm, FF_BN), lambda i, j: (0, i, j)),
        out_shape=jax.ShapeDtypeStruct((2, rows, D_FF), BF16),
        compiler_params=_cparams("parallel", "arbitrary"),
    )(dy, w_down, gate, up)


def _pack_rows(a):
    flat = a.reshape(-1)
    pad = (-flat.shape[0]) % 1024
    if pad:
        flat = jnp.concatenate([flat, jnp.zeros((pad,), flat.dtype)])
    return flat.reshape(-1, 1024)


def _pack_shards(shards, extra_rows, total_rows, dtype):
    parts = [_pack_rows(shards[n].astype(dtype)) for n in SHARDED] + [r.astype(dtype) for r in extra_rows]
    used = sum(p.shape[0] for p in parts)
    parts.append(jnp.zeros((total_rows - used, 1024), dtype))
    return jnp.concatenate(parts, axis=0)


def _shard_offsets():
    offs, o = {}, 0
    for n in SHARDED:
        offs[n] = o
        o += SHARD_ROWS[n]
    return offs, o


def _unpack_shard(pack, name):
    offs, _ = _shard_offsets()
    r, c = SHARD_SHAPE[name]
    seg = pack[offs[name]:offs[name] + SHARD_ROWS[name]].reshape(-1)[:r * c]
    return seg.reshape(1, r, c)


def _gathered_weight(gathered, name):
    offs, _ = _shard_offsets()
    r, c = SHARD_SHAPE[name]
    seg = gathered[:, offs[name]:offs[name] + SHARD_ROWS[name]].reshape(N_DEV, -1)[:, :r * c].reshape(N_DEV, r, c)
    if name in ("w_out", "w_mq", "w_mo", "w_down"):
        return seg.reshape(N_DEV * r, c)
    return seg.transpose(1, 0, 2).reshape(r, N_DEV * c)


def _split_for_owners(name, full):
    r, c = SHARD_SHAPE[name]
    if name in ("w_out", "w_mq", "w_mo", "w_down"):
        return full.reshape(N_DEV, r, c)
    return full.reshape(r, N_DEV, c).transpose(1, 0, 2)


def _repl_row(a):
    flat = a.reshape(-1)
    return jnp.concatenate([flat, jnp.zeros((1024 - flat.shape[0],), flat.dtype)]).reshape(1, 1024)


def _local_step(x, mem, tgt, wf, rp):
    rows = x.shape[0]
    bm = min(512, rows)
    bmx = min(256, rows)
    mt = min(MM_TILE, rows)
    kt = min(512, rows)
    w_cat, wgk_pad = wf["w_cat"], wf["wgk_pad"]
    wg_t = jnp.tile(rp["gla_norm_w"], (1, GLA_HEADS))
    ws_t = jnp.tile(rp["sb_norm_w"], (1, SB_HEADS))
    lane = jnp.arange(GLA_V_W)
    grp_g = (lane[:, None] // GLA_DV == lane[None, :] // GLA_DV).astype(BF16)
    grp_s = (lane[:, None] // SB_DH == lane[None, :] // SB_DH).astype(BF16)

    h1 = _rms_fwd("mix_norm_fwd", x, rp["mix_norm_w"], bm)
    proj = _matmul("in_proj", h1, w_cat, "nn", rows, PROJ_W, D_MODEL, F32, mt, 640, D_MODEL)
    gk = _gate_fwd(proj, wgk_pad, rp["b_gk"], bm)
    o_g, s_all = _gla_fwd(proj, gk, rows)
    o_s = _sb_fwd(proj, rows)
    cat = _mix_out_fwd(o_g, proj, o_s, wg_t, ws_t, grp_g, grp_s, bm)
    x1 = _matmul("out_proj", cat, wf["w_out"], "nn", rows, D_MODEL, D_MODEL, F32, mt, MM_TILE, D_MODEL, residual=x)
    h2 = _rms_fwd("xattn_norm_fwd", x1, rp["xattn_norm_w"], bm)
    qm = _matmul("mq_proj", h2, wf["w_mq"], "nn", rows, D_MODEL, D_MODEL, F32, mt, MM_TILE, D_MODEL)
    mem_n, kpre, kn, v_m = _mem_kv_fwd(mem, rp["mem_norm_w"], wf["w_mkv"], rp["mk_norm_w"])
    o_m = _xattn_fwd(qm, kn, v_m, rp["mq_norm_w"], bmx)
    x2 = _matmul("mo_proj", o_m, wf["w_mo"], "nn", rows, D_MODEL, D_MODEL, F32, mt, MM_TILE, D_MODEL, residual=x1)
    h3 = _rms_fwd("ffn_norm_fwd", x2, rp["ffn_norm_w"], bm)
    gate, up, act = _ffn_up(h3, wf["w_gate_up"], rows, bm)
    y = _matmul("ffn_down", act, wf["w_down"], "nn", rows, D_MODEL, D_FF, F32, mt, MM_TILE, FF_BN, residual=x2)
    dy, sq = _loss_kernel(y, tgt, bm)

    g = {}
    dgu = _ffn_act_bwd(dy, wf["w_down"], gate, up, rows, bm)
    g["w_down"] = _matmul("grad_w_down", act, dy, "tn", D_FF, D_MODEL, rows, F32, FF_BN, MM_TILE, kt)
    nkb = FF_NB
    dh3 = _matmul("ffn_up_bwd", dgu, wf["w_gate_up"], "nt", rows, D_MODEL, 2 * D_FF, F32, mt, MM_TILE, FF_BN,
                  a_spec=pl.BlockSpec((None, mt, FF_BN), lambda i, j, kk: (kk // nkb, i, kk % nkb)))
    g["w_gate_up"] = _matmul(
        "grad_w_gate_up", h3, dgu, "tn", D_MODEL, 2 * D_FF, rows, F32, MM_TILE, FF_BN, kt,
        b_spec=pl.BlockSpec((None, kt, FF_BN), lambda i, j, kk: (j // nkb, kk, j % nkb)))
    dx2, g["ffn_norm_w"] = _rms_bwd("ffn_norm_bwd", x2, rp["ffn_norm_w"], dh3, dy, bm)

    do_m = _matmul("mo_proj_bwd", dx2, wf["w_mo"], "nt", rows, D_MODEL, D_MODEL, BF16, mt, MM_TILE, D_MODEL)
    g["w_mo"] = _matmul("grad_w_mo", o_m, dx2, "tn", D_MODEL, D_MODEL, rows, F32, MM_TILE, MM_TILE, kt)
    dqm, dkn, dv_m, g["mq_norm_w"] = _xattn_bwd(qm, kn, v_m, rp["mq_norm_w"], do_m, bmx)
    g["w_mkv"], g["mk_norm_w"], g["mem_norm_w"] = _mem_kv_bwd(
        mem, rp["mem_norm_w"], wf["w_mkv"], rp["mk_norm_w"], mem_n, kpre, dkn, dv_m)
    dh2 = _matmul("mq_proj_bwd", dqm, wf["w_mq"], "nt", rows, D_MODEL, D_MODEL, F32, mt, MM_TILE, D_MODEL)
    g["w_mq"] = _matmul("grad_w_mq", h2, dqm, "tn", D_MODEL, D_MODEL, rows, F32, MM_TILE, MM_TILE, kt)
    dx1, g["xattn_norm_w"] = _rms_bwd("xattn_norm_bwd", x1, rp["xattn_norm_w"], dh2, dx2, bm)

    dcat = _matmul("out_proj_bwd", dx1, wf["w_out"], "nt", rows, D_MODEL, D_MODEL, F32, mt, MM_TILE, D_MODEL)
    g["w_out"] = _matmul("grad_w_out", cat, dx1, "tn", D_MODEL, D_MODEL, rows, F32, MM_TILE, MM_TILE, kt)
    do_g, dgg, do_s, dwg, dws = _mix_out_bwd(dcat, o_g, proj, o_s, wg_t, ws_t, grp_g, grp_s, bm)
    g["gla_norm_w"] = dwg.reshape(GLA_HEADS, GLA_DV).sum(axis=0, keepdims=True)
    g["sb_norm_w"] = dws.reshape(SB_HEADS, SB_DH).sum(axis=0, keepdims=True)
    dq_s, dk_s, dv_s = _sb_bwd(proj, o_s, do_s, rows)
    dq_g, dk_g, dv_g, dgk = _gla_bwd(proj, gk, do_g, s_all, rows)
    dlr, dwgk, g["b_gk"] = _gate_bwd(proj, wgk_pad, rp["b_gk"], dgk, bm)
    g["w_gk_up"] = dwgk[:GATE_RANK]
    dproj = jnp.concatenate([dq_g.astype(BF16), dk_g.astype(BF16), dv_g.astype(BF16), dgg, dq_s.astype(BF16),
                             dk_s.astype(BF16), dv_s.astype(BF16), dlr], axis=1)
    dh1 = _matmul("in_proj_bwd", dproj, w_cat, "nt", rows, D_MODEL, PROJ_W, F32, mt, MM_TILE, 640)
    dw_cat = _matmul("grad_w_in", h1, dproj, "tn", D_MODEL, PROJ_W, rows, F32, MM_TILE, 640, kt)
    g["w_in"] = jnp.concatenate([dw_cat[:, :C_QS], dw_cat[:, C_LR:C_LR + GATE_RANK], dw_cat[:, C_QS:C_LR]], axis=1)
    dx, g["mix_norm_w"] = _rms_bwd("mix_norm_bwd", x, rp["mix_norm_w"], dh1, dx1, bm)
    return sq, dx, g


def _full_weights(gathered):
    w_in = _gathered_weight(gathered, "w_in")
    lr_end = C_QS + GATE_RANK
    w_cat = jnp.concatenate([w_in[:, :C_QS], w_in[:, lr_end:], w_in[:, C_QS:lr_end],
                             jnp.zeros((D_MODEL, PROJ_W - D_IN), BF16)], axis=1)
    wgk = _gathered_weight(gathered, "w_gk_up")
    wf = {n: _gathered_weight(gathered, n) for n in ("w_out", "w_mq", "w_mkv", "w_mo", "w_gate_up", "w_down")}
    wf["w_cat"] = w_cat
    wf["wgk_pad"] = jnp.concatenate([wgk, jnp.zeros((LANES - GATE_RANK, GLA_QK_W), BF16)], axis=0)
    return wf


def kernel(x, mem, mix_norm_w, w_in, w_gk_up, b_gk, gla_norm_w, sb_norm_w, w_out, xattn_norm_w, mem_norm_w, w_mq, w_mkv, mq_norm_w, mk_norm_w, w_mo, ffn_norm_w, w_gate_up, w_down, loss_target, m_mix_norm_w, m_w_in, m_w_gk_up, m_b_gk, m_gla_norm_w, m_sb_norm_w, m_w_out, m_xattn_norm_w, m_mem_norm_w, m_w_mq, m_w_mkv, m_mq_norm_w, m_mk_norm_w, m_w_mo, m_ffn_norm_w, m_w_gate_up, m_w_down, v_mix_norm_w, v_w_in, v_w_gk_up, v_b_gk, v_gla_norm_w, v_sb_norm_w, v_w_out, v_xattn_norm_w, v_mem_norm_w, v_w_mq, v_w_mkv, v_mq_norm_w, v_mk_norm_w, v_w_mo, v_ffn_norm_w, v_w_gate_up, v_w_down):
    given = dict(locals())
    w = {n: given[n][0] for n in WEIGHTS}
    m = {n: given["m_" + n][0] for n in WEIGHTS}
    v = {n: given["v_" + n][0] for n in WEIGHTS}

    gathered = _all_gather(_pack_shards(w, [], AG_ROWS, BF16))
    wf = _full_weights(gathered)
    rp = {n: w[n].reshape(1, -1) for n in REPL}
    sq, dx, g = _local_step(x[0], mem[0], loss_target[0], wf, rp)

    loss_row = _repl_row(jnp.sum(sq).reshape(1) * (0.5 / D_MODEL))
    repl_rows = [_repl_row(g[n]) for n in REPL] + [loss_row]
    per_owner = {n: _split_for_owners(n, g[n]) for n in SHARDED}
    gpack = jnp.stack([
        _pack_shards({n: per_owner[n][d] for n in SHARDED}, repl_rows, RS_ROWS, F32) for d in range(N_DEV)])
    recv = _scatter_to_owners(gpack)

    zero_row = jnp.zeros((1, 1024), F32)
    packs = [_pack_shards(t, [_repl_row(t[n]) for n in REPL] + [zero_row], RS_ROWS, F32) for t in (w, m, v)]
    out_packs = _adamw(recv, *packs)

    _, n_shard_rows = _shard_offsets()

    def unpack(pack, name):
        if name in SHARDED:
            return _unpack_shard(pack, name)
        row = pack[n_shard_rows + REPL.index(name)]
        return row[:w[name].shape[-1]].reshape(1, -1)

    loss = out_packs[0][n_shard_rows + len(REPL), 0]
    outs = [loss, dx[None]]
    for pack in out_packs:
        outs += [unpack(pack, n) for n in WEIGHTS]
    return tuple(outs)
```

```python
import functools
import math

import jax
import jax.numpy as jnp
from jax import lax
from jax.experimental import pallas as pl
from jax.experimental.pallas import tpu as pltpu

F32 = jnp.float32
BF16 = jnp.bfloat16

N_DEV = 8
D_MODEL = 1024
GLA_HEADS = 4
GLA_DK = 64
GLA_DV = 128
GLA_CHUNK = 64
GLA_SUB = 16
GLA_QK_W = GLA_HEADS * GLA_DK
GLA_V_W = GLA_HEADS * GLA_DV
GATE_RANK = 16
SB_HEADS = 8
SB_DH = 64
SB_W = SB_HEADS * SB_DH
SB_BLK = 128
SB_QT = 1024
SB_DEAD = -104.0
MEM_LEN = 256
MEM_HEADS = 4
MEM_DH = 256
D_FF = 2816
D_IN = 3088
RMS_EPS = 1e-6
LANES = 128

PROJ_W = 3200
C_QG, C_KG, C_VG, C_GG, C_QS, C_KS, C_VS, C_LR = 0, 256, 512, 1024, 1536, 2048, 2560, 3072

ADAM_LR, ADAM_B1, ADAM_B2, ADAM_EPS, ADAM_WD, ADAM_STEP = 0.001, 0.9, 0.999, 1e-08, 0.01, 10

SHARDED = ("w_in", "w_out", "w_mq", "w_mkv", "w_mo", "w_gate_up", "w_down", "w_gk_up")
SHARD_SHAPE = {"w_in": (1024, 386), "w_out": (128, 1024), "w_mq": (128, 1024), "w_mkv": (1024, 256),
               "w_mo": (128, 1024), "w_gate_up": (1024, 704), "w_down": (352, 1024), "w_gk_up": (16, 32)}
SHARD_ROWS = {n: -(-(s[0] * s[1]) // 1024) for n, s in SHARD_SHAPE.items()}
REPL = ("mix_norm_w", "b_gk", "gla_norm_w", "sb_norm_w", "xattn_norm_w", "mem_norm_w", "mq_norm_w",
        "mk_norm_w", "ffn_norm_w")
WEIGHTS = ("mix_norm_w", "w_in", "w_gk_up", "b_gk", "gla_norm_w", "sb_norm_w", "w_out", "xattn_norm_w",
           "mem_norm_w", "w_mq", "w_mkv", "mq_norm_w", "mk_norm_w", "w_mo", "ffn_norm_w", "w_gate_up", "w_down")
AG_ROWS = 2096
RS_BLOCK = 192
RS_ROWS = 11 * RS_BLOCK
VMEM_LIMIT = 56 * 1024 * 1024
MM_TILE = 1024


def _cparams(*sem):
    return pltpu.CompilerParams(dimension_semantics=sem if sem else None, vmem_limit_bytes=VMEM_LIMIT)


def _dot(a, b, ca=1, cb=0):
    return lax.dot_general(a.astype(BF16), b.astype(BF16), (((ca,), (cb,)), ((), ())),
                           preferred_element_type=F32)


def _split(x, parts):
    out = []
    for _ in range(parts - 1):
        hi = x.astype(BF16)
        out.append(hi)
        x = x - hi.astype(F32)
    out.append(x.astype(BF16))
    return out


def _dot_lhs_exact(x, m, ca=1, cb=0, parts=3):
    acc = None
    for p in _split(x, parts):
        t = _dot(p, m, ca, cb)
        acc = t if acc is None else acc + t
    return acc


def _dot_rhs_exact(m, x, ca=1, cb=0, parts=3):
    acc = None
    for p in _split(x, parts):
        t = _dot(m, p, ca, cb)
        acc = t if acc is None else acc + t
    return acc


def _dot3(a, b, ca=1, cb=0):
    a_hi, a_lo = _split(a, 2)
    b_hi, b_lo = _split(b, 2)
    return _dot(a_hi, b_hi, ca, cb) + (_dot(a_hi, b_lo, ca, cb) + _dot(a_lo, b_hi, ca, cb))


def _log_sigmoid(z):
    return jnp.minimum(z, 0.0) - jnp.log(1.0 + jnp.exp(-jnp.abs(z)))


def _sigmoid(z):
    e = jnp.exp(-jnp.abs(z))
    return jnp.where(z >= 0, 1.0, e) / (1.0 + e)


def _iota2(shape, dim):
    return lax.broadcasted_iota(jnp.int32, shape, dim)


def _rowcall(name, fn, row_ins, full_ins, row_outs, acc_outs, bm, rows):
    n_in = len(row_ins) + len(full_ins)
    n_row = len(row_outs)

    def body(*refs):
        ins, outs = refs[:n_in], refs[n_in:]
        res = fn(*[r[...] for r in ins])
        for r, v in zip(outs[:n_row], res[:n_row]):
            r[...] = v.astype(r.dtype)
        first = pl.program_id(0) == 0
        for r, v in zip(outs[n_row:], res[n_row:]):
            def init(r=r):
                r[...] = jnp.zeros(r.shape, r.dtype)
            pl.when(first)(init)
            r[...] += v

    in_specs = [pl.BlockSpec((bm, w), functools.partial(lambda i, c: (i, c), c=c)) for _, w, c in row_ins]
    in_specs += [pl.BlockSpec(a.shape, lambda i: (0, 0)) for a in full_ins]
    out_specs = [pl.BlockSpec((bm, w), lambda i: (i, 0)) for w, _ in row_outs]
    out_specs += [pl.BlockSpec(s, lambda i: (0, 0)) for s in acc_outs]
    out_shape = [jax.ShapeDtypeStruct((rows, w), dt) for w, dt in row_outs]
    out_shape += [jax.ShapeDtypeStruct(s, F32) for s in acc_outs]
    return pl.pallas_call(
        body, name=name, grid=(rows // bm,), in_specs=in_specs, out_specs=out_specs, out_shape=out_shape,
        compiler_params=_cparams("arbitrary"),
    )(*[a for a, _, _ in row_ins], *full_ins)


def _matmul(name, a, b, mode, m, n, k, out_dtype, bm, bn, bk, residual=None, a_spec=None, b_spec=None):
    bm, bn, bk = min(bm, m), min(bn, n), min(bk, k)
    nk = k // bk
    ca, cb = {"nn": (1, 0), "nt": (1, 1), "tn": (0, 0)}[mode]
    if a_spec is None:
        a_spec = (pl.BlockSpec((bk, bm), lambda i, j, kk: (kk, i)) if mode == "tn"
                  else pl.BlockSpec((bm, bk), lambda i, j, kk: (i, kk)))
    if b_spec is None:
        b_spec = (pl.BlockSpec((bn, bk), lambda i, j, kk: (j, kk)) if mode == "nt"
                  else pl.BlockSpec((bk, bn), lambda i, j, kk: (kk, j)))
    has_res = residual is not None

    def body(*refs):
        a_ref, b_ref = refs[0], refs[1]
        res_ref = refs[2] if has_res else None
        o_ref = refs[2 + has_res]
        part = _dot(a_ref[...], b_ref[...], ca, cb)

        def finish(total):
            if has_res:
                total = total + res_ref[...]
            o_ref[...] = total.astype(o_ref.dtype)

        if nk == 1:
            finish(part)
        else:
            acc_ref = refs[3 + has_res]
            kk = pl.program_id(2)

            @pl.when(kk == 0)
            def _():
                acc_ref[...] = part

            @pl.when(kk > 0)
            def _():
                acc_ref[...] += part

            @pl.when(kk == nk - 1)
            def _():
                finish(acc_ref[...])

    in_specs = [a_spec, b_spec]
    args = [a, b]
    if has_res:
        in_specs.append(pl.BlockSpec((bm, bn), lambda i, j, kk: (i, j)))
        args.append(residual)
    return pl.pallas_call(
        body, name=name, grid=(m // bm, n // bn, nk), in_specs=in_specs,
        out_specs=pl.BlockSpec((bm, bn), lambda i, j, kk: (i, j)),
        out_shape=jax.ShapeDtypeStruct((m, n), out_dtype),
        scratch_shapes=[pltpu.VMEM((bm, bn), F32)] if nk > 1 else [],
        compiler_params=_cparams("parallel", "parallel", "arbitrary"),
    )(*args)


def _peer(mask):
    x, y, c = lax.axis_index("x"), lax.axis_index("y"), lax.axis_index("c")
    mx, my, mc = (mask >> 2) & 1, (mask >> 1) & 1, mask & 1
    px, py, pc = (1 - x if mx else x), (1 - y if my else y), (1 - c if mc else c)
    return (px, py, pc), 4 * px + 2 * py + pc


def _my_index():
    return 4 * lax.axis_index("x") + 2 * lax.axis_index("y") + lax.axis_index("c")


def _all_gather(pack):
    rows = pack.shape[0]

    def body(src_ref, out_ref, send_sems, recv_sems, local_sem):
        me = _my_index()
        mine = pltpu.make_async_copy(src_ref, out_ref.at[me], local_sem)
        mine.start()
        copies = []
        for mask in range(1, N_DEV):
            peer, _ = _peer(mask)
            cp = pltpu.make_async_remote_copy(
                src_ref=src_ref, dst_ref=out_ref.at[me], send_sem=send_sems.at[mask - 1],
                recv_sem=recv_sems.at[mask - 1], device_id=peer, device_id_type=pl.DeviceIdType.MESH)
            cp.start()
            copies.append(cp)
        for cp in copies:
            cp.wait()
        mine.wait()

    return pl.pallas_call(
        body, name="all_gather_weights",
        in_specs=[pl.BlockSpec(memory_space=pl.ANY)], out_specs=pl.BlockSpec(memory_space=pl.ANY),
        out_shape=jax.ShapeDtypeStruct((N_DEV, rows, 1024), pack.dtype),
        scratch_shapes=[pltpu.SemaphoreType.DMA((N_DEV - 1,)), pltpu.SemaphoreType.DMA((N_DEV - 1,)),
                        pltpu.SemaphoreType.DMA(())],
        compiler_params=pltpu.CompilerParams(has_side_effects=True),
    )(pack)


def _scatter_to_owners(gpack):
    rows = gpack.shape[1]

    def body(src_ref, out_ref, send_sems, recv_sems, local_sem):
        me = _my_index()
        mine = pltpu.make_async_copy(src_ref.at[me], out_ref.at[me], local_sem)
        mine.start()
        copies = []
        for mask in range(1, N_DEV):
            peer, peer_index = _peer(mask)
            cp = pltpu.make_async_remote_copy(
                src_ref=src_ref.at[peer_index], dst_ref=out_ref.at[me], send_sem=send_sems.at[mask - 1],
                recv_sem=recv_sems.at[mask - 1], device_id=peer, device_id_type=pl.DeviceIdType.MESH)
            cp.start()
            copies.append(cp)
        for cp in copies:
            cp.wait()
        mine.wait()

    return pl.pallas_call(
        body, name="scatter_gradients",
        in_specs=[pl.BlockSpec(memory_space=pl.ANY)], out_specs=pl.BlockSpec(memory_space=pl.ANY),
        out_shape=jax.ShapeDtypeStruct((N_DEV, rows, 1024), gpack.dtype),
        scratch_shapes=[pltpu.SemaphoreType.DMA((N_DEV - 1,)), pltpu.SemaphoreType.DMA((N_DEV - 1,)),
                        pltpu.SemaphoreType.DMA(())],
        compiler_params=pltpu.CompilerParams(has_side_effects=True),
    )(gpack)


def _adamw(recv, w, m, v):
    rows = w.shape[0]
    c1 = 1.0 - ADAM_B1 ** ADAM_STEP
    c2 = 1.0 - ADAM_B2 ** ADAM_STEP

    def body(r_ref, w_ref, m_ref, v_ref, g_out, d_out, m_out, v_out):
        g = r_ref[0]
        for s in range(1, N_DEV):
            g = g + r_ref[s]
        m_new = ADAM_B1 * m_ref[...] + (1.0 - ADAM_B1) * g
        v_new = ADAM_B2 * v_ref[...] + (1.0 - ADAM_B2) * (g * g)
        m_hat = m_new / c1
        v_hat = v_new / c2
        g_out[...] = g
        d_out[...] = -ADAM_LR * (m_hat / (jnp.sqrt(v_hat) + ADAM_EPS) + ADAM_WD * w_ref[...])
        m_out[...] = m_new
        v_out[...] = v_new

    blk = pl.BlockSpec((RS_BLOCK, 1024), lambda i: (i, 0))
    return pl.pallas_call(
        body, name="sum_adamw", grid=(rows // RS_BLOCK,),
        in_specs=[pl.BlockSpec((N_DEV, RS_BLOCK, 1024), lambda i: (0, i, 0)), blk, blk, blk],
        out_specs=[blk] * 4, out_shape=[jax.ShapeDtypeStruct((rows, 1024), F32)] * 4,
        compiler_params=_cparams("parallel"),
    )(recv, w, m, v)


def _rms_fwd(name, x, w, bm):
    def fn(xb, wb):
        r = lax.rsqrt(jnp.mean(xb * xb, axis=-1, keepdims=True) + RMS_EPS)
        return (xb * r * wb,)
    return _rowcall(name, fn, [(x, D_MODEL, 0)], [w], [(D_MODEL, BF16)], [], bm, x.shape[0])[0]


def _rms_bwd(name, x, w, dh, dres, bm):
    def fn(xb, dhb, drb, wb):
        r = lax.rsqrt(jnp.mean(xb * xb, axis=-1, keepdims=True) + RMS_EPS)
        xh = xb * r
        dxh = dhb.astype(F32) * wb
        dx = drb + r * (dxh - xh * jnp.mean(dxh * xh, axis=-1, keepdims=True))
        return dx, jnp.sum(dhb.astype(F32) * xh, axis=0, keepdims=True)
    return _rowcall(name, fn, [(x, D_MODEL, 0), (dh, D_MODEL, 0), (dres, D_MODEL, 0)], [w],
                    [(D_MODEL, F32)], [(1, D_MODEL)], bm, x.shape[0])


def _gate_fwd(proj, wgk_pad, b_gk, bm):
    def fn(lr, wg, bg):
        z = _dot(lr, wg) + bg
        return (_log_sigmoid(z) * (1.0 / 16.0),)
    return _rowcall("gla_gate_fwd", fn, [(proj, LANES, C_LR // LANES)], [wgk_pad, b_gk],
                    [(GLA_QK_W, F32)], [], bm, proj.shape[0])[0]


def _gate_bwd(proj, wgk_pad, b_gk, dgk, bm):
    def fn(lr, dg, wg, bg):
        z = _dot(lr, wg) + bg
        dz = dg * _sigmoid(-z) * (1.0 / 16.0)
        return _dot(dz, wg, 1, 1), _dot(lr, dz, 0, 0), jnp.sum(dz, axis=0, keepdims=True)
    return _rowcall("gla_gate_bwd", fn, [(proj, LANES, C_LR // LANES), (dgk, GLA_QK_W, 0)], [wgk_pad, b_gk],
                    [(LANES, BF16)], [(LANES, GLA_QK_W), (1, GLA_QK_W)], bm, proj.shape[0])


def _group_mean(x, g, size):
    return _dot_lhs_exact(x, g, parts=2) * (1.0 / size)


def _mix_out_fwd(o_g, proj, o_s, wg_t, ws_t, grp_g, grp_s, bm):
    def fn(og, gg, os_, wg, ws, gmat, smat):
        rg = lax.rsqrt(_group_mean(og * og, gmat, GLA_DV) + RMS_EPS)
        yg = og * rg * wg * (gg * _sigmoid(gg))
        rs = lax.rsqrt(_group_mean(os_ * os_, smat, SB_DH) + RMS_EPS)
        ys = os_ * rs * ws
        return (jnp.concatenate([yg, ys], axis=1),)
    return _rowcall("mix_out_fwd", fn, [(o_g, GLA_V_W, 0), (proj, GLA_V_W, C_GG // GLA_V_W), (o_s, SB_W, 0)],
                    [wg_t, ws_t, grp_g, grp_s], [(D_MODEL, BF16)], [], bm, o_g.shape[0])[0]


def _mix_out_bwd(dcat, o_g, proj, o_s, wg_t, ws_t, grp_g, grp_s, bm):
    def fn(dyg, dys, og, gg, os_, wg, ws, gmat, smat):
        dyg = dyg.astype(F32)
        dys = dys.astype(F32)
        rg = lax.rsqrt(_group_mean(og * og, gmat, GLA_DV) + RMS_EPS)
        xh = og * rg
        sg = _sigmoid(gg)
        silu = gg * sg
        dxh = dyg * wg * silu
        dgg = dyg * xh * wg * (sg * (1.0 + gg * (1.0 - sg)))
        dwg = jnp.sum(dyg * xh * silu, axis=0, keepdims=True)
        dog = rg * (dxh - xh * _group_mean(dxh * xh, gmat, GLA_DV))
        rs = lax.rsqrt(_group_mean(os_ * os_, smat, SB_DH) + RMS_EPS)
        xs = os_ * rs
        dxs = dys * ws
        dws = jnp.sum(dys * xs, axis=0, keepdims=True)
        dos = rs * (dxs - xs * _group_mean(dxs * xs, smat, SB_DH))
        return dog, dgg, dos, dwg, dws
    return _rowcall("mix_out_bwd", fn,
                    [(dcat, GLA_V_W, 0), (dcat, SB_W, 1), (o_g, GLA_V_W, 0), (proj, GLA_V_W, C_GG // GLA_V_W),
                     (o_s, SB_W, 0)],
                    [wg_t, ws_t, grp_g, grp_s], [(GLA_V_W, F32), (GLA_V_W, BF16), (SB_W, F32)],
                    [(1, GLA_V_W), (1, SB_W)], bm, o_g.shape[0])


def _loss_kernel(y, tgt, bm):
    def fn(yb, tb):
        err = yb - tb
        return err * (1.0 / D_MODEL), jnp.sum(err * err, axis=0, keepdims=True)
    return _rowcall("loss_head", fn, [(y, D_MODEL, 0), (tgt, D_MODEL, 0)], [], [(D_MODEL, F32)], [(1, D_MODEL)],
                    bm, y.shape[0])


def _sb_tri(inclusive):
    j, s = _iota2((2 * SB_BLK, 2 * SB_BLK), 0), _iota2((2 * SB_BLK, 2 * SB_BLK), 1)
    j = jnp.where(j >= SB_BLK, j - SB_BLK, j)
    keep = (j >= s) if inclusive else (j > s)
    return ((s >= SB_BLK) | keep).astype(BF16)


def _dot_hilo(x, m2):
    hi, lo = _split(x, 2)
    return _dot(jnp.concatenate([hi, lo], axis=1), m2)


def _sb_mask(n):
    return _iota2((n, SB_BLK), 1) < _iota2((n, SB_BLK), 0)


def _add_rows(full, part, row0):
    if row0 == 0:
        return full + part
    return jnp.concatenate([full[:row0], full[row0:] + part], axis=0)


def _sb_sweep(n_tiles, step, carry, c_slots):
    def alive(state):
        jj, carry = state
        c_max = jnp.max(functools.reduce(jnp.maximum, [carry[s] for s in c_slots]))
        return jnp.logical_and(jj < n_tiles, c_max > SB_DEAD)

    def body(state):
        jj, carry = state
        return jj + 1, step(jj, carry)

    return lax.while_loop(alive, body, (jnp.int32(0), carry))[1]


def _row_blocks(n):
    return [slice(r, r + SB_BLK) for r in range(0, n, SB_BLK)]


def _hilo(x):
    hi, lo = _split(x, 2)
    return jnp.concatenate([hi, lo], axis=1)


def _sb_tile(q, k, c, tri_excl, diag):
    blocks = _row_blocks(q.shape[0])
    strict = _sb_mask(SB_BLK) if diag else None
    z = _dot(q, k, 1, 1)
    lbs, pieces = [], []
    for r, rs in enumerate(blocks):
        lb = _log_sigmoid(z[rs])
        l1 = lb - z[rs]
        if diag and r == 0:
            l1 = jnp.where(strict, l1, 0.0)
        lbs.append(lb)
        pieces.append(_hilo(l1))
    sums = _dot(jnp.concatenate(pieces, axis=0), tri_excl)
    a = []
    for r, rs in enumerate(blocks):
        ar = jnp.exp(lbs[r] + sums[rs, :SB_BLK] + c[rs])
        if diag and r == 0:
            ar = jnp.where(strict, ar, 0.0)
        a.append(ar.astype(BF16))
    return lbs, a, sums[:, SB_BLK:]


def _sb_fwd(proj, rows):
    qt = min(SB_QT, rows)
    subs = qt // SB_BLK

    def body(q_ref, k_ref, v_ref, o_ref):
        i = pl.program_id(1)
        tri_excl = _sb_tri(False)
        heads = [slice(SB_DH * hh, SB_DH * (hh + 1)) for hh in range(2)]
        qs = [(q_ref[:, sl] * 0.125).astype(BF16) for sl in heads]

        def tiles(start, carry, diag, row0=0):
            out = []
            for hh, sl in enumerate(heads):
                o, c = carry[2 * hh], carry[2 * hh + 1]
                k = k_ref[pl.ds(start, SB_BLK), sl].astype(BF16)
                v = v_ref[pl.ds(start, SB_BLK), sl].astype(BF16)
                _, a, dc = _sb_tile(qs[hh][row0:], k, c[row0:], tri_excl, diag)
                out += [_add_rows(o, _dot(jnp.concatenate(a, axis=0), v), row0), _add_rows(c, dc, row0)]
            return tuple(out)

        carry = (jnp.zeros((qt, SB_DH), F32), jnp.zeros((qt, SB_BLK), F32)) * 2
        for sub in reversed(range(subs)):
            start = pl.multiple_of(i * qt + sub * SB_BLK, SB_BLK)
            carry = tiles(start, carry, True, sub * SB_BLK)

        def step(jj, carry):
            start = pl.multiple_of((i * subs - 1 - jj) * SB_BLK, SB_BLK)
            return tiles(start, carry, False)

        carry = _sb_sweep(i * subs, step, carry, (1, 3))
        for hh, sl in enumerate(heads):
            o_ref[:, sl] = carry[2 * hh]

    return pl.pallas_call(
        body, name="sb_attention_fwd", grid=(SB_HEADS // 2, rows // qt),
        in_specs=[pl.BlockSpec((qt, LANES), lambda h, i: (i, C_QS // LANES + h)),
                  pl.BlockSpec((rows, LANES), lambda h, i: (0, C_KS // LANES + h)),
                  pl.BlockSpec((rows, LANES), lambda h, i: (0, C_VS // LANES + h))],
        out_specs=pl.BlockSpec((qt, LANES), lambda h, i: (i, h)),
        out_shape=jax.ShapeDtypeStruct((rows, SB_W), F32),
        compiler_params=_cparams("parallel", "arbitrary"),
    )(proj, proj, proj)


def _sb_bwd(proj, o_s, do_s, rows):
    qt = min(SB_QT, rows)
    subs = qt // SB_BLK

    def body(q_ref, k_ref, v_ref, o_ref, do_ref, dq_ref, dk_ref, dv_ref):
        i = pl.program_id(1)

        @pl.when(i == 0)
        def _():
            dk_ref[...] = jnp.zeros(dk_ref.shape, F32)
            dv_ref[...] = jnp.zeros(dv_ref.shape, F32)

        tri_excl, tri_incl = _sb_tri(False), _sb_tri(True)
        heads = [slice(SB_DH * hh, SB_DH * (hh + 1)) for hh in range(2)]
        qs = [(q_ref[:, sl] * 0.125).astype(BF16) for sl in heads]
        dobs = [do_ref[:, sl].astype(BF16) for sl in heads]
        dsums = [jnp.broadcast_to(jnp.sum(dob.astype(F32) * o_ref[:, sl], axis=1, keepdims=True), (qt, SB_BLK))
                 for dob, sl in zip(dobs, heads)]

        def tiles(start, carry, diag, row0=0):
            out = []
            strict = _sb_mask(SB_BLK) if diag else None
            for hh, sl in enumerate(heads):
                dq, c, cp = carry[3 * hh:3 * hh + 3]
                q, dob = qs[hh][row0:], dobs[hh][row0:]
                dsum, cpr = dsums[hh][row0:], cp[row0:]
                blocks = _row_blocks(q.shape[0])
                k = k_ref[pl.ds(start, SB_BLK), sl].astype(BF16)
                v = v_ref[pl.ds(start, SB_BLK), sl].astype(BF16)
                lbs, a, dc = _sb_tile(q, k, c[row0:], tri_excl, diag)
                da = _dot(dob, v, 1, 1)
                ps = [a[r].astype(F32) * da[rs] for r, rs in enumerate(blocks)]
                psums = _dot(jnp.concatenate([_hilo(p) for p in ps], axis=0), tri_incl)
                dzs = []
                for r, rs in enumerate(blocks):
                    left = dsum[rs] - (psums[rs, :SB_BLK] + cpr[rs])
                    dz = ps[r] - jnp.exp(lbs[r]) * (ps[r] + left)
                    if diag and r == 0:
                        dz = jnp.where(strict, dz, 0.0)
                    dzs.append(dz.astype(BF16))
                dzb, ab = jnp.concatenate(dzs, axis=0), jnp.concatenate(a, axis=0)
                dk_ref[pl.ds(start, SB_BLK), sl] += _dot(dzb, q, 0, 0)
                dv_ref[pl.ds(start, SB_BLK), sl] += _dot(ab, dob, 0, 0)
                out += [_add_rows(dq, _dot(dzb, k), row0), _add_rows(c, dc, row0),
                        _add_rows(cp, psums[:, SB_BLK:], row0)]
            return tuple(out)

        zero = jnp.zeros((qt, SB_BLK), F32)
        carry = (jnp.zeros((qt, SB_DH), F32), zero, zero) * 2
        for sub in reversed(range(subs)):
            start = pl.multiple_of(i * qt + sub * SB_BLK, SB_BLK)
            carry = tiles(start, carry, True, sub * SB_BLK)

        def step(jj, carry):
            start = pl.multiple_of((i * subs - 1 - jj) * SB_BLK, SB_BLK)
            return tiles(start, carry, False)

        carry = _sb_sweep(i * subs, step, carry, (1, 4))
        for hh, sl in enumerate(heads):
            dq_ref[:, sl] = carry[3 * hh] * 0.125

    whole = lambda base: pl.BlockSpec((rows, LANES), functools.partial(lambda h, i, b: (0, b + h), b=base))
    blk = lambda base: pl.BlockSpec((qt, LANES), functools.partial(lambda h, i, b: (i, b + h), b=base))
    return pl.pallas_call(
        body, name="sb_attention_bwd", grid=(SB_HEADS // 2, rows // qt),
        in_specs=[blk(C_QS // LANES), whole(C_KS // LANES), whole(C_VS // LANES), blk(0), blk(0)],
        out_specs=[blk(0), whole(0), whole(0)],
        out_shape=[jax.ShapeDtypeStruct((rows, SB_W), F32)] * 3,
        compiler_params=_cparams("parallel", "arbitrary"),
    )(proj, proj, proj, o_s, do_s)


def _gla_chunk_common(g_all):
    r_i, c_i = _iota2((GLA_CHUNK, GLA_CHUNK), 0), _iota2((GLA_CHUNK, GLA_CHUNK), 1)
    tri = (c_i <= r_i).astype(BF16)
    return _dot_rhs_exact(tri, g_all), r_i, c_i


def _gla_sub(qh, kh, bh, sub):
    rs, nc = GLA_SUB * sub, GLA_SUB * (sub + 1)
    ref = bh[rs:rs + 1, :]
    eq = jnp.exp(bh[rs:rs + GLA_SUB] - ref)
    ek = jnp.exp(jnp.where(_iota2((GLA_CHUNK, GLA_DK), 0) < nc, ref - bh, 0.0))
    mask = _iota2((GLA_SUB, GLA_CHUNK), 1) <= _iota2((GLA_SUB, GLA_CHUNK), 0) + rs
    return rs, eq, ek, qh[rs:rs + GLA_SUB] * eq, kh * ek, mask


def _gla_fwd(proj, gk, rows):
    n_chunks = rows // GLA_CHUNK

    def body(q_ref, k_ref, v_ref, g_ref, o_ref, sall_ref, s_scr):
        @pl.when(pl.program_id(0) == 0)
        def _():
            s_scr[...] = jnp.zeros(s_scr.shape, F32)

        g_all = g_ref[...]
        b_all, _, _ = _gla_chunk_common(g_all)
        ones = jnp.ones((GLA_CHUNK, GLA_DV), BF16)
        for h in range(GLA_HEADS):
            sl = slice(GLA_DK * h, GLA_DK * (h + 1))
            vs = slice(GLA_DV * h, GLA_DV * (h + 1))
            qh, kh, vh = q_ref[:, sl] * 0.125, k_ref[:, sl], v_ref[:, vs]
            bh, gh = b_all[:, sl], g_all[:, sl]
            s = s_scr[h]
            sall_ref[0, h] = s
            o = _dot(qh * jnp.exp(bh), s)
            parts = []
            for sub in range(GLA_CHUNK // GLA_SUB):
                _, _, _, qs, ks, mask = _gla_sub(qh, kh, bh, sub)
                a = jnp.where(mask, _dot(qs, ks, 1, 1), 0.0)
                parts.append(_dot(a, vh))
            o_ref[:, vs] = o + jnp.concatenate(parts, axis=0)
            bl_col = _dot_lhs_exact(gh, ones, 0, 0)
            kd = kh * jnp.exp(bh[GLA_CHUNK - 1:GLA_CHUNK, :] - bh)
            s_scr[h] = jnp.exp(bl_col) * s + _dot(kd, vh, 0, 0)

    c64 = lambda w, base: pl.BlockSpec((GLA_CHUNK, w), functools.partial(lambda n, b: (n, b), b=base))
    return pl.pallas_call(
        body, name="gla_fwd", grid=(n_chunks,),
        in_specs=[c64(GLA_QK_W, C_QG // GLA_QK_W), c64(GLA_QK_W, C_KG // GLA_QK_W), c64(GLA_V_W, C_VG // GLA_V_W),
                  c64(GLA_QK_W, 0)],
        out_specs=[c64(GLA_V_W, 0), pl.BlockSpec((1, GLA_HEADS, GLA_DK, GLA_DV), lambda n: (n, 0, 0, 0))],
        out_shape=[jax.ShapeDtypeStruct((rows, GLA_V_W), F32),
                   jax.ShapeDtypeStruct((n_chunks, GLA_HEADS, GLA_DK, GLA_DV), F32)],
        scratch_shapes=[pltpu.VMEM((GLA_HEADS, GLA_DK, GLA_DV), F32)],
        compiler_params=_cparams("arbitrary"),
    )(proj, proj, proj, gk)


def _gla_bwd(proj, gk, do_g, s_all, rows):
    n_chunks = rows // GLA_CHUNK

    def body(q_ref, k_ref, v_ref, g_ref, do_ref, sall_ref, dq_ref, dk_ref, dv_ref, dg_ref, ds_scr):
        @pl.when(pl.program_id(0) == 0)
        def _():
            ds_scr[...] = jnp.zeros(ds_scr.shape, F32)

        g_all = g_ref[...]
        b_all, r_i, c_i = _gla_chunk_common(g_all)
        triu = (c_i >= r_i).astype(BF16)
        ones = jnp.ones((GLA_CHUNK, GLA_DV), BF16)
        ones8 = jnp.ones((8, GLA_DV), F32)
        last_row = _iota2((GLA_CHUNK, GLA_DK), 0) == GLA_CHUNK - 1
        for h in range(GLA_HEADS):
            sl = slice(GLA_DK * h, GLA_DK * (h + 1))
            vs = slice(GLA_DV * h, GLA_DV * (h + 1))
            qh, kh, vh = q_ref[:, sl] * 0.125, k_ref[:, sl], v_ref[:, vs]
            bh, gh = b_all[:, sl], g_all[:, sl]
            doh = do_ref[:, vs]
            s, ds = sall_ref[0, h], ds_scr[h]
            eb = jnp.exp(bh)
            ekd = jnp.exp(bh[GLA_CHUNK - 1:GLA_CHUNK, :] - bh)
            ebl = jnp.exp(_dot_lhs_exact(gh, ones, 0, 0))
            qb, kd = qh * eb, kh * ekd
            dq = _dot(doh, s, 1, 1) * eb
            dk = _dot(vh, ds, 1, 1) * ekd
            dv = _dot(kd, ds)
            dbl = jnp.sum(dk * kh, axis=0, keepdims=True) + _dot3(ones8, ebl * s * ds, 1, 1)[0:1, :]
            dq_parts = []
            for sub in range(GLA_CHUNK // GLA_SUB):
                rs, eq, ek, qs, ks, mask = _gla_sub(qh, kh, bh, sub)
                do_sub = doh[rs:rs + GLA_SUB]
                a = jnp.where(mask, _dot(qs, ks, 1, 1), 0.0)
                da = jnp.where(mask, _dot(do_sub, vh, 1, 1), 0.0)
                dq_parts.append(_dot(da, ks) * eq)
                dk = dk + _dot(da, qs, 0, 0) * ek
                dv = dv + _dot(a, do_sub, 0, 0)
            dq = dq + jnp.concatenate(dq_parts, axis=0)
            db = qh * dq - kh * dk + jnp.where(last_row, dbl, 0.0)
            dq_ref[:, sl] = dq * 0.125
            dk_ref[:, sl] = dk
            dv_ref[:, vs] = dv
            dg_ref[:, sl] = _dot_rhs_exact(triu, db)
            ds_scr[h] = _dot(qb, doh, 0, 0) + ebl * ds

    last = n_chunks - 1
    c64 = lambda w, base: pl.BlockSpec((GLA_CHUNK, w), functools.partial(lambda n, b: (last - n, b), b=base))
    return pl.pallas_call(
        body, name="gla_bwd", grid=(n_chunks,),
        in_specs=[c64(GLA_QK_W, C_QG // GLA_QK_W), c64(GLA_QK_W, C_KG // GLA_QK_W), c64(GLA_V_W, C_VG // GLA_V_W),
                  c64(GLA_QK_W, 0), c64(GLA_V_W, 0),
                  pl.BlockSpec((1, GLA_HEADS, GLA_DK, GLA_DV), lambda n: (last - n, 0, 0, 0))],
        out_specs=[c64(GLA_QK_W, 0), c64(GLA_QK_W, 0), c64(GLA_V_W, 0), c64(GLA_QK_W, 0)],
        out_shape=[jax.ShapeDtypeStruct((rows, GLA_QK_W), F32), jax.ShapeDtypeStruct((rows, GLA_QK_W), F32),
                   jax.ShapeDtypeStruct((rows, GLA_V_W), F32), jax.ShapeDtypeStruct((rows, GLA_QK_W), F32)],
        scratch_shapes=[pltpu.VMEM((GLA_HEADS, GLA_DK, GLA_DV), F32)],
        compiler_params=_cparams("arbitrary"),
    )(proj, proj, proj, gk, do_g, s_all)


def _mem_kv_fwd(mem, mem_norm_w, w_mkv, mk_norm_w):
    def body(mem_ref, mw_ref, w_ref, kw_ref, memn_ref, kpre_ref, kn_ref, v_ref):
        xb = mem_ref[...]
        r = lax.rsqrt(jnp.mean(xb * xb, axis=-1, keepdims=True) + RMS_EPS)
        mem_n = (xb * r * mw_ref[...]).astype(BF16)
        memn_ref[...] = mem_n
        kv = _dot(mem_n, w_ref[...])
        kpre_ref[...] = kv[:, :D_MODEL]
        v_ref[...] = kv[:, D_MODEL:].astype(BF16)
        for h in range(MEM_HEADS):
            sl = slice(MEM_DH * h, MEM_DH * (h + 1))
            kh = kv[:, sl]
            rk = lax.rsqrt(jnp.mean(kh * kh, axis=-1, keepdims=True) + RMS_EPS)
            kn_ref[:, sl] = (kh * rk * kw_ref[...]).astype(BF16)

    return pl.pallas_call(
        body, name="mem_kv_fwd",
        out_shape=[jax.ShapeDtypeStruct((MEM_LEN, D_MODEL), BF16), jax.ShapeDtypeStruct((MEM_LEN, D_MODEL), F32),
                   jax.ShapeDtypeStruct((MEM_LEN, D_MODEL), BF16), jax.ShapeDtypeStruct((MEM_LEN, D_MODEL), BF16)],
        compiler_params=_cparams(),
    )(mem, mem_norm_w, w_mkv, mk_norm_w)


def _mem_kv_bwd(mem, mem_norm_w, w_mkv, mk_norm_w, mem_n, kpre, dkn, dv):
    def body(mem_ref, mw_ref, w_ref, kw_ref, memn_ref, kpre_ref, dkn_ref, dv_ref, dw_ref, dkw_ref, dmw_ref):
        dkw = jnp.zeros((1, MEM_DH), F32)
        dk_parts = []
        for h in range(MEM_HEADS):
            sl = slice(MEM_DH * h, MEM_DH * (h + 1))
            kh, dkh = kpre_ref[:, sl], dkn_ref[:, sl]
            rk = lax.rsqrt(jnp.mean(kh * kh, axis=-1, keepdims=True) + RMS_EPS)
            xh = kh * rk
            dxh = dkh * kw_ref[...]
            dkw = dkw + jnp.sum(dkh * xh, axis=0, keepdims=True)
            dk_parts.append(rk * (dxh - xh * jnp.mean(dxh * xh, axis=-1, keepdims=True)))
        dkw_ref[...] = dkw
        dkv = jnp.concatenate(dk_parts + [dv_ref[...]], axis=1).astype(BF16)
        dw_ref[...] = _dot(memn_ref[...], dkv, 0, 0)
        dmem_n = _dot(dkv, w_ref[...], 1, 1)
        xb = mem_ref[...]
        r = lax.rsqrt(jnp.mean(xb * xb, axis=-1, keepdims=True) + RMS_EPS)
        dmw_ref[...] = jnp.sum(dmem_n * (xb * r), axis=0, keepdims=True)

    return pl.pallas_call(
        body, name="mem_kv_bwd",
        out_shape=[jax.ShapeDtypeStruct((D_MODEL, 2 * D_MODEL), F32), jax.ShapeDtypeStruct((1, MEM_DH), F32),
                   jax.ShapeDtypeStruct((1, D_MODEL), F32)],
        compiler_params=_cparams(),
    )(mem, mem_norm_w, w_mkv, mk_norm_w, mem_n, kpre, dkn, dv)


def _xattn_head(qh, kn_h, qw):
    rq = lax.rsqrt(jnp.mean(qh * qh, axis=-1, keepdims=True) + RMS_EPS)
    xh = qh * rq
    qn = (xh * qw).astype(BF16)
    s = _dot(qn, kn_h, 1, 1) * (1.0 / 16.0)
    e = jnp.exp(s - jnp.max(s, axis=-1, keepdims=True))
    p = e / jnp.sum(e, axis=-1, keepdims=True)
    return rq, xh, qn, p


def _xattn_fwd(qm, kn, v, mq_norm_w, bm):
    def fn(qb, knb, vb, qw):
        outs = []
        for h in range(MEM_HEADS):
            sl = slice(MEM_DH * h, MEM_DH * (h + 1))
            _, _, _, p = _xattn_head(qb[:, sl], knb[:, sl], qw)
            outs.append(_dot(p, vb[:, sl]))
        return (jnp.concatenate(outs, axis=1),)
    return _rowcall("xattn_fwd", fn, [(qm, D_MODEL, 0)], [kn, v, mq_norm_w], [(D_MODEL, BF16)], [], bm,
                    qm.shape[0])[0]


def _xattn_bwd(qm, kn, v, mq_norm_w, do, bm):
    def fn(qb, dob, knb, vb, qw):
        dq_parts, dkn_parts, dv_parts = [], [], []
        dqw = jnp.zeros((1, MEM_DH), F32)
        for h in range(MEM_HEADS):
            sl = slice(MEM_DH * h, MEM_DH * (h + 1))
            rq, xh, qn, p = _xattn_head(qb[:, sl], knb[:, sl], qw)
            doh = dob[:, sl].astype(BF16)
            dp = _dot(doh, vb[:, sl], 1, 1)
            ds = (p * (dp - jnp.sum(dp * p, axis=-1, keepdims=True)) * (1.0 / 16.0)).astype(BF16)
            dqn = _dot(ds, knb[:, sl])
            dkn_parts.append(_dot(ds, qn, 0, 0))
            dv_parts.append(_dot(p, doh, 0, 0))
            dqw = dqw + jnp.sum(dqn * xh, axis=0, keepdims=True)
            dxh = dqn * qw
            dq_parts.append(rq * (dxh - xh * jnp.mean(dxh * xh, axis=-1, keepdims=True)))
        return (jnp.concatenate(dq_parts, axis=1), jnp.concatenate(dkn_parts, axis=1),
                jnp.concatenate(dv_parts, axis=1), dqw)
    return _rowcall("xattn_bwd", fn, [(qm, D_MODEL, 0), (do, D_MODEL, 0)], [kn, v, mq_norm_w],
                    [(D_MODEL, BF16)], [(MEM_LEN, D_MODEL), (MEM_LEN, D_MODEL), (1, MEM_DH)], bm, qm.shape[0])


FF_BN = 1408
FF_NB = D_FF // FF_BN


def _ffn_up(h3, w_gate_up, rows, bm):
    def body(h_ref, wg_ref, wu_ref, gate_ref, up_ref, act_ref):
        hb = h_ref[...]
        gate = _dot(hb, wg_ref[...])
        up = _dot(hb, wu_ref[...])
        gate_ref[...] = gate
        up_ref[...] = up
        act_ref[...] = (gate * _sigmoid(gate) * up).astype(BF16)

    out_blk = pl.BlockSpec((bm, FF_BN), lambda i, j: (i, j))
    return pl.pallas_call(
        body, name="ffn_up", grid=(rows // bm, FF_NB),
        in_specs=[pl.BlockSpec((bm, D_MODEL), lambda i, j: (i, 0)),
                  pl.BlockSpec((D_MODEL, FF_BN), lambda i, j: (0, j)),
                  pl.BlockSpec((D_MODEL, FF_BN), lambda i, j: (0, FF_NB + j))],
        out_specs=[out_blk, out_blk, out_blk],
        out_shape=[jax.ShapeDtypeStruct((rows, D_FF), F32), jax.ShapeDtypeStruct((rows, D_FF), F32),
                   jax.ShapeDtypeStruct((rows, D_FF), BF16)],
        compiler_params=_cparams("parallel", "arbitrary"),
    )(h3, w_gate_up, w_gate_up)


def _ffn_act_bwd(dy, w_down, gate, up, rows, bm):
    def body(dy_ref, wd_ref, gate_ref, up_ref, o_ref):
        dact = _dot(dy_ref[...], wd_ref[...], 1, 1)
        g, u = gate_ref[...], up_ref[...]
        sg = _sigmoid(g)
        o_ref[0] = (dact * u * (sg * (1.0 + g * (1.0 - sg)))).astype(BF16)
        o_ref[1] = (dact * (g * sg)).astype(BF16)

    blk = pl.BlockSpec((bm, FF_BN), lambda i, j: (i, j))
    return pl.pallas_call(
        body, name="ffn_act_bwd", grid=(rows // bm, FF_NB),
        in_specs=[pl.BlockSpec((bm, D_MODEL), lambda i, j: (i, 0)),
                  pl.BlockSpec((FF_BN, D_MODEL), lambda i, j: (j, 0)), blk, blk],
        out_specs=pl.BlockSpec((2, bm, FF_BN), lambda i, j: (0, i, j)),
        out_shape=jax.ShapeDtypeStruct((2, rows, D_FF), BF16),
        compiler_params=_cparams("parallel", "arbitrary"),
    )(dy, w_down, gate, up)


def _pack_rows(a):
    flat = a.reshape(-1)
    pad = (-flat.shape[0]) % 1024
    if pad:
        flat = jnp.concatenate([flat, jnp.zeros((pad,), flat.dtype)])
    return flat.reshape(-1, 1024)


def _pack_owner_rows(a8):
    flat = a8.reshape(N_DEV, -1)
    pad = (-flat.shape[1]) % 1024
    if pad:
        flat = jnp.concatenate([flat, jnp.zeros((N_DEV, pad), flat.dtype)], axis=1)
    return flat.reshape(N_DEV, -1, 1024)


def _pack_shards(shards, extra_rows, total_rows, dtype):
    parts = [_pack_rows(shards[n].astype(dtype)) for n in SHARDED] + [r.astype(dtype) for r in extra_rows]
    used = sum(p.shape[0] for p in parts)
    parts.append(jnp.zeros((total_rows - used, 1024), dtype))
    return jnp.concatenate(parts, axis=0)


def _shard_offsets():
    offs, o = {}, 0
    for n in SHARDED:
        offs[n] = o
        o += SHARD_ROWS[n]
    return offs, o


def _unpack_shard(pack, name):
    offs, _ = _shard_offsets()
    r, c = SHARD_SHAPE[name]
    seg = pack[offs[name]:offs[name] + SHARD_ROWS[name]].reshape(-1)[:r * c]
    return seg.reshape(1, r, c)


def _gathered_weight(gathered, name):
    offs, _ = _shard_offsets()
    r, c = SHARD_SHAPE[name]
    seg = gathered[:, offs[name]:offs[name] + SHARD_ROWS[name]].reshape(N_DEV, -1)[:, :r * c].reshape(N_DEV, r, c)
    if name in ("w_out", "w_mq", "w_mo", "w_down"):
        return seg.reshape(N_DEV * r, c)
    return seg.transpose(1, 0, 2).reshape(r, N_DEV * c)


def _split_for_owners(name, full):
    r, c = SHARD_SHAPE[name]
    if name in ("w_out", "w_mq", "w_mo", "w_down"):
        return full.reshape(N_DEV, r, c)
    return full.reshape(r, N_DEV, c).transpose(1, 0, 2)


def _repl_row(a):
    flat = a.reshape(-1)
    return jnp.concatenate([flat, jnp.zeros((1024 - flat.shape[0],), flat.dtype)]).reshape(1, 1024)


def _local_step(x, mem, tgt, wf, rp):
    rows = x.shape[0]
    bm = min(512, rows)
    bmx = min(256, rows)
    mt = min(MM_TILE, rows)
    kt = min(512, rows)
    w_cat, wgk_pad = wf["w_cat"], wf["wgk_pad"]
    wg_t = jnp.tile(rp["gla_norm_w"], (1, GLA_HEADS))
    ws_t = jnp.tile(rp["sb_norm_w"], (1, SB_HEADS))
    lane = jnp.arange(GLA_V_W)
    grp_g = (lane[:, None] // GLA_DV == lane[None, :] // GLA_DV).astype(BF16)
    grp_s = (lane[:, None] // SB_DH == lane[None, :] // SB_DH).astype(BF16)

    h1 = _rms_fwd("mix_norm_fwd", x, rp["mix_norm_w"], bm)
    proj = _matmul("in_proj", h1, w_cat, "nn", rows, PROJ_W, D_MODEL, F32, mt, 640, D_MODEL)
    gk = _gate_fwd(proj, wgk_pad, rp["b_gk"], bm)
    o_g, s_all = _gla_fwd(proj, gk, rows)
    o_s = _sb_fwd(proj, rows)
    cat = _mix_out_fwd(o_g, proj, o_s, wg_t, ws_t, grp_g, grp_s, bm)
    x1 = _matmul("out_proj", cat, wf["w_out"], "nn", rows, D_MODEL, D_MODEL, F32, mt, MM_TILE, D_MODEL, residual=x)
    h2 = _rms_fwd("xattn_norm_fwd", x1, rp["xattn_norm_w"], bm)
    qm = _matmul("mq_proj", h2, wf["w_mq"], "nn", rows, D_MODEL, D_MODEL, F32, mt, MM_TILE, D_MODEL)
    mem_n, kpre, kn, v_m = _mem_kv_fwd(mem, rp["mem_norm_w"], wf["w_mkv"], rp["mk_norm_w"])
    o_m = _xattn_fwd(qm, kn, v_m, rp["mq_norm_w"], bmx)
    x2 = _matmul("mo_proj", o_m, wf["w_mo"], "nn", rows, D_MODEL, D_MODEL, F32, mt, MM_TILE, D_MODEL, residual=x1)
    h3 = _rms_fwd("ffn_norm_fwd", x2, rp["ffn_norm_w"], bm)
    gate, up, act = _ffn_up(h3, wf["w_gate_up"], rows, bm)
    y = _matmul("ffn_down", act, wf["w_down"], "nn", rows, D_MODEL, D_FF, F32, mt, MM_TILE, FF_BN, residual=x2)
    dy, sq = _loss_kernel(y, tgt, bm)

    g = {}
    dgu = _ffn_act_bwd(dy, wf["w_down"], gate, up, rows, bm)
    g["w_down"] = _matmul("grad_w_down", act, dy, "tn", D_FF, D_MODEL, rows, F32, FF_BN, MM_TILE, kt)
    nkb = FF_NB
    dh3 = _matmul("ffn_up_bwd", dgu, wf["w_gate_up"], "nt", rows, D_MODEL, 2 * D_FF, F32, mt, MM_TILE, FF_BN,
                  a_spec=pl.BlockSpec((None, mt, FF_BN), lambda i, j, kk: (kk // nkb, i, kk % nkb)))
    g["w_gate_up"] = _matmul(
        "grad_w_gate_up", h3, dgu, "tn", D_MODEL, 2 * D_FF, rows, F32, MM_TILE, FF_BN, kt,
        b_spec=pl.BlockSpec((None, kt, FF_BN), lambda i, j, kk: (j // nkb, kk, j % nkb)))
    dx2, g["ffn_norm_w"] = _rms_bwd("ffn_norm_bwd", x2, rp["ffn_norm_w"], dh3, dy, bm)

    do_m = _matmul("mo_proj_bwd", dx2, wf["w_mo"], "nt", rows, D_MODEL, D_MODEL, BF16, mt, MM_TILE, D_MODEL)
    g["w_mo"] = _matmul("grad_w_mo", o_m, dx2, "tn", D_MODEL, D_MODEL, rows, F32, MM_TILE, MM_TILE, kt)
    dqm, dkn, dv_m, g["mq_norm_w"] = _xattn_bwd(qm, kn, v_m, rp["mq_norm_w"], do_m, bmx)
    g["w_mkv"], g["mk_norm_w"], g["mem_norm_w"] = _mem_kv_bwd(
        mem, rp["mem_norm_w"], wf["w_mkv"], rp["mk_norm_w"], mem_n, kpre, dkn, dv_m)
    dh2 = _matmul("mq_proj_bwd", dqm, wf["w_mq"], "nt", rows, D_MODEL, D_MODEL, F32, mt, MM_TILE, D_MODEL)
    g["w_mq"] = _matmul("grad_w_mq", h2, dqm, "tn", D_MODEL, D_MODEL, rows, F32, MM_TILE, MM_TILE, kt)
    dx1, g["xattn_norm_w"] = _rms_bwd("xattn_norm_bwd", x1, rp["xattn_norm_w"], dh2, dx2, bm)

    dcat = _matmul("out_proj_bwd", dx1, wf["w_out"], "nt", rows, D_MODEL, D_MODEL, F32, mt, MM_TILE, D_MODEL)
    g["w_out"] = _matmul("grad_w_out", cat, dx1, "tn", D_MODEL, D_MODEL, rows, F32, MM_TILE, MM_TILE, kt)
    do_g, dgg, do_s, dwg, dws = _mix_out_bwd(dcat, o_g, proj, o_s, wg_t, ws_t, grp_g, grp_s, bm)
    g["gla_norm_w"] = dwg.reshape(GLA_HEADS, GLA_DV).sum(axis=0, keepdims=True)
    g["sb_norm_w"] = dws.reshape(SB_HEADS, SB_DH).sum(axis=0, keepdims=True)
    dq_s, dk_s, dv_s = _sb_bwd(proj, o_s, do_s, rows)
    dq_g, dk_g, dv_g, dgk = _gla_bwd(proj, gk, do_g, s_all, rows)
    dlr, dwgk, g["b_gk"] = _gate_bwd(proj, wgk_pad, rp["b_gk"], dgk, bm)
    g["w_gk_up"] = dwgk[:GATE_RANK]
    dproj = jnp.concatenate([dq_g.astype(BF16), dk_g.astype(BF16), dv_g.astype(BF16), dgg, dq_s.astype(BF16),
                             dk_s.astype(BF16), dv_s.astype(BF16), dlr], axis=1)
    dh1 = _matmul("in_proj_bwd", dproj, w_cat, "nt", rows, D_MODEL, PROJ_W, F32, mt, MM_TILE, 640)
    dw_cat = _matmul("grad_w_in", h1, dproj, "tn", D_MODEL, PROJ_W, rows, F32, MM_TILE, 640, kt)
    g["w_in"] = jnp.concatenate([dw_cat[:, :C_QS], dw_cat[:, C_LR:C_LR + GATE_RANK], dw_cat[:, C_QS:C_LR]], axis=1)
    dx, g["mix_norm_w"] = _rms_bwd("mix_norm_bwd", x, rp["mix_norm_w"], dh1, dx1, bm)
    return sq, dx, g


def _full_weights(gathered):
    w_in = _gathered_weight(gathered, "w_in")
    lr_end = C_QS + GATE_RANK
    w_cat = jnp.concatenate([w_in[:, :C_QS], w_in[:, lr_end:], w_in[:, C_QS:lr_end],
                             jnp.zeros((D_MODEL, PROJ_W - D_IN), BF16)], axis=1)
    wgk = _gathered_weight(gathered, "w_gk_up")
    wf = {n: _gathered_weight(gathered, n) for n in ("w_out", "w_mq", "w_mkv", "w_mo", "w_gate_up", "w_down")}
    wf["w_cat"] = w_cat
    wf["wgk_pad"] = jnp.concatenate([wgk, jnp.zeros((LANES - GATE_RANK, GLA_QK_W), BF16)], axis=0)
    return wf


def kernel(x, mem, mix_norm_w, w_in, w_gk_up, b_gk, gla_norm_w, sb_norm_w, w_out, xattn_norm_w, mem_norm_w, w_mq, w_mkv, mq_norm_w, mk_norm_w, w_mo, ffn_norm_w, w_gate_up, w_down, loss_target, m_mix_norm_w, m_w_in, m_w_gk_up, m_b_gk, m_gla_norm_w, m_sb_norm_w, m_w_out, m_xattn_norm_w, m_mem_norm_w, m_w_mq, m_w_mkv, m_mq_norm_w, m_mk_norm_w, m_w_mo, m_ffn_norm_w, m_w_gate_up, m_w_down, v_mix_norm_w, v_w_in, v_w_gk_up, v_b_gk, v_gla_norm_w, v_sb_norm_w, v_w_out, v_xattn_norm_w, v_mem_norm_w, v_w_mq, v_w_mkv, v_mq_norm_w, v_mk_norm_w, v_w_mo, v_ffn_norm_w, v_w_gate_up, v_w_down):
    given = dict(locals())
    w = {n: given[n][0] for n in WEIGHTS}
    m = {n: given["m_" + n][0] for n in WEIGHTS}
    v = {n: given["v_" + n][0] for n in WEIGHTS}

    gathered = _all_gather(_pack_shards(w, [], AG_ROWS, BF16))
    wf = _full_weights(gathered)
    rp = {n: w[n].reshape(1, -1) for n in REPL}
    sq, dx, g = _local_step(x[0], mem[0], loss_target[0], wf, rp)

    loss_row = _repl_row(jnp.sum(sq).reshape(1) * (0.5 / D_MODEL))
    repl_rows = [_repl_row(g[n]) for n in REPL] + [loss_row]
    parts = [_pack_owner_rows(_split_for_owners(n, g[n])) for n in SHARDED]
    parts.append(jnp.broadcast_to(jnp.concatenate(repl_rows, axis=0)[None], (N_DEV, len(repl_rows), 1024)))
    used = sum(p.shape[1] for p in parts)
    parts.append(jnp.zeros((N_DEV, RS_ROWS - used, 1024), F32))
    gpack = jnp.concatenate(parts, axis=1)
    recv = _scatter_to_owners(gpack)

    zero_row = jnp.zeros((1, 1024), F32)
    packs = [_pack_shards(t, [_repl_row(t[n]) for n in REPL] + [zero_row], RS_ROWS, F32) for t in (w, m, v)]
    out_packs = _adamw(recv, *packs)

    _, n_shard_rows = _shard_offsets()

    def unpack(pack, name):
        if name in SHARDED:
            return _unpack_shard(pack, name)
        row = pack[n_shard_rows + REPL.index(name)]
        return row[:w[name].shape[-1]].reshape(1, -1)

    loss = out_packs[0][n_shard_rows + len(REPL), 0]
    outs = [loss, dx[None]]
    for pack in out_packs:
        outs += [unpack(pack, n) for n in WEIGHTS]
    return tuple(outs)
```

```python
import functools
import math

import jax
import jax.numpy as jnp
from jax import lax
from jax.experimental import pallas as pl
from jax.experimental.pallas import tpu as pltpu

F32 = jnp.float32
BF16 = jnp.bfloat16

N_DEV = 8
D_MODEL = 1024
GLA_HEADS = 4
GLA_DK = 64
GLA_DV = 128
GLA_CHUNK = 64
GLA_SUB = 16
GLA_QK_W = GLA_HEADS * GLA_DK
GLA_V_W = GLA_HEADS * GLA_DV
GATE_RANK = 16
SB_HEADS = 8
SB_DH = 64
SB_W = SB_HEADS * SB_DH
SB_BLK = 128
SB_QT = 1024
SB_DEAD = -104.0
MEM_LEN = 256
MEM_HEADS = 4
MEM_DH = 256
D_FF = 2816
D_IN = 3088
RMS_EPS = 1e-6
LANES = 128

PROJ_W = 3200
C_QG, C_KG, C_VG, C_GG, C_QS, C_KS, C_VS, C_LR = 0, 256, 512, 1024, 1536, 2048, 2560, 3072

ADAM_LR, ADAM_B1, ADAM_B2, ADAM_EPS, ADAM_WD, ADAM_STEP = 0.001, 0.9, 0.999, 1e-08, 0.01, 10

SHARD_SHAPE = {"w_in": (1024, 386), "w_out": (128, 1024), "w_mq": (128, 1024), "w_mkv": (1024, 256),
               "w_mo": (128, 1024), "w_gate_up": (1024, 704), "w_down": (352, 1024), "w_gk_up": (16, 32)}
SHARD_ROWS = {n: -(-(s[0] * s[1]) // 1024) for n, s in SHARD_SHAPE.items()}
REPL = ("mix_norm_w", "b_gk", "gla_norm_w", "sb_norm_w", "xattn_norm_w", "mem_norm_w", "mq_norm_w",
        "mk_norm_w", "ffn_norm_w")
WEIGHTS = ("mix_norm_w", "w_in", "w_gk_up", "b_gk", "gla_norm_w", "sb_norm_w", "w_out", "xattn_norm_w",
           "mem_norm_w", "w_mq", "w_mkv", "mq_norm_w", "mk_norm_w", "w_mo", "ffn_norm_w", "w_gate_up", "w_down")
VMEM_LIMIT = 56 * 1024 * 1024
MM_TILE = 1024


def _cparams(*sem):
    return pltpu.CompilerParams(dimension_semantics=sem if sem else None, vmem_limit_bytes=VMEM_LIMIT)


def _dot(a, b, ca=1, cb=0):
    return lax.dot_general(a.astype(BF16), b.astype(BF16), (((ca,), (cb,)), ((), ())),
                           preferred_element_type=F32)


def _split(x, parts):
    out = []
    for _ in range(parts - 1):
        hi = x.astype(BF16)
        out.append(hi)
        x = x - hi.astype(F32)
    out.append(x.astype(BF16))
    return out


def _dot_lhs_exact(x, m, ca=1, cb=0, parts=3):
    acc = None
    for p in _split(x, parts):
        t = _dot(p, m, ca, cb)
        acc = t if acc is None else acc + t
    return acc


def _dot_rhs_exact(m, x, ca=1, cb=0, parts=3):
    acc = None
    for p in _split(x, parts):
        t = _dot(m, p, ca, cb)
        acc = t if acc is None else acc + t
    return acc


def _dot3(a, b, ca=1, cb=0):
    a_hi, a_lo = _split(a, 2)
    b_hi, b_lo = _split(b, 2)
    return _dot(a_hi, b_hi, ca, cb) + (_dot(a_hi, b_lo, ca, cb) + _dot(a_lo, b_hi, ca, cb))


def _log_sigmoid(z):
    return jnp.minimum(z, 0.0) - jnp.log(1.0 + jnp.exp(-jnp.abs(z)))


def _sigmoid(z):
    e = jnp.exp(-jnp.abs(z))
    return jnp.where(z >= 0, 1.0, e) / (1.0 + e)


def _iota2(shape, dim):
    return lax.broadcasted_iota(jnp.int32, shape, dim)


def _rowcall(name, fn, row_ins, full_ins, row_outs, acc_outs, bm, rows):
    n_in = len(row_ins) + len(full_ins)
    n_row = len(row_outs)

    def body(*refs):
        ins, outs = refs[:n_in], refs[n_in:]
        res = fn(*[r[...] for r in ins])
        for r, v in zip(outs[:n_row], res[:n_row]):
            r[...] = v.astype(r.dtype)
        first = pl.program_id(0) == 0
        for r, v in zip(outs[n_row:], res[n_row:]):
            def init(r=r):
                r[...] = jnp.zeros(r.shape, r.dtype)
            pl.when(first)(init)
            r[...] += v

    in_specs = [pl.BlockSpec((bm, w), functools.partial(lambda i, c: (i, c), c=c)) for _, w, c in row_ins]
    in_specs += [pl.BlockSpec(a.shape, lambda i: (0, 0)) for a in full_ins]
    out_specs = [pl.BlockSpec((bm, w), lambda i: (i, 0)) for w, _ in row_outs]
    out_specs += [pl.BlockSpec(s, lambda i: (0, 0)) for s in acc_outs]
    out_shape = [jax.ShapeDtypeStruct((rows, w), dt) for w, dt in row_outs]
    out_shape += [jax.ShapeDtypeStruct(s, F32) for s in acc_outs]
    return pl.pallas_call(
        body, name=name, grid=(rows // bm,), in_specs=in_specs, out_specs=out_specs, out_shape=out_shape,
        compiler_params=_cparams("arbitrary"),
    )(*[a for a, _, _ in row_ins], *full_ins)


def _matmul(name, a, b, mode, m, n, k, out_dtype, bm, bn, bk, residual=None, a_spec=None, b_spec=None):
    bm, bn, bk = min(bm, m), min(bn, n), min(bk, k)
    nk = k // bk
    ca, cb = {"nn": (1, 0), "nt": (1, 1), "tn": (0, 0)}[mode]
    if a_spec is None:
        a_spec = (pl.BlockSpec((bk, bm), lambda i, j, kk: (kk, i)) if mode == "tn"
                  else pl.BlockSpec((bm, bk), lambda i, j, kk: (i, kk)))
    if b_spec is None:
        b_spec = (pl.BlockSpec((bn, bk), lambda i, j, kk: (j, kk)) if mode == "nt"
                  else pl.BlockSpec((bk, bn), lambda i, j, kk: (kk, j)))
    has_res = residual is not None

    def body(*refs):
        a_ref, b_ref = refs[0], refs[1]
        res_ref = refs[2] if has_res else None
        o_ref = refs[2 + has_res]
        part = _dot(a_ref[...], b_ref[...], ca, cb)

        def finish(total):
            if has_res:
                total = total + res_ref[...]
            o_ref[...] = total.astype(o_ref.dtype)

        if nk == 1:
            finish(part)
        else:
            acc_ref = refs[3 + has_res]
            kk = pl.program_id(2)

            @pl.when(kk == 0)
            def _():
                acc_ref[...] = part

            @pl.when(kk > 0)
            def _():
                acc_ref[...] += part

            @pl.when(kk == nk - 1)
            def _():
                finish(acc_ref[...])

    in_specs = [a_spec, b_spec]
    args = [a, b]
    if has_res:
        in_specs.append(pl.BlockSpec((bm, bn), lambda i, j, kk: (i, j)))
        args.append(residual)
    return pl.pallas_call(
        body, name=name, grid=(m // bm, n // bn, nk), in_specs=in_specs,
        out_specs=pl.BlockSpec((bm, bn), lambda i, j, kk: (i, j)),
        out_shape=jax.ShapeDtypeStruct((m, n), out_dtype),
        scratch_shapes=[pltpu.VMEM((bm, bn), F32)] if nk > 1 else [],
        compiler_params=_cparams("parallel", "parallel", "arbitrary"),
    )(*args)


def _peer(mask):
    x, y, c = lax.axis_index("x"), lax.axis_index("y"), lax.axis_index("c")
    mx, my, mc = (mask >> 2) & 1, (mask >> 1) & 1, mask & 1
    px, py, pc = (1 - x if mx else x), (1 - y if my else y), (1 - c if mc else c)
    return (px, py, pc), 4 * px + 2 * py + pc


def _my_index():
    return 4 * lax.axis_index("x") + 2 * lax.axis_index("y") + lax.axis_index("c")


class _Exchange:
    def __init__(self, src, scatter):
        self.src, self.scatter = src, scatter
        self.in_spec = pl.BlockSpec(memory_space=pl.ANY)
        self.out_spec = pl.BlockSpec(memory_space=pl.ANY)
        self.out_shape = jax.ShapeDtypeStruct((N_DEV, src.shape[-2], 1024), src.dtype)
        self.scratch = [pltpu.SemaphoreType.DMA((N_DEV - 1,)), pltpu.SemaphoreType.DMA((N_DEV - 1,)),
                        pltpu.SemaphoreType.DMA(())]

    def _copies(self, src_ref, out_ref, sems):
        send_sems, recv_sems, local_sem = sems
        me = _my_index()
        copies = [pltpu.make_async_copy(src_ref.at[me] if self.scatter else src_ref, out_ref.at[me], local_sem)]
        for mask in range(1, N_DEV):
            peer, peer_index = _peer(mask)
            copies.append(pltpu.make_async_remote_copy(
                src_ref=src_ref.at[peer_index] if self.scatter else src_ref, dst_ref=out_ref.at[me],
                send_sem=send_sems.at[mask - 1], recv_sem=recv_sems.at[mask - 1],
                device_id=peer, device_id_type=pl.DeviceIdType.MESH))
        return copies

    def start(self, src_ref, out_ref, sems):
        for cp in self._copies(src_ref, out_ref, sems):
            cp.start()

    def wait(self, src_ref, out_ref, sems):
        for cp in self._copies(src_ref, out_ref, sems):
            cp.wait()


def _hosted_call(body, ex, grid, in_specs, out_specs, out_shape, scratch_shapes, args, **kw):
    if ex is None:
        return pl.pallas_call(body, grid=grid, in_specs=in_specs, out_specs=out_specs, out_shape=out_shape,
                              scratch_shapes=scratch_shapes, **kw)(*args)
    n_in, n_out, n_scr = len(in_specs), len(out_specs), len(scratch_shapes)

    def hosted(*refs):
        ins, src_ref = refs[:n_in], refs[n_in]
        outs, out_ref = refs[n_in + 1:n_in + 1 + n_out], refs[n_in + 1 + n_out]
        scr, sems = refs[n_in + 2 + n_out:n_in + 2 + n_out + n_scr], refs[n_in + 2 + n_out + n_scr:]
        ids = [pl.program_id(a) for a in range(len(grid))]
        first = functools.reduce(jnp.logical_and, [p == 0 for p in ids])
        last = functools.reduce(jnp.logical_and, [p == n - 1 for p, n in zip(ids, grid)])

        @pl.when(first)
        def _():
            ex.start(src_ref, out_ref, sems)

        body(*ins, *outs, *scr)

        @pl.when(last)
        def _():
            ex.wait(src_ref, out_ref, sems)

    res = pl.pallas_call(
        hosted, grid=grid, in_specs=list(in_specs) + [ex.in_spec], out_specs=list(out_specs) + [ex.out_spec],
        out_shape=list(out_shape) + [ex.out_shape], scratch_shapes=list(scratch_shapes) + ex.scratch, **kw,
    )(*args, ex.src)
    return res


def _exchange_call(name, src, scatter):
    ex = _Exchange(src, scatter)

    def body(src_ref, out_ref, *sems):
        ex.start(src_ref, out_ref, sems)
        ex.wait(src_ref, out_ref, sems)

    return pl.pallas_call(
        body, name=name, in_specs=[ex.in_spec], out_specs=ex.out_spec, out_shape=ex.out_shape,
        scratch_shapes=ex.scratch, compiler_params=pltpu.CompilerParams(has_side_effects=True),
    )(src)


def _adamw(name, recv, w, m, v, block):
    rows = w.shape[0]
    c1 = 1.0 - ADAM_B1 ** ADAM_STEP
    c2 = 1.0 - ADAM_B2 ** ADAM_STEP

    def body(r_ref, w_ref, m_ref, v_ref, g_out, d_out, m_out, v_out):
        g = r_ref[0]
        for s in range(1, N_DEV):
            g = g + r_ref[s]
        m_new = ADAM_B1 * m_ref[...] + (1.0 - ADAM_B1) * g
        v_new = ADAM_B2 * v_ref[...] + (1.0 - ADAM_B2) * (g * g)
        m_hat = m_new / c1
        v_hat = v_new / c2
        g_out[...] = g
        d_out[...] = -ADAM_LR * (m_hat / (jnp.sqrt(v_hat) + ADAM_EPS) + ADAM_WD * w_ref[...])
        m_out[...] = m_new
        v_out[...] = v_new

    blk = pl.BlockSpec((block, 1024), lambda i: (i, 0))
    return pl.pallas_call(
        body, name=name, grid=(rows // block,),
        in_specs=[pl.BlockSpec((N_DEV, block, 1024), lambda i: (0, i, 0)), blk, blk, blk],
        out_specs=[blk] * 4, out_shape=[jax.ShapeDtypeStruct((rows, 1024), F32)] * 4,
        compiler_params=_cparams("parallel"),
    )(recv, w, m, v)


def _rms_fwd(name, x, w, bm):
    def fn(xb, wb):
        r = lax.rsqrt(jnp.mean(xb * xb, axis=-1, keepdims=True) + RMS_EPS)
        return (xb * r * wb,)
    return _rowcall(name, fn, [(x, D_MODEL, 0)], [w], [(D_MODEL, BF16)], [], bm, x.shape[0])[0]


def _rms_bwd(name, x, w, dh, dres, bm):
    def fn(xb, dhb, drb, wb):
        r = lax.rsqrt(jnp.mean(xb * xb, axis=-1, keepdims=True) + RMS_EPS)
        xh = xb * r
        dxh = dhb.astype(F32) * wb
        dx = drb + r * (dxh - xh * jnp.mean(dxh * xh, axis=-1, keepdims=True))
        return dx, jnp.sum(dhb.astype(F32) * xh, axis=0, keepdims=True)
    return _rowcall(name, fn, [(x, D_MODEL, 0), (dh, D_MODEL, 0), (dres, D_MODEL, 0)], [w],
                    [(D_MODEL, F32)], [(1, D_MODEL)], bm, x.shape[0])


def _gate_fwd(proj, wgk_pad, b_gk, bm):
    def fn(lr, wg, bg):
        z = _dot(lr, wg) + bg
        return (_log_sigmoid(z) * (1.0 / 16.0),)
    return _rowcall("gla_gate_fwd", fn, [(proj, LANES, C_LR // LANES)], [wgk_pad, b_gk],
                    [(GLA_QK_W, F32)], [], bm, proj.shape[0])[0]


def _gate_bwd(proj, wgk_pad, b_gk, dgk, bm):
    def fn(lr, dg, wg, bg):
        z = _dot(lr, wg) + bg
        dz = dg * _sigmoid(-z) * (1.0 / 16.0)
        return _dot(dz, wg, 1, 1), _dot(lr, dz, 0, 0), jnp.sum(dz, axis=0, keepdims=True)
    return _rowcall("gla_gate_bwd", fn, [(proj, LANES, C_LR // LANES), (dgk, GLA_QK_W, 0)], [wgk_pad, b_gk],
                    [(LANES, BF16)], [(LANES, GLA_QK_W), (1, GLA_QK_W)], bm, proj.shape[0])


def _group_mean(x, g, size):
    return _dot_lhs_exact(x, g, parts=2) * (1.0 / size)


def _mix_out_fwd(o_g, proj, o_s, wg_t, ws_t, grp_g, grp_s, bm):
    def fn(og, gg, os_, wg, ws, gmat, smat):
        rg = lax.rsqrt(_group_mean(og * og, gmat, GLA_DV) + RMS_EPS)
        yg = og * rg * wg * (gg * _sigmoid(gg))
        rs = lax.rsqrt(_group_mean(os_ * os_, smat, SB_DH) + RMS_EPS)
        ys = os_ * rs * ws
        return (jnp.concatenate([yg, ys], axis=1),)
    return _rowcall("mix_out_fwd", fn, [(o_g, GLA_V_W, 0), (proj, GLA_V_W, C_GG // GLA_V_W), (o_s, SB_W, 0)],
                    [wg_t, ws_t, grp_g, grp_s], [(D_MODEL, BF16)], [], bm, o_g.shape[0])[0]


def _mix_out_bwd(dcat, o_g, proj, o_s, wg_t, ws_t, grp_g, grp_s, bm):
    def fn(dyg, dys, og, gg, os_, wg, ws, gmat, smat):
        dyg = dyg.astype(F32)
        dys = dys.astype(F32)
        rg = lax.rsqrt(_group_mean(og * og, gmat, GLA_DV) + RMS_EPS)
        xh = og * rg
        sg = _sigmoid(gg)
        silu = gg * sg
        dxh = dyg * wg * silu
        dgg = dyg * xh * wg * (sg * (1.0 + gg * (1.0 - sg)))
        dwg = jnp.sum(dyg * xh * silu, axis=0, keepdims=True)
        dog = rg * (dxh - xh * _group_mean(dxh * xh, gmat, GLA_DV))
        rs = lax.rsqrt(_group_mean(os_ * os_, smat, SB_DH) + RMS_EPS)
        xs = os_ * rs
        dxs = dys * ws
        dws = jnp.sum(dys * xs, axis=0, keepdims=True)
        dos = rs * (dxs - xs * _group_mean(dxs * xs, smat, SB_DH))
        return dog, dgg, dos, dwg, dws
    return _rowcall("mix_out_bwd", fn,
                    [(dcat, GLA_V_W, 0), (dcat, SB_W, 1), (o_g, GLA_V_W, 0), (proj, GLA_V_W, C_GG // GLA_V_W),
                     (o_s, SB_W, 0)],
                    [wg_t, ws_t, grp_g, grp_s], [(GLA_V_W, F32), (GLA_V_W, BF16), (SB_W, F32)],
                    [(1, GLA_V_W), (1, SB_W)], bm, o_g.shape[0])


def _loss_kernel(y, tgt, bm):
    def fn(yb, tb):
        err = yb - tb
        return err * (1.0 / D_MODEL), jnp.sum(err * err, axis=0, keepdims=True)
    return _rowcall("loss_head", fn, [(y, D_MODEL, 0), (tgt, D_MODEL, 0)], [], [(D_MODEL, F32)], [(1, D_MODEL)],
                    bm, y.shape[0])


def _sb_tri(inclusive):
    j, s = _iota2((2 * SB_BLK, 2 * SB_BLK), 0), _iota2((2 * SB_BLK, 2 * SB_BLK), 1)
    j = jnp.where(j >= SB_BLK, j - SB_BLK, j)
    keep = (j >= s) if inclusive else (j > s)
    return ((s >= SB_BLK) | keep).astype(BF16)


def _dot_hilo(x, m2):
    hi, lo = _split(x, 2)
    return _dot(jnp.concatenate([hi, lo], axis=1), m2)


def _sb_mask(n):
    return _iota2((n, SB_BLK), 1) < _iota2((n, SB_BLK), 0)


def _add_rows(full, part, row0):
    if row0 == 0:
        return full + part
    return jnp.concatenate([full[:row0], full[row0:] + part], axis=0)


def _sb_sweep(n_tiles, step, carry, c_slots):
    def alive(state):
        jj, carry = state
        c_max = jnp.max(functools.reduce(jnp.maximum, [carry[s] for s in c_slots]))
        return jnp.logical_and(jj < n_tiles, c_max > SB_DEAD)

    def body(state):
        jj, carry = state
        return jj + 1, step(jj, carry)

    return lax.while_loop(alive, body, (jnp.int32(0), carry))[1]


def _row_blocks(n):
    return [slice(r, r + SB_BLK) for r in range(0, n, SB_BLK)]


def _hilo(x):
    hi, lo = _split(x, 2)
    return jnp.concatenate([hi, lo], axis=1)


def _sb_tile(q, k, c, tri_excl, diag):
    blocks = _row_blocks(q.shape[0])
    strict = _sb_mask(SB_BLK) if diag else None
    z = _dot(q, k, 1, 1)
    lbs, pieces = [], []
    for r, rs in enumerate(blocks):
        lb = _log_sigmoid(z[rs])
        l1 = lb - z[rs]
        if diag and r == 0:
            l1 = jnp.where(strict, l1, 0.0)
        lbs.append(lb)
        pieces.append(_hilo(l1))
    sums = _dot(jnp.concatenate(pieces, axis=0), tri_excl)
    a = []
    for r, rs in enumerate(blocks):
        ar = jnp.exp(lbs[r] + sums[rs, :SB_BLK] + c[rs])
        if diag and r == 0:
            ar = jnp.where(strict, ar, 0.0)
        a.append(ar.astype(BF16))
    return lbs, a, sums[:, SB_BLK:]


def _sb_fwd(proj, rows):
    qt = min(SB_QT, rows)
    subs = qt // SB_BLK

    def body(q_ref, k_ref, v_ref, o_ref):
        i = pl.program_id(1)
        tri_excl = _sb_tri(False)
        heads = [slice(SB_DH * hh, SB_DH * (hh + 1)) for hh in range(2)]
        qs = [(q_ref[:, sl] * 0.125).astype(BF16) for sl in heads]

        def tiles(start, carry, diag, row0=0):
            out = []
            for hh, sl in enumerate(heads):
                o, c = carry[2 * hh], carry[2 * hh + 1]
                k = k_ref[pl.ds(start, SB_BLK), sl].astype(BF16)
                v = v_ref[pl.ds(start, SB_BLK), sl].astype(BF16)
                _, a, dc = _sb_tile(qs[hh][row0:], k, c[row0:], tri_excl, diag)
                out += [_add_rows(o, _dot(jnp.concatenate(a, axis=0), v), row0), _add_rows(c, dc, row0)]
            return tuple(out)

        carry = (jnp.zeros((qt, SB_DH), F32), jnp.zeros((qt, SB_BLK), F32)) * 2
        for sub in reversed(range(subs)):
            start = pl.multiple_of(i * qt + sub * SB_BLK, SB_BLK)
            carry = tiles(start, carry, True, sub * SB_BLK)

        def step(jj, carry):
            start = pl.multiple_of((i * subs - 1 - jj) * SB_BLK, SB_BLK)
            return tiles(start, carry, False)

        carry = _sb_sweep(i * subs, step, carry, (1, 3))
        for hh, sl in enumerate(heads):
            o_ref[:, sl] = carry[2 * hh]

    return pl.pallas_call(
        body, name="sb_attention_fwd", grid=(SB_HEADS // 2, rows // qt),
        in_specs=[pl.BlockSpec((qt, LANES), lambda h, i: (i, C_QS // LANES + h)),
                  pl.BlockSpec((rows, LANES), lambda h, i: (0, C_KS // LANES + h)),
                  pl.BlockSpec((rows, LANES), lambda h, i: (0, C_VS // LANES + h))],
        out_specs=pl.BlockSpec((qt, LANES), lambda h, i: (i, h)),
        out_shape=jax.ShapeDtypeStruct((rows, SB_W), F32),
        compiler_params=_cparams("parallel", "arbitrary"),
    )(proj, proj, proj)


def _sb_bwd(proj, o_s, do_s, rows, exchange=None):
    qt = min(SB_QT, rows)
    subs = qt // SB_BLK

    def body(q_ref, k_ref, v_ref, o_ref, do_ref, dq_ref, dk_ref, dv_ref):
        i = pl.program_id(1)

        @pl.when(i == 0)
        def _():
            dk_ref[...] = jnp.zeros(dk_ref.shape, F32)
            dv_ref[...] = jnp.zeros(dv_ref.shape, F32)

        tri_excl, tri_incl = _sb_tri(False), _sb_tri(True)
        heads = [slice(SB_DH * hh, SB_DH * (hh + 1)) for hh in range(2)]
        qs = [(q_ref[:, sl] * 0.125).astype(BF16) for sl in heads]
        dobs = [do_ref[:, sl].astype(BF16) for sl in heads]
        dsums = [jnp.broadcast_to(jnp.sum(dob.astype(F32) * o_ref[:, sl], axis=1, keepdims=True), (qt, SB_BLK))
                 for dob, sl in zip(dobs, heads)]

        def tiles(start, carry, diag, row0=0):
            out = []
            strict = _sb_mask(SB_BLK) if diag else None
            for hh, sl in enumerate(heads):
                dq, c, cp = carry[3 * hh:3 * hh + 3]
                q, dob = qs[hh][row0:], dobs[hh][row0:]
                dsum, cpr = dsums[hh][row0:], cp[row0:]
                blocks = _row_blocks(q.shape[0])
                k = k_ref[pl.ds(start, SB_BLK), sl].astype(BF16)
                v = v_ref[pl.ds(start, SB_BLK), sl].astype(BF16)
                lbs, a, dc = _sb_tile(q, k, c[row0:], tri_excl, diag)
                da = _dot(dob, v, 1, 1)
                ps = [a[r].astype(F32) * da[rs] for r, rs in enumerate(blocks)]
                psums = _dot(jnp.concatenate([_hilo(p) for p in ps], axis=0), tri_incl)
                dzs = []
                for r, rs in enumerate(blocks):
                    left = dsum[rs] - (psums[rs, :SB_BLK] + cpr[rs])
                    dz = ps[r] - jnp.exp(lbs[r]) * (ps[r] + left)
                    if diag and r == 0:
                        dz = jnp.where(strict, dz, 0.0)
                    dzs.append(dz.astype(BF16))
                dzb, ab = jnp.concatenate(dzs, axis=0), jnp.concatenate(a, axis=0)
                dk_ref[pl.ds(start, SB_BLK), sl] += _dot(dzb, q, 0, 0)
                dv_ref[pl.ds(start, SB_BLK), sl] += _dot(ab, dob, 0, 0)
                out += [_add_rows(dq, _dot(dzb, k), row0), _add_rows(c, dc, row0),
                        _add_rows(cp, psums[:, SB_BLK:], row0)]
            return tuple(out)

        zero = jnp.zeros((qt, SB_BLK), F32)
        carry = (jnp.zeros((qt, SB_DH), F32), zero, zero) * 2
        for sub in reversed(range(subs)):
            start = pl.multiple_of(i * qt + sub * SB_BLK, SB_BLK)
            carry = tiles(start, carry, True, sub * SB_BLK)

        def step(jj, carry):
            start = pl.multiple_of((i * subs - 1 - jj) * SB_BLK, SB_BLK)
            return tiles(start, carry, False)

        carry = _sb_sweep(i * subs, step, carry, (1, 4))
        for hh, sl in enumerate(heads):
            dq_ref[:, sl] = carry[3 * hh] * 0.125

    whole = lambda base: pl.BlockSpec((rows, LANES), functools.partial(lambda h, i, b: (0, b + h), b=base))
    blk = lambda base: pl.BlockSpec((qt, LANES), functools.partial(lambda h, i, b: (i, b + h), b=base))
    return _hosted_call(
        body, exchange, grid=(SB_HEADS // 2, rows // qt),
        in_specs=[blk(C_QS // LANES), whole(C_KS // LANES), whole(C_VS // LANES), blk(0), blk(0)],
        out_specs=[blk(0), whole(0), whole(0)],
        out_shape=[jax.ShapeDtypeStruct((rows, SB_W), F32)] * 3, scratch_shapes=[],
        args=(proj, proj, proj, o_s, do_s), name="sb_attention_bwd",
        compiler_params=_cparams("arbitrary", "arbitrary"))


def _gla_chunk_common(g_all):
    r_i, c_i = _iota2((GLA_CHUNK, GLA_CHUNK), 0), _iota2((GLA_CHUNK, GLA_CHUNK), 1)
    tri = (c_i <= r_i).astype(BF16)
    return _dot_rhs_exact(tri, g_all), r_i, c_i


def _gla_sub(qh, kh, bh, sub):
    rs, nc = GLA_SUB * sub, GLA_SUB * (sub + 1)
    ref = bh[rs:rs + 1, :]
    eq = jnp.exp(bh[rs:rs + GLA_SUB] - ref)
    ek = jnp.exp(jnp.where(_iota2((GLA_CHUNK, GLA_DK), 0) < nc, ref - bh, 0.0))
    mask = _iota2((GLA_SUB, GLA_CHUNK), 1) <= _iota2((GLA_SUB, GLA_CHUNK), 0) + rs
    return rs, eq, ek, qh[rs:rs + GLA_SUB] * eq, kh * ek, mask


def _gla_fwd(proj, gk, rows, exchange=None):
    n_chunks = rows // GLA_CHUNK

    def body(q_ref, k_ref, v_ref, g_ref, o_ref, sall_ref, s_scr):
        @pl.when(pl.program_id(0) == 0)
        def _():
            s_scr[...] = jnp.zeros(s_scr.shape, F32)

        g_all = g_ref[...]
        b_all, _, _ = _gla_chunk_common(g_all)
        ones = jnp.ones((GLA_CHUNK, GLA_DV), BF16)
        for h in range(GLA_HEADS):
            sl = slice(GLA_DK * h, GLA_DK * (h + 1))
            vs = slice(GLA_DV * h, GLA_DV * (h + 1))
            qh, kh, vh = q_ref[:, sl] * 0.125, k_ref[:, sl], v_ref[:, vs]
            bh, gh = b_all[:, sl], g_all[:, sl]
            s = s_scr[h]
            sall_ref[0, h] = s
            o = _dot(qh * jnp.exp(bh), s)
            parts = []
            for sub in range(GLA_CHUNK // GLA_SUB):
                _, _, _, qs, ks, mask = _gla_sub(qh, kh, bh, sub)
                a = jnp.where(mask, _dot(qs, ks, 1, 1), 0.0)
                parts.append(_dot(a, vh))
            o_ref[:, vs] = o + jnp.concatenate(parts, axis=0)
            bl_col = _dot_lhs_exact(gh, ones, 0, 0)
            kd = kh * jnp.exp(bh[GLA_CHUNK - 1:GLA_CHUNK, :] - bh)
            s_scr[h] = jnp.exp(bl_col) * s + _dot(kd, vh, 0, 0)

    c64 = lambda w, base: pl.BlockSpec((GLA_CHUNK, w), functools.partial(lambda n, b: (n, b), b=base))
    return _hosted_call(
        body, exchange, grid=(n_chunks,),
        in_specs=[c64(GLA_QK_W, C_QG // GLA_QK_W), c64(GLA_QK_W, C_KG // GLA_QK_W), c64(GLA_V_W, C_VG // GLA_V_W),
                  c64(GLA_QK_W, 0)],
        out_specs=[c64(GLA_V_W, 0), pl.BlockSpec((1, GLA_HEADS, GLA_DK, GLA_DV), lambda n: (n, 0, 0, 0))],
        out_shape=[jax.ShapeDtypeStruct((rows, GLA_V_W), F32),
                   jax.ShapeDtypeStruct((n_chunks, GLA_HEADS, GLA_DK, GLA_DV), F32)],
        scratch_shapes=[pltpu.VMEM((GLA_HEADS, GLA_DK, GLA_DV), F32)],
        args=(proj, proj, proj, gk), name="gla_fwd", compiler_params=_cparams("arbitrary"))


def _gla_bwd(proj, gk, do_g, s_all, rows, exchange=None):
    n_chunks = rows // GLA_CHUNK

    def body(q_ref, k_ref, v_ref, g_ref, do_ref, sall_ref, dq_ref, dk_ref, dv_ref, dg_ref, ds_scr):
        @pl.when(pl.program_id(0) == 0)
        def _():
            ds_scr[...] = jnp.zeros(ds_scr.shape, F32)

        g_all = g_ref[...]
        b_all, r_i, c_i = _gla_chunk_common(g_all)
        triu = (c_i >= r_i).astype(BF16)
        ones = jnp.ones((GLA_CHUNK, GLA_DV), BF16)
        ones8 = jnp.ones((8, GLA_DV), F32)
        last_row = _iota2((GLA_CHUNK, GLA_DK), 0) == GLA_CHUNK - 1
        for h in range(GLA_HEADS):
            sl = slice(GLA_DK * h, GLA_DK * (h + 1))
            vs = slice(GLA_DV * h, GLA_DV * (h + 1))
            qh, kh, vh = q_ref[:, sl] * 0.125, k_ref[:, sl], v_ref[:, vs]
            bh, gh = b_all[:, sl], g_all[:, sl]
            doh = do_ref[:, vs]
            s, ds = sall_ref[0, h], ds_scr[h]
            eb = jnp.exp(bh)
            ekd = jnp.exp(bh[GLA_CHUNK - 1:GLA_CHUNK, :] - bh)
            ebl = jnp.exp(_dot_lhs_exact(gh, ones, 0, 0))
            qb, kd = qh * eb, kh * ekd
            dq = _dot(doh, s, 1, 1) * eb
            dk = _dot(vh, ds, 1, 1) * ekd
            dv = _dot(kd, ds)
            dbl = jnp.sum(dk * kh, axis=0, keepdims=True) + _dot3(ones8, ebl * s * ds, 1, 1)[0:1, :]
            dq_parts = []
            for sub in range(GLA_CHUNK // GLA_SUB):
                rs, eq, ek, qs, ks, mask = _gla_sub(qh, kh, bh, sub)
                do_sub = doh[rs:rs + GLA_SUB]
                a = jnp.where(mask, _dot(qs, ks, 1, 1), 0.0)
                da = jnp.where(mask, _dot(do_sub, vh, 1, 1), 0.0)
                dq_parts.append(_dot(da, ks) * eq)
                dk = dk + _dot(da, qs, 0, 0) * ek
                dv = dv + _dot(a, do_sub, 0, 0)
            dq = dq + jnp.concatenate(dq_parts, axis=0)
            db = qh * dq - kh * dk + jnp.where(last_row, dbl, 0.0)
            dq_ref[:, sl] = dq * 0.125
            dk_ref[:, sl] = dk
            dv_ref[:, vs] = dv
            dg_ref[:, sl] = _dot_rhs_exact(triu, db)
            ds_scr[h] = _dot(qb, doh, 0, 0) + ebl * ds

    last = n_chunks - 1
    c64 = lambda w, base: pl.BlockSpec((GLA_CHUNK, w), functools.partial(lambda n, b: (last - n, b), b=base))
    return _hosted_call(
        body, exchange, grid=(n_chunks,),
        in_specs=[c64(GLA_QK_W, C_QG // GLA_QK_W), c64(GLA_QK_W, C_KG // GLA_QK_W), c64(GLA_V_W, C_VG // GLA_V_W),
                  c64(GLA_QK_W, 0), c64(GLA_V_W, 0),
                  pl.BlockSpec((1, GLA_HEADS, GLA_DK, GLA_DV), lambda n: (last - n, 0, 0, 0))],
        out_specs=[c64(GLA_QK_W, 0), c64(GLA_QK_W, 0), c64(GLA_V_W, 0), c64(GLA_QK_W, 0)],
        out_shape=[jax.ShapeDtypeStruct((rows, GLA_QK_W), F32), jax.ShapeDtypeStruct((rows, GLA_QK_W), F32),
                   jax.ShapeDtypeStruct((rows, GLA_V_W), F32), jax.ShapeDtypeStruct((rows, GLA_QK_W), F32)],
        scratch_shapes=[pltpu.VMEM((GLA_HEADS, GLA_DK, GLA_DV), F32)],
        args=(proj, proj, proj, gk, do_g, s_all), name="gla_bwd", compiler_params=_cparams("arbitrary"))


def _mem_kv_fwd(mem, mem_norm_w, w_mkv, mk_norm_w):
    def body(mem_ref, mw_ref, w_ref, kw_ref, memn_ref, kpre_ref, kn_ref, v_ref):
        xb = mem_ref[...]
        r = lax.rsqrt(jnp.mean(xb * xb, axis=-1, keepdims=True) + RMS_EPS)
        mem_n = (xb * r * mw_ref[...]).astype(BF16)
        memn_ref[...] = mem_n
        kv = _dot(mem_n, w_ref[...])
        kpre_ref[...] = kv[:, :D_MODEL]
        v_ref[...] = kv[:, D_MODEL:].astype(BF16)
        for h in range(MEM_HEADS):
            sl = slice(MEM_DH * h, MEM_DH * (h + 1))
            kh = kv[:, sl]
            rk = lax.rsqrt(jnp.mean(kh * kh, axis=-1, keepdims=True) + RMS_EPS)
            kn_ref[:, sl] = (kh * rk * kw_ref[...]).astype(BF16)

    return pl.pallas_call(
        body, name="mem_kv_fwd",
        out_shape=[jax.ShapeDtypeStruct((MEM_LEN, D_MODEL), BF16), jax.ShapeDtypeStruct((MEM_LEN, D_MODEL), F32),
                   jax.ShapeDtypeStruct((MEM_LEN, D_MODEL), BF16), jax.ShapeDtypeStruct((MEM_LEN, D_MODEL), BF16)],
        compiler_params=_cparams(),
    )(mem, mem_norm_w, w_mkv, mk_norm_w)


def _mem_kv_bwd(mem, mem_norm_w, w_mkv, mk_norm_w, mem_n, kpre, dkn, dv):
    def body(mem_ref, mw_ref, w_ref, kw_ref, memn_ref, kpre_ref, dkn_ref, dv_ref, dw_ref, dkw_ref, dmw_ref):
        dkw = jnp.zeros((1, MEM_DH), F32)
        dk_parts = []
        for h in range(MEM_HEADS):
            sl = slice(MEM_DH * h, MEM_DH * (h + 1))
            kh, dkh = kpre_ref[:, sl], dkn_ref[:, sl]
            rk = lax.rsqrt(jnp.mean(kh * kh, axis=-1, keepdims=True) + RMS_EPS)
            xh = kh * rk
            dxh = dkh * kw_ref[...]
            dkw = dkw + jnp.sum(dkh * xh, axis=0, keepdims=True)
            dk_parts.append(rk * (dxh - xh * jnp.mean(dxh * xh, axis=-1, keepdims=True)))
        dkw_ref[...] = dkw
        dkv = jnp.concatenate(dk_parts + [dv_ref[...]], axis=1).astype(BF16)
        dw_ref[...] = _dot(memn_ref[...], dkv, 0, 0)
        dmem_n = _dot(dkv, w_ref[...], 1, 1)
        xb = mem_ref[...]
        r = lax.rsqrt(jnp.mean(xb * xb, axis=-1, keepdims=True) + RMS_EPS)
        dmw_ref[...] = jnp.sum(dmem_n * (xb * r), axis=0, keepdims=True)

    return pl.pallas_call(
        body, name="mem_kv_bwd",
        out_shape=[jax.ShapeDtypeStruct((D_MODEL, 2 * D_MODEL), F32), jax.ShapeDtypeStruct((1, MEM_DH), F32),
                   jax.ShapeDtypeStruct((1, D_MODEL), F32)],
        compiler_params=_cparams(),
    )(mem, mem_norm_w, w_mkv, mk_norm_w, mem_n, kpre, dkn, dv)


def _xattn_head(qh, kn_h, qw):
    rq = lax.rsqrt(jnp.mean(qh * qh, axis=-1, keepdims=True) + RMS_EPS)
    xh = qh * rq
    qn = (xh * qw).astype(BF16)
    s = _dot(qn, kn_h, 1, 1) * (1.0 / 16.0)
    e = jnp.exp(s - jnp.max(s, axis=-1, keepdims=True))
    p = e / jnp.sum(e, axis=-1, keepdims=True)
    return rq, xh, qn, p


def _xattn_fwd(qm, kn, v, mq_norm_w, bm):
    def fn(qb, knb, vb, qw):
        outs = []
        for h in range(MEM_HEADS):
            sl = slice(MEM_DH * h, MEM_DH * (h + 1))
            _, _, _, p = _xattn_head(qb[:, sl], knb[:, sl], qw)
            outs.append(_dot(p, vb[:, sl]))
        return (jnp.concatenate(outs, axis=1),)
    return _rowcall("xattn_fwd", fn, [(qm, D_MODEL, 0)], [kn, v, mq_norm_w], [(D_MODEL, BF16)], [], bm,
                    qm.shape[0])[0]


def _xattn_bwd(qm, kn, v, mq_norm_w, do, bm):
    def fn(qb, dob, knb, vb, qw):
        dq_parts, dkn_parts, dv_parts = [], [], []
        dqw = jnp.zeros((1, MEM_DH), F32)
        for h in range(MEM_HEADS):
            sl = slice(MEM_DH * h, MEM_DH * (h + 1))
            rq, xh, qn, p = _xattn_head(qb[:, sl], knb[:, sl], qw)
            doh = dob[:, sl].astype(BF16)
            dp = _dot(doh, vb[:, sl], 1, 1)
            ds = (p * (dp - jnp.sum(dp * p, axis=-1, keepdims=True)) * (1.0 / 16.0)).astype(BF16)
            dqn = _dot(ds, knb[:, sl])
            dkn_parts.append(_dot(ds, qn, 0, 0))
            dv_parts.append(_dot(p, doh, 0, 0))
            dqw = dqw + jnp.sum(dqn * xh, axis=0, keepdims=True)
            dxh = dqn * qw
            dq_parts.append(rq * (dxh - xh * jnp.mean(dxh * xh, axis=-1, keepdims=True)))
        return (jnp.concatenate(dq_parts, axis=1), jnp.concatenate(dkn_parts, axis=1),
                jnp.concatenate(dv_parts, axis=1), dqw)
    return _rowcall("xattn_bwd", fn, [(qm, D_MODEL, 0), (do, D_MODEL, 0)], [kn, v, mq_norm_w],
                    [(D_MODEL, BF16)], [(MEM_LEN, D_MODEL), (MEM_LEN, D_MODEL), (1, MEM_DH)], bm, qm.shape[0])


FF_BN = 1408
FF_NB = D_FF // FF_BN


def _ffn_up(h3, w_gate_up, rows, bm):
    def body(h_ref, wg_ref, wu_ref, gate_ref, up_ref, act_ref):
        hb = h_ref[...]
        gate = _dot(hb, wg_ref[...])
        up = _dot(hb, wu_ref[...])
        gate_ref[...] = gate
        up_ref[...] = up
        act_ref[...] = (gate * _sigmoid(gate) * up).astype(BF16)

    out_blk = pl.BlockSpec((bm, FF_BN), lambda i, j: (i, j))
    return pl.pallas_call(
        body, name="ffn_up", grid=(rows // bm, FF_NB),
        in_specs=[pl.BlockSpec((bm, D_MODEL), lambda i, j: (i, 0)),
                  pl.BlockSpec((D_MODEL, FF_BN), lambda i, j: (0, j)),
                  pl.BlockSpec((D_MODEL, FF_BN), lambda i, j: (0, FF_NB + j))],
        out_specs=[out_blk, out_blk, out_blk],
        out_shape=[jax.ShapeDtypeStruct((rows, D_FF), F32), jax.ShapeDtypeStruct((rows, D_FF), F32),
                   jax.ShapeDtypeStruct((rows, D_FF), BF16)],
        compiler_params=_cparams("parallel", "arbitrary"),
    )(h3, w_gate_up, w_gate_up)


def _ffn_act_bwd(dy, w_down, gate, up, rows, bm):
    def body(dy_ref, wd_ref, gate_ref, up_ref, o_ref):
        dact = _dot(dy_ref[...], wd_ref[...], 1, 1)
        g, u = gate_ref[...], up_ref[...]
        sg = _sigmoid(g)
        o_ref[0] = (dact * u * (sg * (1.0 + g * (1.0 - sg)))).astype(BF16)
        o_ref[1] = (dact * (g * sg)).astype(BF16)

    blk = pl.BlockSpec((bm, FF_BN), lambda i, j: (i, j))
    return pl.pallas_call(
        body, name="ffn_act_bwd", grid=(rows // bm, FF_NB),
        in_specs=[pl.BlockSpec((bm, D_MODEL), lambda i, j: (i, 0)),
                  pl.BlockSpec((FF_BN, D_MODEL), lambda i, j: (j, 0)), blk, blk],
        out_specs=pl.BlockSpec((2, bm, FF_BN), lambda i, j: (0, i, j)),
        out_shape=jax.ShapeDtypeStruct((2, rows, D_FF), BF16),
        compiler_params=_cparams("parallel", "arbitrary"),
    )(dy, w_down, gate, up)


def _pack_rows(a):
    flat = a.reshape(-1)
    pad = (-flat.shape[0]) % 1024
    if pad:
        flat = jnp.concatenate([flat, jnp.zeros((pad,), flat.dtype)])
    return flat.reshape(-1, 1024)


def _pack_owner_rows(a8):
    flat = a8.reshape(N_DEV, -1)
    pad = (-flat.shape[1]) % 1024
    if pad:
        flat = jnp.concatenate([flat, jnp.zeros((N_DEV, pad), flat.dtype)], axis=1)
    return flat.reshape(N_DEV, -1, 1024)


class _Group:
    def __init__(self, names, extra=0, block=None):
        self.names, self.extra, self.offs = names, extra, {}
        o = 0
        for n in names:
            self.offs[n] = o
            o += SHARD_ROWS[n]
        self.shard_rows = o
        self.rows = -(-(o + extra) // 16) * 16
        self.block = block

    def _fill(self, parts, axis, dtype):
        used = sum(p.shape[axis] for p in parts)
        if used < self.rows:
            shape = list(parts[0].shape)
            shape[axis] = self.rows - used
            parts = parts + [jnp.zeros(shape, dtype)]
        return jnp.concatenate(parts, axis=axis)

    def pack(self, shards, extra_rows, dtype):
        parts = [_pack_rows(shards[n].astype(dtype)) for n in self.names] + [r.astype(dtype) for r in extra_rows]
        return self._fill(parts, 0, dtype)

    def pack_for_owners(self, full, extra_rows):
        parts = [_pack_owner_rows(_split_for_owners(n, full[n])) for n in self.names]
        if extra_rows:
            parts.append(jnp.broadcast_to(jnp.concatenate(extra_rows, axis=0)[None], (N_DEV, len(extra_rows), 1024)))
        return self._fill(parts, 1, F32)

    def unpack(self, pack, name):
        r, c = SHARD_SHAPE[name]
        seg = pack[self.offs[name]:self.offs[name] + SHARD_ROWS[name]].reshape(-1)[:r * c]
        return seg.reshape(1, r, c)

    def gathered(self, gathered, name):
        r, c = SHARD_SHAPE[name]
        seg = gathered[:, self.offs[name]:self.offs[name] + SHARD_ROWS[name]]
        seg = seg.reshape(N_DEV, -1)[:, :r * c].reshape(N_DEV, r, c)
        if name in ROW_SHARDED:
            return seg.reshape(N_DEV * r, c)
        return seg.transpose(1, 0, 2).reshape(r, N_DEV * c)


ROW_SHARDED = ("w_out", "w_mq", "w_mo", "w_down")
N_EXTRA = len(REPL) + 1
AG_FIRST = _Group(("w_in", "w_gk_up"))
AG_REST = _Group(("w_out", "w_mq", "w_mkv", "w_mo", "w_gate_up", "w_down"))
RS_FFN = _Group(("w_gate_up", "w_down"), block=264)
RS_MID = _Group(("w_out", "w_mq", "w_mkv", "w_mo"), block=320)
RS_LAST = _Group(("w_in", "w_gk_up"), extra=N_EXTRA, block=200)


def _split_for_owners(name, full):
    r, c = SHARD_SHAPE[name]
    if name in ROW_SHARDED:
        return full.reshape(N_DEV, r, c)
    return full.reshape(r, N_DEV, c).transpose(1, 0, 2)


def _repl_row(a):
    flat = a.reshape(-1)
    return jnp.concatenate([flat, jnp.zeros((1024 - flat.shape[0],), flat.dtype)]).reshape(1, 1024)


def _local_step(x, mem, tgt, wf, rp, rest_pack=None, scatter=False):
    rows = x.shape[0]
    bm = min(512, rows)
    bmx = min(256, rows)
    mt = min(MM_TILE, rows)
    kt = min(512, rows)
    w_cat, wgk_pad = wf["w_cat"], wf["wgk_pad"]
    wg_t = jnp.tile(rp["gla_norm_w"], (1, GLA_HEADS))
    ws_t = jnp.tile(rp["sb_norm_w"], (1, SB_HEADS))
    lane = jnp.arange(GLA_V_W)
    grp_g = (lane[:, None] // GLA_DV == lane[None, :] // GLA_DV).astype(BF16)
    grp_s = (lane[:, None] // SB_DH == lane[None, :] // SB_DH).astype(BF16)

    h1 = _rms_fwd("mix_norm_fwd", x, rp["mix_norm_w"], bm)
    proj = _matmul("in_proj", h1, w_cat, "nn", rows, PROJ_W, D_MODEL, F32, mt, 640, D_MODEL)
    gk = _gate_fwd(proj, wgk_pad, rp["b_gk"], bm)
    if rest_pack is None:
        o_g, s_all = _gla_fwd(proj, gk, rows)
    else:
        o_g, s_all, gathered = _gla_fwd(proj, gk, rows, _Exchange(rest_pack, scatter=False))
        wf = {**wf, **{n: AG_REST.gathered(gathered, n) for n in AG_REST.names}}
    o_s = _sb_fwd(proj, rows)
    cat = _mix_out_fwd(o_g, proj, o_s, wg_t, ws_t, grp_g, grp_s, bm)
    x1 = _matmul("out_proj", cat, wf["w_out"], "nn", rows, D_MODEL, D_MODEL, F32, mt, MM_TILE, D_MODEL, residual=x)
    h2 = _rms_fwd("xattn_norm_fwd", x1, rp["xattn_norm_w"], bm)
    qm = _matmul("mq_proj", h2, wf["w_mq"], "nn", rows, D_MODEL, D_MODEL, F32, mt, MM_TILE, D_MODEL)
    mem_n, kpre, kn, v_m = _mem_kv_fwd(mem, rp["mem_norm_w"], wf["w_mkv"], rp["mk_norm_w"])
    o_m = _xattn_fwd(qm, kn, v_m, rp["mq_norm_w"], bmx)
    x2 = _matmul("mo_proj", o_m, wf["w_mo"], "nn", rows, D_MODEL, D_MODEL, F32, mt, MM_TILE, D_MODEL, residual=x1)
    h3 = _rms_fwd("ffn_norm_fwd", x2, rp["ffn_norm_w"], bm)
    gate, up, act = _ffn_up(h3, wf["w_gate_up"], rows, bm)
    y = _matmul("ffn_down", act, wf["w_down"], "nn", rows, D_MODEL, D_FF, F32, mt, MM_TILE, FF_BN, residual=x2)
    dy, sq = _loss_kernel(y, tgt, bm)

    g = {}
    dgu = _ffn_act_bwd(dy, wf["w_down"], gate, up, rows, bm)
    g["w_down"] = _matmul("grad_w_down", act, dy, "tn", D_FF, D_MODEL, rows, F32, FF_BN, MM_TILE, kt)
    nkb = FF_NB
    dh3 = _matmul("ffn_up_bwd", dgu, wf["w_gate_up"], "nt", rows, D_MODEL, 2 * D_FF, F32, mt, MM_TILE, FF_BN,
                  a_spec=pl.BlockSpec((None, mt, FF_BN), lambda i, j, kk: (kk // nkb, i, kk % nkb)))
    g["w_gate_up"] = _matmul(
        "grad_w_gate_up", h3, dgu, "tn", D_MODEL, 2 * D_FF, rows, F32, MM_TILE, FF_BN, kt,
        b_spec=pl.BlockSpec((None, kt, FF_BN), lambda i, j, kk: (j // nkb, kk, j % nkb)))
    dx2, g["ffn_norm_w"] = _rms_bwd("ffn_norm_bwd", x2, rp["ffn_norm_w"], dh3, dy, bm)

    do_m = _matmul("mo_proj_bwd", dx2, wf["w_mo"], "nt", rows, D_MODEL, D_MODEL, BF16, mt, MM_TILE, D_MODEL)
    g["w_mo"] = _matmul("grad_w_mo", o_m, dx2, "tn", D_MODEL, D_MODEL, rows, F32, MM_TILE, MM_TILE, kt)
    dqm, dkn, dv_m, g["mq_norm_w"] = _xattn_bwd(qm, kn, v_m, rp["mq_norm_w"], do_m, bmx)
    g["w_mkv"], g["mk_norm_w"], g["mem_norm_w"] = _mem_kv_bwd(
        mem, rp["mem_norm_w"], wf["w_mkv"], rp["mk_norm_w"], mem_n, kpre, dkn, dv_m)
    dh2 = _matmul("mq_proj_bwd", dqm, wf["w_mq"], "nt", rows, D_MODEL, D_MODEL, F32, mt, MM_TILE, D_MODEL)
    g["w_mq"] = _matmul("grad_w_mq", h2, dqm, "tn", D_MODEL, D_MODEL, rows, F32, MM_TILE, MM_TILE, kt)
    dx1, g["xattn_norm_w"] = _rms_bwd("xattn_norm_bwd", x1, rp["xattn_norm_w"], dh2, dx2, bm)

    dcat = _matmul("out_proj_bwd", dx1, wf["w_out"], "nt", rows, D_MODEL, D_MODEL, F32, mt, MM_TILE, D_MODEL)
    g["w_out"] = _matmul("grad_w_out", cat, dx1, "tn", D_MODEL, D_MODEL, rows, F32, MM_TILE, MM_TILE, kt)
    do_g, dgg, do_s, dwg, dws = _mix_out_bwd(dcat, o_g, proj, o_s, wg_t, ws_t, grp_g, grp_s, bm)
    g["gla_norm_w"] = dwg.reshape(GLA_HEADS, GLA_DV).sum(axis=0, keepdims=True)
    g["sb_norm_w"] = dws.reshape(SB_HEADS, SB_DH).sum(axis=0, keepdims=True)
    recv = {}
    if scatter:
        dq_s, dk_s, dv_s, recv["ffn"] = _sb_bwd(
            proj, o_s, do_s, rows, _Exchange(RS_FFN.pack_for_owners(g, []), scatter=True))
        dq_g, dk_g, dv_g, dgk, recv["mid"] = _gla_bwd(
            proj, gk, do_g, s_all, rows, _Exchange(RS_MID.pack_for_owners(g, []), scatter=True))
    else:
        dq_s, dk_s, dv_s = _sb_bwd(proj, o_s, do_s, rows)
        dq_g, dk_g, dv_g, dgk = _gla_bwd(proj, gk, do_g, s_all, rows)
    dlr, dwgk, g["b_gk"] = _gate_bwd(proj, wgk_pad, rp["b_gk"], dgk, bm)
    g["w_gk_up"] = dwgk[:GATE_RANK]
    dproj = jnp.concatenate([dq_g.astype(BF16), dk_g.astype(BF16), dv_g.astype(BF16), dgg, dq_s.astype(BF16),
                             dk_s.astype(BF16), dv_s.astype(BF16), dlr], axis=1)
    dh1 = _matmul("in_proj_bwd", dproj, w_cat, "nt", rows, D_MODEL, PROJ_W, F32, mt, MM_TILE, 640)
    dw_cat = _matmul("grad_w_in", h1, dproj, "tn", D_MODEL, PROJ_W, rows, F32, MM_TILE, 640, kt)
    g["w_in"] = jnp.concatenate([dw_cat[:, :C_QS], dw_cat[:, C_LR:C_LR + GATE_RANK], dw_cat[:, C_QS:C_LR]], axis=1)
    dx, g["mix_norm_w"] = _rms_bwd("mix_norm_bwd", x, rp["mix_norm_w"], dh1, dx1, bm)
    return sq, dx, g, recv


def _first_weights(gathered):
    w_in = AG_FIRST.gathered(gathered, "w_in")
    lr_end = C_QS + GATE_RANK
    w_cat = jnp.concatenate([w_in[:, :C_QS], w_in[:, lr_end:], w_in[:, C_QS:lr_end],
                             jnp.zeros((D_MODEL, PROJ_W - D_IN), BF16)], axis=1)
    wgk = AG_FIRST.gathered(gathered, "w_gk_up")
    return {"w_cat": w_cat, "wgk_pad": jnp.concatenate([wgk, jnp.zeros((LANES - GATE_RANK, GLA_QK_W), BF16)], axis=0)}


def kernel(x, mem, mix_norm_w, w_in, w_gk_up, b_gk, gla_norm_w, sb_norm_w, w_out, xattn_norm_w, mem_norm_w, w_mq, w_mkv, mq_norm_w, mk_norm_w, w_mo, ffn_norm_w, w_gate_up, w_down, loss_target, m_mix_norm_w, m_w_in, m_w_gk_up, m_b_gk, m_gla_norm_w, m_sb_norm_w, m_w_out, m_xattn_norm_w, m_mem_norm_w, m_w_mq, m_w_mkv, m_mq_norm_w, m_mk_norm_w, m_w_mo, m_ffn_norm_w, m_w_gate_up, m_w_down, v_mix_norm_w, v_w_in, v_w_gk_up, v_b_gk, v_gla_norm_w, v_sb_norm_w, v_w_out, v_xattn_norm_w, v_mem_norm_w, v_w_mq, v_w_mkv, v_mq_norm_w, v_mk_norm_w, v_w_mo, v_ffn_norm_w, v_w_gate_up, v_w_down):
    given = dict(locals())
    w = {n: given[n][0] for n in WEIGHTS}
    m = {n: given["m_" + n][0] for n in WEIGHTS}
    v = {n: given["v_" + n][0] for n in WEIGHTS}

    wf = _first_weights(_exchange_call("gather_first_weights", AG_FIRST.pack(w, [], BF16), scatter=False))
    rp = {n: w[n].reshape(1, -1) for n in REPL}
    sq, dx, g, recv = _local_step(x[0], mem[0], loss_target[0], wf, rp, AG_REST.pack(w, [], BF16), scatter=True)

    loss_row = _repl_row(jnp.sum(sq).reshape(1) * (0.5 / D_MODEL))
    repl_rows = [_repl_row(g[n]) for n in REPL] + [loss_row]
    recv["last"] = _exchange_call("scatter_last_gradients", RS_LAST.pack_for_owners(g, repl_rows), scatter=True)

    zero_row = jnp.zeros((1, 1024), F32)
    out_packs = {}
    for key, grp in (("ffn", RS_FFN), ("mid", RS_MID), ("last", RS_LAST)):
        extra = lambda t: [_repl_row(t[n]) for n in REPL] + [zero_row] if grp.extra else []
        out_packs[key] = _adamw("sum_adamw_" + key, recv[key], *[grp.pack(t, extra(t), F32) for t in (w, m, v)],
                                block=grp.block)

    def unpack(kind, name):
        for key, grp in (("ffn", RS_FFN), ("mid", RS_MID), ("last", RS_LAST)):
            if name in grp.names:
                return grp.unpack(out_packs[key][kind], name)
        row = out_packs["last"][kind][RS_LAST.shard_rows + REPL.index(name)]
        return row[:w[name].shape[-1]].reshape(1, -1)

    loss = out_packs["last"][0][RS_LAST.shard_rows + len(REPL), 0]
    outs = [loss, dx[None]]
    for kind in range(4):
        outs += [unpack(kind, n) for n in WEIGHTS]
    return tuple(outs)
```

```python
import functools
import math

import jax
import jax.numpy as jnp
from jax import lax
from jax.experimental import pallas as pl
from jax.experimental.pallas import tpu as pltpu

F32 = jnp.float32
BF16 = jnp.bfloat16

N_DEV = 8
D_MODEL = 1024
GLA_HEADS = 4
GLA_DK = 64
GLA_DV = 128
GLA_CHUNK = 64
GLA_STEP_CHUNKS = 4
GLA_QK_W = GLA_HEADS * GLA_DK
GLA_V_W = GLA_HEADS * GLA_DV
GATE_RANK = 16
SB_HEADS = 8
SB_DH = 64
SB_W = SB_HEADS * SB_DH
SB_BLK = 128
SB_QT = 1024
SB_DEAD = -104.0
MEM_LEN = 256
MEM_HEADS = 4
MEM_DH = 256
D_FF = 2816
D_IN = 3088
RMS_EPS = 1e-6
LANES = 128

PROJ_W = 3200
C_QG, C_KG, C_VG, C_GG, C_QS, C_KS, C_VS, C_LR = 0, 256, 512, 1024, 1536, 2048, 2560, 3072

ADAM_LR, ADAM_B1, ADAM_B2, ADAM_EPS, ADAM_WD, ADAM_STEP = 0.001, 0.9, 0.999, 1e-08, 0.01, 10

SHARD_SHAPE = {"w_in": (1024, 386), "w_out": (128, 1024), "w_mq": (128, 1024), "w_mkv": (1024, 256),
               "w_mo": (128, 1024), "w_gate_up": (1024, 704), "w_down": (352, 1024), "w_gk_up": (16, 32)}
SHARD_ROWS = {n: -(-(s[0] * s[1]) // 1024) for n, s in SHARD_SHAPE.items()}
REPL = ("mix_norm_w", "b_gk", "gla_norm_w", "sb_norm_w", "xattn_norm_w", "mem_norm_w", "mq_norm_w",
        "mk_norm_w", "ffn_norm_w")
WEIGHTS = ("mix_norm_w", "w_in", "w_gk_up", "b_gk", "gla_norm_w", "sb_norm_w", "w_out", "xattn_norm_w",
           "mem_norm_w", "w_mq", "w_mkv", "mq_norm_w", "mk_norm_w", "w_mo", "ffn_norm_w", "w_gate_up", "w_down")
VMEM_LIMIT = 56 * 1024 * 1024
MM_TILE = 1024


def _cparams(*sem):
    return pltpu.CompilerParams(dimension_semantics=sem if sem else None, vmem_limit_bytes=VMEM_LIMIT)


def _dot(a, b, ca=1, cb=0):
    return lax.dot_general(a.astype(BF16), b.astype(BF16), (((ca,), (cb,)), ((), ())),
                           preferred_element_type=F32)


def _split(x, parts):
    out = []
    for _ in range(parts - 1):
        hi = x.astype(BF16)
        out.append(hi)
        x = x - hi.astype(F32)
    out.append(x.astype(BF16))
    return out


def _dot_lhs_exact(x, m, ca=1, cb=0, parts=3):
    acc = None
    for p in _split(x, parts):
        t = _dot(p, m, ca, cb)
        acc = t if acc is None else acc + t
    return acc


def _dot_rhs_exact(m, x, ca=1, cb=0, parts=3):
    acc = None
    for p in _split(x, parts):
        t = _dot(m, p, ca, cb)
        acc = t if acc is None else acc + t
    return acc


def _dot3(a, b, ca=1, cb=0):
    a_hi, a_lo = _split(a, 2)
    b_hi, b_lo = _split(b, 2)
    return _dot(a_hi, b_hi, ca, cb) + (_dot(a_hi, b_lo, ca, cb) + _dot(a_lo, b_hi, ca, cb))


def _log_sigmoid(z):
    return jnp.minimum(z, 0.0) - jnp.log(1.0 + jnp.exp(-jnp.abs(z)))


def _sigmoid(z):
    e = jnp.exp(-jnp.abs(z))
    return jnp.where(z >= 0, 1.0, e) / (1.0 + e)


def _iota2(shape, dim):
    return lax.broadcasted_iota(jnp.int32, shape, dim)


def _rowcall(name, fn, row_ins, full_ins, row_outs, acc_outs, bm, rows):
    n_in = len(row_ins) + len(full_ins)
    n_row = len(row_outs)

    def body(*refs):
        ins, outs = refs[:n_in], refs[n_in:]
        res = fn(*[r[...] for r in ins])
        for r, v in zip(outs[:n_row], res[:n_row]):
            r[...] = v.astype(r.dtype)
        first = pl.program_id(0) == 0
        for r, v in zip(outs[n_row:], res[n_row:]):
            def init(r=r):
                r[...] = jnp.zeros(r.shape, r.dtype)
            pl.when(first)(init)
            r[...] += v

    in_specs = [pl.BlockSpec((bm, w), functools.partial(lambda i, c: (i, c), c=c)) for _, w, c in row_ins]
    in_specs += [pl.BlockSpec(a.shape, lambda i: (0, 0)) for a in full_ins]
    out_specs = [pl.BlockSpec((bm, w), lambda i: (i, 0)) for w, _ in row_outs]
    out_specs += [pl.BlockSpec(s, lambda i: (0, 0)) for s in acc_outs]
    out_shape = [jax.ShapeDtypeStruct((rows, w), dt) for w, dt in row_outs]
    out_shape += [jax.ShapeDtypeStruct(s, F32) for s in acc_outs]
    return pl.pallas_call(
        body, name=name, grid=(rows // bm,), in_specs=in_specs, out_specs=out_specs, out_shape=out_shape,
        compiler_params=_cparams("arbitrary"),
    )(*[a for a, _, _ in row_ins], *full_ins)


def _matmul(name, a, b, mode, m, n, k, out_dtype, bm, bn, bk, residual=None, a_spec=None, b_spec=None):
    bm, bn, bk = min(bm, m), min(bn, n), min(bk, k)
    nk = k // bk
    ca, cb = {"nn": (1, 0), "nt": (1, 1), "tn": (0, 0)}[mode]
    if a_spec is None:
        a_spec = (pl.BlockSpec((bk, bm), lambda i, j, kk: (kk, i)) if mode == "tn"
                  else pl.BlockSpec((bm, bk), lambda i, j, kk: (i, kk)))
    if b_spec is None:
        b_spec = (pl.BlockSpec((bn, bk), lambda i, j, kk: (j, kk)) if mode == "nt"
                  else pl.BlockSpec((bk, bn), lambda i, j, kk: (kk, j)))
    has_res = residual is not None

    def body(*refs):
        a_ref, b_ref = refs[0], refs[1]
        res_ref = refs[2] if has_res else None
        o_ref = refs[2 + has_res]
        part = _dot(a_ref[...], b_ref[...], ca, cb)

        def finish(total):
            if has_res:
                total = total + res_ref[...]
            o_ref[...] = total.astype(o_ref.dtype)

        if nk == 1:
            finish(part)
        else:
            acc_ref = refs[3 + has_res]
            kk = pl.program_id(2)

            @pl.when(kk == 0)
            def _():
                acc_ref[...] = part

            @pl.when(kk > 0)
            def _():
                acc_ref[...] += part

            @pl.when(kk == nk - 1)
            def _():
                finish(acc_ref[...])

    in_specs = [a_spec, b_spec]
    args = [a, b]
    if has_res:
        in_specs.append(pl.BlockSpec((bm, bn), lambda i, j, kk: (i, j)))
        args.append(residual)
    return pl.pallas_call(
        body, name=name, grid=(m // bm, n // bn, nk), in_specs=in_specs,
        out_specs=pl.BlockSpec((bm, bn), lambda i, j, kk: (i, j)),
        out_shape=jax.ShapeDtypeStruct((m, n), out_dtype),
        scratch_shapes=[pltpu.VMEM((bm, bn), F32)] if nk > 1 else [],
        compiler_params=_cparams("parallel", "parallel", "arbitrary"),
    )(*args)


def _peer(mask):
    x, y, c = lax.axis_index("x"), lax.axis_index("y"), lax.axis_index("c")
    mx, my, mc = (mask >> 2) & 1, (mask >> 1) & 1, mask & 1
    px, py, pc = (1 - x if mx else x), (1 - y if my else y), (1 - c if mc else c)
    return (px, py, pc), 4 * px + 2 * py + pc


def _my_index():
    return 4 * lax.axis_index("x") + 2 * lax.axis_index("y") + lax.axis_index("c")


class _Exchange:
    def __init__(self, src, scatter):
        self.src, self.scatter = src, scatter
        self.in_spec = pl.BlockSpec(memory_space=pl.ANY)
        self.out_spec = pl.BlockSpec(memory_space=pl.ANY)
        self.out_shape = jax.ShapeDtypeStruct((N_DEV, src.shape[-2], 1024), src.dtype)
        self.scratch = [pltpu.SemaphoreType.DMA((N_DEV - 1,)), pltpu.SemaphoreType.DMA((N_DEV - 1,)),
                        pltpu.SemaphoreType.DMA(())]

    def _copies(self, src_ref, out_ref, sems):
        send_sems, recv_sems, local_sem = sems
        me = _my_index()
        copies = [pltpu.make_async_copy(src_ref.at[me] if self.scatter else src_ref, out_ref.at[me], local_sem)]
        for mask in range(1, N_DEV):
            peer, peer_index = _peer(mask)
            copies.append(pltpu.make_async_remote_copy(
                src_ref=src_ref.at[peer_index] if self.scatter else src_ref, dst_ref=out_ref.at[me],
                send_sem=send_sems.at[mask - 1], recv_sem=recv_sems.at[mask - 1],
                device_id=peer, device_id_type=pl.DeviceIdType.MESH))
        return copies

    def start(self, src_ref, out_ref, sems):
        for cp in self._copies(src_ref, out_ref, sems):
            cp.start()

    def wait(self, src_ref, out_ref, sems):
        for cp in self._copies(src_ref, out_ref, sems):
            cp.wait()


def _hosted_call(body, ex, grid, in_specs, out_specs, out_shape, scratch_shapes, args, **kw):
    if ex is None:
        return pl.pallas_call(body, grid=grid, in_specs=in_specs, out_specs=out_specs, out_shape=out_shape,
                              scratch_shapes=scratch_shapes, **kw)(*args)
    n_in, n_out, n_scr = len(in_specs), len(out_specs), len(scratch_shapes)

    def hosted(*refs):
        ins, src_ref = refs[:n_in], refs[n_in]
        outs, out_ref = refs[n_in + 1:n_in + 1 + n_out], refs[n_in + 1 + n_out]
        scr, sems = refs[n_in + 2 + n_out:n_in + 2 + n_out + n_scr], refs[n_in + 2 + n_out + n_scr:]
        ids = [pl.program_id(a) for a in range(len(grid))]
        first = functools.reduce(jnp.logical_and, [p == 0 for p in ids])
        last = functools.reduce(jnp.logical_and, [p == n - 1 for p, n in zip(ids, grid)])

        @pl.when(first)
        def _():
            ex.start(src_ref, out_ref, sems)

        body(*ins, *outs, *scr)

        @pl.when(last)
        def _():
            ex.wait(src_ref, out_ref, sems)

    res = pl.pallas_call(
        hosted, grid=grid, in_specs=list(in_specs) + [ex.in_spec], out_specs=list(out_specs) + [ex.out_spec],
        out_shape=list(out_shape) + [ex.out_shape], scratch_shapes=list(scratch_shapes) + ex.scratch, **kw,
    )(*args, ex.src)
    return res


def _exchange_call(name, src, scatter):
    ex = _Exchange(src, scatter)

    def body(src_ref, out_ref, *sems):
        ex.start(src_ref, out_ref, sems)
        ex.wait(src_ref, out_ref, sems)

    return pl.pallas_call(
        body, name=name, in_specs=[ex.in_spec], out_specs=ex.out_spec, out_shape=ex.out_shape,
        scratch_shapes=ex.scratch, compiler_params=pltpu.CompilerParams(has_side_effects=True),
    )(src)


def _adamw(name, recv, w, m, v, block):
    rows = w.shape[0]
    c1 = 1.0 - ADAM_B1 ** ADAM_STEP
    c2 = 1.0 - ADAM_B2 ** ADAM_STEP

    def body(r_ref, w_ref, m_ref, v_ref, g_out, d_out, m_out, v_out):
        g = r_ref[0]
        for s in range(1, N_DEV):
            g = g + r_ref[s]
        m_new = ADAM_B1 * m_ref[...] + (1.0 - ADAM_B1) * g
        v_new = ADAM_B2 * v_ref[...] + (1.0 - ADAM_B2) * (g * g)
        m_hat = m_new / c1
        v_hat = v_new / c2
        g_out[...] = g
        d_out[...] = -ADAM_LR * (m_hat / (jnp.sqrt(v_hat) + ADAM_EPS) + ADAM_WD * w_ref[...])
        m_out[...] = m_new
        v_out[...] = v_new

    blk = pl.BlockSpec((block, 1024), lambda i: (i, 0))
    return pl.pallas_call(
        body, name=name, grid=(rows // block,),
        in_specs=[pl.BlockSpec((N_DEV, block, 1024), lambda i: (0, i, 0)), blk, blk, blk],
        out_specs=[blk] * 4, out_shape=[jax.ShapeDtypeStruct((rows, 1024), F32)] * 4,
        compiler_params=_cparams("parallel"),
    )(recv, w, m, v)


def _rms_fwd(name, x, w, bm):
    def fn(xb, wb):
        r = lax.rsqrt(jnp.mean(xb * xb, axis=-1, keepdims=True) + RMS_EPS)
        return (xb * r * wb,)
    return _rowcall(name, fn, [(x, D_MODEL, 0)], [w], [(D_MODEL, BF16)], [], bm, x.shape[0])[0]


def _rms_bwd(name, x, w, dh, dres, bm):
    def fn(xb, dhb, drb, wb):
        r = lax.rsqrt(jnp.mean(xb * xb, axis=-1, keepdims=True) + RMS_EPS)
        xh = xb * r
        dxh = dhb.astype(F32) * wb
        dx = drb + r * (dxh - xh * jnp.mean(dxh * xh, axis=-1, keepdims=True))
        return dx, jnp.sum(dhb.astype(F32) * xh, axis=0, keepdims=True)
    return _rowcall(name, fn, [(x, D_MODEL, 0), (dh, D_MODEL, 0), (dres, D_MODEL, 0)], [w],
                    [(D_MODEL, F32)], [(1, D_MODEL)], bm, x.shape[0])


def _gate_fwd(proj, wgk_pad, b_gk, bm):
    def fn(lr, wg, bg):
        z = _dot(lr, wg) + bg
        return (_log_sigmoid(z) * (1.0 / 16.0),)
    return _rowcall("gla_gate_fwd", fn, [(proj, LANES, C_LR // LANES)], [wgk_pad, b_gk],
                    [(GLA_QK_W, F32)], [], bm, proj.shape[0])[0]


def _gate_bwd(proj, wgk_pad, b_gk, dgk, bm):
    def fn(lr, dg, wg, bg):
        z = _dot(lr, wg) + bg
        dz = dg * _sigmoid(-z) * (1.0 / 16.0)
        return _dot(dz, wg, 1, 1), _dot(lr, dz, 0, 0), jnp.sum(dz, axis=0, keepdims=True)
    return _rowcall("gla_gate_bwd", fn, [(proj, LANES, C_LR // LANES), (dgk, GLA_QK_W, 0)], [wgk_pad, b_gk],
                    [(LANES, BF16)], [(LANES, GLA_QK_W), (1, GLA_QK_W)], bm, proj.shape[0])


def _group_mean(x, g, size):
    return _dot_lhs_exact(x, g, parts=2) * (1.0 / size)


def _mix_out_fwd(o_g, proj, o_s, wg_t, ws_t, grp_g, grp_s, bm):
    def fn(og, gg, os_, wg, ws, gmat, smat):
        rg = lax.rsqrt(_group_mean(og * og, gmat, GLA_DV) + RMS_EPS)
        yg = og * rg * wg * (gg * _sigmoid(gg))
        rs = lax.rsqrt(_group_mean(os_ * os_, smat, SB_DH) + RMS_EPS)
        ys = os_ * rs * ws
        return (jnp.concatenate([yg, ys], axis=1),)
    return _rowcall("mix_out_fwd", fn, [(o_g, GLA_V_W, 0), (proj, GLA_V_W, C_GG // GLA_V_W), (o_s, SB_W, 0)],
                    [wg_t, ws_t, grp_g, grp_s], [(D_MODEL, BF16)], [], bm, o_g.shape[0])[0]


def _mix_out_bwd(dcat, o_g, proj, o_s, wg_t, ws_t, grp_g, grp_s, bm):
    def fn(dyg, dys, og, gg, os_, wg, ws, gmat, smat):
        dyg = dyg.astype(F32)
        dys = dys.astype(F32)
        rg = lax.rsqrt(_group_mean(og * og, gmat, GLA_DV) + RMS_EPS)
        xh = og * rg
        sg = _sigmoid(gg)
        silu = gg * sg
        dxh = dyg * wg * silu
        dgg = dyg * xh * wg * (sg * (1.0 + gg * (1.0 - sg)))
        dwg = jnp.sum(dyg * xh * silu, axis=0, keepdims=True)
        dog = rg * (dxh - xh * _group_mean(dxh * xh, gmat, GLA_DV))
        rs = lax.rsqrt(_group_mean(os_ * os_, smat, SB_DH) + RMS_EPS)
        xs = os_ * rs
        dxs = dys * ws
        dws = jnp.sum(dys * xs, axis=0, keepdims=True)
        dos = rs * (dxs - xs * _group_mean(dxs * xs, smat, SB_DH))
        return dog, dgg, dos, dwg, dws
    return _rowcall("mix_out_bwd", fn,
                    [(dcat, GLA_V_W, 0), (dcat, SB_W, 1), (o_g, GLA_V_W, 0), (proj, GLA_V_W, C_GG // GLA_V_W),
                     (o_s, SB_W, 0)],
                    [wg_t, ws_t, grp_g, grp_s], [(GLA_V_W, F32), (GLA_V_W, BF16), (SB_W, F32)],
                    [(1, GLA_V_W), (1, SB_W)], bm, o_g.shape[0])


def _loss_kernel(y, tgt, bm):
    def fn(yb, tb):
        err = yb - tb
        return err * (1.0 / D_MODEL), jnp.sum(err * err, axis=0, keepdims=True)
    return _rowcall("loss_head", fn, [(y, D_MODEL, 0), (tgt, D_MODEL, 0)], [], [(D_MODEL, F32)], [(1, D_MODEL)],
                    bm, y.shape[0])


def _sb_tri(inclusive):
    j, s = _iota2((2 * SB_BLK, 2 * SB_BLK), 0), _iota2((2 * SB_BLK, 2 * SB_BLK), 1)
    j = jnp.where(j >= SB_BLK, j - SB_BLK, j)
    keep = (j >= s) if inclusive else (j > s)
    return ((s >= SB_BLK) | keep).astype(BF16)


def _dot_hilo(x, m2):
    hi, lo = _split(x, 2)
    return _dot(jnp.concatenate([hi, lo], axis=1), m2)


def _sb_mask(n):
    return _iota2((n, SB_BLK), 1) < _iota2((n, SB_BLK), 0)


def _add_rows(full, part, row0):
    if row0 == 0:
        return full + part
    return jnp.concatenate([full[:row0], full[row0:] + part], axis=0)


def _sb_sweep(n_tiles, step, carry, c_slots):
    def alive(state):
        jj, carry = state
        c_max = jnp.max(functools.reduce(jnp.maximum, [carry[s] for s in c_slots]))
        return jnp.logical_and(jj < n_tiles, c_max > SB_DEAD)

    def body(state):
        jj, carry = state
        return jj + 1, step(jj, carry)

    return lax.while_loop(alive, body, (jnp.int32(0), carry))[1]


def _row_blocks(n):
    return [slice(r, r + SB_BLK) for r in range(0, n, SB_BLK)]


def _hilo(x):
    hi, lo = _split(x, 2)
    return jnp.concatenate([hi, lo], axis=1)


def _sb_tile(q, k, c, tri_excl, diag):
    blocks = _row_blocks(q.shape[0])
    strict = _sb_mask(SB_BLK) if diag else None
    z = _dot(q, k, 1, 1)
    lbs, pieces = [], []
    for r, rs in enumerate(blocks):
        lb = _log_sigmoid(z[rs])
        l1 = lb - z[rs]
        if diag and r == 0:
            l1 = jnp.where(strict, l1, 0.0)
        lbs.append(lb)
        pieces.append(_hilo(l1))
    sums = _dot(jnp.concatenate(pieces, axis=0), tri_excl)
    a = []
    for r, rs in enumerate(blocks):
        ar = jnp.exp(lbs[r] + sums[rs, :SB_BLK] + c[rs])
        if diag and r == 0:
            ar = jnp.where(strict, ar, 0.0)
        a.append(ar.astype(BF16))
    return lbs, a, sums[:, SB_BLK:]


def _sb_fwd(proj, rows, exchange=None):
    qt = min(SB_QT, rows)
    subs = qt // SB_BLK

    def body(q_ref, k_ref, v_ref, o_ref):
        i = pl.program_id(1)
        tri_excl = _sb_tri(False)
        heads = [slice(SB_DH * hh, SB_DH * (hh + 1)) for hh in range(2)]
        qs = [(q_ref[:, sl] * 0.125).astype(BF16) for sl in heads]

        def tiles(start, carry, diag, row0=0):
            out = []
            for hh, sl in enumerate(heads):
                o, c = carry[2 * hh], carry[2 * hh + 1]
                k = k_ref[pl.ds(start, SB_BLK), sl].astype(BF16)
                v = v_ref[pl.ds(start, SB_BLK), sl].astype(BF16)
                _, a, dc = _sb_tile(qs[hh][row0:], k, c[row0:], tri_excl, diag)
                out += [_add_rows(o, _dot(jnp.concatenate(a, axis=0), v), row0), _add_rows(c, dc, row0)]
            return tuple(out)

        carry = (jnp.zeros((qt, SB_DH), F32), jnp.zeros((qt, SB_BLK), F32)) * 2
        for sub in reversed(range(subs)):
            start = pl.multiple_of(i * qt + sub * SB_BLK, SB_BLK)
            carry = tiles(start, carry, True, sub * SB_BLK)

        def step(jj, carry):
            start = pl.multiple_of((i * subs - 1 - jj) * SB_BLK, SB_BLK)
            return tiles(start, carry, False)

        carry = _sb_sweep(i * subs, step, carry, (1, 3))
        for hh, sl in enumerate(heads):
            o_ref[:, sl] = carry[2 * hh]

    return _hosted_call(
        body, exchange, grid=(SB_HEADS // 2, rows // qt),
        in_specs=[pl.BlockSpec((qt, LANES), lambda h, i: (i, C_QS // LANES + h)),
                  pl.BlockSpec((rows, LANES), lambda h, i: (0, C_KS // LANES + h)),
                  pl.BlockSpec((rows, LANES), lambda h, i: (0, C_VS // LANES + h))],
        out_specs=[pl.BlockSpec((qt, LANES), lambda h, i: (i, h))],
        out_shape=[jax.ShapeDtypeStruct((rows, SB_W), F32)], scratch_shapes=[],
        args=(proj, proj, proj), name="sb_attention_fwd", compiler_params=_cparams("arbitrary", "arbitrary"))


def _sb_bwd(proj, o_s, do_s, rows, exchange=None):
    qt = min(SB_QT, rows)
    subs = qt // SB_BLK

    def body(q_ref, k_ref, v_ref, o_ref, do_ref, dq_ref, dk_ref, dv_ref):
        i = pl.program_id(1)

        @pl.when(i == 0)
        def _():
            dk_ref[...] = jnp.zeros(dk_ref.shape, F32)
            dv_ref[...] = jnp.zeros(dv_ref.shape, F32)

        tri_excl, tri_incl = _sb_tri(False), _sb_tri(True)
        heads = [slice(SB_DH * hh, SB_DH * (hh + 1)) for hh in range(2)]
        qs = [(q_ref[:, sl] * 0.125).astype(BF16) for sl in heads]
        dobs = [do_ref[:, sl].astype(BF16) for sl in heads]
        dsums = [jnp.broadcast_to(jnp.sum(dob.astype(F32) * o_ref[:, sl], axis=1, keepdims=True), (qt, SB_BLK))
                 for dob, sl in zip(dobs, heads)]

        def tiles(start, carry, diag, row0=0):
            out = []
            strict = _sb_mask(SB_BLK) if diag else None
            for hh, sl in enumerate(heads):
                dq, c, cp = carry[3 * hh:3 * hh + 3]
                q, dob = qs[hh][row0:], dobs[hh][row0:]
                dsum, cpr = dsums[hh][row0:], cp[row0:]
                blocks = _row_blocks(q.shape[0])
                k = k_ref[pl.ds(start, SB_BLK), sl].astype(BF16)
                v = v_ref[pl.ds(start, SB_BLK), sl].astype(BF16)
                lbs, a, dc = _sb_tile(q, k, c[row0:], tri_excl, diag)
                da = _dot(dob, v, 1, 1)
                ps = [a[r].astype(F32) * da[rs] for r, rs in enumerate(blocks)]
                psums = _dot(jnp.concatenate([_hilo(p) for p in ps], axis=0), tri_incl)
                dzs = []
                for r, rs in enumerate(blocks):
                    left = dsum[rs] - (psums[rs, :SB_BLK] + cpr[rs])
                    dz = ps[r] - jnp.exp(lbs[r]) * (ps[r] + left)
                    if diag and r == 0:
                        dz = jnp.where(strict, dz, 0.0)
                    dzs.append(dz.astype(BF16))
                dzb, ab = jnp.concatenate(dzs, axis=0), jnp.concatenate(a, axis=0)
                dk_ref[pl.ds(start, SB_BLK), sl] += _dot(dzb, q, 0, 0)
                dv_ref[pl.ds(start, SB_BLK), sl] += _dot(ab, dob, 0, 0)
                out += [_add_rows(dq, _dot(dzb, k), row0), _add_rows(c, dc, row0),
                        _add_rows(cp, psums[:, SB_BLK:], row0)]
            return tuple(out)

        zero = jnp.zeros((qt, SB_BLK), F32)
        carry = (jnp.zeros((qt, SB_DH), F32), zero, zero) * 2
        for sub in reversed(range(subs)):
            start = pl.multiple_of(i * qt + sub * SB_BLK, SB_BLK)
            carry = tiles(start, carry, True, sub * SB_BLK)

        def step(jj, carry):
            start = pl.multiple_of((i * subs - 1 - jj) * SB_BLK, SB_BLK)
            return tiles(start, carry, False)

        carry = _sb_sweep(i * subs, step, carry, (1, 4))
        for hh, sl in enumerate(heads):
            dq_ref[:, sl] = carry[3 * hh] * 0.125

    whole = lambda base: pl.BlockSpec((rows, LANES), functools.partial(lambda h, i, b: (0, b + h), b=base))
    blk = lambda base: pl.BlockSpec((qt, LANES), functools.partial(lambda h, i, b: (i, b + h), b=base))
    return _hosted_call(
        body, exchange, grid=(SB_HEADS // 2, rows // qt),
        in_specs=[blk(C_QS // LANES), whole(C_KS // LANES), whole(C_VS // LANES), blk(0), blk(0)],
        out_specs=[blk(0), whole(0), whole(0)],
        out_shape=[jax.ShapeDtypeStruct((rows, SB_W), F32)] * 3, scratch_shapes=[],
        args=(proj, proj, proj, o_s, do_s), name="sb_attention_bwd",
        compiler_params=_cparams("arbitrary", "arbitrary"))


def _gla_chunk_common(g_all):
    r_i, c_i = _iota2((GLA_CHUNK, GLA_CHUNK), 0), _iota2((GLA_CHUNK, GLA_CHUNK), 1)
    tri = (c_i <= r_i).astype(BF16)
    return _dot_rhs_exact(tri, g_all), r_i, c_i


def _gla_scaled(qh, kh, bh):
    ref = bh[GLA_CHUNK // 2:GLA_CHUNK // 2 + 1, :]
    eq, ek = jnp.exp(bh - ref), jnp.exp(ref - bh)
    mask = _iota2((GLA_CHUNK, GLA_CHUNK), 1) <= _iota2((GLA_CHUNK, GLA_CHUNK), 0)
    return eq, ek, qh * eq, kh * ek, mask


def _gla_fwd(proj, gk, rows, exchange=None):
    n_chunks = rows // GLA_CHUNK
    step_rows = GLA_CHUNK * GLA_STEP_CHUNKS

    def body(q_ref, k_ref, v_ref, g_ref, o_ref, sall_ref, s_scr):
        @pl.when(pl.program_id(0) == 0)
        def _():
            s_scr[...] = jnp.zeros(s_scr.shape, F32)

        ones = jnp.ones((GLA_CHUNK, GLA_DV), BF16)
        states = [s_scr[h] for h in range(GLA_HEADS)]
        for ci in range(GLA_STEP_CHUNKS):
            rs = slice(GLA_CHUNK * ci, GLA_CHUNK * (ci + 1))
            g_all = g_ref[rs, :]
            b_all, _, _ = _gla_chunk_common(g_all)
            for h in range(GLA_HEADS):
                sl = slice(GLA_DK * h, GLA_DK * (h + 1))
                vs = slice(GLA_DV * h, GLA_DV * (h + 1))
                qh, kh, vh = q_ref[rs, sl] * 0.125, k_ref[rs, sl], v_ref[rs, vs]
                bh, gh = b_all[:, sl], g_all[:, sl]
                s = states[h]
                sall_ref[ci, h] = s
                _, _, qs, ks, mask = _gla_scaled(qh, kh, bh)
                a = jnp.where(mask, _dot(qs, ks, 1, 1), 0.0)
                o_ref[rs, vs] = _dot(qh * jnp.exp(bh), s) + _dot(a, vh)
                bl_col = _dot_lhs_exact(gh, ones, 0, 0)
                kd = kh * jnp.exp(bh[GLA_CHUNK - 1:GLA_CHUNK, :] - bh)
                states[h] = jnp.exp(bl_col) * s + _dot(kd, vh, 0, 0)
        for h in range(GLA_HEADS):
            s_scr[h] = states[h]

    c64 = lambda w, base: pl.BlockSpec((step_rows, w), functools.partial(lambda n, b: (n, b), b=base))
    return _hosted_call(
        body, exchange, grid=(rows // step_rows,),
        in_specs=[c64(GLA_QK_W, C_QG // GLA_QK_W), c64(GLA_QK_W, C_KG // GLA_QK_W), c64(GLA_V_W, C_VG // GLA_V_W),
                  c64(GLA_QK_W, 0)],
        out_specs=[c64(GLA_V_W, 0),
                   pl.BlockSpec((GLA_STEP_CHUNKS, GLA_HEADS, GLA_DK, GLA_DV), lambda n: (n, 0, 0, 0))],
        out_shape=[jax.ShapeDtypeStruct((rows, GLA_V_W), F32),
                   jax.ShapeDtypeStruct((n_chunks, GLA_HEADS, GLA_DK, GLA_DV), F32)],
        scratch_shapes=[pltpu.VMEM((GLA_HEADS, GLA_DK, GLA_DV), F32)],
        args=(proj, proj, proj, gk), name="gla_fwd", compiler_params=_cparams("arbitrary"))


def _gla_bwd(proj, gk, do_g, s_all, rows, exchange=None):
    n_chunks = rows // GLA_CHUNK

    def body(q_ref, k_ref, v_ref, g_ref, do_ref, sall_ref, dq_ref, dk_ref, dv_ref, dg_ref, ds_scr):
        @pl.when(pl.program_id(0) == 0)
        def _():
            ds_scr[...] = jnp.zeros(ds_scr.shape, F32)

        ones = jnp.ones((GLA_CHUNK, GLA_DV), BF16)
        ones8 = jnp.ones((8, GLA_DV), F32)
        last_row = _iota2((GLA_CHUNK, GLA_DK), 0) == GLA_CHUNK - 1
        dstates = [ds_scr[h] for h in range(GLA_HEADS)]
        for ci in reversed(range(GLA_STEP_CHUNKS)):
            cs = slice(GLA_CHUNK * ci, GLA_CHUNK * (ci + 1))
            g_all = g_ref[cs, :]
            b_all, r_i, c_i = _gla_chunk_common(g_all)
            triu = (c_i >= r_i).astype(BF16)
            for h in range(GLA_HEADS):
                sl = slice(GLA_DK * h, GLA_DK * (h + 1))
                vs = slice(GLA_DV * h, GLA_DV * (h + 1))
                qh, kh, vh = q_ref[cs, sl] * 0.125, k_ref[cs, sl], v_ref[cs, vs]
                bh, gh = b_all[:, sl], g_all[:, sl]
                doh = do_ref[cs, vs]
                s, ds = sall_ref[ci, h], dstates[h]
                eb = jnp.exp(bh)
                ekd = jnp.exp(bh[GLA_CHUNK - 1:GLA_CHUNK, :] - bh)
                ebl = jnp.exp(_dot_lhs_exact(gh, ones, 0, 0))
                qb, kd = qh * eb, kh * ekd
                dq = _dot(doh, s, 1, 1) * eb
                dk = _dot(vh, ds, 1, 1) * ekd
                dv = _dot(kd, ds)
                dbl = jnp.sum(dk * kh, axis=0, keepdims=True) + _dot3(ones8, ebl * s * ds, 1, 1)[0:1, :]
                eq, ek, qs, ks, mask = _gla_scaled(qh, kh, bh)
                a = jnp.where(mask, _dot(qs, ks, 1, 1), 0.0)
                da = jnp.where(mask, _dot(doh, vh, 1, 1), 0.0)
                dq = dq + _dot(da, ks) * eq
                dk = dk + _dot(da, qs, 0, 0) * ek
                dv = dv + _dot(a, doh, 0, 0)
                db = qh * dq - kh * dk + jnp.where(last_row, dbl, 0.0)
                dq_ref[cs, sl] = dq * 0.125
                dk_ref[cs, sl] = dk
                dv_ref[cs, vs] = dv
                dg_ref[cs, sl] = _dot_rhs_exact(triu, db)
                dstates[h] = _dot(qb, doh, 0, 0) + ebl * ds
        for h in range(GLA_HEADS):
            ds_scr[h] = dstates[h]

    step_rows = GLA_CHUNK * GLA_STEP_CHUNKS
    last = rows // step_rows - 1
    c64 = lambda w, base: pl.BlockSpec((step_rows, w), functools.partial(lambda n, b: (last - n, b), b=base))
    return _hosted_call(
        body, exchange, grid=(rows // step_rows,),
        in_specs=[c64(GLA_QK_W, C_QG // GLA_QK_W), c64(GLA_QK_W, C_KG // GLA_QK_W), c64(GLA_V_W, C_VG // GLA_V_W),
                  c64(GLA_QK_W, 0), c64(GLA_V_W, 0),
                  pl.BlockSpec((GLA_STEP_CHUNKS, GLA_HEADS, GLA_DK, GLA_DV), lambda n: (last - n, 0, 0, 0))],
        out_specs=[c64(GLA_QK_W, 0), c64(GLA_QK_W, 0), c64(GLA_V_W, 0), c64(GLA_QK_W, 0)],
        out_shape=[jax.ShapeDtypeStruct((rows, GLA_QK_W), F32), jax.ShapeDtypeStruct((rows, GLA_QK_W), F32),
                   jax.ShapeDtypeStruct((rows, GLA_V_W), F32), jax.ShapeDtypeStruct((rows, GLA_QK_W), F32)],
        scratch_shapes=[pltpu.VMEM((GLA_HEADS, GLA_DK, GLA_DV), F32)],
        args=(proj, proj, proj, gk, do_g, s_all), name="gla_bwd", compiler_params=_cparams("arbitrary"))


def _mem_kv_fwd(mem, mem_norm_w, w_mkv, mk_norm_w):
    def body(mem_ref, mw_ref, w_ref, kw_ref, memn_ref, kpre_ref, kn_ref, v_ref):
        xb = mem_ref[...]
        r = lax.rsqrt(jnp.mean(xb * xb, axis=-1, keepdims=True) + RMS_EPS)
        mem_n = (xb * r * mw_ref[...]).astype(BF16)
        memn_ref[...] = mem_n
        kv = _dot(mem_n, w_ref[...])
        kpre_ref[...] = kv[:, :D_MODEL]
        v_ref[...] = kv[:, D_MODEL:].astype(BF16)
        for h in range(MEM_HEADS):
            sl = slice(MEM_DH * h, MEM_DH * (h + 1))
            kh = kv[:, sl]
            rk = lax.rsqrt(jnp.mean(kh * kh, axis=-1, keepdims=True) + RMS_EPS)
            kn_ref[:, sl] = (kh * rk * kw_ref[...]).astype(BF16)

    return pl.pallas_call(
        body, name="mem_kv_fwd",
        out_shape=[jax.ShapeDtypeStruct((MEM_LEN, D_MODEL), BF16), jax.ShapeDtypeStruct((MEM_LEN, D_MODEL), F32),
                   jax.ShapeDtypeStruct((MEM_LEN, D_MODEL), BF16), jax.ShapeDtypeStruct((MEM_LEN, D_MODEL), BF16)],
        compiler_params=_cparams(),
    )(mem, mem_norm_w, w_mkv, mk_norm_w)


def _mem_kv_bwd(mem, mem_norm_w, w_mkv, mk_norm_w, mem_n, kpre, dkn, dv):
    def body(mem_ref, mw_ref, w_ref, kw_ref, memn_ref, kpre_ref, dkn_ref, dv_ref, dw_ref, dkw_ref, dmw_ref):
        dkw = jnp.zeros((1, MEM_DH), F32)
        dk_parts = []
        for h in range(MEM_HEADS):
            sl = slice(MEM_DH * h, MEM_DH * (h + 1))
            kh, dkh = kpre_ref[:, sl], dkn_ref[:, sl]
            rk = lax.rsqrt(jnp.mean(kh * kh, axis=-1, keepdims=True) + RMS_EPS)
            xh = kh * rk
            dxh = dkh * kw_ref[...]
            dkw = dkw + jnp.sum(dkh * xh, axis=0, keepdims=True)
            dk_parts.append(rk * (dxh - xh * jnp.mean(dxh * xh, axis=-1, keepdims=True)))
        dkw_ref[...] = dkw
        dkv = jnp.concatenate(dk_parts + [dv_ref[...]], axis=1).astype(BF16)
        dw_ref[...] = _dot(memn_ref[...], dkv, 0, 0)
        dmem_n = _dot(dkv, w_ref[...], 1, 1)
        xb = mem_ref[...]
        r = lax.rsqrt(jnp.mean(xb * xb, axis=-1, keepdims=True) + RMS_EPS)
        dmw_ref[...] = jnp.sum(dmem_n * (xb * r), axis=0, keepdims=True)

    return pl.pallas_call(
        body, name="mem_kv_bwd",
        out_shape=[jax.ShapeDtypeStruct((D_MODEL, 2 * D_MODEL), F32), jax.ShapeDtypeStruct((1, MEM_DH), F32),
                   jax.ShapeDtypeStruct((1, D_MODEL), F32)],
        compiler_params=_cparams(),
    )(mem, mem_norm_w, w_mkv, mk_norm_w, mem_n, kpre, dkn, dv)


def _xattn_head(qh, kn_h, qw):
    rq = lax.rsqrt(jnp.mean(qh * qh, axis=-1, keepdims=True) + RMS_EPS)
    xh = qh * rq
    qn = (xh * qw).astype(BF16)
    s = _dot(qn, kn_h, 1, 1) * (1.0 / 16.0)
    e = jnp.exp(s - jnp.max(s, axis=-1, keepdims=True))
    p = e / jnp.sum(e, axis=-1, keepdims=True)
    return rq, xh, qn, p


def _xattn_fwd(qm, kn, v, mq_norm_w, bm):
    def fn(qb, knb, vb, qw):
        outs = []
        for h in range(MEM_HEADS):
            sl = slice(MEM_DH * h, MEM_DH * (h + 1))
            _, _, _, p = _xattn_head(qb[:, sl], knb[:, sl], qw)
            outs.append(_dot(p, vb[:, sl]))
        return (jnp.concatenate(outs, axis=1),)
    return _rowcall("xattn_fwd", fn, [(qm, D_MODEL, 0)], [kn, v, mq_norm_w], [(D_MODEL, BF16)], [], bm,
                    qm.shape[0])[0]


def _xattn_bwd(qm, kn, v, mq_norm_w, do, bm):
    def fn(qb, dob, knb, vb, qw):
        dq_parts, dkn_parts, dv_parts = [], [], []
        dqw = jnp.zeros((1, MEM_DH), F32)
        for h in range(MEM_HEADS):
            sl = slice(MEM_DH * h, MEM_DH * (h + 1))
            rq, xh, qn, p = _xattn_head(qb[:, sl], knb[:, sl], qw)
            doh = dob[:, sl].astype(BF16)
            dp = _dot(doh, vb[:, sl], 1, 1)
            ds = (p * (dp - jnp.sum(dp * p, axis=-1, keepdims=True)) * (1.0 / 16.0)).astype(BF16)
            dqn = _dot(ds, knb[:, sl])
            dkn_parts.append(_dot(ds, qn, 0, 0))
            dv_parts.append(_dot(p, doh, 0, 0))
            dqw = dqw + jnp.sum(dqn * xh, axis=0, keepdims=True)
            dxh = dqn * qw
            dq_parts.append(rq * (dxh - xh * jnp.mean(dxh * xh, axis=-1, keepdims=True)))
        return (jnp.concatenate(dq_parts, axis=1), jnp.concatenate(dkn_parts, axis=1),
                jnp.concatenate(dv_parts, axis=1), dqw)
    return _rowcall("xattn_bwd", fn, [(qm, D_MODEL, 0), (do, D_MODEL, 0)], [kn, v, mq_norm_w],
                    [(D_MODEL, BF16)], [(MEM_LEN, D_MODEL), (MEM_LEN, D_MODEL), (1, MEM_DH)], bm, qm.shape[0])


FF_BN = 1408
FF_NB = D_FF // FF_BN


def _ffn_up(h3, w_gate_up, rows, bm):
    def body(h_ref, wg_ref, wu_ref, gate_ref, up_ref, act_ref):
        hb = h_ref[...]
        gate = _dot(hb, wg_ref[...])
        up = _dot(hb, wu_ref[...])
        gate_ref[...] = gate
        up_ref[...] = up
        act_ref[...] = (gate * _sigmoid(gate) * up).astype(BF16)

    out_blk = pl.BlockSpec((bm, FF_BN), lambda i, j: (i, j))
    return pl.pallas_call(
        body, name="ffn_up", grid=(rows // bm, FF_NB),
        in_specs=[pl.BlockSpec((bm, D_MODEL), lambda i, j: (i, 0)),
                  pl.BlockSpec((D_MODEL, FF_BN), lambda i, j: (0, j)),
                  pl.BlockSpec((D_MODEL, FF_BN), lambda i, j: (0, FF_NB + j))],
        out_specs=[out_blk, out_blk, out_blk],
        out_shape=[jax.ShapeDtypeStruct((rows, D_FF), F32), jax.ShapeDtypeStruct((rows, D_FF), F32),
                   jax.ShapeDtypeStruct((rows, D_FF), BF16)],
        compiler_params=_cparams("parallel", "arbitrary"),
    )(h3, w_gate_up, w_gate_up)


def _ffn_act_bwd(dy, w_down, gate, up, rows, bm):
    def body(dy_ref, wd_ref, gate_ref, up_ref, o_ref):
        dact = _dot(dy_ref[...], wd_ref[...], 1, 1)
        g, u = gate_ref[...], up_ref[...]
        sg = _sigmoid(g)
        o_ref[0] = (dact * u * (sg * (1.0 + g * (1.0 - sg)))).astype(BF16)
        o_ref[1] = (dact * (g * sg)).astype(BF16)

    blk = pl.BlockSpec((bm, FF_BN), lambda i, j: (i, j))
    return pl.pallas_call(
        body, name="ffn_act_bwd", grid=(rows // bm, FF_NB),
        in_specs=[pl.BlockSpec((bm, D_MODEL), lambda i, j: (i, 0)),
                  pl.BlockSpec((FF_BN, D_MODEL), lambda i, j: (j, 0)), blk, blk],
        out_specs=pl.BlockSpec((2, bm, FF_BN), lambda i, j: (0, i, j)),
        out_shape=jax.ShapeDtypeStruct((2, rows, D_FF), BF16),
        compiler_params=_cparams("parallel", "arbitrary"),
    )(dy, w_down, gate, up)


def _pack_rows(a):
    flat = a.reshape(-1)
    pad = (-flat.shape[0]) % 1024
    if pad:
        flat = jnp.concatenate([flat, jnp.zeros((pad,), flat.dtype)])
    return flat.reshape(-1, 1024)


def _pack_owner_rows(a8):
    flat = a8.reshape(N_DEV, -1)
    pad = (-flat.shape[1]) % 1024
    if pad:
        flat = jnp.concatenate([flat, jnp.zeros((N_DEV, pad), flat.dtype)], axis=1)
    return flat.reshape(N_DEV, -1, 1024)


class _Group:
    def __init__(self, names, extra=0, block=None):
        self.names, self.extra, self.offs = names, extra, {}
        o = 0
        for n in names:
            self.offs[n] = o
            o += SHARD_ROWS[n]
        self.shard_rows = o
        self.rows = -(-(o + extra) // 16) * 16
        self.block = block

    def _fill(self, parts, axis, dtype):
        used = sum(p.shape[axis] for p in parts)
        if used < self.rows:
            shape = list(parts[0].shape)
            shape[axis] = self.rows - used
            parts = parts + [jnp.zeros(shape, dtype)]
        return jnp.concatenate(parts, axis=axis)

    def pack(self, shards, extra_rows, dtype):
        parts = [_pack_rows(shards[n].astype(dtype)) for n in self.names] + [r.astype(dtype) for r in extra_rows]
        return self._fill(parts, 0, dtype)

    def pack_for_owners(self, full, extra_rows):
        parts = [_pack_owner_rows(_split_for_owners(n, full[n])) for n in self.names]
        if extra_rows:
            parts.append(jnp.broadcast_to(jnp.concatenate(extra_rows, axis=0)[None], (N_DEV, len(extra_rows), 1024)))
        return self._fill(parts, 1, F32)

    def unpack(self, pack, name):
        r, c = SHARD_SHAPE[name]
        seg = pack[self.offs[name]:self.offs[name] + SHARD_ROWS[name]].reshape(-1)[:r * c]
        return seg.reshape(1, r, c)

    def gathered(self, gathered, name):
        r, c = SHARD_SHAPE[name]
        seg = gathered[:, self.offs[name]:self.offs[name] + SHARD_ROWS[name]]
        seg = seg.reshape(N_DEV, -1)[:, :r * c].reshape(N_DEV, r, c)
        if name in ROW_SHARDED:
            return seg.reshape(N_DEV * r, c)
        return seg.transpose(1, 0, 2).reshape(r, N_DEV * c)


ROW_SHARDED = ("w_out", "w_mq", "w_mo", "w_down")
N_EXTRA = len(REPL) + 1
AG_FIRST = _Group(("w_in", "w_gk_up"))
AG_MID = _Group(("w_out", "w_mq", "w_mkv"))
AG_LATE = _Group(("w_mo", "w_gate_up", "w_down"))
RS_FFN = _Group(("w_gate_up", "w_down"), block=264)
RS_MID = _Group(("w_out", "w_mq", "w_mkv", "w_mo"), block=320)
RS_LAST = _Group(("w_in", "w_gk_up"), extra=N_EXTRA, block=200)


def _split_for_owners(name, full):
    r, c = SHARD_SHAPE[name]
    if name in ROW_SHARDED:
        return full.reshape(N_DEV, r, c)
    return full.reshape(r, N_DEV, c).transpose(1, 0, 2)


def _repl_row(a):
    flat = a.reshape(-1)
    return jnp.concatenate([flat, jnp.zeros((1024 - flat.shape[0],), flat.dtype)]).reshape(1, 1024)


def _local_step(x, mem, tgt, wf, rp, shards=None, scatter=False):
    rows = x.shape[0]
    bm = min(512, rows)
    bmx = min(256, rows)
    mt = min(MM_TILE, rows)
    kt = min(512, rows)
    w_cat, wgk_pad = wf["w_cat"], wf["wgk_pad"]
    wg_t = jnp.tile(rp["gla_norm_w"], (1, GLA_HEADS))
    ws_t = jnp.tile(rp["sb_norm_w"], (1, SB_HEADS))
    lane = jnp.arange(GLA_V_W)
    grp_g = (lane[:, None] // GLA_DV == lane[None, :] // GLA_DV).astype(BF16)
    grp_s = (lane[:, None] // SB_DH == lane[None, :] // SB_DH).astype(BF16)

    h1 = _rms_fwd("mix_norm_fwd", x, rp["mix_norm_w"], bm)
    proj = _matmul("in_proj", h1, w_cat, "nn", rows, PROJ_W, D_MODEL, F32, mt, 640, D_MODEL)
    gk = _gate_fwd(proj, wgk_pad, rp["b_gk"], bm)
    if shards is None:
        o_g, s_all = _gla_fwd(proj, gk, rows)
        (o_s,) = _sb_fwd(proj, rows)
    else:
        o_g, s_all, got_mid = _gla_fwd(proj, gk, rows, _Exchange(AG_MID.pack(shards, [], BF16), scatter=False))
        o_s, got_late = _sb_fwd(proj, rows, _Exchange(AG_LATE.pack(shards, [], BF16), scatter=False))
        wf = {**wf, **{n: AG_MID.gathered(got_mid, n) for n in AG_MID.names},
              **{n: AG_LATE.gathered(got_late, n) for n in AG_LATE.names}}
    cat = _mix_out_fwd(o_g, proj, o_s, wg_t, ws_t, grp_g, grp_s, bm)
    x1 = _matmul("out_proj", cat, wf["w_out"], "nn", rows, D_MODEL, D_MODEL, F32, mt, MM_TILE, D_MODEL, residual=x)
    h2 = _rms_fwd("xattn_norm_fwd", x1, rp["xattn_norm_w"], bm)
    qm = _matmul("mq_proj", h2, wf["w_mq"], "nn", rows, D_MODEL, D_MODEL, F32, mt, MM_TILE, D_MODEL)
    mem_n, kpre, kn, v_m = _mem_kv_fwd(mem, rp["mem_norm_w"], wf["w_mkv"], rp["mk_norm_w"])
    o_m = _xattn_fwd(qm, kn, v_m, rp["mq_norm_w"], bmx)
    x2 = _matmul("mo_proj", o_m, wf["w_mo"], "nn", rows, D_MODEL, D_MODEL, F32, mt, MM_TILE, D_MODEL, residual=x1)
    h3 = _rms_fwd("ffn_norm_fwd", x2, rp["ffn_norm_w"], bm)
    gate, up, act = _ffn_up(h3, wf["w_gate_up"], rows, bm)
    y = _matmul("ffn_down", act, wf["w_down"], "nn", rows, D_MODEL, D_FF, F32, mt, MM_TILE, FF_BN, residual=x2)
    dy, sq = _loss_kernel(y, tgt, bm)

    g = {}
    dgu = _ffn_act_bwd(dy, wf["w_down"], gate, up, rows, bm)
    g["w_down"] = _matmul("grad_w_down", act, dy, "tn", D_FF, D_MODEL, rows, F32, FF_BN, MM_TILE, kt)
    nkb = FF_NB
    dh3 = _matmul("ffn_up_bwd", dgu, wf["w_gate_up"], "nt", rows, D_MODEL, 2 * D_FF, F32, mt, MM_TILE, FF_BN,
                  a_spec=pl.BlockSpec((None, mt, FF_BN), lambda i, j, kk: (kk // nkb, i, kk % nkb)))
    g["w_gate_up"] = _matmul(
        "grad_w_gate_up", h3, dgu, "tn", D_MODEL, 2 * D_FF, rows, F32, MM_TILE, FF_BN, kt,
        b_spec=pl.BlockSpec((None, kt, FF_BN), lambda i, j, kk: (j // nkb, kk, j % nkb)))
    dx2, g["ffn_norm_w"] = _rms_bwd("ffn_norm_bwd", x2, rp["ffn_norm_w"], dh3, dy, bm)

    do_m = _matmul("mo_proj_bwd", dx2, wf["w_mo"], "nt", rows, D_MODEL, D_MODEL, BF16, mt, MM_TILE, D_MODEL)
    g["w_mo"] = _matmul("grad_w_mo", o_m, dx2, "tn", D_MODEL, D_MODEL, rows, F32, MM_TILE, MM_TILE, kt)
    dqm, dkn, dv_m, g["mq_norm_w"] = _xattn_bwd(qm, kn, v_m, rp["mq_norm_w"], do_m, bmx)
    g["w_mkv"], g["mk_norm_w"], g["mem_norm_w"] = _mem_kv_bwd(
        mem, rp["mem_norm_w"], wf["w_mkv"], rp["mk_norm_w"], mem_n, kpre, dkn, dv_m)
    dh2 = _matmul("mq_proj_bwd", dqm, wf["w_mq"], "nt", rows, D_MODEL, D_MODEL, F32, mt, MM_TILE, D_MODEL)
    g["w_mq"] = _matmul("grad_w_mq", h2, dqm, "tn", D_MODEL, D_MODEL, rows, F32, MM_TILE, MM_TILE, kt)
    dx1, g["xattn_norm_w"] = _rms_bwd("xattn_norm_bwd", x1, rp["xattn_norm_w"], dh2, dx2, bm)

    dcat = _matmul("out_proj_bwd", dx1, wf["w_out"], "nt", rows, D_MODEL, D_MODEL, F32, mt, MM_TILE, D_MODEL)
    g["w_out"] = _matmul("grad_w_out", cat, dx1, "tn", D_MODEL, D_MODEL, rows, F32, MM_TILE, MM_TILE, kt)
    do_g, dgg, do_s, dwg, dws = _mix_out_bwd(dcat, o_g, proj, o_s, wg_t, ws_t, grp_g, grp_s, bm)
    g["gla_norm_w"] = dwg.reshape(GLA_HEADS, GLA_DV).sum(axis=0, keepdims=True)
    g["sb_norm_w"] = dws.reshape(SB_HEADS, SB_DH).sum(axis=0, keepdims=True)
    recv = {}
    if scatter:
        dq_s, dk_s, dv_s, recv["ffn"] = _sb_bwd(
            proj, o_s, do_s, rows, _Exchange(RS_FFN.pack_for_owners(g, []), scatter=True))
        dq_g, dk_g, dv_g, dgk, recv["mid"] = _gla_bwd(
            proj, gk, do_g, s_all, rows, _Exchange(RS_MID.pack_for_owners(g, []), scatter=True))
    else:
        dq_s, dk_s, dv_s = _sb_bwd(proj, o_s, do_s, rows)
        dq_g, dk_g, dv_g, dgk = _gla_bwd(proj, gk, do_g, s_all, rows)
    dlr, dwgk, g["b_gk"] = _gate_bwd(proj, wgk_pad, rp["b_gk"], dgk, bm)
    g["w_gk_up"] = dwgk[:GATE_RANK]
    dproj = jnp.concatenate([dq_g.astype(BF16), dk_g.astype(BF16), dv_g.astype(BF16), dgg, dq_s.astype(BF16),
                             dk_s.astype(BF16), dv_s.astype(BF16), dlr], axis=1)
    dh1 = _matmul("in_proj_bwd", dproj, w_cat, "nt", rows, D_MODEL, PROJ_W, F32, mt, MM_TILE, 640)
    dw_cat = _matmul("grad_w_in", h1, dproj, "tn", D_MODEL, PROJ_W, rows, F32, MM_TILE, 640, kt)
    g["w_in"] = jnp.concatenate([dw_cat[:, :C_QS], dw_cat[:, C_LR:C_LR + GATE_RANK], dw_cat[:, C_QS:C_LR]], axis=1)
    dx, g["mix_norm_w"] = _rms_bwd("mix_norm_bwd", x, rp["mix_norm_w"], dh1, dx1, bm)
    return sq, dx, g, recv


def _first_weights(gathered):
    w_in = AG_FIRST.gathered(gathered, "w_in")
    lr_end = C_QS + GATE_RANK
    w_cat = jnp.concatenate([w_in[:, :C_QS], w_in[:, lr_end:], w_in[:, C_QS:lr_end],
                             jnp.zeros((D_MODEL, PROJ_W - D_IN), BF16)], axis=1)
    wgk = AG_FIRST.gathered(gathered, "w_gk_up")
    return {"w_cat": w_cat, "wgk_pad": jnp.concatenate([wgk, jnp.zeros((LANES - GATE_RANK, GLA_QK_W), BF16)], axis=0)}


def kernel(x, mem, mix_norm_w, w_in, w_gk_up, b_gk, gla_norm_w, sb_norm_w, w_out, xattn_norm_w, mem_norm_w, w_mq, w_mkv, mq_norm_w, mk_norm_w, w_mo, ffn_norm_w, w_gate_up, w_down, loss_target, m_mix_norm_w, m_w_in, m_w_gk_up, m_b_gk, m_gla_norm_w, m_sb_norm_w, m_w_out, m_xattn_norm_w, m_mem_norm_w, m_w_mq, m_w_mkv, m_mq_norm_w, m_mk_norm_w, m_w_mo, m_ffn_norm_w, m_w_gate_up, m_w_down, v_mix_norm_w, v_w_in, v_w_gk_up, v_b_gk, v_gla_norm_w, v_sb_norm_w, v_w_out, v_xattn_norm_w, v_mem_norm_w, v_w_mq, v_w_mkv, v_mq_norm_w, v_mk_norm_w, v_w_mo, v_ffn_norm_w, v_w_gate_up, v_w_down):
    given = dict(locals())
    w = {n: given[n][0] for n in WEIGHTS}
    m = {n: given["m_" + n][0] for n in WEIGHTS}
    v = {n: given["v_" + n][0] for n in WEIGHTS}

    wf = _first_weights(_exchange_call("gather_first_weights", AG_FIRST.pack(w, [], BF16), scatter=False))
    rp = {n: w[n].reshape(1, -1) for n in REPL}
    sq, dx, g, recv = _local_step(x[0], mem[0], loss_target[0], wf, rp, shards=w, scatter=True)

    loss_row = _repl_row(jnp.sum(sq).reshape(1) * (0.5 / D_MODEL))
    repl_rows = [_repl_row(g[n]) for n in REPL] + [loss_row]
    recv["last"] = _exchange_call("scatter_last_gradients", RS_LAST.pack_for_owners(g, repl_rows), scatter=True)

    zero_row = jnp.zeros((1, 1024), F32)
    out_packs = {}
    for key, grp in (("ffn", RS_FFN), ("mid", RS_MID), ("last", RS_LAST)):
        extra = lambda t: [_repl_row(t[n]) for n in REPL] + [zero_row] if grp.extra else []
        out_packs[key] = _adamw("sum_adamw_" + key, recv[key], *[grp.pack(t, extra(t), F32) for t in (w, m, v)],
                                block=grp.block)

    def unpack(kind, name):
        for key, grp in (("ffn", RS_FFN), ("mid", RS_MID), ("last", RS_LAST)):
            if name in grp.names:
                return grp.unpack(out_packs[key][kind], name)
        row = out_packs["last"][kind][RS_LAST.shard_rows + REPL.index(name)]
        return row[:w[name].shape[-1]].reshape(1, -1)

    loss = out_packs["last"][0][RS_LAST.shard_rows + len(REPL), 0]
    outs = [loss, dx[None]]
    for kind in range(4):
        outs += [unpack(kind, n) for n in WEIGHTS]
    return tuple(outs)
```

```python
import functools
import math

import jax
import jax.numpy as jnp
from jax import lax
from jax.experimental import pallas as pl
from jax.experimental.pallas import tpu as pltpu

F32 = jnp.float32
BF16 = jnp.bfloat16

N_DEV = 8
D_MODEL = 1024
GLA_HEADS = 4
GLA_DK = 64
GLA_DV = 128
GLA_CHUNK = 64
GLA_STEP_CHUNKS = 4
GLA_QK_W = GLA_HEADS * GLA_DK
GLA_V_W = GLA_HEADS * GLA_DV
GATE_RANK = 16
SB_HEADS = 8
SB_DH = 64
SB_W = SB_HEADS * SB_DH
SB_BLK = 128
SB_QT = 1024
SB_BAND = 3
SB_DEAD = -104.0
MEM_LEN = 256
MEM_HEADS = 4
MEM_DH = 256
D_FF = 2816
D_IN = 3088
RMS_EPS = 1e-6
LANES = 128

PROJ_W = 3200
C_QG, C_KG, C_VG, C_GG, C_QS, C_KS, C_VS, C_LR = 0, 256, 512, 1024, 1536, 2048, 2560, 3072

ADAM_LR, ADAM_B1, ADAM_B2, ADAM_EPS, ADAM_WD, ADAM_STEP = 0.001, 0.9, 0.999, 1e-08, 0.01, 10

SHARD_SHAPE = {"w_in": (1024, 386), "w_out": (128, 1024), "w_mq": (128, 1024), "w_mkv": (1024, 256),
               "w_mo": (128, 1024), "w_gate_up": (1024, 704), "w_down": (352, 1024), "w_gk_up": (16, 32)}
SHARD_ROWS = {n: -(-(s[0] * s[1]) // 1024) for n, s in SHARD_SHAPE.items()}
REPL = ("mix_norm_w", "b_gk", "gla_norm_w", "sb_norm_w", "xattn_norm_w", "mem_norm_w", "mq_norm_w",
        "mk_norm_w", "ffn_norm_w")
WEIGHTS = ("mix_norm_w", "w_in", "w_gk_up", "b_gk", "gla_norm_w", "sb_norm_w", "w_out", "xattn_norm_w",
           "mem_norm_w", "w_mq", "w_mkv", "mq_norm_w", "mk_norm_w", "w_mo", "ffn_norm_w", "w_gate_up", "w_down")
VMEM_LIMIT = 56 * 1024 * 1024
MM_TILE = 1024


def _cparams(*sem):
    return pltpu.CompilerParams(dimension_semantics=sem if sem else None, vmem_limit_bytes=VMEM_LIMIT)


def _dot(a, b, ca=1, cb=0):
    return lax.dot_general(a.astype(BF16), b.astype(BF16), (((ca,), (cb,)), ((), ())),
                           preferred_element_type=F32)


def _split(x, parts):
    out = []
    for _ in range(parts - 1):
        hi = x.astype(BF16)
        out.append(hi)
        x = x - hi.astype(F32)
    out.append(x.astype(BF16))
    return out


def _dot_lhs_exact(x, m, ca=1, cb=0, parts=3):
    acc = None
    for p in _split(x, parts):
        t = _dot(p, m, ca, cb)
        acc = t if acc is None else acc + t
    return acc


def _dot_rhs_exact(m, x, ca=1, cb=0, parts=3):
    acc = None
    for p in _split(x, parts):
        t = _dot(m, p, ca, cb)
        acc = t if acc is None else acc + t
    return acc


def _dot3(a, b, ca=1, cb=0):
    a_hi, a_lo = _split(a, 2)
    b_hi, b_lo = _split(b, 2)
    return _dot(a_hi, b_hi, ca, cb) + (_dot(a_hi, b_lo, ca, cb) + _dot(a_lo, b_hi, ca, cb))


def _log_sigmoid(z):
    return jnp.minimum(z, 0.0) - jnp.log(1.0 + jnp.exp(-jnp.abs(z)))


def _sigmoid(z):
    e = jnp.exp(-jnp.abs(z))
    return jnp.where(z >= 0, 1.0, e) / (1.0 + e)


def _iota2(shape, dim):
    return lax.broadcasted_iota(jnp.int32, shape, dim)


def _rowcall(name, fn, row_ins, full_ins, row_outs, acc_outs, bm, rows):
    n_in = len(row_ins) + len(full_ins)
    n_row = len(row_outs)

    def body(*refs):
        ins, outs = refs[:n_in], refs[n_in:]
        res = fn(*[r[...] for r in ins])
        for r, v in zip(outs[:n_row], res[:n_row]):
            r[...] = v.astype(r.dtype)
        first = pl.program_id(0) == 0
        for r, v in zip(outs[n_row:], res[n_row:]):
            def init(r=r):
                r[...] = jnp.zeros(r.shape, r.dtype)
            pl.when(first)(init)
            r[...] += v

    in_specs = [pl.BlockSpec((bm, w), functools.partial(lambda i, c: (i, c), c=c)) for _, w, c in row_ins]
    in_specs += [pl.BlockSpec(a.shape, lambda i: (0, 0)) for a in full_ins]
    out_specs = [pl.BlockSpec((bm, w), lambda i: (i, 0)) for w, _ in row_outs]
    out_specs += [pl.BlockSpec(s, lambda i: (0, 0)) for s in acc_outs]
    out_shape = [jax.ShapeDtypeStruct((rows, w), dt) for w, dt in row_outs]
    out_shape += [jax.ShapeDtypeStruct(s, F32) for s in acc_outs]
    return pl.pallas_call(
        body, name=name, grid=(rows // bm,), in_specs=in_specs, out_specs=out_specs, out_shape=out_shape,
        compiler_params=_cparams("arbitrary"),
    )(*[a for a, _, _ in row_ins], *full_ins)


def _matmul(name, a, b, mode, m, n, k, out_dtype, bm, bn, bk, residual=None, a_spec=None, b_spec=None):
    bm, bn, bk = min(bm, m), min(bn, n), min(bk, k)
    nk = k // bk
    ca, cb = {"nn": (1, 0), "nt": (1, 1), "tn": (0, 0)}[mode]
    if a_spec is None:
        a_spec = (pl.BlockSpec((bk, bm), lambda i, j, kk: (kk, i)) if mode == "tn"
                  else pl.BlockSpec((bm, bk), lambda i, j, kk: (i, kk)))
    if b_spec is None:
        b_spec = (pl.BlockSpec((bn, bk), lambda i, j, kk: (j, kk)) if mode == "nt"
                  else pl.BlockSpec((bk, bn), lambda i, j, kk: (kk, j)))
    has_res = residual is not None

    def body(*refs):
        a_ref, b_ref = refs[0], refs[1]
        res_ref = refs[2] if has_res else None
        o_ref = refs[2 + has_res]
        part = _dot(a_ref[...], b_ref[...], ca, cb)

        def finish(total):
            if has_res:
                total = total + res_ref[...]
            o_ref[...] = total.astype(o_ref.dtype)

        if nk == 1:
            finish(part)
        else:
            acc_ref = refs[3 + has_res]
            kk = pl.program_id(2)

            @pl.when(kk == 0)
            def _():
                acc_ref[...] = part

            @pl.when(kk > 0)
            def _():
                acc_ref[...] += part

            @pl.when(kk == nk - 1)
            def _():
                finish(acc_ref[...])

    in_specs = [a_spec, b_spec]
    args = [a, b]
    if has_res:
        in_specs.append(pl.BlockSpec((bm, bn), lambda i, j, kk: (i, j)))
        args.append(residual)
    return pl.pallas_call(
        body, name=name, grid=(m // bm, n // bn, nk), in_specs=in_specs,
        out_specs=pl.BlockSpec((bm, bn), lambda i, j, kk: (i, j)),
        out_shape=jax.ShapeDtypeStruct((m, n), out_dtype),
        scratch_shapes=[pltpu.VMEM((bm, bn), F32)] if nk > 1 else [],
        compiler_params=_cparams("parallel", "parallel", "arbitrary"),
    )(*args)


def _peer(mask):
    x, y, c = lax.axis_index("x"), lax.axis_index("y"), lax.axis_index("c")
    mx, my, mc = (mask >> 2) & 1, (mask >> 1) & 1, mask & 1
    px, py, pc = (1 - x if mx else x), (1 - y if my else y), (1 - c if mc else c)
    return (px, py, pc), 4 * px + 2 * py + pc


def _my_index():
    return 4 * lax.axis_index("x") + 2 * lax.axis_index("y") + lax.axis_index("c")


class _Exchange:
    def __init__(self, src, scatter):
        self.src, self.scatter = src, scatter
        self.in_spec = pl.BlockSpec(memory_space=pl.ANY)
        self.out_spec = pl.BlockSpec(memory_space=pl.ANY)
        self.out_shape = jax.ShapeDtypeStruct((N_DEV, src.shape[-2], 1024), src.dtype)
        self.scratch = [pltpu.SemaphoreType.DMA((N_DEV - 1,)), pltpu.SemaphoreType.DMA((N_DEV - 1,)),
                        pltpu.SemaphoreType.DMA(())]

    def _copies(self, src_ref, out_ref, sems):
        send_sems, recv_sems, local_sem = sems
        me = _my_index()
        copies = [pltpu.make_async_copy(src_ref.at[me] if self.scatter else src_ref, out_ref.at[me], local_sem)]
        for mask in range(1, N_DEV):
            peer, peer_index = _peer(mask)
            copies.append(pltpu.make_async_remote_copy(
                src_ref=src_ref.at[peer_index] if self.scatter else src_ref, dst_ref=out_ref.at[me],
                send_sem=send_sems.at[mask - 1], recv_sem=recv_sems.at[mask - 1],
                device_id=peer, device_id_type=pl.DeviceIdType.MESH))
        return copies

    def start(self, src_ref, out_ref, sems):
        for cp in self._copies(src_ref, out_ref, sems):
            cp.start()

    def wait(self, src_ref, out_ref, sems):
        for cp in self._copies(src_ref, out_ref, sems):
            cp.wait()


def _hosted_call(body, ex, grid, in_specs, out_specs, out_shape, scratch_shapes, args, **kw):
    if ex is None:
        return pl.pallas_call(body, grid=grid, in_specs=in_specs, out_specs=out_specs, out_shape=out_shape,
                              scratch_shapes=scratch_shapes, **kw)(*args)
    n_in, n_out, n_scr = len(in_specs), len(out_specs), len(scratch_shapes)

    def hosted(*refs):
        ins, src_ref = refs[:n_in], refs[n_in]
        outs, out_ref = refs[n_in + 1:n_in + 1 + n_out], refs[n_in + 1 + n_out]
        scr, sems = refs[n_in + 2 + n_out:n_in + 2 + n_out + n_scr], refs[n_in + 2 + n_out + n_scr:]
        ids = [pl.program_id(a) for a in range(len(grid))]
        first = functools.reduce(jnp.logical_and, [p == 0 for p in ids])
        last = functools.reduce(jnp.logical_and, [p == n - 1 for p, n in zip(ids, grid)])

        @pl.when(first)
        def _():
            ex.start(src_ref, out_ref, sems)

        body(*ins, *outs, *scr)

        @pl.when(last)
        def _():
            ex.wait(src_ref, out_ref, sems)

    res = pl.pallas_call(
        hosted, grid=grid, in_specs=list(in_specs) + [ex.in_spec], out_specs=list(out_specs) + [ex.out_spec],
        out_shape=list(out_shape) + [ex.out_shape], scratch_shapes=list(scratch_shapes) + ex.scratch, **kw,
    )(*args, ex.src)
    return res


def _exchange_call(name, src, scatter):
    ex = _Exchange(src, scatter)

    def body(src_ref, out_ref, *sems):
        ex.start(src_ref, out_ref, sems)
        ex.wait(src_ref, out_ref, sems)

    return pl.pallas_call(
        body, name=name, in_specs=[ex.in_spec], out_specs=ex.out_spec, out_shape=ex.out_shape,
        scratch_shapes=ex.scratch, compiler_params=pltpu.CompilerParams(has_side_effects=True),
    )(src)


def _adamw(name, recv, w, m, v, block):
    rows = w.shape[0]
    c1 = 1.0 - ADAM_B1 ** ADAM_STEP
    c2 = 1.0 - ADAM_B2 ** ADAM_STEP

    def body(r_ref, w_ref, m_ref, v_ref, g_out, d_out, m_out, v_out):
        g = r_ref[0]
        for s in range(1, N_DEV):
            g = g + r_ref[s]
        m_new = ADAM_B1 * m_ref[...] + (1.0 - ADAM_B1) * g
        v_new = ADAM_B2 * v_ref[...] + (1.0 - ADAM_B2) * (g * g)
        m_hat = m_new / c1
        v_hat = v_new / c2
        g_out[...] = g
        d_out[...] = -ADAM_LR * (m_hat / (jnp.sqrt(v_hat) + ADAM_EPS) + ADAM_WD * w_ref[...])
        m_out[...] = m_new
        v_out[...] = v_new

    blk = pl.BlockSpec((block, 1024), lambda i: (i, 0))
    return pl.pallas_call(
        body, name=name, grid=(rows // block,),
        in_specs=[pl.BlockSpec((N_DEV, block, 1024), lambda i: (0, i, 0)), blk, blk, blk],
        out_specs=[blk] * 4, out_shape=[jax.ShapeDtypeStruct((rows, 1024), F32)] * 4,
        compiler_params=_cparams("parallel"),
    )(recv, w, m, v)


def _rms_fwd(name, x, w, bm):
    def fn(xb, wb):
        r = lax.rsqrt(jnp.mean(xb * xb, axis=-1, keepdims=True) + RMS_EPS)
        return (xb * r * wb,)
    return _rowcall(name, fn, [(x, D_MODEL, 0)], [w], [(D_MODEL, BF16)], [], bm, x.shape[0])[0]


def _rms_bwd(name, x, w, dh, dres, bm):
    def fn(xb, dhb, drb, wb):
        r = lax.rsqrt(jnp.mean(xb * xb, axis=-1, keepdims=True) + RMS_EPS)
        xh = xb * r
        dxh = dhb.astype(F32) * wb
        dx = drb + r * (dxh - xh * jnp.mean(dxh * xh, axis=-1, keepdims=True))
        return dx, jnp.sum(dhb.astype(F32) * xh, axis=0, keepdims=True)
    return _rowcall(name, fn, [(x, D_MODEL, 0), (dh, D_MODEL, 0), (dres, D_MODEL, 0)], [w],
                    [(D_MODEL, F32)], [(1, D_MODEL)], bm, x.shape[0])


def _gate_fwd(proj, wgk_pad, b_gk, bm):
    def fn(lr, wg, bg):
        z = _dot(lr, wg) + bg
        return (_log_sigmoid(z) * (1.0 / 16.0),)
    return _rowcall("gla_gate_fwd", fn, [(proj, LANES, C_LR // LANES)], [wgk_pad, b_gk],
                    [(GLA_QK_W, F32)], [], bm, proj.shape[0])[0]


def _gate_bwd(proj, wgk_pad, b_gk, dgk, bm):
    def fn(lr, dg, wg, bg):
        z = _dot(lr, wg) + bg
        dz = dg * _sigmoid(-z) * (1.0 / 16.0)
        return _dot(dz, wg, 1, 1), _dot(lr, dz, 0, 0), jnp.sum(dz, axis=0, keepdims=True)
    return _rowcall("gla_gate_bwd", fn, [(proj, LANES, C_LR // LANES), (dgk, GLA_QK_W, 0)], [wgk_pad, b_gk],
                    [(LANES, BF16)], [(LANES, GLA_QK_W), (1, GLA_QK_W)], bm, proj.shape[0])


def _group_mean(x, g, size):
    return _dot_lhs_exact(x, g, parts=2) * (1.0 / size)


def _mix_out_fwd(o_g, proj, o_s, wg_t, ws_t, grp_g, grp_s, bm):
    def fn(og, gg, os_, wg, ws, gmat, smat):
        rg = lax.rsqrt(_group_mean(og * og, gmat, GLA_DV) + RMS_EPS)
        yg = og * rg * wg * (gg * _sigmoid(gg))
        rs = lax.rsqrt(_group_mean(os_ * os_, smat, SB_DH) + RMS_EPS)
        ys = os_ * rs * ws
        return (jnp.concatenate([yg, ys], axis=1),)
    return _rowcall("mix_out_fwd", fn, [(o_g, GLA_V_W, 0), (proj, GLA_V_W, C_GG // GLA_V_W), (o_s, SB_W, 0)],
                    [wg_t, ws_t, grp_g, grp_s], [(D_MODEL, BF16)], [], bm, o_g.shape[0])[0]


def _mix_out_bwd(dcat, o_g, proj, o_s, wg_t, ws_t, grp_g, grp_s, bm):
    def fn(dyg, dys, og, gg, os_, wg, ws, gmat, smat):
        dyg = dyg.astype(F32)
        dys = dys.astype(F32)
        rg = lax.rsqrt(_group_mean(og * og, gmat, GLA_DV) + RMS_EPS)
        xh = og * rg
        sg = _sigmoid(gg)
        silu = gg * sg
        dxh = dyg * wg * silu
        dgg = dyg * xh * wg * (sg * (1.0 + gg * (1.0 - sg)))
        dwg = jnp.sum(dyg * xh * silu, axis=0, keepdims=True)
        dog = rg * (dxh - xh * _group_mean(dxh * xh, gmat, GLA_DV))
        rs = lax.rsqrt(_group_mean(os_ * os_, smat, SB_DH) + RMS_EPS)
        xs = os_ * rs
        dxs = dys * ws
        dws = jnp.sum(dys * xs, axis=0, keepdims=True)
        dos = rs * (dxs - xs * _group_mean(dxs * xs, smat, SB_DH))
        return dog, dgg, dos, dwg, dws
    return _rowcall("mix_out_bwd", fn,
                    [(dcat, GLA_V_W, 0), (dcat, SB_W, 1), (o_g, GLA_V_W, 0), (proj, GLA_V_W, C_GG // GLA_V_W),
                     (o_s, SB_W, 0)],
                    [wg_t, ws_t, grp_g, grp_s], [(GLA_V_W, F32), (GLA_V_W, BF16), (SB_W, F32)],
                    [(1, GLA_V_W), (1, SB_W)], bm, o_g.shape[0])


def _loss_kernel(y, tgt, bm):
    def fn(yb, tb):
        err = yb - tb
        return err * (1.0 / D_MODEL), jnp.sum(err * err, axis=0, keepdims=True)
    return _rowcall("loss_head", fn, [(y, D_MODEL, 0), (tgt, D_MODEL, 0)], [], [(D_MODEL, F32)], [(1, D_MODEL)],
                    bm, y.shape[0])


def _sb_tri(inclusive):
    j, s = _iota2((2 * SB_BLK, 2 * SB_BLK), 0), _iota2((2 * SB_BLK, 2 * SB_BLK), 1)
    j = jnp.where(j >= SB_BLK, j - SB_BLK, j)
    keep = (j >= s) if inclusive else (j > s)
    return ((s >= SB_BLK) | keep).astype(BF16)


def _dot_hilo(x, m2):
    hi, lo = _split(x, 2)
    return _dot(jnp.concatenate([hi, lo], axis=1), m2)


def _sb_mask(n):
    return _iota2((n, SB_BLK), 1) < _iota2((n, SB_BLK), 0)


def _add_rows(full, part, row0, row1):
    pieces = [full[:row0]] if row0 else []
    pieces.append(full[row0:row1] + part)
    if row1 < full.shape[0]:
        pieces.append(full[row1:])
    return pieces[0] if len(pieces) == 1 else jnp.concatenate(pieces, axis=0)


def _sb_dead_from(carry, c_slots, row):
    return jnp.max(functools.reduce(jnp.maximum, [carry[s][row:] for s in c_slots])) <= SB_DEAD


def _sb_visit(tiles, carry, c_slots, qt, diag_start, n_left, left_start):
    subs = qt // SB_BLK
    for sub in reversed(range(subs)):
        start, row0 = diag_start(sub), sub * SB_BLK
        row1 = min(qt, row0 + SB_BAND * SB_BLK)
        if row1 == qt:
            carry = tiles(start, carry, True, row0, qt)
        else:
            carry = lax.cond(_sb_dead_from(carry, c_slots, row1),
                             functools.partial(tiles, start, diag=True, row0=row0, row1=row1),
                             functools.partial(tiles, start, diag=True, row0=row0, row1=qt), carry)

    def sweep(row1):
        def step(jj, cr):
            return tiles(left_start(jj), cr, False, 0, row1)
        return functools.partial(_sb_sweep, n_left, step, c_slots=c_slots)

    band = min(qt, SB_BAND * SB_BLK)
    if band == qt:
        return sweep(qt)(carry)
    return lax.cond(_sb_dead_from(carry, c_slots, band), sweep(band), sweep(qt), carry)


def _sb_sweep(n_tiles, step, carry, c_slots):
    def alive(state):
        jj, carry = state
        c_max = jnp.max(functools.reduce(jnp.maximum, [carry[s] for s in c_slots]))
        return jnp.logical_and(jj < n_tiles, c_max > SB_DEAD)

    def body(state):
        jj, carry = state
        return jj + 1, step(jj, carry)

    return lax.while_loop(alive, body, (jnp.int32(0), carry))[1]


def _row_blocks(n):
    return [slice(r, r + SB_BLK) for r in range(0, n, SB_BLK)]


def _hilo(x):
    hi, lo = _split(x, 2)
    return jnp.concatenate([hi, lo], axis=1)


def _sb_tile(q, k, c, tri_excl, diag):
    blocks = _row_blocks(q.shape[0])
    strict = _sb_mask(SB_BLK) if diag else None
    z = _dot(q, k, 1, 1)
    lbs, pieces = [], []
    for r, rs in enumerate(blocks):
        lb = _log_sigmoid(z[rs])
        l1 = lb - z[rs]
        if diag and r == 0:
            l1 = jnp.where(strict, l1, 0.0)
        lbs.append(lb)
        pieces.append(_hilo(l1))
    sums = _dot(jnp.concatenate(pieces, axis=0), tri_excl)
    a = []
    for r, rs in enumerate(blocks):
        ar = jnp.exp(lbs[r] + sums[rs, :SB_BLK] + c[rs])
        if diag and r == 0:
            ar = jnp.where(strict, ar, 0.0)
        a.append(ar.astype(BF16))
    return lbs, a, sums[:, SB_BLK:]


def _sb_fwd(proj, rows, exchange=None):
    qt = min(SB_QT, rows)
    subs = qt // SB_BLK

    def body(q_ref, k_ref, v_ref, o_ref):
        i = pl.program_id(1)
        tri_excl = _sb_tri(False)
        heads = [slice(SB_DH * hh, SB_DH * (hh + 1)) for hh in range(2)]
        qs = [(q_ref[:, sl] * 0.125).astype(BF16) for sl in heads]

        def tiles(start, carry, diag, row0, row1):
            out = []
            for hh, sl in enumerate(heads):
                o, c = carry[2 * hh], carry[2 * hh + 1]
                k = k_ref[pl.ds(start, SB_BLK), sl].astype(BF16)
                v = v_ref[pl.ds(start, SB_BLK), sl].astype(BF16)
                _, a, dc = _sb_tile(qs[hh][row0:row1], k, c[row0:row1], tri_excl, diag)
                out += [_add_rows(o, _dot(jnp.concatenate(a, axis=0), v), row0, row1), _add_rows(c, dc, row0, row1)]
            return tuple(out)

        carry = (jnp.zeros((qt, SB_DH), F32), jnp.zeros((qt, SB_BLK), F32)) * 2
        carry = _sb_visit(tiles, carry, (1, 3), qt,
                          lambda sub: pl.multiple_of(i * qt + sub * SB_BLK, SB_BLK), i * subs,
                          lambda jj: pl.multiple_of((i * subs - 1 - jj) * SB_BLK, SB_BLK))
        for hh, sl in enumerate(heads):
            o_ref[:, sl] = carry[2 * hh]

    return _hosted_call(
        body, exchange, grid=(SB_HEADS // 2, rows // qt),
        in_specs=[pl.BlockSpec((qt, LANES), lambda h, i: (i, C_QS // LANES + h)),
                  pl.BlockSpec((rows, LANES), lambda h, i: (0, C_KS // LANES + h)),
                  pl.BlockSpec((rows, LANES), lambda h, i: (0, C_VS // LANES + h))],
        out_specs=[pl.BlockSpec((qt, LANES), lambda h, i: (i, h))],
        out_shape=[jax.ShapeDtypeStruct((rows, SB_W), F32)], scratch_shapes=[],
        args=(proj, proj, proj), name="sb_attention_fwd", compiler_params=_cparams("arbitrary", "arbitrary"))


def _sb_bwd(proj, o_s, do_s, rows, exchange=None):
    qt = min(SB_QT, rows)
    subs = qt // SB_BLK

    def body(q_ref, k_ref, v_ref, o_ref, do_ref, dq_ref, dk_ref, dv_ref):
        i = pl.program_id(1)

        @pl.when(i == 0)
        def _():
            dk_ref[...] = jnp.zeros(dk_ref.shape, F32)
            dv_ref[...] = jnp.zeros(dv_ref.shape, F32)

        tri_excl, tri_incl = _sb_tri(False), _sb_tri(True)
        heads = [slice(SB_DH * hh, SB_DH * (hh + 1)) for hh in range(2)]
        qs = [(q_ref[:, sl] * 0.125).astype(BF16) for sl in heads]
        dobs = [do_ref[:, sl].astype(BF16) for sl in heads]
        dsums = [jnp.broadcast_to(jnp.sum(dob.astype(F32) * o_ref[:, sl], axis=1, keepdims=True), (qt, SB_BLK))
                 for dob, sl in zip(dobs, heads)]

        def tiles(start, carry, diag, row0, row1):
            out = []
            strict = _sb_mask(SB_BLK) if diag else None
            for hh, sl in enumerate(heads):
                dq, c, cp = carry[3 * hh:3 * hh + 3]
                q, dob = qs[hh][row0:row1], dobs[hh][row0:row1]
                dsum, cpr = dsums[hh][row0:row1], cp[row0:row1]
                blocks = _row_blocks(q.shape[0])
                k = k_ref[pl.ds(start, SB_BLK), sl].astype(BF16)
                v = v_ref[pl.ds(start, SB_BLK), sl].astype(BF16)
                lbs, a, dc = _sb_tile(q, k, c[row0:row1], tri_excl, diag)
                da = _dot(dob, v, 1, 1)
                ps = [a[r].astype(F32) * da[rs] for r, rs in enumerate(blocks)]
                psums = _dot(jnp.concatenate([_hilo(p) for p in ps], axis=0), tri_incl)
                dzs = []
                for r, rs in enumerate(blocks):
                    left = dsum[rs] - (psums[rs, :SB_BLK] + cpr[rs])
                    dz = ps[r] - jnp.exp(lbs[r]) * (ps[r] + left)
                    if diag and r == 0:
                        dz = jnp.where(strict, dz, 0.0)
                    dzs.append(dz.astype(BF16))
                dzb, ab = jnp.concatenate(dzs, axis=0), jnp.concatenate(a, axis=0)
                dk_ref[pl.ds(start, SB_BLK), sl] += _dot(dzb, q, 0, 0)
                dv_ref[pl.ds(start, SB_BLK), sl] += _dot(ab, dob, 0, 0)
                out += [_add_rows(dq, _dot(dzb, k), row0, row1), _add_rows(c, dc, row0, row1),
                        _add_rows(cp, psums[:, SB_BLK:], row0, row1)]
            return tuple(out)

        zero = jnp.zeros((qt, SB_BLK), F32)
        carry = (jnp.zeros((qt, SB_DH), F32), zero, zero) * 2
        carry = _sb_visit(tiles, carry, (1, 4), qt,
                          lambda sub: pl.multiple_of(i * qt + sub * SB_BLK, SB_BLK), i * subs,
                          lambda jj: pl.multiple_of((i * subs - 1 - jj) * SB_BLK, SB_BLK))
        for hh, sl in enumerate(heads):
            dq_ref[:, sl] = carry[3 * hh] * 0.125

    whole = lambda base: pl.BlockSpec((rows, LANES), functools.partial(lambda h, i, b: (0, b + h), b=base))
    blk = lambda base: pl.BlockSpec((qt, LANES), functools.partial(lambda h, i, b: (i, b + h), b=base))
    return _hosted_call(
        body, exchange, grid=(SB_HEADS // 2, rows // qt),
        in_specs=[blk(C_QS // LANES), whole(C_KS // LANES), whole(C_VS // LANES), blk(0), blk(0)],
        out_specs=[blk(0), whole(0), whole(0)],
        out_shape=[jax.ShapeDtypeStruct((rows, SB_W), F32)] * 3, scratch_shapes=[],
        args=(proj, proj, proj, o_s, do_s), name="sb_attention_bwd",
        compiler_params=_cparams("arbitrary", "arbitrary"))


def _gla_chunk_common(g_all):
    r_i, c_i = _iota2((GLA_CHUNK, GLA_CHUNK), 0), _iota2((GLA_CHUNK, GLA_CHUNK), 1)
    tri = (c_i <= r_i).astype(BF16)
    return _dot_rhs_exact(tri, g_all), r_i, c_i


def _gla_scaled(qh, kh, bh):
    ref = bh[GLA_CHUNK // 2:GLA_CHUNK // 2 + 1, :]
    eq, ek = jnp.exp(bh - ref), jnp.exp(ref - bh)
    mask = _iota2((GLA_CHUNK, GLA_CHUNK), 1) <= _iota2((GLA_CHUNK, GLA_CHUNK), 0)
    return eq, ek, qh * eq, kh * ek, mask


def _gla_fwd(proj, gk, rows, exchange=None):
    n_chunks = rows // GLA_CHUNK
    step_rows = GLA_CHUNK * GLA_STEP_CHUNKS

    def body(q_ref, k_ref, v_ref, g_ref, o_ref, sall_ref, s_scr):
        @pl.when(pl.program_id(0) == 0)
        def _():
            s_scr[...] = jnp.zeros(s_scr.shape, F32)

        ones = jnp.ones((GLA_CHUNK, GLA_DV), BF16)
        states = [s_scr[h] for h in range(GLA_HEADS)]
        for ci in range(GLA_STEP_CHUNKS):
            rs = slice(GLA_CHUNK * ci, GLA_CHUNK * (ci + 1))
            g_all = g_ref[rs, :]
            b_all, _, _ = _gla_chunk_common(g_all)
            for h in range(GLA_HEADS):
                sl = slice(GLA_DK * h, GLA_DK * (h + 1))
                vs = slice(GLA_DV * h, GLA_DV * (h + 1))
                qh, kh, vh = q_ref[rs, sl] * 0.125, k_ref[rs, sl], v_ref[rs, vs]
                bh, gh = b_all[:, sl], g_all[:, sl]
                s = states[h]
                sall_ref[ci, h] = s
                _, _, qs, ks, mask = _gla_scaled(qh, kh, bh)
                a = jnp.where(mask, _dot(qs, ks, 1, 1), 0.0)
                o_ref[rs, vs] = _dot(qh * jnp.exp(bh), s) + _dot(a, vh)
                bl_col = _dot_lhs_exact(gh, ones, 0, 0)
                kd = kh * jnp.exp(bh[GLA_CHUNK - 1:GLA_CHUNK, :] - bh)
                states[h] = jnp.exp(bl_col) * s + _dot(kd, vh, 0, 0)
        for h in range(GLA_HEADS):
            s_scr[h] = states[h]

    c64 = lambda w, base: pl.BlockSpec((step_rows, w), functools.partial(lambda n, b: (n, b), b=base))
    return _hosted_call(
        body, exchange, grid=(rows // step_rows,),
        in_specs=[c64(GLA_QK_W, C_QG // GLA_QK_W), c64(GLA_QK_W, C_KG // GLA_QK_W), c64(GLA_V_W, C_VG // GLA_V_W),
                  c64(GLA_QK_W, 0)],
        out_specs=[c64(GLA_V_W, 0),
                   pl.BlockSpec((GLA_STEP_CHUNKS, GLA_HEADS, GLA_DK, GLA_DV), lambda n: (n, 0, 0, 0))],
        out_shape=[jax.ShapeDtypeStruct((rows, GLA_V_W), F32),
                   jax.ShapeDtypeStruct((n_chunks, GLA_HEADS, GLA_DK, GLA_DV), F32)],
        scratch_shapes=[pltpu.VMEM((GLA_HEADS, GLA_DK, GLA_DV), F32)],
        args=(proj, proj, proj, gk), name="gla_fwd", compiler_params=_cparams("arbitrary"))


def _gla_bwd(proj, gk, do_g, s_all, rows, exchange=None):
    n_chunks = rows // GLA_CHUNK

    def body(q_ref, k_ref, v_ref, g_ref, do_ref, sall_ref, dq_ref, dk_ref, dv_ref, dg_ref, ds_scr):
        @pl.when(pl.program_id(0) == 0)
        def _():
            ds_scr[...] = jnp.zeros(ds_scr.shape, F32)

        ones = jnp.ones((GLA_CHUNK, GLA_DV), BF16)
        ones8 = jnp.ones((8, GLA_DV), F32)
        last_row = _iota2((GLA_CHUNK, GLA_DK), 0) == GLA_CHUNK - 1
        dstates = [ds_scr[h] for h in range(GLA_HEADS)]
        for ci in reversed(range(GLA_STEP_CHUNKS)):
            cs = slice(GLA_CHUNK * ci, GLA_CHUNK * (ci + 1))
            g_all = g_ref[cs, :]
            b_all, r_i, c_i = _gla_chunk_common(g_all)
            triu = (c_i >= r_i).astype(BF16)
            for h in range(GLA_HEADS):
                sl = slice(GLA_DK * h, GLA_DK * (h + 1))
                vs = slice(GLA_DV * h, GLA_DV * (h + 1))
                qh, kh, vh = q_ref[cs, sl] * 0.125, k_ref[cs, sl], v_ref[cs, vs]
                bh, gh = b_all[:, sl], g_all[:, sl]
                doh = do_ref[cs, vs]
                s, ds = sall_ref[ci, h], dstates[h]
                eb = jnp.exp(bh)
                ekd = jnp.exp(bh[GLA_CHUNK - 1:GLA_CHUNK, :] - bh)
                ebl = jnp.exp(_dot_lhs_exact(gh, ones, 0, 0))
                qb, kd = qh * eb, kh * ekd
                dq = _dot(doh, s, 1, 1) * eb
                dk = _dot(vh, ds, 1, 1) * ekd
                dv = _dot(kd, ds)
                dbl = jnp.sum(dk * kh, axis=0, keepdims=True) + _dot3(ones8, ebl * s * ds, 1, 1)[0:1, :]
                eq, ek, qs, ks, mask = _gla_scaled(qh, kh, bh)
                a = jnp.where(mask, _dot(qs, ks, 1, 1), 0.0)
                da = jnp.where(mask, _dot(doh, vh, 1, 1), 0.0)
                dq = dq + _dot(da, ks) * eq
                dk = dk + _dot(da, qs, 0, 0) * ek
                dv = dv + _dot(a, doh, 0, 0)
                db = qh * dq - kh * dk + jnp.where(last_row, dbl, 0.0)
                dq_ref[cs, sl] = dq * 0.125
                dk_ref[cs, sl] = dk
                dv_ref[cs, vs] = dv
                dg_ref[cs, sl] = _dot_rhs_exact(triu, db)
                dstates[h] = _dot(qb, doh, 0, 0) + ebl * ds
        for h in range(GLA_HEADS):
            ds_scr[h] = dstates[h]

    step_rows = GLA_CHUNK * GLA_STEP_CHUNKS
    last = rows // step_rows - 1
    c64 = lambda w, base: pl.BlockSpec((step_rows, w), functools.partial(lambda n, b: (last - n, b), b=base))
    return _hosted_call(
        body, exchange, grid=(rows // step_rows,),
        in_specs=[c64(GLA_QK_W, C_QG // GLA_QK_W), c64(GLA_QK_W, C_KG // GLA_QK_W), c64(GLA_V_W, C_VG // GLA_V_W),
                  c64(GLA_QK_W, 0), c64(GLA_V_W, 0),
                  pl.BlockSpec((GLA_STEP_CHUNKS, GLA_HEADS, GLA_DK, GLA_DV), lambda n: (last - n, 0, 0, 0))],
        out_specs=[c64(GLA_QK_W, 0), c64(GLA_QK_W, 0), c64(GLA_V_W, 0), c64(GLA_QK_W, 0)],
        out_shape=[jax.ShapeDtypeStruct((rows, GLA_QK_W), F32), jax.ShapeDtypeStruct((rows, GLA_QK_W), F32),
                   jax.ShapeDtypeStruct((rows, GLA_V_W), F32), jax.ShapeDtypeStruct((rows, GLA_QK_W), F32)],
        scratch_shapes=[pltpu.VMEM((GLA_HEADS, GLA_DK, GLA_DV), F32)],
        args=(proj, proj, proj, gk, do_g, s_all), name="gla_bwd", compiler_params=_cparams("arbitrary"))


def _mem_kv_fwd(mem, mem_norm_w, w_mkv, mk_norm_w):
    def body(mem_ref, mw_ref, w_ref, kw_ref, memn_ref, kpre_ref, kn_ref, v_ref):
        xb = mem_ref[...]
        r = lax.rsqrt(jnp.mean(xb * xb, axis=-1, keepdims=True) + RMS_EPS)
        mem_n = (xb * r * mw_ref[...]).astype(BF16)
        memn_ref[...] = mem_n
        kv = _dot(mem_n, w_ref[...])
        kpre_ref[...] = kv[:, :D_MODEL]
        v_ref[...] = kv[:, D_MODEL:].astype(BF16)
        for h in range(MEM_HEADS):
            sl = slice(MEM_DH * h, MEM_DH * (h + 1))
            kh = kv[:, sl]
            rk = lax.rsqrt(jnp.mean(kh * kh, axis=-1, keepdims=True) + RMS_EPS)
            kn_ref[:, sl] = (kh * rk * kw_ref[...]).astype(BF16)

    return pl.pallas_call(
        body, name="mem_kv_fwd",
        out_shape=[jax.ShapeDtypeStruct((MEM_LEN, D_MODEL), BF16), jax.ShapeDtypeStruct((MEM_LEN, D_MODEL), F32),
                   jax.ShapeDtypeStruct((MEM_LEN, D_MODEL), BF16), jax.ShapeDtypeStruct((MEM_LEN, D_MODEL), BF16)],
        compiler_params=_cparams(),
    )(mem, mem_norm_w, w_mkv, mk_norm_w)


def _mem_kv_bwd(mem, mem_norm_w, w_mkv, mk_norm_w, mem_n, kpre, dkn, dv):
    def body(mem_ref, mw_ref, w_ref, kw_ref, memn_ref, kpre_ref, dkn_ref, dv_ref, dw_ref, dkw_ref, dmw_ref):
        dkw = jnp.zeros((1, MEM_DH), F32)
        dk_parts = []
        for h in range(MEM_HEADS):
            sl = slice(MEM_DH * h, MEM_DH * (h + 1))
            kh, dkh = kpre_ref[:, sl], dkn_ref[:, sl]
            rk = lax.rsqrt(jnp.mean(kh * kh, axis=-1, keepdims=True) + RMS_EPS)
            xh = kh * rk
            dxh = dkh * kw_ref[...]
            dkw = dkw + jnp.sum(dkh * xh, axis=0, keepdims=True)
            dk_parts.append(rk * (dxh - xh * jnp.mean(dxh * xh, axis=-1, keepdims=True)))
        dkw_ref[...] = dkw
        dkv = jnp.concatenate(dk_parts + [dv_ref[...]], axis=1).astype(BF16)
        dw_ref[...] = _dot(memn_ref[...], dkv, 0, 0)
        dmem_n = _dot(dkv, w_ref[...], 1, 1)
        xb = mem_ref[...]
        r = lax.rsqrt(jnp.mean(xb * xb, axis=-1, keepdims=True) + RMS_EPS)
        dmw_ref[...] = jnp.sum(dmem_n * (xb * r), axis=0, keepdims=True)

    return pl.pallas_call(
        body, name="mem_kv_bwd",
        out_shape=[jax.ShapeDtypeStruct((D_MODEL, 2 * D_MODEL), F32), jax.ShapeDtypeStruct((1, MEM_DH), F32),
                   jax.ShapeDtypeStruct((1, D_MODEL), F32)],
        compiler_params=_cparams(),
    )(mem, mem_norm_w, w_mkv, mk_norm_w, mem_n, kpre, dkn, dv)


def _xattn_head(qh, kn_h, qw):
    rq = lax.rsqrt(jnp.mean(qh * qh, axis=-1, keepdims=True) + RMS_EPS)
    xh = qh * rq
    qn = (xh * qw).astype(BF16)
    s = _dot(qn, kn_h, 1, 1) * (1.0 / 16.0)
    e = jnp.exp(s - jnp.max(s, axis=-1, keepdims=True))
    p = e / jnp.sum(e, axis=-1, keepdims=True)
    return rq, xh, qn, p


def _xattn_fwd(qm, kn, v, mq_norm_w, bm):
    def fn(qb, knb, vb, qw):
        outs = []
        for h in range(MEM_HEADS):
            sl = slice(MEM_DH * h, MEM_DH * (h + 1))
            _, _, _, p = _xattn_head(qb[:, sl], knb[:, sl], qw)
            outs.append(_dot(p, vb[:, sl]))
        return (jnp.concatenate(outs, axis=1),)
    return _rowcall("xattn_fwd", fn, [(qm, D_MODEL, 0)], [kn, v, mq_norm_w], [(D_MODEL, BF16)], [], bm,
                    qm.shape[0])[0]


def _xattn_bwd(qm, kn, v, mq_norm_w, do, bm):
    def fn(qb, dob, knb, vb, qw):
        dq_parts, dkn_parts, dv_parts = [], [], []
        dqw = jnp.zeros((1, MEM_DH), F32)
        for h in range(MEM_HEADS):
            sl = slice(MEM_DH * h, MEM_DH * (h + 1))
            rq, xh, qn, p = _xattn_head(qb[:, sl], knb[:, sl], qw)
            doh = dob[:, sl].astype(BF16)
            dp = _dot(doh, vb[:, sl], 1, 1)
            ds = (p * (dp - jnp.sum(dp * p, axis=-1, keepdims=True)) * (1.0 / 16.0)).astype(BF16)
            dqn = _dot(ds, knb[:, sl])
            dkn_parts.append(_dot(ds, qn, 0, 0))
            dv_parts.append(_dot(p, doh, 0, 0))
            dqw = dqw + jnp.sum(dqn * xh, axis=0, keepdims=True)
            dxh = dqn * qw
            dq_parts.append(rq * (dxh - xh * jnp.mean(dxh * xh, axis=-1, keepdims=True)))
        return (jnp.concatenate(dq_parts, axis=1), jnp.concatenate(dkn_parts, axis=1),
                jnp.concatenate(dv_parts, axis=1), dqw)
    return _rowcall("xattn_bwd", fn, [(qm, D_MODEL, 0), (do, D_MODEL, 0)], [kn, v, mq_norm_w],
                    [(D_MODEL, BF16)], [(MEM_LEN, D_MODEL), (MEM_LEN, D_MODEL), (1, MEM_DH)], bm, qm.shape[0])


FF_BN = 1408
FF_NB = D_FF // FF_BN


def _ffn_up(h3, w_gate_up, rows, bm):
    def body(h_ref, wg_ref, wu_ref, gate_ref, up_ref, act_ref):
        hb = h_ref[...]
        gate = _dot(hb, wg_ref[...])
        up = _dot(hb, wu_ref[...])
        gate_ref[...] = gate.astype(BF16)
        up_ref[...] = up.astype(BF16)
        act_ref[...] = (gate * _sigmoid(gate) * up).astype(BF16)

    out_blk = pl.BlockSpec((bm, FF_BN), lambda i, j: (i, j))
    return pl.pallas_call(
        body, name="ffn_up", grid=(rows // bm, FF_NB),
        in_specs=[pl.BlockSpec((bm, D_MODEL), lambda i, j: (i, 0)),
                  pl.BlockSpec((D_MODEL, FF_BN), lambda i, j: (0, j)),
                  pl.BlockSpec((D_MODEL, FF_BN), lambda i, j: (0, FF_NB + j))],
        out_specs=[out_blk, out_blk, out_blk],
        out_shape=[jax.ShapeDtypeStruct((rows, D_FF), BF16)] * 3,
        compiler_params=_cparams("parallel", "arbitrary"),
    )(h3, w_gate_up, w_gate_up)


def _ffn_act_bwd(dy, w_down, gate, up, rows, bm):
    def body(dy_ref, wd_ref, gate_ref, up_ref, o_ref):
        dact = _dot(dy_ref[...], wd_ref[...], 1, 1)
        g, u = gate_ref[...].astype(F32), up_ref[...].astype(F32)
        sg = _sigmoid(g)
        o_ref[0] = (dact * u * (sg * (1.0 + g * (1.0 - sg)))).astype(BF16)
        o_ref[1] = (dact * (g * sg)).astype(BF16)

    blk = pl.BlockSpec((bm, FF_BN), lambda i, j: (i, j))
    return pl.pallas_call(
        body, name="ffn_act_bwd", grid=(rows // bm, FF_NB),
        in_specs=[pl.BlockSpec((bm, D_MODEL), lambda i, j: (i, 0)),
                  pl.BlockSpec((FF_BN, D_MODEL), lambda i, j: (j, 0)), blk, blk],
        out_specs=pl.BlockSpec((2, bm, FF_BN), lambda i, j: (0, i, j)),
        out_shape=jax.ShapeDtypeStruct((2, rows, D_FF), BF16),
        compiler_params=_cparams("parallel", "arbitrary"),
    )(dy, w_down, gate, up)


def _pack_rows(a):
    flat = a.reshape(-1)
    pad = (-flat.shape[0]) % 1024
    if pad:
        flat = jnp.concatenate([flat, jnp.zeros((pad,), flat.dtype)])
    return flat.reshape(-1, 1024)


def _pack_owner_rows(a8):
    flat = a8.reshape(N_DEV, -1)
    pad = (-flat.shape[1]) % 1024
    if pad:
        flat = jnp.concatenate([flat, jnp.zeros((N_DEV, pad), flat.dtype)], axis=1)
    return flat.reshape(N_DEV, -1, 1024)


class _Group:
    def __init__(self, names, extra=0, block=None):
        self.names, self.extra, self.offs = names, extra, {}
        o = 0
        for n in names:
            self.offs[n] = o
            o += SHARD_ROWS[n]
        self.shard_rows = o
        self.rows = -(-(o + extra) // 16) * 16
        self.block = block

    def _fill(self, parts, axis, dtype):
        used = sum(p.shape[axis] for p in parts)
        if used < self.rows:
            shape = list(parts[0].shape)
            shape[axis] = self.rows - used
            parts = parts + [jnp.zeros(shape, dtype)]
        return jnp.concatenate(parts, axis=axis)

    def pack(self, shards, extra_rows, dtype):
        parts = [_pack_rows(shards[n].astype(dtype)) for n in self.names] + [r.astype(dtype) for r in extra_rows]
        return self._fill(parts, 0, dtype)

    def pack_for_owners(self, full, extra_rows):
        parts = [_pack_owner_rows(_split_for_owners(n, full[n])) for n in self.names]
        if extra_rows:
            parts.append(jnp.broadcast_to(jnp.concatenate(extra_rows, axis=0)[None], (N_DEV, len(extra_rows), 1024)))
        return self._fill(parts, 1, F32)

    def unpack(self, pack, name):
        r, c = SHARD_SHAPE[name]
        seg = pack[self.offs[name]:self.offs[name] + SHARD_ROWS[name]].reshape(-1)[:r * c]
        return seg.reshape(1, r, c)

    def gathered(self, gathered, name):
        r, c = SHARD_SHAPE[name]
        seg = gathered[:, self.offs[name]:self.offs[name] + SHARD_ROWS[name]]
        seg = seg.reshape(N_DEV, -1)[:, :r * c].reshape(N_DEV, r, c)
        if name in ROW_SHARDED:
            return seg.reshape(N_DEV * r, c)
        return seg.transpose(1, 0, 2).reshape(r, N_DEV * c)


ROW_SHARDED = ("w_out", "w_mq", "w_mo", "w_down")
N_EXTRA = len(REPL) + 1
AG_FIRST = _Group(("w_in", "w_gk_up"))
AG_MID = _Group(("w_out", "w_mq", "w_mkv"))
AG_LATE = _Group(("w_mo", "w_gate_up", "w_down"))
RS_FFN = _Group(("w_gate_up", "w_down"), block=264)
RS_MID = _Group(("w_out", "w_mq", "w_mkv", "w_mo"), block=320)
RS_LAST = _Group(("w_in", "w_gk_up"), extra=N_EXTRA, block=200)


def _split_for_owners(name, full):
    r, c = SHARD_SHAPE[name]
    if name in ROW_SHARDED:
        return full.reshape(N_DEV, r, c)
    return full.reshape(r, N_DEV, c).transpose(1, 0, 2)


def _repl_row(a):
    flat = a.reshape(-1)
    return jnp.concatenate([flat, jnp.zeros((1024 - flat.shape[0],), flat.dtype)]).reshape(1, 1024)


def _local_step(x, mem, tgt, wf, rp, shards=None, scatter=False):
    rows = x.shape[0]
    bm = min(512, rows)
    bmx = min(256, rows)
    mt = min(MM_TILE, rows)
    kt = min(512, rows)
    w_cat, wgk_pad = wf["w_cat"], wf["wgk_pad"]
    wg_t = jnp.tile(rp["gla_norm_w"], (1, GLA_HEADS))
    ws_t = jnp.tile(rp["sb_norm_w"], (1, SB_HEADS))
    lane = jnp.arange(GLA_V_W)
    grp_g = (lane[:, None] // GLA_DV == lane[None, :] // GLA_DV).astype(BF16)
    grp_s = (lane[:, None] // SB_DH == lane[None, :] // SB_DH).astype(BF16)

    h1 = _rms_fwd("mix_norm_fwd", x, rp["mix_norm_w"], bm)
    proj = _matmul("in_proj", h1, w_cat, "nn", rows, PROJ_W, D_MODEL, F32, mt, 640, D_MODEL)
    gk = _gate_fwd(proj, wgk_pad, rp["b_gk"], bm)
    if shards is None:
        o_g, s_all = _gla_fwd(proj, gk, rows)
        (o_s,) = _sb_fwd(proj, rows)
    else:
        o_g, s_all, got_mid = _gla_fwd(proj, gk, rows, _Exchange(AG_MID.pack(shards, [], BF16), scatter=False))
        o_s, got_late = _sb_fwd(proj, rows, _Exchange(AG_LATE.pack(shards, [], BF16), scatter=False))
        wf = {**wf, **{n: AG_MID.gathered(got_mid, n) for n in AG_MID.names},
              **{n: AG_LATE.gathered(got_late, n) for n in AG_LATE.names}}
    cat = _mix_out_fwd(o_g, proj, o_s, wg_t, ws_t, grp_g, grp_s, bm)
    x1 = _matmul("out_proj", cat, wf["w_out"], "nn", rows, D_MODEL, D_MODEL, F32, mt, MM_TILE, D_MODEL, residual=x)
    h2 = _rms_fwd("xattn_norm_fwd", x1, rp["xattn_norm_w"], bm)
    qm = _matmul("mq_proj", h2, wf["w_mq"], "nn", rows, D_MODEL, D_MODEL, F32, mt, MM_TILE, D_MODEL)
    mem_n, kpre, kn, v_m = _mem_kv_fwd(mem, rp["mem_norm_w"], wf["w_mkv"], rp["mk_norm_w"])
    o_m = _xattn_fwd(qm, kn, v_m, rp["mq_norm_w"], bmx)
    x2 = _matmul("mo_proj", o_m, wf["w_mo"], "nn", rows, D_MODEL, D_MODEL, F32, mt, MM_TILE, D_MODEL, residual=x1)
    h3 = _rms_fwd("ffn_norm_fwd", x2, rp["ffn_norm_w"], bm)
    gate, up, act = _ffn_up(h3, wf["w_gate_up"], rows, bm)
    y = _matmul("ffn_down", act, wf["w_down"], "nn", rows, D_MODEL, D_FF, F32, mt, MM_TILE, FF_BN, residual=x2)
    dy, sq = _loss_kernel(y, tgt, bm)

    g = {}
    dgu = _ffn_act_bwd(dy, wf["w_down"], gate, up, rows, bm)
    g["w_down"] = _matmul("grad_w_down", act, dy, "tn", D_FF, D_MODEL, rows, F32, FF_BN, MM_TILE, kt)
    nkb = FF_NB
    dh3 = _matmul("ffn_up_bwd", dgu, wf["w_gate_up"], "nt", rows, D_MODEL, 2 * D_FF, F32, mt, MM_TILE, FF_BN,
                  a_spec=pl.BlockSpec((None, mt, FF_BN), lambda i, j, kk: (kk // nkb, i, kk % nkb)))
    g["w_gate_up"] = _matmul(
        "grad_w_gate_up", h3, dgu, "tn", D_MODEL, 2 * D_FF, rows, F32, MM_TILE, FF_BN, kt,
        b_spec=pl.BlockSpec((None, kt, FF_BN), lambda i, j, kk: (j // nkb, kk, j % nkb)))
    dx2, g["ffn_norm_w"] = _rms_bwd("ffn_norm_bwd", x2, rp["ffn_norm_w"], dh3, dy, bm)

    do_m = _matmul("mo_proj_bwd", dx2, wf["w_mo"], "nt", rows, D_MODEL, D_MODEL, BF16, mt, MM_TILE, D_MODEL)
    g["w_mo"] = _matmul("grad_w_mo", o_m, dx2, "tn", D_MODEL, D_MODEL, rows, F32, MM_TILE, MM_TILE, kt)
    dqm, dkn, dv_m, g["mq_norm_w"] = _xattn_bwd(qm, kn, v_m, rp["mq_norm_w"], do_m, bmx)
    g["w_mkv"], g["mk_norm_w"], g["mem_norm_w"] = _mem_kv_bwd(
        mem, rp["mem_norm_w"], wf["w_mkv"], rp["mk_norm_w"], mem_n, kpre, dkn, dv_m)
    dh2 = _matmul("mq_proj_bwd", dqm, wf["w_mq"], "nt", rows, D_MODEL, D_MODEL, F32, mt, MM_TILE, D_MODEL)
    g["w_mq"] = _matmul("grad_w_mq", h2, dqm, "tn", D_MODEL, D_MODEL, rows, F32, MM_TILE, MM_TILE, kt)
    dx1, g["xattn_norm_w"] = _rms_bwd("xattn_norm_bwd", x1, rp["xattn_norm_w"], dh2, dx2, bm)

    dcat = _matmul("out_proj_bwd", dx1, wf["w_out"], "nt", rows, D_MODEL, D_MODEL, F32, mt, MM_TILE, D_MODEL)
    g["w_out"] = _matmul("grad_w_out", cat, dx1, "tn", D_MODEL, D_MODEL, rows, F32, MM_TILE, MM_TILE, kt)
    do_g, dgg, do_s, dwg, dws = _mix_out_bwd(dcat, o_g, proj, o_s, wg_t, ws_t, grp_g, grp_s, bm)
    g["gla_norm_w"] = dwg.reshape(GLA_HEADS, GLA_DV).sum(axis=0, keepdims=True)
    g["sb_norm_w"] = dws.reshape(SB_HEADS, SB_DH).sum(axis=0, keepdims=True)
    recv = {}
    if scatter:
        dq_s, dk_s, dv_s, recv["ffn"] = _sb_bwd(
            proj, o_s, do_s, rows, _Exchange(RS_FFN.pack_for_owners(g, []), scatter=True))
        dq_g, dk_g, dv_g, dgk, recv["mid"] = _gla_bwd(
            proj, gk, do_g, s_all, rows, _Exchange(RS_MID.pack_for_owners(g, []), scatter=True))
    else:
        dq_s, dk_s, dv_s = _sb_bwd(proj, o_s, do_s, rows)
        dq_g, dk_g, dv_g, dgk = _gla_bwd(proj, gk, do_g, s_all, rows)
    dlr, dwgk, g["b_gk"] = _gate_bwd(proj, wgk_pad, rp["b_gk"], dgk, bm)
    g["w_gk_up"] = dwgk[:GATE_RANK]
    dproj = jnp.concatenate([dq_g.astype(BF16), dk_g.astype(BF16), dv_g.astype(BF16), dgg, dq_s.astype(BF16),
                             dk_s.astype(BF16), dv_s.astype(BF16), dlr], axis=1)
    dh1 = _matmul("in_proj_bwd", dproj, w_cat, "nt", rows, D_MODEL, PROJ_W, F32, mt, MM_TILE, 640)
    dw_cat = _matmul("grad_w_in", h1, dproj, "tn", D_MODEL, PROJ_W, rows, F32, MM_TILE, 640, kt)
    g["w_in"] = jnp.concatenate([dw_cat[:, :C_QS], dw_cat[:, C_LR:C_LR + GATE_RANK], dw_cat[:, C_QS:C_LR]], axis=1)
    dx, g["mix_norm_w"] = _rms_bwd("mix_norm_bwd", x, rp["mix_norm_w"], dh1, dx1, bm)
    return sq, dx, g, recv


def _first_weights(gathered):
    w_in = AG_FIRST.gathered(gathered, "w_in")
    lr_end = C_QS + GATE_RANK
    w_cat = jnp.concatenate([w_in[:, :C_QS], w_in[:, lr_end:], w_in[:, C_QS:lr_end],
                             jnp.zeros((D_MODEL, PROJ_W - D_IN), BF16)], axis=1)
    wgk = AG_FIRST.gathered(gathered, "w_gk_up")
    return {"w_cat": w_cat, "wgk_pad": jnp.concatenate([wgk, jnp.zeros((LANES - GATE_RANK, GLA_QK_W), BF16)], axis=0)}


def kernel(x, mem, mix_norm_w, w_in, w_gk_up, b_gk, gla_norm_w, sb_norm_w, w_out, xattn_norm_w, mem_norm_w, w_mq, w_mkv, mq_norm_w, mk_norm_w, w_mo, ffn_norm_w, w_gate_up, w_down, loss_target, m_mix_norm_w, m_w_in, m_w_gk_up, m_b_gk, m_gla_norm_w, m_sb_norm_w, m_w_out, m_xattn_norm_w, m_mem_norm_w, m_w_mq, m_w_mkv, m_mq_norm_w, m_mk_norm_w, m_w_mo, m_ffn_norm_w, m_w_gate_up, m_w_down, v_mix_norm_w, v_w_in, v_w_gk_up, v_b_gk, v_gla_norm_w, v_sb_norm_w, v_w_out, v_xattn_norm_w, v_mem_norm_w, v_w_mq, v_w_mkv, v_mq_norm_w, v_mk_norm_w, v_w_mo, v_ffn_norm_w, v_w_gate_up, v_w_down):
    given = dict(locals())
    w = {n: given[n][0] for n in WEIGHTS}
    m = {n: given["m_" + n][0] for n in WEIGHTS}
    v = {n: given["v_" + n][0] for n in WEIGHTS}

    wf = _first_weights(_exchange_call("gather_first_weights", AG_FIRST.pack(w, [], BF16), scatter=False))
    rp = {n: w[n].reshape(1, -1) for n in REPL}
    sq, dx, g, recv = _local_step(x[0], mem[0], loss_target[0], wf, rp, shards=w, scatter=True)

    loss_row = _repl_row(jnp.sum(sq).reshape(1) * (0.5 / D_MODEL))
    repl_rows = [_repl_row(g[n]) for n in REPL] + [loss_row]
    recv["last"] = _exchange_call("scatter_last_gradients", RS_LAST.pack_for_owners(g, repl_rows), scatter=True)

    zero_row = jnp.zeros((1, 1024), F32)
    out_packs = {}
    for key, grp in (("ffn", RS_FFN), ("mid", RS_MID), ("last", RS_LAST)):
        extra = lambda t: [_repl_row(t[n]) for n in REPL] + [zero_row] if grp.extra else []
        out_packs[key] = _adamw("sum_adamw_" + key, recv[key], *[grp.pack(t, extra(t), F32) for t in (w, m, v)],
                                block=grp.block)

    def unpack(kind, name):
        for key, grp in (("ffn", RS_FFN), ("mid", RS_MID), ("last", RS_LAST)):
            if name in grp.names:
                return grp.unpack(out_packs[key][kind], name)
        row = out_packs["last"][kind][RS_LAST.shard_rows + REPL.index(name)]
        return row[:w[name].shape[-1]].reshape(1, -1)

    loss = out_packs["last"][0][RS_LAST.shard_rows + len(REPL), 0]
    outs = [loss, dx[None]]
    for kind in range(4):
        outs += [unpack(kind, n) for n in WEIGHTS]
    return tuple(outs)
```

```python
import functools
import math

import jax
import jax.numpy as jnp
from jax import lax
from jax.experimental import pallas as pl
from jax.experimental.pallas import tpu as pltpu

F32 = jnp.float32
BF16 = jnp.bfloat16

N_DEV = 8
D_MODEL = 1024
GLA_HEADS = 4
GLA_DK = 64
GLA_DV = 128
GLA_CHUNK = 64
GLA_STEP_CHUNKS = 4
GLA_QK_W = GLA_HEADS * GLA_DK
GLA_V_W = GLA_HEADS * GLA_DV
GATE_RANK = 16
SB_HEADS = 8
SB_DH = 64
SB_W = SB_HEADS * SB_DH
SB_BLK = 128
SB_QT = 1024
SB_DEAD = -104.0
MEM_LEN = 256
MEM_HEADS = 4
MEM_DH = 256
D_FF = 2816
D_IN = 3088
RMS_EPS = 1e-6
LANES = 128

PROJ_W = 3200
C_QG, C_KG, C_VG, C_GG, C_QS, C_KS, C_VS, C_LR = 0, 256, 512, 1024, 1536, 2048, 2560, 3072

ADAM_LR, ADAM_B1, ADAM_B2, ADAM_EPS, ADAM_WD, ADAM_STEP = 0.001, 0.9, 0.999, 1e-08, 0.01, 10

SHARD_SHAPE = {"w_in": (1024, 386), "w_out": (128, 1024), "w_mq": (128, 1024), "w_mkv": (1024, 256),
               "w_mo": (128, 1024), "w_gate_up": (1024, 704), "w_down": (352, 1024), "w_gk_up": (16, 32)}
SHARD_ROWS = {n: -(-(s[0] * s[1]) // 1024) for n, s in SHARD_SHAPE.items()}
REPL = ("mix_norm_w", "b_gk", "gla_norm_w", "sb_norm_w", "xattn_norm_w", "mem_norm_w", "mq_norm_w",
        "mk_norm_w", "ffn_norm_w")
WEIGHTS = ("mix_norm_w", "w_in", "w_gk_up", "b_gk", "gla_norm_w", "sb_norm_w", "w_out", "xattn_norm_w",
           "mem_norm_w", "w_mq", "w_mkv", "mq_norm_w", "mk_norm_w", "w_mo", "ffn_norm_w", "w_gate_up", "w_down")
VMEM_LIMIT = 56 * 1024 * 1024
MM_TILE = 1024


def _cparams(*sem):
    return pltpu.CompilerParams(dimension_semantics=sem if sem else None, vmem_limit_bytes=VMEM_LIMIT)


def _dot(a, b, ca=1, cb=0):
    return lax.dot_general(a.astype(BF16), b.astype(BF16), (((ca,), (cb,)), ((), ())),
                           preferred_element_type=F32)


def _split(x, parts):
    out = []
    for _ in range(parts - 1):
        hi = x.astype(BF16)
        out.append(hi)
        x = x - hi.astype(F32)
    out.append(x.astype(BF16))
    return out


def _dot_lhs_exact(x, m, ca=1, cb=0, parts=3):
    acc = None
    for p in _split(x, parts):
        t = _dot(p, m, ca, cb)
        acc = t if acc is None else acc + t
    return acc


def _dot_rhs_exact(m, x, ca=1, cb=0, parts=3):
    acc = None
    for p in _split(x, parts):
        t = _dot(m, p, ca, cb)
        acc = t if acc is None else acc + t
    return acc


def _dot3(a, b, ca=1, cb=0):
    a_hi, a_lo = _split(a, 2)
    b_hi, b_lo = _split(b, 2)
    return _dot(a_hi, b_hi, ca, cb) + (_dot(a_hi, b_lo, ca, cb) + _dot(a_lo, b_hi, ca, cb))


def _log_sigmoid(z):
    return jnp.minimum(z, 0.0) - jnp.log(1.0 + jnp.exp(-jnp.abs(z)))


def _sigmoid(z):
    e = jnp.exp(-jnp.abs(z))
    return jnp.where(z >= 0, 1.0, e) / (1.0 + e)


def _iota2(shape, dim):
    return lax.broadcasted_iota(jnp.int32, shape, dim)


def _rowcall(name, fn, row_ins, full_ins, row_outs, acc_outs, bm, rows):
    n_in = len(row_ins) + len(full_ins)
    n_row = len(row_outs)

    def body(*refs):
        ins, outs = refs[:n_in], refs[n_in:]
        res = fn(*[r[...] for r in ins])
        for r, v in zip(outs[:n_row], res[:n_row]):
            r[...] = v.astype(r.dtype)
        first = pl.program_id(0) == 0
        for r, v in zip(outs[n_row:], res[n_row:]):
            def init(r=r):
                r[...] = jnp.zeros(r.shape, r.dtype)
            pl.when(first)(init)
            r[...] += v

    in_specs = [pl.BlockSpec((bm, w), functools.partial(lambda i, c: (i, c), c=c)) for _, w, c in row_ins]
    in_specs += [pl.BlockSpec(a.shape, lambda i: (0, 0)) for a in full_ins]
    out_specs = [pl.BlockSpec((bm, w), lambda i: (i, 0)) for w, _ in row_outs]
    out_specs += [pl.BlockSpec(s, lambda i: (0, 0)) for s in acc_outs]
    out_shape = [jax.ShapeDtypeStruct((rows, w), dt) for w, dt in row_outs]
    out_shape += [jax.ShapeDtypeStruct(s, F32) for s in acc_outs]
    return pl.pallas_call(
        body, name=name, grid=(rows // bm,), in_specs=in_specs, out_specs=out_specs, out_shape=out_shape,
        compiler_params=_cparams("arbitrary"),
    )(*[a for a, _, _ in row_ins], *full_ins)


def _matmul(name, a, b, mode, m, n, k, out_dtype, bm, bn, bk, residual=None, a_spec=None, b_spec=None,
            exchange=None):
    bm, bn, bk = min(bm, m), min(bn, n), min(bk, k)
    nk = k // bk
    ca, cb = {"nn": (1, 0), "nt": (1, 1), "tn": (0, 0)}[mode]
    if a_spec is None:
        a_spec = (pl.BlockSpec((bk, bm), lambda i, j, kk: (kk, i)) if mode == "tn"
                  else pl.BlockSpec((bm, bk), lambda i, j, kk: (i, kk)))
    if b_spec is None:
        b_spec = (pl.BlockSpec((bn, bk), lambda i, j, kk: (j, kk)) if mode == "nt"
                  else pl.BlockSpec((bk, bn), lambda i, j, kk: (kk, j)))
    has_res = residual is not None

    def body(*refs):
        a_ref, b_ref = refs[0], refs[1]
        res_ref = refs[2] if has_res else None
        o_ref = refs[2 + has_res]
        part = _dot(a_ref[...], b_ref[...], ca, cb)

        def finish(total):
            if has_res:
                total = total + res_ref[...]
            o_ref[...] = total.astype(o_ref.dtype)

        if nk == 1:
            finish(part)
        else:
            acc_ref = refs[3 + has_res]
            kk = pl.program_id(2)

            @pl.when(kk == 0)
            def _():
                acc_ref[...] = part

            @pl.when(kk > 0)
            def _():
                acc_ref[...] += part

            @pl.when(kk == nk - 1)
            def _():
                finish(acc_ref[...])

    in_specs = [a_spec, b_spec]
    args = [a, b]
    if has_res:
        in_specs.append(pl.BlockSpec((bm, bn), lambda i, j, kk: (i, j)))
        args.append(residual)
    sem = ("parallel", "parallel", "arbitrary") if exchange is None else ("arbitrary",) * 3
    res = _hosted_call(
        body, exchange, grid=(m // bm, n // bn, nk), in_specs=in_specs,
        out_specs=[pl.BlockSpec((bm, bn), lambda i, j, kk: (i, j))],
        out_shape=[jax.ShapeDtypeStruct((m, n), out_dtype)],
        scratch_shapes=[pltpu.VMEM((bm, bn), F32)] if nk > 1 else [],
        args=args, name=name, compiler_params=_cparams(*sem))
    return res[0] if exchange is None else res


def _peer(mask):
    x, y, c = lax.axis_index("x"), lax.axis_index("y"), lax.axis_index("c")
    mx, my, mc = (mask >> 2) & 1, (mask >> 1) & 1, mask & 1
    px, py, pc = (1 - x if mx else x), (1 - y if my else y), (1 - c if mc else c)
    return (px, py, pc), 4 * px + 2 * py + pc


def _my_index():
    return 4 * lax.axis_index("x") + 2 * lax.axis_index("y") + lax.axis_index("c")


class _Exchange:
    def __init__(self, src, scatter):
        self.src, self.scatter = src, scatter
        self.in_spec = pl.BlockSpec(memory_space=pl.ANY)
        self.out_spec = pl.BlockSpec(memory_space=pl.ANY)
        self.out_shape = jax.ShapeDtypeStruct((N_DEV, src.shape[-2], 1024), src.dtype)
        self.scratch = [pltpu.SemaphoreType.DMA((N_DEV - 1,)), pltpu.SemaphoreType.DMA((N_DEV - 1,)),
                        pltpu.SemaphoreType.DMA(())]

    def _copies(self, src_ref, out_ref, sems):
        send_sems, recv_sems, local_sem = sems
        me = _my_index()
        copies = [pltpu.make_async_copy(src_ref.at[me] if self.scatter else src_ref, out_ref.at[me], local_sem)]
        for mask in range(1, N_DEV):
            peer, peer_index = _peer(mask)
            copies.append(pltpu.make_async_remote_copy(
                src_ref=src_ref.at[peer_index] if self.scatter else src_ref, dst_ref=out_ref.at[me],
                send_sem=send_sems.at[mask - 1], recv_sem=recv_sems.at[mask - 1],
                device_id=peer, device_id_type=pl.DeviceIdType.MESH))
        return copies

    def start(self, src_ref, out_ref, sems):
        for cp in self._copies(src_ref, out_ref, sems):
            cp.start()

    def wait(self, src_ref, out_ref, sems):
        for cp in self._copies(src_ref, out_ref, sems):
            cp.wait()


def _hosted_call(body, ex, grid, in_specs, out_specs, out_shape, scratch_shapes, args, **kw):
    if ex is None:
        return pl.pallas_call(body, grid=grid, in_specs=in_specs, out_specs=out_specs, out_shape=out_shape,
                              scratch_shapes=scratch_shapes, **kw)(*args)
    n_in, n_out, n_scr = len(in_specs), len(out_specs), len(scratch_shapes)

    def hosted(*refs):
        ins, src_ref = refs[:n_in], refs[n_in]
        outs, out_ref = refs[n_in + 1:n_in + 1 + n_out], refs[n_in + 1 + n_out]
        scr, sems = refs[n_in + 2 + n_out:n_in + 2 + n_out + n_scr], refs[n_in + 2 + n_out + n_scr:]
        ids = [pl.program_id(a) for a in range(len(grid))]
        first = functools.reduce(jnp.logical_and, [p == 0 for p in ids])
        last = functools.reduce(jnp.logical_and, [p == n - 1 for p, n in zip(ids, grid)])

        @pl.when(first)
        def _():
            ex.start(src_ref, out_ref, sems)

        body(*ins, *outs, *scr)

        @pl.when(last)
        def _():
            ex.wait(src_ref, out_ref, sems)

    res = pl.pallas_call(
        hosted, grid=grid, in_specs=list(in_specs) + [ex.in_spec], out_specs=list(out_specs) + [ex.out_spec],
        out_shape=list(out_shape) + [ex.out_shape], scratch_shapes=list(scratch_shapes) + ex.scratch, **kw,
    )(*args, ex.src)
    return res


def _exchange_call(name, src, scatter):
    ex = _Exchange(src, scatter)

    def body(src_ref, out_ref, *sems):
        ex.start(src_ref, out_ref, sems)
        ex.wait(src_ref, out_ref, sems)

    return pl.pallas_call(
        body, name=name, in_specs=[ex.in_spec], out_specs=ex.out_spec, out_shape=ex.out_shape,
        scratch_shapes=ex.scratch, compiler_params=pltpu.CompilerParams(has_side_effects=True),
    )(src)


def _adamw(name, recv, w, m, v, block):
    rows = w.shape[0]
    c1 = 1.0 - ADAM_B1 ** ADAM_STEP
    c2 = 1.0 - ADAM_B2 ** ADAM_STEP

    def body(r_ref, w_ref, m_ref, v_ref, g_out, d_out, m_out, v_out):
        g = r_ref[0]
        for s in range(1, N_DEV):
            g = g + r_ref[s]
        m_new = ADAM_B1 * m_ref[...] + (1.0 - ADAM_B1) * g
        v_new = ADAM_B2 * v_ref[...] + (1.0 - ADAM_B2) * (g * g)
        m_hat = m_new / c1
        v_hat = v_new / c2
        g_out[...] = g
        d_out[...] = -ADAM_LR * (m_hat / (jnp.sqrt(v_hat) + ADAM_EPS) + ADAM_WD * w_ref[...])
        m_out[...] = m_new
        v_out[...] = v_new

    blk = pl.BlockSpec((block, 1024), lambda i: (i, 0))
    return pl.pallas_call(
        body, name=name, grid=(rows // block,),
        in_specs=[pl.BlockSpec((N_DEV, block, 1024), lambda i: (0, i, 0)), blk, blk, blk],
        out_specs=[blk] * 4, out_shape=[jax.ShapeDtypeStruct((rows, 1024), F32)] * 4,
        compiler_params=_cparams("parallel"),
    )(recv, w, m, v)


def _rms_fwd(name, x, w, bm):
    def fn(xb, wb):
        r = lax.rsqrt(jnp.mean(xb * xb, axis=-1, keepdims=True) + RMS_EPS)
        return (xb * r * wb,)
    return _rowcall(name, fn, [(x, D_MODEL, 0)], [w], [(D_MODEL, BF16)], [], bm, x.shape[0])[0]


def _rms_bwd(name, x, w, dh, dres, bm):
    def fn(xb, dhb, drb, wb):
        r = lax.rsqrt(jnp.mean(xb * xb, axis=-1, keepdims=True) + RMS_EPS)
        xh = xb * r
        dxh = dhb.astype(F32) * wb
        dx = drb + r * (dxh - xh * jnp.mean(dxh * xh, axis=-1, keepdims=True))
        return dx, jnp.sum(dhb.astype(F32) * xh, axis=0, keepdims=True)
    return _rowcall(name, fn, [(x, D_MODEL, 0), (dh, D_MODEL, 0), (dres, D_MODEL, 0)], [w],
                    [(D_MODEL, F32)], [(1, D_MODEL)], bm, x.shape[0])


def _gate_fwd(proj, wgk_pad, b_gk, bm):
    def fn(lr, wg, bg):
        z = _dot(lr, wg) + bg
        return (_log_sigmoid(z) * (1.0 / 16.0),)
    return _rowcall("gla_gate_fwd", fn, [(proj, LANES, C_LR // LANES)], [wgk_pad, b_gk],
                    [(GLA_QK_W, F32)], [], bm, proj.shape[0])[0]


def _gate_bwd(proj, wgk_pad, b_gk, dgk, bm):
    def fn(lr, dg, wg, bg):
        z = _dot(lr, wg) + bg
        dz = dg * _sigmoid(-z) * (1.0 / 16.0)
        return _dot(dz, wg, 1, 1), _dot(lr, dz, 0, 0), jnp.sum(dz, axis=0, keepdims=True)
    return _rowcall("gla_gate_bwd", fn, [(proj, LANES, C_LR // LANES), (dgk, GLA_QK_W, 0)], [wgk_pad, b_gk],
                    [(LANES, BF16)], [(LANES, GLA_QK_W), (1, GLA_QK_W)], bm, proj.shape[0])


def _group_mean(x, g, size):
    return _dot_lhs_exact(x, g, parts=2) * (1.0 / size)


def _mix_out_fwd(o_g, proj, o_s, wg_t, ws_t, grp_g, grp_s, bm):
    def fn(og, gg, os_, wg, ws, gmat, smat):
        rg = lax.rsqrt(_group_mean(og * og, gmat, GLA_DV) + RMS_EPS)
        yg = og * rg * wg * (gg * _sigmoid(gg))
        rs = lax.rsqrt(_group_mean(os_ * os_, smat, SB_DH) + RMS_EPS)
        ys = os_ * rs * ws
        return (jnp.concatenate([yg, ys], axis=1),)
    return _rowcall("mix_out_fwd", fn, [(o_g, GLA_V_W, 0), (proj, GLA_V_W, C_GG // GLA_V_W), (o_s, SB_W, 0)],
                    [wg_t, ws_t, grp_g, grp_s], [(D_MODEL, BF16)], [], bm, o_g.shape[0])[0]


def _mix_out_bwd(dcat, o_g, proj, o_s, wg_t, ws_t, grp_g, grp_s, bm):
    def fn(dyg, dys, og, gg, os_, wg, ws, gmat, smat):
        dyg = dyg.astype(F32)
        dys = dys.astype(F32)
        rg = lax.rsqrt(_group_mean(og * og, gmat, GLA_DV) + RMS_EPS)
        xh = og * rg
        sg = _sigmoid(gg)
        silu = gg * sg
        dxh = dyg * wg * silu
        dgg = dyg * xh * wg * (sg * (1.0 + gg * (1.0 - sg)))
        dwg = jnp.sum(dyg * xh * silu, axis=0, keepdims=True)
        dog = rg * (dxh - xh * _group_mean(dxh * xh, gmat, GLA_DV))
        rs = lax.rsqrt(_group_mean(os_ * os_, smat, SB_DH) + RMS_EPS)
        xs = os_ * rs
        dxs = dys * ws
        dws = jnp.sum(dys * xs, axis=0, keepdims=True)
        dos = rs * (dxs - xs * _group_mean(dxs * xs, smat, SB_DH))
        return dog, dgg, dos, dwg, dws
    return _rowcall("mix_out_bwd", fn,
                    [(dcat, GLA_V_W, 0), (dcat, SB_W, 1), (o_g, GLA_V_W, 0), (proj, GLA_V_W, C_GG // GLA_V_W),
                     (o_s, SB_W, 0)],
                    [wg_t, ws_t, grp_g, grp_s], [(GLA_V_W, F32), (GLA_V_W, BF16), (SB_W, F32)],
                    [(1, GLA_V_W), (1, SB_W)], bm, o_g.shape[0])


def _loss_kernel(y, tgt, bm):
    def fn(yb, tb):
        err = yb - tb
        return err * (1.0 / D_MODEL), jnp.sum(err * err, axis=0, keepdims=True)
    return _rowcall("loss_head", fn, [(y, D_MODEL, 0), (tgt, D_MODEL, 0)], [], [(D_MODEL, F32)], [(1, D_MODEL)],
                    bm, y.shape[0])


def _sb_tri(inclusive):
    j, s = _iota2((2 * SB_BLK, 2 * SB_BLK), 0), _iota2((2 * SB_BLK, 2 * SB_BLK), 1)
    j = jnp.where(j >= SB_BLK, j - SB_BLK, j)
    keep = (j >= s) if inclusive else (j > s)
    return ((s >= SB_BLK) | keep).astype(BF16)


def _dot_hilo(x, m2):
    hi, lo = _split(x, 2)
    return _dot(jnp.concatenate([hi, lo], axis=1), m2)


def _sb_mask(n):
    return _iota2((n, SB_BLK), 1) < _iota2((n, SB_BLK), 0)


def _add_rows(full, part, row0, row1):
    pieces = [full[:row0]] if row0 else []
    pieces.append(full[row0:row1] + part)
    if row1 < full.shape[0]:
        pieces.append(full[row1:])
    return pieces[0] if len(pieces) == 1 else jnp.concatenate(pieces, axis=0)


def _sb_visit(tiles, carry, c_slots, qt, diag_start, n_left, left_start):
    for sub in reversed(range(qt // SB_BLK)):
        carry = tiles(diag_start(sub), carry, True, sub * SB_BLK, qt)

    def step(jj, cr):
        return tiles(left_start(jj), cr, False, 0, qt)

    return _sb_sweep(n_left, step, carry, c_slots)


def _sb_sweep(n_tiles, step, carry, c_slots):
    def alive(state):
        jj, carry = state
        c_max = jnp.max(functools.reduce(jnp.maximum, [carry[s] for s in c_slots]))
        return jnp.logical_and(jj < n_tiles, c_max > SB_DEAD)

    def body(state):
        jj, carry = state
        return jj + 1, step(jj, carry)

    return lax.while_loop(alive, body, (jnp.int32(0), carry))[1]


def _row_blocks(n):
    return [slice(r, r + SB_BLK) for r in range(0, n, SB_BLK)]


def _hilo(x):
    hi, lo = _split(x, 2)
    return jnp.concatenate([hi, lo], axis=1)


def _sb_tile(q, k, c, tri_excl, diag):
    blocks = _row_blocks(q.shape[0])
    strict = _sb_mask(SB_BLK) if diag else None
    z = _dot(q, k, 1, 1)
    lbs, pieces = [], []
    for r, rs in enumerate(blocks):
        lb = _log_sigmoid(z[rs])
        l1 = lb - z[rs]
        if diag and r == 0:
            l1 = jnp.where(strict, l1, 0.0)
        lbs.append(lb)
        pieces.append(_hilo(l1))
    sums = _dot(jnp.concatenate(pieces, axis=0), tri_excl)
    a = []
    for r, rs in enumerate(blocks):
        ar = jnp.exp(lbs[r] + sums[rs, :SB_BLK] + c[rs])
        if diag and r == 0:
            ar = jnp.where(strict, ar, 0.0)
        a.append(ar.astype(BF16))
    return lbs, a, sums[:, SB_BLK:]


def _sb_fwd(proj, rows, exchange=None):
    qt = min(SB_QT, rows)
    subs = qt // SB_BLK

    def body(q_ref, k_ref, v_ref, o_ref):
        i = pl.program_id(1)
        tri_excl = _sb_tri(False)
        heads = [slice(SB_DH * hh, SB_DH * (hh + 1)) for hh in range(2)]
        qs = [(q_ref[:, sl] * 0.125).astype(BF16) for sl in heads]

        def tiles(start, carry, diag, row0, row1):
            out = []
            for hh, sl in enumerate(heads):
                o, c = carry[2 * hh], carry[2 * hh + 1]
                k = k_ref[pl.ds(start, SB_BLK), sl].astype(BF16)
                v = v_ref[pl.ds(start, SB_BLK), sl].astype(BF16)
                _, a, dc = _sb_tile(qs[hh][row0:row1], k, c[row0:row1], tri_excl, diag)
                out += [_add_rows(o, _dot(jnp.concatenate(a, axis=0), v), row0, row1), _add_rows(c, dc, row0, row1)]
            return tuple(out)

        carry = (jnp.zeros((qt, SB_DH), F32), jnp.zeros((qt, SB_BLK), F32)) * 2
        carry = _sb_visit(tiles, carry, (1, 3), qt,
                          lambda sub: pl.multiple_of(i * qt + sub * SB_BLK, SB_BLK), i * subs,
                          lambda jj: pl.multiple_of((i * subs - 1 - jj) * SB_BLK, SB_BLK))
        for hh, sl in enumerate(heads):
            o_ref[:, sl] = carry[2 * hh]

    return _hosted_call(
        body, exchange, grid=(SB_HEADS // 2, rows // qt),
        in_specs=[pl.BlockSpec((qt, LANES), lambda h, i: (i, C_QS // LANES + h)),
                  pl.BlockSpec((rows, LANES), lambda h, i: (0, C_KS // LANES + h)),
                  pl.BlockSpec((rows, LANES), lambda h, i: (0, C_VS // LANES + h))],
        out_specs=[pl.BlockSpec((qt, LANES), lambda h, i: (i, h))],
        out_shape=[jax.ShapeDtypeStruct((rows, SB_W), F32)], scratch_shapes=[],
        args=(proj, proj, proj), name="sb_attention_fwd", compiler_params=_cparams("arbitrary", "arbitrary"))


def _sb_bwd(proj, o_s, do_s, rows, exchange=None):
    qt = min(SB_QT, rows)
    subs = qt // SB_BLK

    def body(q_ref, k_ref, v_ref, o_ref, do_ref, dq_ref, dk_ref, dv_ref):
        i = pl.program_id(1)

        @pl.when(i == 0)
        def _():
            dk_ref[...] = jnp.zeros(dk_ref.shape, F32)
            dv_ref[...] = jnp.zeros(dv_ref.shape, F32)

        tri_excl, tri_incl = _sb_tri(False), _sb_tri(True)
        heads = [slice(SB_DH * hh, SB_DH * (hh + 1)) for hh in range(2)]
        qs = [(q_ref[:, sl] * 0.125).astype(BF16) for sl in heads]
        dobs = [do_ref[:, sl].astype(BF16) for sl in heads]
        dsums = [jnp.broadcast_to(jnp.sum(dob.astype(F32) * o_ref[:, sl], axis=1, keepdims=True), (qt, SB_BLK))
                 for dob, sl in zip(dobs, heads)]

        def tiles(start, carry, diag, row0, row1):
            out = []
            strict = _sb_mask(SB_BLK) if diag else None
            for hh, sl in enumerate(heads):
                dq, c, cp = carry[3 * hh:3 * hh + 3]
                q, dob = qs[hh][row0:row1], dobs[hh][row0:row1]
                dsum, cpr = dsums[hh][row0:row1], cp[row0:row1]
                blocks = _row_blocks(q.shape[0])
                k = k_ref[pl.ds(start, SB_BLK), sl].astype(BF16)
                v = v_ref[pl.ds(start, SB_BLK), sl].astype(BF16)
                lbs, a, dc = _sb_tile(q, k, c[row0:row1], tri_excl, diag)
                da = _dot(dob, v, 1, 1)
                ps = [a[r].astype(F32) * da[rs] for r, rs in enumerate(blocks)]
                psums = _dot(jnp.concatenate([_hilo(p) for p in ps], axis=0), tri_incl)
                dzs = []
                for r, rs in enumerate(blocks):
                    left = dsum[rs] - (psums[rs, :SB_BLK] + cpr[rs])
                    dz = ps[r] - jnp.exp(lbs[r]) * (ps[r] + left)
                    if diag and r == 0:
                        dz = jnp.where(strict, dz, 0.0)
                    dzs.append(dz.astype(BF16))
                dzb, ab = jnp.concatenate(dzs, axis=0), jnp.concatenate(a, axis=0)
                dk_ref[pl.ds(start, SB_BLK), sl] += _dot(dzb, q, 0, 0)
                dv_ref[pl.ds(start, SB_BLK), sl] += _dot(ab, dob, 0, 0)
                out += [_add_rows(dq, _dot(dzb, k), row0, row1), _add_rows(c, dc, row0, row1),
                        _add_rows(cp, psums[:, SB_BLK:], row0, row1)]
            return tuple(out)

        zero = jnp.zeros((qt, SB_BLK), F32)
        carry = (jnp.zeros((qt, SB_DH), F32), zero, zero) * 2
        carry = _sb_visit(tiles, carry, (1, 4), qt,
                          lambda sub: pl.multiple_of(i * qt + sub * SB_BLK, SB_BLK), i * subs,
                          lambda jj: pl.multiple_of((i * subs - 1 - jj) * SB_BLK, SB_BLK))
        for hh, sl in enumerate(heads):
            dq_ref[:, sl] = carry[3 * hh] * 0.125

    whole = lambda base: pl.BlockSpec((rows, LANES), functools.partial(lambda h, i, b: (0, b + h), b=base))
    blk = lambda base: pl.BlockSpec((qt, LANES), functools.partial(lambda h, i, b: (i, b + h), b=base))
    return _hosted_call(
        body, exchange, grid=(SB_HEADS // 2, rows // qt),
        in_specs=[blk(C_QS // LANES), whole(C_KS // LANES), whole(C_VS // LANES), blk(0), blk(0)],
        out_specs=[blk(0), whole(0), whole(0)],
        out_shape=[jax.ShapeDtypeStruct((rows, SB_W), F32)] * 3, scratch_shapes=[],
        args=(proj, proj, proj, o_s, do_s), name="sb_attention_bwd",
        compiler_params=_cparams("arbitrary", "arbitrary"))


def _gla_chunk_common(g_all):
    r_i, c_i = _iota2((GLA_CHUNK, GLA_CHUNK), 0), _iota2((GLA_CHUNK, GLA_CHUNK), 1)
    tri = (c_i <= r_i).astype(BF16)
    return _dot_rhs_exact(tri, g_all), r_i, c_i


def _gla_scaled(qh, kh, bh):
    ref = bh[GLA_CHUNK // 2:GLA_CHUNK // 2 + 1, :]
    eq, ek = jnp.exp(bh - ref), jnp.exp(ref - bh)
    mask = _iota2((GLA_CHUNK, GLA_CHUNK), 1) <= _iota2((GLA_CHUNK, GLA_CHUNK), 0)
    return eq, ek, qh * eq, kh * ek, mask


def _gla_fwd(proj, gk, rows, exchange=None):
    n_chunks = rows // GLA_CHUNK
    step_rows = GLA_CHUNK * GLA_STEP_CHUNKS

    def body(q_ref, k_ref, v_ref, g_ref, o_ref, sall_ref, s_scr):
        @pl.when(pl.program_id(0) == 0)
        def _():
            s_scr[...] = jnp.zeros(s_scr.shape, F32)

        ones = jnp.ones((GLA_CHUNK, GLA_DV), BF16)
        states = [s_scr[h] for h in range(GLA_HEADS)]
        for ci in range(GLA_STEP_CHUNKS):
            rs = slice(GLA_CHUNK * ci, GLA_CHUNK * (ci + 1))
            g_all = g_ref[rs, :]
            b_all, _, _ = _gla_chunk_common(g_all)
            for h in range(GLA_HEADS):
                sl = slice(GLA_DK * h, GLA_DK * (h + 1))
                vs = slice(GLA_DV * h, GLA_DV * (h + 1))
                qh, kh, vh = q_ref[rs, sl] * 0.125, k_ref[rs, sl], v_ref[rs, vs]
                bh, gh = b_all[:, sl], g_all[:, sl]
                s = states[h]
                sall_ref[ci, h] = s
                _, _, qs, ks, mask = _gla_scaled(qh, kh, bh)
                a = jnp.where(mask, _dot(qs, ks, 1, 1), 0.0)
                o_ref[rs, vs] = _dot(qh * jnp.exp(bh), s) + _dot(a, vh)
                bl_col = _dot_lhs_exact(gh, ones, 0, 0)
                kd = kh * jnp.exp(bh[GLA_CHUNK - 1:GLA_CHUNK, :] - bh)
                states[h] = jnp.exp(bl_col) * s + _dot(kd, vh, 0, 0)
        for h in range(GLA_HEADS):
            s_scr[h] = states[h]

    c64 = lambda w, base: pl.BlockSpec((step_rows, w), functools.partial(lambda n, b: (n, b), b=base))
    return _hosted_call(
        body, exchange, grid=(rows // step_rows,),
        in_specs=[c64(GLA_QK_W, C_QG // GLA_QK_W), c64(GLA_QK_W, C_KG // GLA_QK_W), c64(GLA_V_W, C_VG // GLA_V_W),
                  c64(GLA_QK_W, 0)],
        out_specs=[c64(GLA_V_W, 0),
                   pl.BlockSpec((GLA_STEP_CHUNKS, GLA_HEADS, GLA_DK, GLA_DV), lambda n: (n, 0, 0, 0))],
        out_shape=[jax.ShapeDtypeStruct((rows, GLA_V_W), F32),
                   jax.ShapeDtypeStruct((n_chunks, GLA_HEADS, GLA_DK, GLA_DV), F32)],
        scratch_shapes=[pltpu.VMEM((GLA_HEADS, GLA_DK, GLA_DV), F32)],
        args=(proj, proj, proj, gk), name="gla_fwd", compiler_params=_cparams("arbitrary"))


def _gla_bwd(proj, gk, do_g, s_all, rows, exchange=None):
    n_chunks = rows // GLA_CHUNK

    def body(q_ref, k_ref, v_ref, g_ref, do_ref, sall_ref, dq_ref, dk_ref, dv_ref, dg_ref, ds_scr):
        @pl.when(pl.program_id(0) == 0)
        def _():
            ds_scr[...] = jnp.zeros(ds_scr.shape, F32)

        ones = jnp.ones((GLA_CHUNK, GLA_DV), BF16)
        ones8 = jnp.ones((8, GLA_DV), F32)
        last_row = _iota2((GLA_CHUNK, GLA_DK), 0) == GLA_CHUNK - 1
        dstates = [ds_scr[h] for h in range(GLA_HEADS)]
        for ci in reversed(range(GLA_STEP_CHUNKS)):
            cs = slice(GLA_CHUNK * ci, GLA_CHUNK * (ci + 1))
            g_all = g_ref[cs, :]
            b_all, r_i, c_i = _gla_chunk_common(g_all)
            triu = (c_i >= r_i).astype(BF16)
            for h in range(GLA_HEADS):
                sl = slice(GLA_DK * h, GLA_DK * (h + 1))
                vs = slice(GLA_DV * h, GLA_DV * (h + 1))
                qh, kh, vh = q_ref[cs, sl] * 0.125, k_ref[cs, sl], v_ref[cs, vs]
                bh, gh = b_all[:, sl], g_all[:, sl]
                doh = do_ref[cs, vs]
                s, ds = sall_ref[ci, h], dstates[h]
                eb = jnp.exp(bh)
                ekd = jnp.exp(bh[GLA_CHUNK - 1:GLA_CHUNK, :] - bh)
                ebl = jnp.exp(_dot_lhs_exact(gh, ones, 0, 0))
                qb, kd = qh * eb, kh * ekd
                dq = _dot(doh, s, 1, 1) * eb
                dk = _dot(vh, ds, 1, 1) * ekd
                dv = _dot(kd, ds)
                dbl = jnp.sum(dk * kh, axis=0, keepdims=True) + _dot3(ones8, ebl * s * ds, 1, 1)[0:1, :]
                eq, ek, qs, ks, mask = _gla_scaled(qh, kh, bh)
                a = jnp.where(mask, _dot(qs, ks, 1, 1), 0.0)
                da = jnp.where(mask, _dot(doh, vh, 1, 1), 0.0)
                dq = dq + _dot(da, ks) * eq
                dk = dk + _dot(da, qs, 0, 0) * ek
                dv = dv + _dot(a, doh, 0, 0)
                db = qh * dq - kh * dk + jnp.where(last_row, dbl, 0.0)
                dq_ref[cs, sl] = dq * 0.125
                dk_ref[cs, sl] = dk
                dv_ref[cs, vs] = dv
                dg_ref[cs, sl] = _dot_rhs_exact(triu, db)
                dstates[h] = _dot(qb, doh, 0, 0) + ebl * ds
        for h in range(GLA_HEADS):
            ds_scr[h] = dstates[h]

    step_rows = GLA_CHUNK * GLA_STEP_CHUNKS
    last = rows // step_rows - 1
    c64 = lambda w, base: pl.BlockSpec((step_rows, w), functools.partial(lambda n, b: (last - n, b), b=base))
    return _hosted_call(
        body, exchange, grid=(rows // step_rows,),
        in_specs=[c64(GLA_QK_W, C_QG // GLA_QK_W), c64(GLA_QK_W, C_KG // GLA_QK_W), c64(GLA_V_W, C_VG // GLA_V_W),
                  c64(GLA_QK_W, 0), c64(GLA_V_W, 0),
                  pl.BlockSpec((GLA_STEP_CHUNKS, GLA_HEADS, GLA_DK, GLA_DV), lambda n: (last - n, 0, 0, 0))],
        out_specs=[c64(GLA_QK_W, 0), c64(GLA_QK_W, 0), c64(GLA_V_W, 0), c64(GLA_QK_W, 0)],
        out_shape=[jax.ShapeDtypeStruct((rows, GLA_QK_W), F32), jax.ShapeDtypeStruct((rows, GLA_QK_W), F32),
                   jax.ShapeDtypeStruct((rows, GLA_V_W), F32), jax.ShapeDtypeStruct((rows, GLA_QK_W), F32)],
        scratch_shapes=[pltpu.VMEM((GLA_HEADS, GLA_DK, GLA_DV), F32)],
        args=(proj, proj, proj, gk, do_g, s_all), name="gla_bwd", compiler_params=_cparams("arbitrary"))


def _mem_kv_fwd(mem, mem_norm_w, w_mkv, mk_norm_w):
    def body(mem_ref, mw_ref, w_ref, kw_ref, memn_ref, kpre_ref, kn_ref, v_ref):
        xb = mem_ref[...]
        r = lax.rsqrt(jnp.mean(xb * xb, axis=-1, keepdims=True) + RMS_EPS)
        mem_n = (xb * r * mw_ref[...]).astype(BF16)
        memn_ref[...] = mem_n
        kv = _dot(mem_n, w_ref[...])
        kpre_ref[...] = kv[:, :D_MODEL]
        v_ref[...] = kv[:, D_MODEL:].astype(BF16)
        for h in range(MEM_HEADS):
            sl = slice(MEM_DH * h, MEM_DH * (h + 1))
            kh = kv[:, sl]
            rk = lax.rsqrt(jnp.mean(kh * kh, axis=-1, keepdims=True) + RMS_EPS)
            kn_ref[:, sl] = (kh * rk * kw_ref[...]).astype(BF16)

    return pl.pallas_call(
        body, name="mem_kv_fwd",
        out_shape=[jax.ShapeDtypeStruct((MEM_LEN, D_MODEL), BF16), jax.ShapeDtypeStruct((MEM_LEN, D_MODEL), F32),
                   jax.ShapeDtypeStruct((MEM_LEN, D_MODEL), BF16), jax.ShapeDtypeStruct((MEM_LEN, D_MODEL), BF16)],
        compiler_params=_cparams(),
    )(mem, mem_norm_w, w_mkv, mk_norm_w)


def _mem_kv_bwd(mem, mem_norm_w, w_mkv, mk_norm_w, mem_n, kpre, dkn, dv):
    def body(mem_ref, mw_ref, w_ref, kw_ref, memn_ref, kpre_ref, dkn_ref, dv_ref, dw_ref, dkw_ref, dmw_ref):
        dkw = jnp.zeros((1, MEM_DH), F32)
        dk_parts = []
        for h in range(MEM_HEADS):
            sl = slice(MEM_DH * h, MEM_DH * (h + 1))
            kh, dkh = kpre_ref[:, sl], dkn_ref[:, sl]
            rk = lax.rsqrt(jnp.mean(kh * kh, axis=-1, keepdims=True) + RMS_EPS)
            xh = kh * rk
            dxh = dkh * kw_ref[...]
            dkw = dkw + jnp.sum(dkh * xh, axis=0, keepdims=True)
            dk_parts.append(rk * (dxh - xh * jnp.mean(dxh * xh, axis=-1, keepdims=True)))
        dkw_ref[...] = dkw
        dkv = jnp.concatenate(dk_parts + [dv_ref[...]], axis=1).astype(BF16)
        dw_ref[...] = _dot(memn_ref[...], dkv, 0, 0)
        dmem_n = _dot(dkv, w_ref[...], 1, 1)
        xb = mem_ref[...]
        r = lax.rsqrt(jnp.mean(xb * xb, axis=-1, keepdims=True) + RMS_EPS)
        dmw_ref[...] = jnp.sum(dmem_n * (xb * r), axis=0, keepdims=True)

    return pl.pallas_call(
        body, name="mem_kv_bwd",
        out_shape=[jax.ShapeDtypeStruct((D_MODEL, 2 * D_MODEL), F32), jax.ShapeDtypeStruct((1, MEM_DH), F32),
                   jax.ShapeDtypeStruct((1, D_MODEL), F32)],
        compiler_params=_cparams(),
    )(mem, mem_norm_w, w_mkv, mk_norm_w, mem_n, kpre, dkn, dv)


def _xattn_head(qh, kn_h, qw):
    rq = lax.rsqrt(jnp.mean(qh * qh, axis=-1, keepdims=True) + RMS_EPS)
    xh = qh * rq
    qn = (xh * qw).astype(BF16)
    s = _dot(qn, kn_h, 1, 1) * (1.0 / 16.0)
    e = jnp.exp(s - jnp.max(s, axis=-1, keepdims=True))
    p = e / jnp.sum(e, axis=-1, keepdims=True)
    return rq, xh, qn, p


def _xattn_fwd(qm, kn, v, mq_norm_w, bm):
    def fn(qb, knb, vb, qw):
        outs = []
        for h in range(MEM_HEADS):
            sl = slice(MEM_DH * h, MEM_DH * (h + 1))
            _, _, _, p = _xattn_head(qb[:, sl], knb[:, sl], qw)
            outs.append(_dot(p, vb[:, sl]))
        return (jnp.concatenate(outs, axis=1),)
    return _rowcall("xattn_fwd", fn, [(qm, D_MODEL, 0)], [kn, v, mq_norm_w], [(D_MODEL, BF16)], [], bm,
                    qm.shape[0])[0]


def _xattn_bwd(qm, kn, v, mq_norm_w, do, bm):
    def fn(qb, dob, knb, vb, qw):
        dq_parts, dkn_parts, dv_parts = [], [], []
        dqw = jnp.zeros((1, MEM_DH), F32)
        for h in range(MEM_HEADS):
            sl = slice(MEM_DH * h, MEM_DH * (h + 1))
            rq, xh, qn, p = _xattn_head(qb[:, sl], knb[:, sl], qw)
            doh = dob[:, sl].astype(BF16)
            dp = _dot(doh, vb[:, sl], 1, 1)
            ds = (p * (dp - jnp.sum(dp * p, axis=-1, keepdims=True)) * (1.0 / 16.0)).astype(BF16)
            dqn = _dot(ds, knb[:, sl])
            dkn_parts.append(_dot(ds, qn, 0, 0))
            dv_parts.append(_dot(p, doh, 0, 0))
            dqw = dqw + jnp.sum(dqn * xh, axis=0, keepdims=True)
            dxh = dqn * qw
            dq_parts.append(rq * (dxh - xh * jnp.mean(dxh * xh, axis=-1, keepdims=True)))
        return (jnp.concatenate(dq_parts, axis=1), jnp.concatenate(dkn_parts, axis=1),
                jnp.concatenate(dv_parts, axis=1), dqw)
    return _rowcall("xattn_bwd", fn, [(qm, D_MODEL, 0), (do, D_MODEL, 0)], [kn, v, mq_norm_w],
                    [(D_MODEL, BF16)], [(MEM_LEN, D_MODEL), (MEM_LEN, D_MODEL), (1, MEM_DH)], bm, qm.shape[0])


FF_BN = 1408
FF_NB = D_FF // FF_BN


def _ffn_up(h3, w_gate_up, rows, bm):
    def body(h_ref, wg_ref, wu_ref, gate_ref, up_ref, act_ref):
        hb = h_ref[...]
        gate = _dot(hb, wg_ref[...])
        up = _dot(hb, wu_ref[...])
        gate_ref[...] = gate.astype(BF16)
        up_ref[...] = up.astype(BF16)
        act_ref[...] = (gate * _sigmoid(gate) * up).astype(BF16)

    out_blk = pl.BlockSpec((bm, FF_BN), lambda i, j: (i, j))
    return pl.pallas_call(
        body, name="ffn_up", grid=(rows // bm, FF_NB),
        in_specs=[pl.BlockSpec((bm, D_MODEL), lambda i, j: (i, 0)),
                  pl.BlockSpec((D_MODEL, FF_BN), lambda i, j: (0, j)),
                  pl.BlockSpec((D_MODEL, FF_BN), lambda i, j: (0, FF_NB + j))],
        out_specs=[out_blk, out_blk, out_blk],
        out_shape=[jax.ShapeDtypeStruct((rows, D_FF), BF16)] * 3,
        compiler_params=_cparams("parallel", "arbitrary"),
    )(h3, w_gate_up, w_gate_up)


def _ffn_act_bwd(dy, w_down, gate, up, rows, bm):
    def body(dy_ref, wd_ref, gate_ref, up_ref, o_ref):
        dact = _dot(dy_ref[...], wd_ref[...], 1, 1)
        g, u = gate_ref[...].astype(F32), up_ref[...].astype(F32)
        sg = _sigmoid(g)
        o_ref[0] = (dact * u * (sg * (1.0 + g * (1.0 - sg)))).astype(BF16)
        o_ref[1] = (dact * (g * sg)).astype(BF16)

    blk = pl.BlockSpec((bm, FF_BN), lambda i, j: (i, j))
    return pl.pallas_call(
        body, name="ffn_act_bwd", grid=(rows // bm, FF_NB),
        in_specs=[pl.BlockSpec((bm, D_MODEL), lambda i, j: (i, 0)),
                  pl.BlockSpec((FF_BN, D_MODEL), lambda i, j: (j, 0)), blk, blk],
        out_specs=pl.BlockSpec((2, bm, FF_BN), lambda i, j: (0, i, j)),
        out_shape=jax.ShapeDtypeStruct((2, rows, D_FF), BF16),
        compiler_params=_cparams("parallel", "arbitrary"),
    )(dy, w_down, gate, up)


def _pack_rows(a):
    flat = a.reshape(-1)
    pad = (-flat.shape[0]) % 1024
    if pad:
        flat = jnp.concatenate([flat, jnp.zeros((pad,), flat.dtype)])
    return flat.reshape(-1, 1024)


def _pack_owner_rows(a8):
    flat = a8.reshape(N_DEV, -1)
    pad = (-flat.shape[1]) % 1024
    if pad:
        flat = jnp.concatenate([flat, jnp.zeros((N_DEV, pad), flat.dtype)], axis=1)
    return flat.reshape(N_DEV, -1, 1024)


class _Group:
    def __init__(self, names, extra=0, block=None):
        self.names, self.extra, self.offs = names, extra, {}
        o = 0
        for n in names:
            self.offs[n] = o
            o += SHARD_ROWS[n]
        self.shard_rows = o
        self.rows = -(-(o + extra) // 16) * 16
        self.block = block

    def _fill(self, parts, axis, dtype):
        used = sum(p.shape[axis] for p in parts)
        if used < self.rows:
            shape = list(parts[0].shape)
            shape[axis] = self.rows - used
            parts = parts + [jnp.zeros(shape, dtype)]
        return jnp.concatenate(parts, axis=axis)

    def pack(self, shards, extra_rows, dtype):
        parts = [_pack_rows(shards[n].astype(dtype)) for n in self.names] + [r.astype(dtype) for r in extra_rows]
        return self._fill(parts, 0, dtype)

    def pack_for_owners(self, full, extra_rows):
        parts = [_pack_owner_rows(_split_for_owners(n, full[n])) for n in self.names]
        if extra_rows:
            parts.append(jnp.broadcast_to(jnp.concatenate(extra_rows, axis=0)[None], (N_DEV, len(extra_rows), 1024)))
        return self._fill(parts, 1, F32)

    def unpack(self, pack, name):
        r, c = SHARD_SHAPE[name]
        seg = pack[self.offs[name]:self.offs[name] + SHARD_ROWS[name]].reshape(-1)[:r * c]
        return seg.reshape(1, r, c)

    def gathered(self, gathered, name):
        r, c = SHARD_SHAPE[name]
        seg = gathered[:, self.offs[name]:self.offs[name] + SHARD_ROWS[name]]
        seg = seg.reshape(N_DEV, -1)[:, :r * c].reshape(N_DEV, r, c)
        if name in ROW_SHARDED:
            return seg.reshape(N_DEV * r, c)
        return seg.transpose(1, 0, 2).reshape(r, N_DEV * c)


ROW_SHARDED = ("w_out", "w_mq", "w_mo", "w_down")
N_EXTRA = len(REPL) + 1
AG_FIRST = _Group(("w_in", "w_gk_up"))
AG_MID = _Group(("w_out", "w_mq", "w_mkv"))
AG_LATE = _Group(("w_mo", "w_gate_up", "w_down"))
RS_FFN = _Group(("w_gate_up", "w_down"), block=264)
RS_MID = _Group(("w_out", "w_mq", "w_mkv", "w_mo"), block=320)
RS_WIN = _Group(("w_in", "w_gk_up"), block=200)
RS_TAIL = _Group((), extra=N_EXTRA, block=16)


def _split_for_owners(name, full):
    r, c = SHARD_SHAPE[name]
    if name in ROW_SHARDED:
        return full.reshape(N_DEV, r, c)
    return full.reshape(r, N_DEV, c).transpose(1, 0, 2)


def _repl_row(a):
    flat = a.reshape(-1)
    return jnp.concatenate([flat, jnp.zeros((1024 - flat.shape[0],), flat.dtype)]).reshape(1, 1024)


def _local_step(x, mem, tgt, wf, rp, shards=None, scatter=False):
    rows = x.shape[0]
    bm = min(512, rows)
    bmx = min(256, rows)
    mt = min(MM_TILE, rows)
    kt = min(512, rows)
    w_cat, wgk_pad = wf["w_cat"], wf["wgk_pad"]
    wg_t = jnp.tile(rp["gla_norm_w"], (1, GLA_HEADS))
    ws_t = jnp.tile(rp["sb_norm_w"], (1, SB_HEADS))
    lane = jnp.arange(GLA_V_W)
    grp_g = (lane[:, None] // GLA_DV == lane[None, :] // GLA_DV).astype(BF16)
    grp_s = (lane[:, None] // SB_DH == lane[None, :] // SB_DH).astype(BF16)

    h1 = _rms_fwd("mix_norm_fwd", x, rp["mix_norm_w"], bm)
    proj = _matmul("in_proj", h1, w_cat, "nn", rows, PROJ_W, D_MODEL, F32, mt, 640, D_MODEL)
    gk = _gate_fwd(proj, wgk_pad, rp["b_gk"], bm)
    if shards is None:
        o_g, s_all = _gla_fwd(proj, gk, rows)
        (o_s,) = _sb_fwd(proj, rows)
    else:
        o_g, s_all, got_mid = _gla_fwd(proj, gk, rows, _Exchange(AG_MID.pack(shards, [], BF16), scatter=False))
        o_s, got_late = _sb_fwd(proj, rows, _Exchange(AG_LATE.pack(shards, [], BF16), scatter=False))
        wf = {**wf, **{n: AG_MID.gathered(got_mid, n) for n in AG_MID.names},
              **{n: AG_LATE.gathered(got_late, n) for n in AG_LATE.names}}
    cat = _mix_out_fwd(o_g, proj, o_s, wg_t, ws_t, grp_g, grp_s, bm)
    x1 = _matmul("out_proj", cat, wf["w_out"], "nn", rows, D_MODEL, D_MODEL, F32, mt, MM_TILE, D_MODEL, residual=x)
    h2 = _rms_fwd("xattn_norm_fwd", x1, rp["xattn_norm_w"], bm)
    qm = _matmul("mq_proj", h2, wf["w_mq"], "nn", rows, D_MODEL, D_MODEL, F32, mt, MM_TILE, D_MODEL)
    mem_n, kpre, kn, v_m = _mem_kv_fwd(mem, rp["mem_norm_w"], wf["w_mkv"], rp["mk_norm_w"])
    o_m = _xattn_fwd(qm, kn, v_m, rp["mq_norm_w"], bmx)
    x2 = _matmul("mo_proj", o_m, wf["w_mo"], "nn", rows, D_MODEL, D_MODEL, F32, mt, MM_TILE, D_MODEL, residual=x1)
    h3 = _rms_fwd("ffn_norm_fwd", x2, rp["ffn_norm_w"], bm)
    gate, up, act = _ffn_up(h3, wf["w_gate_up"], rows, bm)
    y = _matmul("ffn_down", act, wf["w_down"], "nn", rows, D_MODEL, D_FF, F32, mt, MM_TILE, FF_BN, residual=x2)
    dy, sq = _loss_kernel(y, tgt, bm)

    g = {}
    dgu = _ffn_act_bwd(dy, wf["w_down"], gate, up, rows, bm)
    g["w_down"] = _matmul("grad_w_down", act, dy, "tn", D_FF, D_MODEL, rows, F32, FF_BN, MM_TILE, kt)
    nkb = FF_NB
    dh3 = _matmul("ffn_up_bwd", dgu, wf["w_gate_up"], "nt", rows, D_MODEL, 2 * D_FF, F32, mt, MM_TILE, FF_BN,
                  a_spec=pl.BlockSpec((None, mt, FF_BN), lambda i, j, kk: (kk // nkb, i, kk % nkb)))
    g["w_gate_up"] = _matmul(
        "grad_w_gate_up", h3, dgu, "tn", D_MODEL, 2 * D_FF, rows, F32, MM_TILE, FF_BN, kt,
        b_spec=pl.BlockSpec((None, kt, FF_BN), lambda i, j, kk: (j // nkb, kk, j % nkb)))
    dx2, g["ffn_norm_w"] = _rms_bwd("ffn_norm_bwd", x2, rp["ffn_norm_w"], dh3, dy, bm)

    do_m = _matmul("mo_proj_bwd", dx2, wf["w_mo"], "nt", rows, D_MODEL, D_MODEL, BF16, mt, MM_TILE, D_MODEL)
    g["w_mo"] = _matmul("grad_w_mo", o_m, dx2, "tn", D_MODEL, D_MODEL, rows, F32, MM_TILE, MM_TILE, kt)
    dqm, dkn, dv_m, g["mq_norm_w"] = _xattn_bwd(qm, kn, v_m, rp["mq_norm_w"], do_m, bmx)
    g["w_mkv"], g["mk_norm_w"], g["mem_norm_w"] = _mem_kv_bwd(
        mem, rp["mem_norm_w"], wf["w_mkv"], rp["mk_norm_w"], mem_n, kpre, dkn, dv_m)
    dh2 = _matmul("mq_proj_bwd", dqm, wf["w_mq"], "nt", rows, D_MODEL, D_MODEL, F32, mt, MM_TILE, D_MODEL)
    g["w_mq"] = _matmul("grad_w_mq", h2, dqm, "tn", D_MODEL, D_MODEL, rows, F32, MM_TILE, MM_TILE, kt)
    dx1, g["xattn_norm_w"] = _rms_bwd("xattn_norm_bwd", x1, rp["xattn_norm_w"], dh2, dx2, bm)

    dcat = _matmul("out_proj_bwd", dx1, wf["w_out"], "nt", rows, D_MODEL, D_MODEL, F32, mt, MM_TILE, D_MODEL)
    g["w_out"] = _matmul("grad_w_out", cat, dx1, "tn", D_MODEL, D_MODEL, rows, F32, MM_TILE, MM_TILE, kt)
    do_g, dgg, do_s, dwg, dws = _mix_out_bwd(dcat, o_g, proj, o_s, wg_t, ws_t, grp_g, grp_s, bm)
    g["gla_norm_w"] = dwg.reshape(GLA_HEADS, GLA_DV).sum(axis=0, keepdims=True)
    g["sb_norm_w"] = dws.reshape(SB_HEADS, SB_DH).sum(axis=0, keepdims=True)
    recv = {}
    if scatter:
        dq_s, dk_s, dv_s, recv["ffn"] = _sb_bwd(
            proj, o_s, do_s, rows, _Exchange(RS_FFN.pack_for_owners(g, []), scatter=True))
        dq_g, dk_g, dv_g, dgk, recv["mid"] = _gla_bwd(
            proj, gk, do_g, s_all, rows, _Exchange(RS_MID.pack_for_owners(g, []), scatter=True))
    else:
        dq_s, dk_s, dv_s = _sb_bwd(proj, o_s, do_s, rows)
        dq_g, dk_g, dv_g, dgk = _gla_bwd(proj, gk, do_g, s_all, rows)
    dlr, dwgk, g["b_gk"] = _gate_bwd(proj, wgk_pad, rp["b_gk"], dgk, bm)
    g["w_gk_up"] = dwgk[:GATE_RANK]
    dproj = jnp.concatenate([dq_g.astype(BF16), dk_g.astype(BF16), dv_g.astype(BF16), dgg, dq_s.astype(BF16),
                             dk_s.astype(BF16), dv_s.astype(BF16), dlr], axis=1)
    dw_cat = _matmul("grad_w_in", h1, dproj, "tn", D_MODEL, PROJ_W, rows, F32, MM_TILE, 640, kt)
    g["w_in"] = jnp.concatenate([dw_cat[:, :C_QS], dw_cat[:, C_LR:C_LR + GATE_RANK], dw_cat[:, C_QS:C_LR]], axis=1)
    if scatter:
        dh1, recv["win"] = _matmul("in_proj_bwd", dproj, w_cat, "nt", rows, D_MODEL, PROJ_W, F32, mt, MM_TILE, 640,
                                   exchange=_Exchange(RS_WIN.pack_for_owners(g, []), scatter=True))
    else:
        dh1 = _matmul("in_proj_bwd", dproj, w_cat, "nt", rows, D_MODEL, PROJ_W, F32, mt, MM_TILE, 640)
    dx, g["mix_norm_w"] = _rms_bwd("mix_norm_bwd", x, rp["mix_norm_w"], dh1, dx1, bm)
    return sq, dx, g, recv


def _first_weights(gathered):
    w_in = AG_FIRST.gathered(gathered, "w_in")
    lr_end = C_QS + GATE_RANK
    w_cat = jnp.concatenate([w_in[:, :C_QS], w_in[:, lr_end:], w_in[:, C_QS:lr_end],
                             jnp.zeros((D_MODEL, PROJ_W - D_IN), BF16)], axis=1)
    wgk = AG_FIRST.gathered(gathered, "w_gk_up")
    return {"w_cat": w_cat, "wgk_pad": jnp.concatenate([wgk, jnp.zeros((LANES - GATE_RANK, GLA_QK_W), BF16)], axis=0)}


def kernel(x, mem, mix_norm_w, w_in, w_gk_up, b_gk, gla_norm_w, sb_norm_w, w_out, xattn_norm_w, mem_norm_w, w_mq, w_mkv, mq_norm_w, mk_norm_w, w_mo, ffn_norm_w, w_gate_up, w_down, loss_target, m_mix_norm_w, m_w_in, m_w_gk_up, m_b_gk, m_gla_norm_w, m_sb_norm_w, m_w_out, m_xattn_norm_w, m_mem_norm_w, m_w_mq, m_w_mkv, m_mq_norm_w, m_mk_norm_w, m_w_mo, m_ffn_norm_w, m_w_gate_up, m_w_down, v_mix_norm_w, v_w_in, v_w_gk_up, v_b_gk, v_gla_norm_w, v_sb_norm_w, v_w_out, v_xattn_norm_w, v_mem_norm_w, v_w_mq, v_w_mkv, v_mq_norm_w, v_mk_norm_w, v_w_mo, v_ffn_norm_w, v_w_gate_up, v_w_down):
    given = dict(locals())
    w = {n: given[n][0] for n in WEIGHTS}
    m = {n: given["m_" + n][0] for n in WEIGHTS}
    v = {n: given["v_" + n][0] for n in WEIGHTS}

    wf = _first_weights(_exchange_call("gather_first_weights", AG_FIRST.pack(w, [], BF16), scatter=False))
    rp = {n: w[n].reshape(1, -1) for n in REPL}
    sq, dx, g, recv = _local_step(x[0], mem[0], loss_target[0], wf, rp, shards=w, scatter=True)

    loss_row = _repl_row(jnp.sum(sq).reshape(1) * (0.5 / D_MODEL))
    repl_rows = [_repl_row(g[n]) for n in REPL] + [loss_row]
    recv["tail"] = _exchange_call("scatter_tail_gradients", RS_TAIL.pack_for_owners(g, repl_rows), scatter=True)

    zero_row = jnp.zeros((1, 1024), F32)
    groups = (("ffn", RS_FFN), ("mid", RS_MID), ("win", RS_WIN), ("tail", RS_TAIL))
    out_packs = {}
    for key, grp in groups:
        extra = lambda t: [_repl_row(t[n]) for n in REPL] + [zero_row] if grp.extra else []
        out_packs[key] = _adamw("sum_adamw_" + key, recv[key], *[grp.pack(t, extra(t), F32) for t in (w, m, v)],
                                block=grp.block)

    def unpack(kind, name):
        for key, grp in groups:
            if name in grp.names:
                return grp.unpack(out_packs[key][kind], name)
        row = out_packs["tail"][kind][REPL.index(name)]
        return row[:w[name].shape[-1]].reshape(1, -1)

    loss = out_packs["tail"][0][len(REPL), 0]
    outs = [loss, dx[None]]
    for kind in range(4):
        outs += [unpack(kind, n) for n in WEIGHTS]
    return tuple(outs)
```

```python
import functools
import math

import jax
import jax.numpy as jnp
from jax import lax
from jax.experimental import pallas as pl
from jax.experimental.pallas import tpu as pltpu

F32 = jnp.float32
BF16 = jnp.bfloat16

N_DEV = 8
D_MODEL = 1024
GLA_HEADS = 4
GLA_DK = 64
GLA_DV = 128
GLA_CHUNK = 64
GLA_STEP_CHUNKS = 4
GLA_QK_W = GLA_HEADS * GLA_DK
GLA_V_W = GLA_HEADS * GLA_DV
GATE_RANK = 16
SB_HEADS = 8
SB_DH = 64
SB_W = SB_HEADS * SB_DH
SB_BLK = 128
SB_QT = 1024
SB_DEAD = -104.0
MEM_LEN = 256
MEM_HEADS = 4
MEM_DH = 256
D_FF = 2816
D_IN = 3088
RMS_EPS = 1e-6
LANES = 128

PROJ_W = 3200
C_QG, C_KG, C_VG, C_GG, C_QS, C_KS, C_VS, C_LR = 0, 256, 512, 1024, 1536, 2048, 2560, 3072

ADAM_LR, ADAM_B1, ADAM_B2, ADAM_EPS, ADAM_WD, ADAM_STEP = 0.001, 0.9, 0.999, 1e-08, 0.01, 10

SHARD_SHAPE = {"w_in": (1024, 386), "w_out": (128, 1024), "w_mq": (128, 1024), "w_mkv": (1024, 256),
               "w_mo": (128, 1024), "w_gate_up": (1024, 704), "w_down": (352, 1024), "w_gk_up": (16, 32)}
SHARD_ROWS = {n: -(-(s[0] * s[1]) // 1024) for n, s in SHARD_SHAPE.items()}
REPL = ("mix_norm_w", "b_gk", "gla_norm_w", "sb_norm_w", "xattn_norm_w", "mem_norm_w", "mq_norm_w",
        "mk_norm_w", "ffn_norm_w")
WEIGHTS = ("mix_norm_w", "w_in", "w_gk_up", "b_gk", "gla_norm_w", "sb_norm_w", "w_out", "xattn_norm_w",
           "mem_norm_w", "w_mq", "w_mkv", "mq_norm_w", "mk_norm_w", "w_mo", "ffn_norm_w", "w_gate_up", "w_down")
VMEM_LIMIT = 56 * 1024 * 1024
MM_TILE = 1024


def _cparams(*sem):
    return pltpu.CompilerParams(dimension_semantics=sem if sem else None, vmem_limit_bytes=VMEM_LIMIT)


def _dot(a, b, ca=1, cb=0):
    return lax.dot_general(a.astype(BF16), b.astype(BF16), (((ca,), (cb,)), ((), ())),
                           preferred_element_type=F32)


def _split(x, parts):
    out = []
    for _ in range(parts - 1):
        hi = x.astype(BF16)
        out.append(hi)
        x = x - hi.astype(F32)
    out.append(x.astype(BF16))
    return out


def _dot_lhs_exact(x, m, ca=1, cb=0, parts=3):
    acc = None
    for p in _split(x, parts):
        t = _dot(p, m, ca, cb)
        acc = t if acc is None else acc + t
    return acc


def _dot_rhs_exact(m, x, ca=1, cb=0, parts=3):
    acc = None
    for p in _split(x, parts):
        t = _dot(m, p, ca, cb)
        acc = t if acc is None else acc + t
    return acc


def _dot3(a, b, ca=1, cb=0):
    a_hi, a_lo = _split(a, 2)
    b_hi, b_lo = _split(b, 2)
    return _dot(a_hi, b_hi, ca, cb) + (_dot(a_hi, b_lo, ca, cb) + _dot(a_lo, b_hi, ca, cb))


def _log_sigmoid(z):
    return jnp.minimum(z, 0.0) - jnp.log(1.0 + jnp.exp(-jnp.abs(z)))


def _sigmoid(z):
    e = jnp.exp(-jnp.abs(z))
    return jnp.where(z >= 0, 1.0, e) / (1.0 + e)


def _iota2(shape, dim):
    return lax.broadcasted_iota(jnp.int32, shape, dim)


def _rowcall(name, fn, row_ins, full_ins, row_outs, acc_outs, bm, rows):
    n_in = len(row_ins) + len(full_ins)
    n_row = len(row_outs)

    def body(*refs):
        ins, outs = refs[:n_in], refs[n_in:]
        res = fn(*[r[...] for r in ins])
        for r, v in zip(outs[:n_row], res[:n_row]):
            r[...] = v.astype(r.dtype)
        first = pl.program_id(0) == 0
        for r, v in zip(outs[n_row:], res[n_row:]):
            def init(r=r):
                r[...] = jnp.zeros(r.shape, r.dtype)
            pl.when(first)(init)
            r[...] += v

    in_specs = [pl.BlockSpec((bm, w), functools.partial(lambda i, c: (i, c), c=c)) for _, w, c in row_ins]
    in_specs += [pl.BlockSpec(a.shape, lambda i: (0, 0)) for a in full_ins]
    out_specs = [pl.BlockSpec((bm, w), lambda i: (i, 0)) for w, _ in row_outs]
    out_specs += [pl.BlockSpec(s, lambda i: (0, 0)) for s in acc_outs]
    out_shape = [jax.ShapeDtypeStruct((rows, w), dt) for w, dt in row_outs]
    out_shape += [jax.ShapeDtypeStruct(s, F32) for s in acc_outs]
    return pl.pallas_call(
        body, name=name, grid=(rows // bm,), in_specs=in_specs, out_specs=out_specs, out_shape=out_shape,
        compiler_params=_cparams("arbitrary"),
    )(*[a for a, _, _ in row_ins], *full_ins)


def _matmul(name, a, b, mode, m, n, k, out_dtype, bm, bn, bk, residual=None, a_spec=None, b_spec=None,
            exchange=None):
    bm, bn, bk = min(bm, m), min(bn, n), min(bk, k)
    nk = k // bk
    ca, cb = {"nn": (1, 0), "nt": (1, 1), "tn": (0, 0)}[mode]
    if a_spec is None:
        a_spec = (pl.BlockSpec((bk, bm), lambda i, j, kk: (kk, i)) if mode == "tn"
                  else pl.BlockSpec((bm, bk), lambda i, j, kk: (i, kk)))
    if b_spec is None:
        b_spec = (pl.BlockSpec((bn, bk), lambda i, j, kk: (j, kk)) if mode == "nt"
                  else pl.BlockSpec((bk, bn), lambda i, j, kk: (kk, j)))
    has_res = residual is not None

    def body(*refs):
        a_ref, b_ref = refs[0], refs[1]
        res_ref = refs[2] if has_res else None
        o_ref = refs[2 + has_res]
        part = _dot(a_ref[...], b_ref[...], ca, cb)

        def finish(total):
            if has_res:
                total = total + res_ref[...]
            o_ref[...] = total.astype(o_ref.dtype)

        if nk == 1:
            finish(part)
        else:
            acc_ref = refs[3 + has_res]
            kk = pl.program_id(2)

            @pl.when(kk == 0)
            def _():
                acc_ref[...] = part

            @pl.when(kk > 0)
            def _():
                acc_ref[...] += part

            @pl.when(kk == nk - 1)
            def _():
                finish(acc_ref[...])

    in_specs = [a_spec, b_spec]
    args = [a, b]
    if has_res:
        in_specs.append(pl.BlockSpec((bm, bn), lambda i, j, kk: (i, j)))
        args.append(residual)
    sem = ("parallel", "parallel", "arbitrary") if exchange is None else ("arbitrary",) * 3
    res = _hosted_call(
        body, exchange, grid=(m // bm, n // bn, nk), in_specs=in_specs,
        out_specs=[pl.BlockSpec((bm, bn), lambda i, j, kk: (i, j))],
        out_shape=[jax.ShapeDtypeStruct((m, n), out_dtype)],
        scratch_shapes=[pltpu.VMEM((bm, bn), F32)] if nk > 1 else [],
        args=args, name=name, compiler_params=_cparams(*sem))
    return res[0] if exchange is None else res


def _peer(mask):
    x, y, c = lax.axis_index("x"), lax.axis_index("y"), lax.axis_index("c")
    mx, my, mc = (mask >> 2) & 1, (mask >> 1) & 1, mask & 1
    px, py, pc = (1 - x if mx else x), (1 - y if my else y), (1 - c if mc else c)
    return (px, py, pc), 4 * px + 2 * py + pc


def _my_index():
    return 4 * lax.axis_index("x") + 2 * lax.axis_index("y") + lax.axis_index("c")


class _Exchange:
    def __init__(self, src, scatter):
        self.src, self.scatter = src, scatter
        self.in_spec = pl.BlockSpec(memory_space=pl.ANY)
        self.out_spec = pl.BlockSpec(memory_space=pl.ANY)
        self.out_shape = jax.ShapeDtypeStruct((N_DEV, src.shape[-2], 1024), src.dtype)
        self.scratch = [pltpu.SemaphoreType.DMA((N_DEV - 1,)), pltpu.SemaphoreType.DMA((N_DEV - 1,)),
                        pltpu.SemaphoreType.DMA(())]

    def _copies(self, src_ref, out_ref, sems):
        send_sems, recv_sems, local_sem = sems
        me = _my_index()
        copies = [pltpu.make_async_copy(src_ref.at[me] if self.scatter else src_ref, out_ref.at[me], local_sem)]
        for mask in range(1, N_DEV):
            peer, peer_index = _peer(mask)
            copies.append(pltpu.make_async_remote_copy(
                src_ref=src_ref.at[peer_index] if self.scatter else src_ref, dst_ref=out_ref.at[me],
                send_sem=send_sems.at[mask - 1], recv_sem=recv_sems.at[mask - 1],
                device_id=peer, device_id_type=pl.DeviceIdType.MESH))
        return copies

    def start(self, src_ref, out_ref, sems):
        for cp in self._copies(src_ref, out_ref, sems):
            cp.start()

    def wait(self, src_ref, out_ref, sems):
        for cp in self._copies(src_ref, out_ref, sems):
            cp.wait()


def _hosted_call(body, ex, grid, in_specs, out_specs, out_shape, scratch_shapes, args, **kw):
    if ex is None:
        return pl.pallas_call(body, grid=grid, in_specs=in_specs, out_specs=out_specs, out_shape=out_shape,
                              scratch_shapes=scratch_shapes, **kw)(*args)
    n_in, n_out, n_scr = len(in_specs), len(out_specs), len(scratch_shapes)

    def hosted(*refs):
        ins, src_ref = refs[:n_in], refs[n_in]
        outs, out_ref = refs[n_in + 1:n_in + 1 + n_out], refs[n_in + 1 + n_out]
        scr, sems = refs[n_in + 2 + n_out:n_in + 2 + n_out + n_scr], refs[n_in + 2 + n_out + n_scr:]
        ids = [pl.program_id(a) for a in range(len(grid))]
        first = functools.reduce(jnp.logical_and, [p == 0 for p in ids])
        last = functools.reduce(jnp.logical_and, [p == n - 1 for p, n in zip(ids, grid)])

        @pl.when(first)
        def _():
            ex.start(src_ref, out_ref, sems)

        body(*ins, *outs, *scr)

        @pl.when(last)
        def _():
            ex.wait(src_ref, out_ref, sems)

    res = pl.pallas_call(
        hosted, grid=grid, in_specs=list(in_specs) + [ex.in_spec], out_specs=list(out_specs) + [ex.out_spec],
        out_shape=list(out_shape) + [ex.out_shape], scratch_shapes=list(scratch_shapes) + ex.scratch, **kw,
    )(*args, ex.src)
    return res


def _exchange_call(name, src, scatter):
    ex = _Exchange(src, scatter)

    def body(src_ref, out_ref, *sems):
        ex.start(src_ref, out_ref, sems)
        ex.wait(src_ref, out_ref, sems)

    return pl.pallas_call(
        body, name=name, in_specs=[ex.in_spec], out_specs=ex.out_spec, out_shape=ex.out_shape,
        scratch_shapes=ex.scratch, compiler_params=pltpu.CompilerParams(has_side_effects=True),
    )(src)


def _adamw(name, recv, w, m, v, block):
    rows = w.shape[0]
    c1 = 1.0 - ADAM_B1 ** ADAM_STEP
    c2 = 1.0 - ADAM_B2 ** ADAM_STEP

    def body(r_ref, w_ref, m_ref, v_ref, g_out, d_out, m_out, v_out):
        g = r_ref[0].astype(F32)
        for s in range(1, N_DEV):
            g = g + r_ref[s].astype(F32)
        m_new = ADAM_B1 * m_ref[...] + (1.0 - ADAM_B1) * g
        v_new = ADAM_B2 * v_ref[...] + (1.0 - ADAM_B2) * (g * g)
        m_hat = m_new / c1
        v_hat = v_new / c2
        g_out[...] = g
        d_out[...] = -ADAM_LR * (m_hat / (jnp.sqrt(v_hat) + ADAM_EPS) + ADAM_WD * w_ref[...])
        m_out[...] = m_new
        v_out[...] = v_new

    blk = pl.BlockSpec((block, 1024), lambda i: (i, 0))
    return pl.pallas_call(
        body, name=name, grid=(rows // block,),
        in_specs=[pl.BlockSpec((N_DEV, block, 1024), lambda i: (0, i, 0)), blk, blk, blk],
        out_specs=[blk] * 4, out_shape=[jax.ShapeDtypeStruct((rows, 1024), F32)] * 4,
        compiler_params=_cparams("parallel"),
    )(recv, w, m, v)


def _rms_fwd(name, x, w, bm):
    def fn(xb, wb):
        r = lax.rsqrt(jnp.mean(xb * xb, axis=-1, keepdims=True) + RMS_EPS)
        return (xb * r * wb,)
    return _rowcall(name, fn, [(x, D_MODEL, 0)], [w], [(D_MODEL, BF16)], [], bm, x.shape[0])[0]


def _rms_bwd(name, x, w, dh, dres, bm):
    def fn(xb, dhb, drb, wb):
        r = lax.rsqrt(jnp.mean(xb * xb, axis=-1, keepdims=True) + RMS_EPS)
        xh = xb * r
        dxh = dhb.astype(F32) * wb
        dx = drb + r * (dxh - xh * jnp.mean(dxh * xh, axis=-1, keepdims=True))
        return dx, jnp.sum(dhb.astype(F32) * xh, axis=0, keepdims=True)
    return _rowcall(name, fn, [(x, D_MODEL, 0), (dh, D_MODEL, 0), (dres, D_MODEL, 0)], [w],
                    [(D_MODEL, F32)], [(1, D_MODEL)], bm, x.shape[0])


def _gate_fwd(proj, wgk_pad, b_gk, bm):
    def fn(lr, wg, bg):
        z = _dot(lr, wg) + bg
        return (_log_sigmoid(z) * (1.0 / 16.0),)
    return _rowcall("gla_gate_fwd", fn, [(proj, LANES, C_LR // LANES)], [wgk_pad, b_gk],
                    [(GLA_QK_W, F32)], [], bm, proj.shape[0])[0]


def _gate_bwd(proj, wgk_pad, b_gk, dgk, bm):
    def fn(lr, dg, wg, bg):
        z = _dot(lr, wg) + bg
        dz = dg * _sigmoid(-z) * (1.0 / 16.0)
        return _dot(dz, wg, 1, 1), _dot(lr, dz, 0, 0), jnp.sum(dz, axis=0, keepdims=True)
    return _rowcall("gla_gate_bwd", fn, [(proj, LANES, C_LR // LANES), (dgk, GLA_QK_W, 0)], [wgk_pad, b_gk],
                    [(LANES, BF16)], [(LANES, GLA_QK_W), (1, GLA_QK_W)], bm, proj.shape[0])


def _group_mean(x, g, size):
    return _dot_lhs_exact(x, g, parts=2) * (1.0 / size)


def _mix_out_fwd(o_g, proj, o_s, wg_t, ws_t, grp_g, grp_s, bm):
    def fn(og, gg, os_, wg, ws, gmat, smat):
        rg = lax.rsqrt(_group_mean(og * og, gmat, GLA_DV) + RMS_EPS)
        yg = og * rg * wg * (gg * _sigmoid(gg))
        rs = lax.rsqrt(_group_mean(os_ * os_, smat, SB_DH) + RMS_EPS)
        ys = os_ * rs * ws
        return (jnp.concatenate([yg, ys], axis=1),)
    return _rowcall("mix_out_fwd", fn, [(o_g, GLA_V_W, 0), (proj, GLA_V_W, C_GG // GLA_V_W), (o_s, SB_W, 0)],
                    [wg_t, ws_t, grp_g, grp_s], [(D_MODEL, BF16)], [], bm, o_g.shape[0])[0]


def _mix_out_bwd(dcat, o_g, proj, o_s, wg_t, ws_t, grp_g, grp_s, bm):
    def fn(dyg, dys, og, gg, os_, wg, ws, gmat, smat):
        dyg = dyg.astype(F32)
        dys = dys.astype(F32)
        rg = lax.rsqrt(_group_mean(og * og, gmat, GLA_DV) + RMS_EPS)
        xh = og * rg
        sg = _sigmoid(gg)
        silu = gg * sg
        dxh = dyg * wg * silu
        dgg = dyg * xh * wg * (sg * (1.0 + gg * (1.0 - sg)))
        dwg = jnp.sum(dyg * xh * silu, axis=0, keepdims=True)
        dog = rg * (dxh - xh * _group_mean(dxh * xh, gmat, GLA_DV))
        rs = lax.rsqrt(_group_mean(os_ * os_, smat, SB_DH) + RMS_EPS)
        xs = os_ * rs
        dxs = dys * ws
        dws = jnp.sum(dys * xs, axis=0, keepdims=True)
        dos = rs * (dxs - xs * _group_mean(dxs * xs, smat, SB_DH))
        return dog, dgg, dos, dwg, dws
    return _rowcall("mix_out_bwd", fn,
                    [(dcat, GLA_V_W, 0), (dcat, SB_W, 1), (o_g, GLA_V_W, 0), (proj, GLA_V_W, C_GG // GLA_V_W),
                     (o_s, SB_W, 0)],
                    [wg_t, ws_t, grp_g, grp_s], [(GLA_V_W, F32), (GLA_V_W, BF16), (SB_W, F32)],
                    [(1, GLA_V_W), (1, SB_W)], bm, o_g.shape[0])


def _loss_kernel(y, tgt, bm):
    def fn(yb, tb):
        err = yb - tb
        return err * (1.0 / D_MODEL), jnp.sum(err * err, axis=0, keepdims=True)
    return _rowcall("loss_head", fn, [(y, D_MODEL, 0), (tgt, D_MODEL, 0)], [], [(D_MODEL, F32)], [(1, D_MODEL)],
                    bm, y.shape[0])


def _sb_tri(inclusive):
    j, s = _iota2((2 * SB_BLK, 2 * SB_BLK), 0), _iota2((2 * SB_BLK, 2 * SB_BLK), 1)
    j = jnp.where(j >= SB_BLK, j - SB_BLK, j)
    keep = (j >= s) if inclusive else (j > s)
    return ((s >= SB_BLK) | keep).astype(BF16)


def _dot_hilo(x, m2):
    hi, lo = _split(x, 2)
    return _dot(jnp.concatenate([hi, lo], axis=1), m2)


def _sb_mask(n):
    return _iota2((n, SB_BLK), 1) < _iota2((n, SB_BLK), 0)


def _add_rows(full, part, row0, row1):
    pieces = [full[:row0]] if row0 else []
    pieces.append(full[row0:row1] + part)
    if row1 < full.shape[0]:
        pieces.append(full[row1:])
    return pieces[0] if len(pieces) == 1 else jnp.concatenate(pieces, axis=0)


def _sb_visit(tiles, carry, c_slots, qt, diag_start, n_left, left_start):
    for sub in reversed(range(qt // SB_BLK)):
        carry = tiles(diag_start(sub), carry, True, sub * SB_BLK, qt)

    def step(jj, cr):
        return tiles(left_start(jj), cr, False, 0, qt)

    return _sb_sweep(n_left, step, carry, c_slots)


def _sb_sweep(n_tiles, step, carry, c_slots):
    def alive(state):
        jj, carry = state
        c_max = jnp.max(functools.reduce(jnp.maximum, [carry[s] for s in c_slots]))
        return jnp.logical_and(jj < n_tiles, c_max > SB_DEAD)

    def body(state):
        jj, carry = state
        return jj + 1, step(jj, carry)

    return lax.while_loop(alive, body, (jnp.int32(0), carry))[1]


def _row_blocks(n):
    return [slice(r, r + SB_BLK) for r in range(0, n, SB_BLK)]


def _hilo(x):
    hi, lo = _split(x, 2)
    return jnp.concatenate([hi, lo], axis=1)


def _sb_tile(q, k, c, tri_excl, diag):
    blocks = _row_blocks(q.shape[0])
    strict = _sb_mask(SB_BLK) if diag else None
    z = _dot(q, k, 1, 1)
    lbs, pieces = [], []
    for r, rs in enumerate(blocks):
        lb = _log_sigmoid(z[rs])
        l1 = lb - z[rs]
        if diag and r == 0:
            l1 = jnp.where(strict, l1, 0.0)
        lbs.append(lb)
        pieces.append(_hilo(l1))
    sums = _dot(jnp.concatenate(pieces, axis=0), tri_excl)
    a = []
    for r, rs in enumerate(blocks):
        ar = jnp.exp(lbs[r] + sums[rs, :SB_BLK] + c[rs])
        if diag and r == 0:
            ar = jnp.where(strict, ar, 0.0)
        a.append(ar.astype(BF16))
    return lbs, a, sums[:, SB_BLK:]


def _sb_fwd(proj, rows, exchange=None):
    qt = min(SB_QT, rows)
    subs = qt // SB_BLK

    def body(q_ref, k_ref, v_ref, o_ref):
        i = pl.program_id(1)
        tri_excl = _sb_tri(False)
        heads = [slice(SB_DH * hh, SB_DH * (hh + 1)) for hh in range(2)]
        qs = [(q_ref[:, sl] * 0.125).astype(BF16) for sl in heads]

        def tiles(start, carry, diag, row0, row1):
            out = []
            for hh, sl in enumerate(heads):
                o, c = carry[2 * hh], carry[2 * hh + 1]
                k = k_ref[pl.ds(start, SB_BLK), sl].astype(BF16)
                v = v_ref[pl.ds(start, SB_BLK), sl].astype(BF16)
                _, a, dc = _sb_tile(qs[hh][row0:row1], k, c[row0:row1], tri_excl, diag)
                out += [_add_rows(o, _dot(jnp.concatenate(a, axis=0), v), row0, row1), _add_rows(c, dc, row0, row1)]
            return tuple(out)

        carry = (jnp.zeros((qt, SB_DH), F32), jnp.zeros((qt, SB_BLK), F32)) * 2
        carry = _sb_visit(tiles, carry, (1, 3), qt,
                          lambda sub: pl.multiple_of(i * qt + sub * SB_BLK, SB_BLK), i * subs,
                          lambda jj: pl.multiple_of((i * subs - 1 - jj) * SB_BLK, SB_BLK))
        for hh, sl in enumerate(heads):
            o_ref[:, sl] = carry[2 * hh]

    return _hosted_call(
        body, exchange, grid=(SB_HEADS // 2, rows // qt),
        in_specs=[pl.BlockSpec((qt, LANES), lambda h, i: (i, C_QS // LANES + h)),
                  pl.BlockSpec((rows, LANES), lambda h, i: (0, C_KS // LANES + h)),
                  pl.BlockSpec((rows, LANES), lambda h, i: (0, C_VS // LANES + h))],
        out_specs=[pl.BlockSpec((qt, LANES), lambda h, i: (i, h))],
        out_shape=[jax.ShapeDtypeStruct((rows, SB_W), F32)], scratch_shapes=[],
        args=(proj, proj, proj), name="sb_attention_fwd", compiler_params=_cparams("arbitrary", "arbitrary"))


def _sb_bwd(proj, o_s, do_s, rows, exchange=None):
    qt = min(SB_QT, rows)
    subs = qt // SB_BLK

    def body(q_ref, k_ref, v_ref, o_ref, do_ref, dq_ref, dk_ref, dv_ref):
        i = pl.program_id(1)

        @pl.when(i == 0)
        def _():
            dk_ref[...] = jnp.zeros(dk_ref.shape, F32)
            dv_ref[...] = jnp.zeros(dv_ref.shape, F32)

        tri_excl, tri_incl = _sb_tri(False), _sb_tri(True)
        heads = [slice(SB_DH * hh, SB_DH * (hh + 1)) for hh in range(2)]
        qs = [(q_ref[:, sl] * 0.125).astype(BF16) for sl in heads]
        dobs = [do_ref[:, sl].astype(BF16) for sl in heads]
        dsums = [jnp.broadcast_to(jnp.sum(dob.astype(F32) * o_ref[:, sl], axis=1, keepdims=True), (qt, SB_BLK))
                 for dob, sl in zip(dobs, heads)]

        def tiles(start, carry, diag, row0, row1):
            out = []
            strict = _sb_mask(SB_BLK) if diag else None
            for hh, sl in enumerate(heads):
                dq, c, cp = carry[3 * hh:3 * hh + 3]
                q, dob = qs[hh][row0:row1], dobs[hh][row0:row1]
                dsum, cpr = dsums[hh][row0:row1], cp[row0:row1]
                blocks = _row_blocks(q.shape[0])
                k = k_ref[pl.ds(start, SB_BLK), sl].astype(BF16)
                v = v_ref[pl.ds(start, SB_BLK), sl].astype(BF16)
                lbs, a, dc = _sb_tile(q, k, c[row0:row1], tri_excl, diag)
                da = _dot(dob, v, 1, 1)
                ps = [a[r].astype(F32) * da[rs] for r, rs in enumerate(blocks)]
                psums = _dot(jnp.concatenate([_hilo(p) for p in ps], axis=0), tri_incl)
                dzs = []
                for r, rs in enumerate(blocks):
                    left = dsum[rs] - (psums[rs, :SB_BLK] + cpr[rs])
                    dz = ps[r] - jnp.exp(lbs[r]) * (ps[r] + left)
                    if diag and r == 0:
                        dz = jnp.where(strict, dz, 0.0)
                    dzs.append(dz.astype(BF16))
                dzb, ab = jnp.concatenate(dzs, axis=0), jnp.concatenate(a, axis=0)
                dk_ref[pl.ds(start, SB_BLK), sl] += _dot(dzb, q, 0, 0)
                dv_ref[pl.ds(start, SB_BLK), sl] += _dot(ab, dob, 0, 0)
                out += [_add_rows(dq, _dot(dzb, k), row0, row1), _add_rows(c, dc, row0, row1),
                        _add_rows(cp, psums[:, SB_BLK:], row0, row1)]
            return tuple(out)

        zero = jnp.zeros((qt, SB_BLK), F32)
        carry = (jnp.zeros((qt, SB_DH), F32), zero, zero) * 2
        carry = _sb_visit(tiles, carry, (1, 4), qt,
                          lambda sub: pl.multiple_of(i * qt + sub * SB_BLK, SB_BLK), i * subs,
                          lambda jj: pl.multiple_of((i * subs - 1 - jj) * SB_BLK, SB_BLK))
        for hh, sl in enumerate(heads):
            dq_ref[:, sl] = carry[3 * hh] * 0.125

    whole = lambda base: pl.BlockSpec((rows, LANES), functools.partial(lambda h, i, b: (0, b + h), b=base))
    blk = lambda base: pl.BlockSpec((qt, LANES), functools.partial(lambda h, i, b: (i, b + h), b=base))
    return _hosted_call(
        body, exchange, grid=(SB_HEADS // 2, rows // qt),
        in_specs=[blk(C_QS // LANES), whole(C_KS // LANES), whole(C_VS // LANES), blk(0), blk(0)],
        out_specs=[blk(0), whole(0), whole(0)],
        out_shape=[jax.ShapeDtypeStruct((rows, SB_W), F32)] * 3, scratch_shapes=[],
        args=(proj, proj, proj, o_s, do_s), name="sb_attention_bwd",
        compiler_params=_cparams("arbitrary", "arbitrary"))


def _gla_chunk_common(g_all):
    r_i, c_i = _iota2((GLA_CHUNK, GLA_CHUNK), 0), _iota2((GLA_CHUNK, GLA_CHUNK), 1)
    tri = (c_i <= r_i).astype(BF16)
    return _dot_rhs_exact(tri, g_all), r_i, c_i


def _gla_scaled(qh, kh, bh):
    ref = bh[GLA_CHUNK // 2:GLA_CHUNK // 2 + 1, :]
    eq, ek = jnp.exp(bh - ref), jnp.exp(ref - bh)
    mask = _iota2((GLA_CHUNK, GLA_CHUNK), 1) <= _iota2((GLA_CHUNK, GLA_CHUNK), 0)
    return eq, ek, qh * eq, kh * ek, mask


def _gla_fwd(proj, gk, rows, exchange=None):
    n_chunks = rows // GLA_CHUNK
    step_rows = GLA_CHUNK * GLA_STEP_CHUNKS

    def body(q_ref, k_ref, v_ref, g_ref, o_ref, sall_ref, s_scr):
        @pl.when(pl.program_id(0) == 0)
        def _():
            s_scr[...] = jnp.zeros(s_scr.shape, F32)

        ones = jnp.ones((GLA_CHUNK, GLA_DV), BF16)
        states = [s_scr[h] for h in range(GLA_HEADS)]
        for ci in range(GLA_STEP_CHUNKS):
            rs = slice(GLA_CHUNK * ci, GLA_CHUNK * (ci + 1))
            g_all = g_ref[rs, :]
            b_all, _, _ = _gla_chunk_common(g_all)
            for h in range(GLA_HEADS):
                sl = slice(GLA_DK * h, GLA_DK * (h + 1))
                vs = slice(GLA_DV * h, GLA_DV * (h + 1))
                qh, kh, vh = q_ref[rs, sl] * 0.125, k_ref[rs, sl], v_ref[rs, vs]
                bh, gh = b_all[:, sl], g_all[:, sl]
                s = states[h]
                sall_ref[ci, h] = s
                _, _, qs, ks, mask = _gla_scaled(qh, kh, bh)
                a = jnp.where(mask, _dot(qs, ks, 1, 1), 0.0)
                o_ref[rs, vs] = _dot(qh * jnp.exp(bh), s) + _dot(a, vh)
                bl_col = _dot_lhs_exact(gh, ones, 0, 0)
                kd = kh * jnp.exp(bh[GLA_CHUNK - 1:GLA_CHUNK, :] - bh)
                states[h] = jnp.exp(bl_col) * s + _dot(kd, vh, 0, 0)
        for h in range(GLA_HEADS):
            s_scr[h] = states[h]

    c64 = lambda w, base: pl.BlockSpec((step_rows, w), functools.partial(lambda n, b: (n, b), b=base))
    return _hosted_call(
        body, exchange, grid=(rows // step_rows,),
        in_specs=[c64(GLA_QK_W, C_QG // GLA_QK_W), c64(GLA_QK_W, C_KG // GLA_QK_W), c64(GLA_V_W, C_VG // GLA_V_W),
                  c64(GLA_QK_W, 0)],
        out_specs=[c64(GLA_V_W, 0),
                   pl.BlockSpec((GLA_STEP_CHUNKS, GLA_HEADS, GLA_DK, GLA_DV), lambda n: (n, 0, 0, 0))],
        out_shape=[jax.ShapeDtypeStruct((rows, GLA_V_W), F32),
                   jax.ShapeDtypeStruct((n_chunks, GLA_HEADS, GLA_DK, GLA_DV), F32)],
        scratch_shapes=[pltpu.VMEM((GLA_HEADS, GLA_DK, GLA_DV), F32)],
        args=(proj, proj, proj, gk), name="gla_fwd", compiler_params=_cparams("arbitrary"))


def _gla_bwd(proj, gk, do_g, s_all, rows, exchange=None):
    n_chunks = rows // GLA_CHUNK

    def body(q_ref, k_ref, v_ref, g_ref, do_ref, sall_ref, dq_ref, dk_ref, dv_ref, dg_ref, ds_scr):
        @pl.when(pl.program_id(0) == 0)
        def _():
            ds_scr[...] = jnp.zeros(ds_scr.shape, F32)

        ones = jnp.ones((GLA_CHUNK, GLA_DV), BF16)
        ones8 = jnp.ones((8, GLA_DV), F32)
        last_row = _iota2((GLA_CHUNK, GLA_DK), 0) == GLA_CHUNK - 1
        dstates = [ds_scr[h] for h in range(GLA_HEADS)]
        for ci in reversed(range(GLA_STEP_CHUNKS)):
            cs = slice(GLA_CHUNK * ci, GLA_CHUNK * (ci + 1))
            g_all = g_ref[cs, :]
            b_all, r_i, c_i = _gla_chunk_common(g_all)
            triu = (c_i >= r_i).astype(BF16)
            for h in range(GLA_HEADS):
                sl = slice(GLA_DK * h, GLA_DK * (h + 1))
                vs = slice(GLA_DV * h, GLA_DV * (h + 1))
                qh, kh, vh = q_ref[cs, sl] * 0.125, k_ref[cs, sl], v_ref[cs, vs]
                bh, gh = b_all[:, sl], g_all[:, sl]
                doh = do_ref[cs, vs]
                s, ds = sall_ref[ci, h], dstates[h]
                eb = jnp.exp(bh)
                ekd = jnp.exp(bh[GLA_CHUNK - 1:GLA_CHUNK, :] - bh)
                ebl = jnp.exp(_dot_lhs_exact(gh, ones, 0, 0))
                qb, kd = qh * eb, kh * ekd
                dq = _dot(doh, s, 1, 1) * eb
                dk = _dot(vh, ds, 1, 1) * ekd
                dv = _dot(kd, ds)
                dbl = jnp.sum(dk * kh, axis=0, keepdims=True) + _dot3(ones8, ebl * s * ds, 1, 1)[0:1, :]
                eq, ek, qs, ks, mask = _gla_scaled(qh, kh, bh)
                a = jnp.where(mask, _dot(qs, ks, 1, 1), 0.0)
                da = jnp.where(mask, _dot(doh, vh, 1, 1), 0.0)
                dq = dq + _dot(da, ks) * eq
                dk = dk + _dot(da, qs, 0, 0) * ek
                dv = dv + _dot(a, doh, 0, 0)
                db = qh * dq - kh * dk + jnp.where(last_row, dbl, 0.0)
                dq_ref[cs, sl] = dq * 0.125
                dk_ref[cs, sl] = dk
                dv_ref[cs, vs] = dv
                dg_ref[cs, sl] = _dot_rhs_exact(triu, db)
                dstates[h] = _dot(qb, doh, 0, 0) + ebl * ds
        for h in range(GLA_HEADS):
            ds_scr[h] = dstates[h]

    step_rows = GLA_CHUNK * GLA_STEP_CHUNKS
    last = rows // step_rows - 1
    c64 = lambda w, base: pl.BlockSpec((step_rows, w), functools.partial(lambda n, b: (last - n, b), b=base))
    return _hosted_call(
        body, exchange, grid=(rows // step_rows,),
        in_specs=[c64(GLA_QK_W, C_QG // GLA_QK_W), c64(GLA_QK_W, C_KG // GLA_QK_W), c64(GLA_V_W, C_VG // GLA_V_W),
                  c64(GLA_QK_W, 0), c64(GLA_V_W, 0),
                  pl.BlockSpec((GLA_STEP_CHUNKS, GLA_HEADS, GLA_DK, GLA_DV), lambda n: (last - n, 0, 0, 0))],
        out_specs=[c64(GLA_QK_W, 0), c64(GLA_QK_W, 0), c64(GLA_V_W, 0), c64(GLA_QK_W, 0)],
        out_shape=[jax.ShapeDtypeStruct((rows, GLA_QK_W), F32), jax.ShapeDtypeStruct((rows, GLA_QK_W), F32),
                   jax.ShapeDtypeStruct((rows, GLA_V_W), F32), jax.ShapeDtypeStruct((rows, GLA_QK_W), F32)],
        scratch_shapes=[pltpu.VMEM((GLA_HEADS, GLA_DK, GLA_DV), F32)],
        args=(proj, proj, proj, gk, do_g, s_all), name="gla_bwd", compiler_params=_cparams("arbitrary"))


def _mem_kv_fwd(mem, mem_norm_w, w_mkv, mk_norm_w):
    def body(mem_ref, mw_ref, w_ref, kw_ref, memn_ref, kpre_ref, kn_ref, v_ref):
        xb = mem_ref[...]
        r = lax.rsqrt(jnp.mean(xb * xb, axis=-1, keepdims=True) + RMS_EPS)
        mem_n = (xb * r * mw_ref[...]).astype(BF16)
        memn_ref[...] = mem_n
        kv = _dot(mem_n, w_ref[...])
        kpre_ref[...] = kv[:, :D_MODEL]
        v_ref[...] = kv[:, D_MODEL:].astype(BF16)
        for h in range(MEM_HEADS):
            sl = slice(MEM_DH * h, MEM_DH * (h + 1))
            kh = kv[:, sl]
            rk = lax.rsqrt(jnp.mean(kh * kh, axis=-1, keepdims=True) + RMS_EPS)
            kn_ref[:, sl] = (kh * rk * kw_ref[...]).astype(BF16)

    return pl.pallas_call(
        body, name="mem_kv_fwd",
        out_shape=[jax.ShapeDtypeStruct((MEM_LEN, D_MODEL), BF16), jax.ShapeDtypeStruct((MEM_LEN, D_MODEL), F32),
                   jax.ShapeDtypeStruct((MEM_LEN, D_MODEL), BF16), jax.ShapeDtypeStruct((MEM_LEN, D_MODEL), BF16)],
        compiler_params=_cparams(),
    )(mem, mem_norm_w, w_mkv, mk_norm_w)


def _mem_kv_bwd(mem, mem_norm_w, w_mkv, mk_norm_w, mem_n, kpre, dkn, dv):
    def body(mem_ref, mw_ref, w_ref, kw_ref, memn_ref, kpre_ref, dkn_ref, dv_ref, dw_ref, dkw_ref, dmw_ref):
        dkw = jnp.zeros((1, MEM_DH), F32)
        dk_parts = []
        for h in range(MEM_HEADS):
            sl = slice(MEM_DH * h, MEM_DH * (h + 1))
            kh, dkh = kpre_ref[:, sl], dkn_ref[:, sl]
            rk = lax.rsqrt(jnp.mean(kh * kh, axis=-1, keepdims=True) + RMS_EPS)
            xh = kh * rk
            dxh = dkh * kw_ref[...]
            dkw = dkw + jnp.sum(dkh * xh, axis=0, keepdims=True)
            dk_parts.append(rk * (dxh - xh * jnp.mean(dxh * xh, axis=-1, keepdims=True)))
        dkw_ref[...] = dkw
        dkv = jnp.concatenate(dk_parts + [dv_ref[...]], axis=1).astype(BF16)
        dw_ref[...] = _dot(memn_ref[...], dkv, 0, 0)
        dmem_n = _dot(dkv, w_ref[...], 1, 1)
        xb = mem_ref[...]
        r = lax.rsqrt(jnp.mean(xb * xb, axis=-1, keepdims=True) + RMS_EPS)
        dmw_ref[...] = jnp.sum(dmem_n * (xb * r), axis=0, keepdims=True)

    return pl.pallas_call(
        body, name="mem_kv_bwd",
        out_shape=[jax.ShapeDtypeStruct((D_MODEL, 2 * D_MODEL), F32), jax.ShapeDtypeStruct((1, MEM_DH), F32),
                   jax.ShapeDtypeStruct((1, D_MODEL), F32)],
        compiler_params=_cparams(),
    )(mem, mem_norm_w, w_mkv, mk_norm_w, mem_n, kpre, dkn, dv)


def _xattn_head(qh, kn_h, qw):
    rq = lax.rsqrt(jnp.mean(qh * qh, axis=-1, keepdims=True) + RMS_EPS)
    xh = qh * rq
    qn = (xh * qw).astype(BF16)
    s = _dot(qn, kn_h, 1, 1) * (1.0 / 16.0)
    e = jnp.exp(s - jnp.max(s, axis=-1, keepdims=True))
    p = e / jnp.sum(e, axis=-1, keepdims=True)
    return rq, xh, qn, p


def _xattn_fwd(qm, kn, v, mq_norm_w, bm):
    def fn(qb, knb, vb, qw):
        outs = []
        for h in range(MEM_HEADS):
            sl = slice(MEM_DH * h, MEM_DH * (h + 1))
            _, _, _, p = _xattn_head(qb[:, sl], knb[:, sl], qw)
            outs.append(_dot(p, vb[:, sl]))
        return (jnp.concatenate(outs, axis=1),)
    return _rowcall("xattn_fwd", fn, [(qm, D_MODEL, 0)], [kn, v, mq_norm_w], [(D_MODEL, BF16)], [], bm,
                    qm.shape[0])[0]


def _xattn_bwd(qm, kn, v, mq_norm_w, do, bm):
    def fn(qb, dob, knb, vb, qw):
        dq_parts, dkn_parts, dv_parts = [], [], []
        dqw = jnp.zeros((1, MEM_DH), F32)
        for h in range(MEM_HEADS):
            sl = slice(MEM_DH * h, MEM_DH * (h + 1))
            rq, xh, qn, p = _xattn_head(qb[:, sl], knb[:, sl], qw)
            doh = dob[:, sl].astype(BF16)
            dp = _dot(doh, vb[:, sl], 1, 1)
            ds = (p * (dp - jnp.sum(dp * p, axis=-1, keepdims=True)) * (1.0 / 16.0)).astype(BF16)
            dqn = _dot(ds, knb[:, sl])
            dkn_parts.append(_dot(ds, qn, 0, 0))
            dv_parts.append(_dot(p, doh, 0, 0))
            dqw = dqw + jnp.sum(dqn * xh, axis=0, keepdims=True)
            dxh = dqn * qw
            dq_parts.append(rq * (dxh - xh * jnp.mean(dxh * xh, axis=-1, keepdims=True)))
        return (jnp.concatenate(dq_parts, axis=1), jnp.concatenate(dkn_parts, axis=1),
                jnp.concatenate(dv_parts, axis=1), dqw)
    return _rowcall("xattn_bwd", fn, [(qm, D_MODEL, 0), (do, D_MODEL, 0)], [kn, v, mq_norm_w],
                    [(D_MODEL, BF16)], [(MEM_LEN, D_MODEL), (MEM_LEN, D_MODEL), (1, MEM_DH)], bm, qm.shape[0])


FF_BN = 1408
FF_NB = D_FF // FF_BN


def _ffn_up(h3, w_gate_up, rows, bm):
    def body(h_ref, wg_ref, wu_ref, gate_ref, up_ref, act_ref):
        hb = h_ref[...]
        gate = _dot(hb, wg_ref[...])
        up = _dot(hb, wu_ref[...])
        gate_ref[...] = gate.astype(BF16)
        up_ref[...] = up.astype(BF16)
        act_ref[...] = (gate * _sigmoid(gate) * up).astype(BF16)

    out_blk = pl.BlockSpec((bm, FF_BN), lambda i, j: (i, j))
    return pl.pallas_call(
        body, name="ffn_up", grid=(rows // bm, FF_NB),
        in_specs=[pl.BlockSpec((bm, D_MODEL), lambda i, j: (i, 0)),
                  pl.BlockSpec((D_MODEL, FF_BN), lambda i, j: (0, j)),
                  pl.BlockSpec((D_MODEL, FF_BN), lambda i, j: (0, FF_NB + j))],
        out_specs=[out_blk, out_blk, out_blk],
        out_shape=[jax.ShapeDtypeStruct((rows, D_FF), BF16)] * 3,
        compiler_params=_cparams("parallel", "arbitrary"),
    )(h3, w_gate_up, w_gate_up)


def _ffn_act_bwd(dy, w_down, gate, up, rows, bm):
    def body(dy_ref, wd_ref, gate_ref, up_ref, o_ref):
        dact = _dot(dy_ref[...], wd_ref[...], 1, 1)
        g, u = gate_ref[...].astype(F32), up_ref[...].astype(F32)
        sg = _sigmoid(g)
        o_ref[0] = (dact * u * (sg * (1.0 + g * (1.0 - sg)))).astype(BF16)
        o_ref[1] = (dact * (g * sg)).astype(BF16)

    blk = pl.BlockSpec((bm, FF_BN), lambda i, j: (i, j))
    return pl.pallas_call(
        body, name="ffn_act_bwd", grid=(rows // bm, FF_NB),
        in_specs=[pl.BlockSpec((bm, D_MODEL), lambda i, j: (i, 0)),
                  pl.BlockSpec((FF_BN, D_MODEL), lambda i, j: (j, 0)), blk, blk],
        out_specs=pl.BlockSpec((2, bm, FF_BN), lambda i, j: (0, i, j)),
        out_shape=jax.ShapeDtypeStruct((2, rows, D_FF), BF16),
        compiler_params=_cparams("parallel", "arbitrary"),
    )(dy, w_down, gate, up)


def _pack_rows(a):
    flat = a.reshape(-1)
    pad = (-flat.shape[0]) % 1024
    if pad:
        flat = jnp.concatenate([flat, jnp.zeros((pad,), flat.dtype)])
    return flat.reshape(-1, 1024)


def _pack_owner_rows(a8):
    flat = a8.reshape(N_DEV, -1)
    pad = (-flat.shape[1]) % 1024
    if pad:
        flat = jnp.concatenate([flat, jnp.zeros((N_DEV, pad), flat.dtype)], axis=1)
    return flat.reshape(N_DEV, -1, 1024)


class _Group:
    def __init__(self, names, extra=0, block=None):
        self.names, self.extra, self.offs = names, extra, {}
        o = 0
        for n in names:
            self.offs[n] = o
            o += SHARD_ROWS[n]
        self.shard_rows = o
        self.rows = -(-(o + extra) // 16) * 16
        self.block = block

    def _fill(self, parts, axis, dtype):
        used = sum(p.shape[axis] for p in parts)
        if used < self.rows:
            shape = list(parts[0].shape)
            shape[axis] = self.rows - used
            parts = parts + [jnp.zeros(shape, dtype)]
        return jnp.concatenate(parts, axis=axis)

    def pack(self, shards, extra_rows, dtype):
        parts = [_pack_rows(shards[n].astype(dtype)) for n in self.names] + [r.astype(dtype) for r in extra_rows]
        return self._fill(parts, 0, dtype)

    def pack_for_owners(self, full, extra_rows, dtype):
        parts = [_pack_owner_rows(_split_for_owners(n, full[n].astype(dtype))) for n in self.names]
        if extra_rows:
            rows = jnp.concatenate(extra_rows, axis=0).astype(dtype)
            parts.append(jnp.broadcast_to(rows[None], (N_DEV, len(extra_rows), 1024)))
        return self._fill(parts, 1, dtype)

    def unpack(self, pack, name):
        r, c = SHARD_SHAPE[name]
        seg = pack[self.offs[name]:self.offs[name] + SHARD_ROWS[name]].reshape(-1)[:r * c]
        return seg.reshape(1, r, c)

    def gathered(self, gathered, name):
        r, c = SHARD_SHAPE[name]
        seg = gathered[:, self.offs[name]:self.offs[name] + SHARD_ROWS[name]]
        seg = seg.reshape(N_DEV, -1)[:, :r * c].reshape(N_DEV, r, c)
        if name in ROW_SHARDED:
            return seg.reshape(N_DEV * r, c)
        return seg.transpose(1, 0, 2).reshape(r, N_DEV * c)


ROW_SHARDED = ("w_out", "w_mq", "w_mo", "w_down")
N_EXTRA = len(REPL) + 1
AG_FIRST = _Group(("w_in", "w_gk_up"))
AG_MID = _Group(("w_out", "w_mq", "w_mkv"))
AG_LATE = _Group(("w_mo", "w_gate_up", "w_down"))
RS_FFN = _Group(("w_gate_up", "w_down"), block=352)
RS_MID = _Group(("w_out", "w_mq", "w_mkv", "w_mo"), block=320)
RS_WIN = _Group(("w_in", "w_gk_up"), block=400)
RS_TAIL = _Group((), extra=N_EXTRA, block=16)


def _split_for_owners(name, full):
    r, c = SHARD_SHAPE[name]
    if name in ROW_SHARDED:
        return full.reshape(N_DEV, r, c)
    return full.reshape(r, N_DEV, c).transpose(1, 0, 2)


def _repl_row(a):
    flat = a.reshape(-1)
    return jnp.concatenate([flat, jnp.zeros((1024 - flat.shape[0],), flat.dtype)]).reshape(1, 1024)


def _local_step(x, mem, tgt, wf, rp, shards=None, scatter=False):
    rows = x.shape[0]
    bm = min(512, rows)
    bmx = min(256, rows)
    mt = min(MM_TILE, rows)
    kt = min(512, rows)
    w_cat, wgk_pad = wf["w_cat"], wf["wgk_pad"]
    wg_t = jnp.tile(rp["gla_norm_w"], (1, GLA_HEADS))
    ws_t = jnp.tile(rp["sb_norm_w"], (1, SB_HEADS))
    lane = jnp.arange(GLA_V_W)
    grp_g = (lane[:, None] // GLA_DV == lane[None, :] // GLA_DV).astype(BF16)
    grp_s = (lane[:, None] // SB_DH == lane[None, :] // SB_DH).astype(BF16)

    h1 = _rms_fwd("mix_norm_fwd", x, rp["mix_norm_w"], bm)
    proj = _matmul("in_proj", h1, w_cat, "nn", rows, PROJ_W, D_MODEL, F32, mt, 640, D_MODEL)
    gk = _gate_fwd(proj, wgk_pad, rp["b_gk"], bm)
    if shards is None:
        o_g, s_all = _gla_fwd(proj, gk, rows)
        (o_s,) = _sb_fwd(proj, rows)
    else:
        o_g, s_all, got_mid = _gla_fwd(proj, gk, rows, _Exchange(AG_MID.pack(shards, [], BF16), scatter=False))
        o_s, got_late = _sb_fwd(proj, rows, _Exchange(AG_LATE.pack(shards, [], BF16), scatter=False))
        wf = {**wf, **{n: AG_MID.gathered(got_mid, n) for n in AG_MID.names},
              **{n: AG_LATE.gathered(got_late, n) for n in AG_LATE.names}}
    cat = _mix_out_fwd(o_g, proj, o_s, wg_t, ws_t, grp_g, grp_s, bm)
    x1 = _matmul("out_proj", cat, wf["w_out"], "nn", rows, D_MODEL, D_MODEL, F32, mt, MM_TILE, D_MODEL, residual=x)
    h2 = _rms_fwd("xattn_norm_fwd", x1, rp["xattn_norm_w"], bm)
    qm = _matmul("mq_proj", h2, wf["w_mq"], "nn", rows, D_MODEL, D_MODEL, F32, mt, MM_TILE, D_MODEL)
    mem_n, kpre, kn, v_m = _mem_kv_fwd(mem, rp["mem_norm_w"], wf["w_mkv"], rp["mk_norm_w"])
    o_m = _xattn_fwd(qm, kn, v_m, rp["mq_norm_w"], bmx)
    x2 = _matmul("mo_proj", o_m, wf["w_mo"], "nn", rows, D_MODEL, D_MODEL, F32, mt, MM_TILE, D_MODEL, residual=x1)
    h3 = _rms_fwd("ffn_norm_fwd", x2, rp["ffn_norm_w"], bm)
    gate, up, act = _ffn_up(h3, wf["w_gate_up"], rows, bm)
    y = _matmul("ffn_down", act, wf["w_down"], "nn", rows, D_MODEL, D_FF, F32, mt, MM_TILE, FF_BN, residual=x2)
    dy, sq = _loss_kernel(y, tgt, bm)

    g = {}
    dgu = _ffn_act_bwd(dy, wf["w_down"], gate, up, rows, bm)
    g["w_down"] = _matmul("grad_w_down", act, dy, "tn", D_FF, D_MODEL, rows, F32, FF_BN, MM_TILE, kt)
    nkb = FF_NB
    dh3 = _matmul("ffn_up_bwd", dgu, wf["w_gate_up"], "nt", rows, D_MODEL, 2 * D_FF, F32, mt, MM_TILE, FF_BN,
                  a_spec=pl.BlockSpec((None, mt, FF_BN), lambda i, j, kk: (kk // nkb, i, kk % nkb)))
    g["w_gate_up"] = _matmul(
        "grad_w_gate_up", h3, dgu, "tn", D_MODEL, 2 * D_FF, rows, F32, MM_TILE, FF_BN, kt,
        b_spec=pl.BlockSpec((None, kt, FF_BN), lambda i, j, kk: (j // nkb, kk, j % nkb)))
    dx2, g["ffn_norm_w"] = _rms_bwd("ffn_norm_bwd", x2, rp["ffn_norm_w"], dh3, dy, bm)

    do_m = _matmul("mo_proj_bwd", dx2, wf["w_mo"], "nt", rows, D_MODEL, D_MODEL, BF16, mt, MM_TILE, D_MODEL)
    g["w_mo"] = _matmul("grad_w_mo", o_m, dx2, "tn", D_MODEL, D_MODEL, rows, F32, MM_TILE, MM_TILE, kt)
    dqm, dkn, dv_m, g["mq_norm_w"] = _xattn_bwd(qm, kn, v_m, rp["mq_norm_w"], do_m, bmx)
    g["w_mkv"], g["mk_norm_w"], g["mem_norm_w"] = _mem_kv_bwd(
        mem, rp["mem_norm_w"], wf["w_mkv"], rp["mk_norm_w"], mem_n, kpre, dkn, dv_m)
    dh2 = _matmul("mq_proj_bwd", dqm, wf["w_mq"], "nt", rows, D_MODEL, D_MODEL, F32, mt, MM_TILE, D_MODEL)
    g["w_mq"] = _matmul("grad_w_mq", h2, dqm, "tn", D_MODEL, D_MODEL, rows, F32, MM_TILE, MM_TILE, kt)
    dx1, g["xattn_norm_w"] = _rms_bwd("xattn_norm_bwd", x1, rp["xattn_norm_w"], dh2, dx2, bm)

    dcat = _matmul("out_proj_bwd", dx1, wf["w_out"], "nt", rows, D_MODEL, D_MODEL, F32, mt, MM_TILE, D_MODEL)
    g["w_out"] = _matmul("grad_w_out", cat, dx1, "tn", D_MODEL, D_MODEL, rows, F32, MM_TILE, MM_TILE, kt)
    do_g, dgg, do_s, dwg, dws = _mix_out_bwd(dcat, o_g, proj, o_s, wg_t, ws_t, grp_g, grp_s, bm)
    g["gla_norm_w"] = dwg.reshape(GLA_HEADS, GLA_DV).sum(axis=0, keepdims=True)
    g["sb_norm_w"] = dws.reshape(SB_HEADS, SB_DH).sum(axis=0, keepdims=True)
    recv = {}
    if scatter:
        dq_s, dk_s, dv_s, recv["ffn"] = _sb_bwd(
            proj, o_s, do_s, rows, _Exchange(RS_FFN.pack_for_owners(g, [], BF16), scatter=True))
        dq_g, dk_g, dv_g, dgk, recv["mid"] = _gla_bwd(
            proj, gk, do_g, s_all, rows, _Exchange(RS_MID.pack_for_owners(g, [], BF16), scatter=True))
    else:
        dq_s, dk_s, dv_s = _sb_bwd(proj, o_s, do_s, rows)
        dq_g, dk_g, dv_g, dgk = _gla_bwd(proj, gk, do_g, s_all, rows)
    dlr, dwgk, g["b_gk"] = _gate_bwd(proj, wgk_pad, rp["b_gk"], dgk, bm)
    g["w_gk_up"] = dwgk[:GATE_RANK]
    dproj = jnp.concatenate([dq_g.astype(BF16), dk_g.astype(BF16), dv_g.astype(BF16), dgg, dq_s.astype(BF16),
                             dk_s.astype(BF16), dv_s.astype(BF16), dlr], axis=1)
    dw_cat = _matmul("grad_w_in", h1, dproj, "tn", D_MODEL, PROJ_W, rows, F32, MM_TILE, 640, kt)
    g["w_in"] = jnp.concatenate([dw_cat[:, :C_QS], dw_cat[:, C_LR:C_LR + GATE_RANK], dw_cat[:, C_QS:C_LR]], axis=1)
    if scatter:
        dh1, recv["win"] = _matmul("in_proj_bwd", dproj, w_cat, "nt", rows, D_MODEL, PROJ_W, F32, mt, MM_TILE, 640,
                                   exchange=_Exchange(RS_WIN.pack_for_owners(g, [], BF16), scatter=True))
    else:
        dh1 = _matmul("in_proj_bwd", dproj, w_cat, "nt", rows, D_MODEL, PROJ_W, F32, mt, MM_TILE, 640)
    dx, g["mix_norm_w"] = _rms_bwd("mix_norm_bwd", x, rp["mix_norm_w"], dh1, dx1, bm)
    return sq, dx, g, recv


def _first_weights(gathered):
    w_in = AG_FIRST.gathered(gathered, "w_in")
    lr_end = C_QS + GATE_RANK
    w_cat = jnp.concatenate([w_in[:, :C_QS], w_in[:, lr_end:], w_in[:, C_QS:lr_end],
                             jnp.zeros((D_MODEL, PROJ_W - D_IN), BF16)], axis=1)
    wgk = AG_FIRST.gathered(gathered, "w_gk_up")
    return {"w_cat": w_cat, "wgk_pad": jnp.concatenate([wgk, jnp.zeros((LANES - GATE_RANK, GLA_QK_W), BF16)], axis=0)}


def kernel(x, mem, mix_norm_w, w_in, w_gk_up, b_gk, gla_norm_w, sb_norm_w, w_out, xattn_norm_w, mem_norm_w, w_mq, w_mkv, mq_norm_w, mk_norm_w, w_mo, ffn_norm_w, w_gate_up, w_down, loss_target, m_mix_norm_w, m_w_in, m_w_gk_up, m_b_gk, m_gla_norm_w, m_sb_norm_w, m_w_out, m_xattn_norm_w, m_mem_norm_w, m_w_mq, m_w_mkv, m_mq_norm_w, m_mk_norm_w, m_w_mo, m_ffn_norm_w, m_w_gate_up, m_w_down, v_mix_norm_w, v_w_in, v_w_gk_up, v_b_gk, v_gla_norm_w, v_sb_norm_w, v_w_out, v_xattn_norm_w, v_mem_norm_w, v_w_mq, v_w_mkv, v_mq_norm_w, v_mk_norm_w, v_w_mo, v_ffn_norm_w, v_w_gate_up, v_w_down):
    given = dict(locals())
    w = {n: given[n][0] for n in WEIGHTS}
    m = {n: given["m_" + n][0] for n in WEIGHTS}
    v = {n: given["v_" + n][0] for n in WEIGHTS}

    wf = _first_weights(_exchange_call("gather_first_weights", AG_FIRST.pack(w, [], BF16), scatter=False))
    rp = {n: w[n].reshape(1, -1) for n in REPL}
    sq, dx, g, recv = _local_step(x[0], mem[0], loss_target[0], wf, rp, shards=w, scatter=True)

    loss_row = _repl_row(jnp.sum(sq).reshape(1) * (0.5 / D_MODEL))
    repl_rows = [_repl_row(g[n]) for n in REPL] + [loss_row]
    recv["tail"] = _exchange_call("scatter_tail_gradients", RS_TAIL.pack_for_owners(g, repl_rows, F32), scatter=True)

    zero_row = jnp.zeros((1, 1024), F32)
    groups = (("ffn", RS_FFN), ("mid", RS_MID), ("win", RS_WIN), ("tail", RS_TAIL))
    out_packs = {}
    for key, grp in groups:
        extra = lambda t: [_repl_row(t[n]) for n in REPL] + [zero_row] if grp.extra else []
        out_packs[key] = _adamw("sum_adamw_" + key, recv[key], *[grp.pack(t, extra(t), F32) for t in (w, m, v)],
                                block=grp.block)

    def unpack(kind, name):
        for key, grp in groups:
            if name in grp.names:
                return grp.unpack(out_packs[key][kind], name)
        row = out_packs["tail"][kind][REPL.index(name)]
        return row[:w[name].shape[-1]].reshape(1, -1)

    loss = out_packs["tail"][0][len(REPL), 0]
    outs = [loss, dx[None]]
    for kind in range(4):
        outs += [unpack(kind, n) for n in WEIGHTS]
    return tuple(outs)
```

```python
import functools
import math

import jax
import jax.numpy as jnp
from jax import lax
from jax.experimental import pallas as pl
from jax.experimental.pallas import tpu as pltpu

F32 = jnp.float32
BF16 = jnp.bfloat16

N_DEV = 8
D_MODEL = 1024
GLA_HEADS = 4
GLA_DK = 64
GLA_DV = 128
GLA_CHUNK = 64
GLA_STEP_CHUNKS = 4
GLA_QK_W = GLA_HEADS * GLA_DK
GLA_V_W = GLA_HEADS * GLA_DV
GATE_RANK = 16
SB_HEADS = 8
SB_DH = 64
SB_W = SB_HEADS * SB_DH
SB_BLK = 128
SB_QT = 1024
SB_DEAD = -104.0
MEM_LEN = 256
MEM_HEADS = 4
MEM_DH = 256
D_FF = 2816
D_IN = 3088
RMS_EPS = 1e-6
LANES = 128

PROJ_W = 3200
C_QG, C_KG, C_VG, C_GG, C_QS, C_KS, C_VS, C_LR = 0, 256, 512, 1024, 1536, 2048, 2560, 3072

ADAM_LR, ADAM_B1, ADAM_B2, ADAM_EPS, ADAM_WD, ADAM_STEP = 0.001, 0.9, 0.999, 1e-08, 0.01, 10

SHARD_SHAPE = {"w_in": (1024, 386), "w_out": (128, 1024), "w_mq": (128, 1024), "w_mkv": (1024, 256),
               "w_mo": (128, 1024), "w_gate_up": (1024, 704), "w_down": (352, 1024), "w_gk_up": (16, 32)}
REPL = ("mix_norm_w", "b_gk", "gla_norm_w", "sb_norm_w", "xattn_norm_w", "mem_norm_w", "mq_norm_w",
        "mk_norm_w", "ffn_norm_w")
WEIGHTS = ("mix_norm_w", "w_in", "w_gk_up", "b_gk", "gla_norm_w", "sb_norm_w", "w_out", "xattn_norm_w",
           "mem_norm_w", "w_mq", "w_mkv", "mq_norm_w", "mk_norm_w", "w_mo", "ffn_norm_w", "w_gate_up", "w_down")
VMEM_LIMIT = 56 * 1024 * 1024
MM_TILE = 1024


def _cparams(*sem):
    return pltpu.CompilerParams(dimension_semantics=sem if sem else None, vmem_limit_bytes=VMEM_LIMIT)


def _dot(a, b, ca=1, cb=0):
    return lax.dot_general(a.astype(BF16), b.astype(BF16), (((ca,), (cb,)), ((), ())),
                           preferred_element_type=F32)


def _split(x, parts):
    out = []
    for _ in range(parts - 1):
        hi = x.astype(BF16)
        out.append(hi)
        x = x - hi.astype(F32)
    out.append(x.astype(BF16))
    return out


def _dot_lhs_exact(x, m, ca=1, cb=0, parts=3):
    acc = None
    for p in _split(x, parts):
        t = _dot(p, m, ca, cb)
        acc = t if acc is None else acc + t
    return acc


def _dot_rhs_exact(m, x, ca=1, cb=0, parts=3):
    acc = None
    for p in _split(x, parts):
        t = _dot(m, p, ca, cb)
        acc = t if acc is None else acc + t
    return acc


def _dot3(a, b, ca=1, cb=0):
    a_hi, a_lo = _split(a, 2)
    b_hi, b_lo = _split(b, 2)
    return _dot(a_hi, b_hi, ca, cb) + (_dot(a_hi, b_lo, ca, cb) + _dot(a_lo, b_hi, ca, cb))


def _log_sigmoid(z):
    return jnp.minimum(z, 0.0) - jnp.log(1.0 + jnp.exp(-jnp.abs(z)))


def _sigmoid(z):
    e = jnp.exp(-jnp.abs(z))
    return jnp.where(z >= 0, 1.0, e) / (1.0 + e)


def _iota2(shape, dim):
    return lax.broadcasted_iota(jnp.int32, shape, dim)


def _rowcall(name, fn, row_ins, full_ins, row_outs, acc_outs, bm, rows):
    n_in = len(row_ins) + len(full_ins)
    n_row = len(row_outs)

    def body(*refs):
        ins, outs = refs[:n_in], refs[n_in:]
        res = fn(*[r[...] for r in ins])
        for r, v in zip(outs[:n_row], res[:n_row]):
            r[...] = v.astype(r.dtype)
        first = pl.program_id(0) == 0
        for r, v in zip(outs[n_row:], res[n_row:]):
            def init(r=r):
                r[...] = jnp.zeros(r.shape, r.dtype)
            pl.when(first)(init)
            r[...] += v

    in_specs = [pl.BlockSpec((bm, w), functools.partial(lambda i, c: (i, c), c=c)) for _, w, c in row_ins]
    in_specs += [pl.BlockSpec(a.shape, lambda i: (0, 0)) for a in full_ins]
    out_specs = [pl.BlockSpec((bm, w), lambda i: (i, 0)) for w, _ in row_outs]
    out_specs += [pl.BlockSpec(s, lambda i: (0, 0)) for s in acc_outs]
    out_shape = [jax.ShapeDtypeStruct((rows, w), dt) for w, dt in row_outs]
    out_shape += [jax.ShapeDtypeStruct(s, F32) for s in acc_outs]
    return pl.pallas_call(
        body, name=name, grid=(rows // bm,), in_specs=in_specs, out_specs=out_specs, out_shape=out_shape,
        compiler_params=_cparams("arbitrary"),
    )(*[a for a, _, _ in row_ins], *full_ins)


def _matmul(name, a, b, mode, m, n, k, out_dtype, bm, bn, bk, residual=None, a_spec=None, b_spec=None,
            exchange=None):
    bm, bn, bk = min(bm, m), min(bn, n), min(bk, k)
    nk = k // bk
    ca, cb = {"nn": (1, 0), "nt": (1, 1), "tn": (0, 0)}[mode]
    if a_spec is None:
        a_spec = (pl.BlockSpec((bk, bm), lambda i, j, kk: (kk, i)) if mode == "tn"
                  else pl.BlockSpec((bm, bk), lambda i, j, kk: (i, kk)))
    if b_spec is None:
        b_spec = (pl.BlockSpec((bn, bk), lambda i, j, kk: (j, kk)) if mode == "nt"
                  else pl.BlockSpec((bk, bn), lambda i, j, kk: (kk, j)))
    has_res = residual is not None

    def body(*refs):
        a_ref, b_ref = refs[0], refs[1]
        res_ref = refs[2] if has_res else None
        o_ref = refs[2 + has_res]
        part = _dot(a_ref[...], b_ref[...], ca, cb)

        def finish(total):
            if has_res:
                total = total + res_ref[...]
            o_ref[...] = total.astype(o_ref.dtype)

        if nk == 1:
            finish(part)
        else:
            acc_ref = refs[3 + has_res]
            kk = pl.program_id(2)

            @pl.when(kk == 0)
            def _():
                acc_ref[...] = part

            @pl.when(kk > 0)
            def _():
                acc_ref[...] += part

            @pl.when(kk == nk - 1)
            def _():
                finish(acc_ref[...])

    in_specs = [a_spec, b_spec]
    args = [a, b]
    if has_res:
        in_specs.append(pl.BlockSpec((bm, bn), lambda i, j, kk: (i, j)))
        args.append(residual)
    sem = ("parallel", "parallel", "arbitrary") if exchange is None else ("arbitrary",) * 3
    res = _hosted_call(
        body, exchange, grid=(m // bm, n // bn, nk), in_specs=in_specs,
        out_specs=[pl.BlockSpec((bm, bn), lambda i, j, kk: (i, j))],
        out_shape=[jax.ShapeDtypeStruct((m, n), out_dtype)],
        scratch_shapes=[pltpu.VMEM((bm, bn), F32)] if nk > 1 else [],
        args=args, name=name, compiler_params=_cparams(*sem))
    return res[0] if exchange is None else res


def _peer(mask):
    x, y, c = lax.axis_index("x"), lax.axis_index("y"), lax.axis_index("c")
    mx, my, mc = (mask >> 2) & 1, (mask >> 1) & 1, mask & 1
    px, py, pc = (1 - x if mx else x), (1 - y if my else y), (1 - c if mc else c)
    return (px, py, pc), 4 * px + 2 * py + pc


def _my_index():
    return 4 * lax.axis_index("x") + 2 * lax.axis_index("y") + lax.axis_index("c")


class _Exchange:
    def __init__(self, srcs, scatter):
        self.srcs, self.scatter, n = list(srcs), scatter, len(srcs)
        self.in_specs = [pl.BlockSpec(memory_space=pl.ANY)] * n
        self.out_specs = [pl.BlockSpec(memory_space=pl.ANY)] * n
        self.out_shapes = [jax.ShapeDtypeStruct((N_DEV,) + s.shape[-2:], s.dtype) for s in self.srcs]
        self.scratch = [pltpu.SemaphoreType.DMA((n * (N_DEV - 1),)), pltpu.SemaphoreType.DMA((n * (N_DEV - 1),)),
                        pltpu.SemaphoreType.DMA((n,))]

    def _copies(self, src_refs, out_refs, sems):
        send_sems, recv_sems, local_sems = sems
        me = _my_index()
        copies = []
        for a, (src_ref, out_ref) in enumerate(zip(src_refs, out_refs)):
            copies.append(pltpu.make_async_copy(src_ref.at[me] if self.scatter else src_ref, out_ref.at[me],
                                                local_sems.at[a]))
            for mask in range(1, N_DEV):
                peer, peer_index = _peer(mask)
                slot = a * (N_DEV - 1) + mask - 1
                copies.append(pltpu.make_async_remote_copy(
                    src_ref=src_ref.at[peer_index] if self.scatter else src_ref, dst_ref=out_ref.at[me],
                    send_sem=send_sems.at[slot], recv_sem=recv_sems.at[slot],
                    device_id=peer, device_id_type=pl.DeviceIdType.MESH))
        return copies

    def start(self, src_refs, out_refs, sems):
        for cp in self._copies(src_refs, out_refs, sems):
            cp.start()

    def wait(self, src_refs, out_refs, sems):
        for cp in self._copies(src_refs, out_refs, sems):
            cp.wait()


def _hosted_call(body, ex, grid, in_specs, out_specs, out_shape, scratch_shapes, args, **kw):
    if ex is None:
        return pl.pallas_call(body, grid=grid, in_specs=in_specs, out_specs=out_specs, out_shape=out_shape,
                              scratch_shapes=scratch_shapes, **kw)(*args)
    n_in, n_out, n_scr, n_x = len(in_specs), len(out_specs), len(scratch_shapes), len(ex.srcs)

    def hosted(*refs):
        ins, src_refs = refs[:n_in], refs[n_in:n_in + n_x]
        refs = refs[n_in + n_x:]
        outs, out_refs = refs[:n_out], refs[n_out:n_out + n_x]
        refs = refs[n_out + n_x:]
        scr, sems = refs[:n_scr], refs[n_scr:]
        ids = [pl.program_id(a) for a in range(len(grid))]
        first = functools.reduce(jnp.logical_and, [p == 0 for p in ids])
        last = functools.reduce(jnp.logical_and, [p == n - 1 for p, n in zip(ids, grid)])

        @pl.when(first)
        def _():
            ex.start(src_refs, out_refs, sems)

        body(*ins, *outs, *scr)

        @pl.when(last)
        def _():
            ex.wait(src_refs, out_refs, sems)

    res = pl.pallas_call(
        hosted, grid=grid, in_specs=list(in_specs) + ex.in_specs, out_specs=list(out_specs) + ex.out_specs,
        out_shape=list(out_shape) + ex.out_shapes, scratch_shapes=list(scratch_shapes) + ex.scratch, **kw,
    )(*args, *ex.srcs)
    return tuple(res[:n_out]) + (list(res[n_out:]),)


def _exchange_call(name, srcs, scatter):
    ex = _Exchange(srcs, scatter)
    n_x = len(ex.srcs)

    def body(*refs):
        src_refs, out_refs, sems = refs[:n_x], refs[n_x:2 * n_x], refs[2 * n_x:]
        ex.start(src_refs, out_refs, sems)
        ex.wait(src_refs, out_refs, sems)

    return list(pl.pallas_call(
        body, name=name, in_specs=ex.in_specs, out_specs=ex.out_specs, out_shape=ex.out_shapes,
        scratch_shapes=ex.scratch, compiler_params=pltpu.CompilerParams(has_side_effects=True),
    )(*ex.srcs))


def _adamw(name, recv, w, m, v, block):
    rows, cols = w.shape
    c1 = 1.0 - ADAM_B1 ** ADAM_STEP
    c2 = 1.0 - ADAM_B2 ** ADAM_STEP

    def body(r_ref, w_ref, m_ref, v_ref, g_out, d_out, m_out, v_out):
        g = r_ref[0].astype(F32)
        for s in range(1, N_DEV):
            g = g + r_ref[s].astype(F32)
        m_new = ADAM_B1 * m_ref[...] + (1.0 - ADAM_B1) * g
        v_new = ADAM_B2 * v_ref[...] + (1.0 - ADAM_B2) * (g * g)
        m_hat = m_new / c1
        v_hat = v_new / c2
        g_out[...] = g
        d_out[...] = -ADAM_LR * (m_hat / (jnp.sqrt(v_hat) + ADAM_EPS) + ADAM_WD * w_ref[...])
        m_out[...] = m_new
        v_out[...] = v_new

    blk = pl.BlockSpec((block, cols), lambda i: (i, 0))
    return pl.pallas_call(
        body, name=name, grid=(rows // block,),
        in_specs=[pl.BlockSpec((N_DEV, block, cols), lambda i: (0, i, 0)), blk, blk, blk],
        out_specs=[blk] * 4, out_shape=[jax.ShapeDtypeStruct((rows, cols), F32)] * 4,
        compiler_params=_cparams("parallel"),
    )(recv, w, m, v)


def _rms_fwd(name, x, w, bm):
    def fn(xb, wb):
        r = lax.rsqrt(jnp.mean(xb * xb, axis=-1, keepdims=True) + RMS_EPS)
        return (xb * r * wb,)
    return _rowcall(name, fn, [(x, D_MODEL, 0)], [w], [(D_MODEL, BF16)], [], bm, x.shape[0])[0]


def _rms_bwd(name, x, w, dh, dres, bm):
    def fn(xb, dhb, drb, wb):
        r = lax.rsqrt(jnp.mean(xb * xb, axis=-1, keepdims=True) + RMS_EPS)
        xh = xb * r
        dxh = dhb.astype(F32) * wb
        dx = drb + r * (dxh - xh * jnp.mean(dxh * xh, axis=-1, keepdims=True))
        return dx, jnp.sum(dhb.astype(F32) * xh, axis=0, keepdims=True)
    return _rowcall(name, fn, [(x, D_MODEL, 0), (dh, D_MODEL, 0), (dres, D_MODEL, 0)], [w],
                    [(D_MODEL, F32)], [(1, D_MODEL)], bm, x.shape[0])


def _gate_fwd(proj, wgk_pad, b_gk, bm):
    def fn(lr, wg, bg):
        z = _dot(lr, wg) + bg
        return (_log_sigmoid(z) * (1.0 / 16.0),)
    return _rowcall("gla_gate_fwd", fn, [(proj, LANES, C_LR // LANES)], [wgk_pad, b_gk],
                    [(GLA_QK_W, F32)], [], bm, proj.shape[0])[0]


def _gate_bwd(proj, wgk_pad, b_gk, dgk, bm):
    def fn(lr, dg, wg, bg):
        z = _dot(lr, wg) + bg
        dz = dg * _sigmoid(-z) * (1.0 / 16.0)
        return _dot(dz, wg, 1, 1), _dot(lr, dz, 0, 0), jnp.sum(dz, axis=0, keepdims=True)
    return _rowcall("gla_gate_bwd", fn, [(proj, LANES, C_LR // LANES), (dgk, GLA_QK_W, 0)], [wgk_pad, b_gk],
                    [(LANES, BF16)], [(LANES, GLA_QK_W), (1, GLA_QK_W)], bm, proj.shape[0])


def _group_mean(x, g, size):
    return _dot_lhs_exact(x, g, parts=2) * (1.0 / size)


def _mix_out_fwd(o_g, proj, o_s, wg_t, ws_t, grp_g, grp_s, bm):
    def fn(og, gg, os_, wg, ws, gmat, smat):
        rg = lax.rsqrt(_group_mean(og * og, gmat, GLA_DV) + RMS_EPS)
        yg = og * rg * wg * (gg * _sigmoid(gg))
        rs = lax.rsqrt(_group_mean(os_ * os_, smat, SB_DH) + RMS_EPS)
        ys = os_ * rs * ws
        return (jnp.concatenate([yg, ys], axis=1),)
    return _rowcall("mix_out_fwd", fn, [(o_g, GLA_V_W, 0), (proj, GLA_V_W, C_GG // GLA_V_W), (o_s, SB_W, 0)],
                    [wg_t, ws_t, grp_g, grp_s], [(D_MODEL, BF16)], [], bm, o_g.shape[0])[0]


def _mix_out_bwd(dcat, o_g, proj, o_s, wg_t, ws_t, grp_g, grp_s, bm):
    def fn(dyg, dys, og, gg, os_, wg, ws, gmat, smat):
        dyg = dyg.astype(F32)
        dys = dys.astype(F32)
        rg = lax.rsqrt(_group_mean(og * og, gmat, GLA_DV) + RMS_EPS)
        xh = og * rg
        sg = _sigmoid(gg)
        silu = gg * sg
        dxh = dyg * wg * silu
        dgg = dyg * xh * wg * (sg * (1.0 + gg * (1.0 - sg)))
        dwg = jnp.sum(dyg * xh * silu, axis=0, keepdims=True)
        dog = rg * (dxh - xh * _group_mean(dxh * xh, gmat, GLA_DV))
        rs = lax.rsqrt(_group_mean(os_ * os_, smat, SB_DH) + RMS_EPS)
        xs = os_ * rs
        dxs = dys * ws
        dws = jnp.sum(dys * xs, axis=0, keepdims=True)
        dos = rs * (dxs - xs * _group_mean(dxs * xs, smat, SB_DH))
        return dog, dgg, dos, dwg, dws
    return _rowcall("mix_out_bwd", fn,
                    [(dcat, GLA_V_W, 0), (dcat, SB_W, 1), (o_g, GLA_V_W, 0), (proj, GLA_V_W, C_GG // GLA_V_W),
                     (o_s, SB_W, 0)],
                    [wg_t, ws_t, grp_g, grp_s], [(GLA_V_W, F32), (GLA_V_W, BF16), (SB_W, F32)],
                    [(1, GLA_V_W), (1, SB_W)], bm, o_g.shape[0])


def _loss_kernel(y, tgt, bm):
    def fn(yb, tb):
        err = yb - tb
        return err * (1.0 / D_MODEL), jnp.sum(err * err, axis=0, keepdims=True)
    return _rowcall("loss_head", fn, [(y, D_MODEL, 0), (tgt, D_MODEL, 0)], [], [(D_MODEL, F32)], [(1, D_MODEL)],
                    bm, y.shape[0])


def _sb_tri(inclusive):
    j, s = _iota2((2 * SB_BLK, 2 * SB_BLK), 0), _iota2((2 * SB_BLK, 2 * SB_BLK), 1)
    j = jnp.where(j >= SB_BLK, j - SB_BLK, j)
    keep = (j >= s) if inclusive else (j > s)
    return ((s >= SB_BLK) | keep).astype(BF16)


def _dot_hilo(x, m2):
    hi, lo = _split(x, 2)
    return _dot(jnp.concatenate([hi, lo], axis=1), m2)


def _sb_mask(n):
    return _iota2((n, SB_BLK), 1) < _iota2((n, SB_BLK), 0)


def _add_rows(full, part, row0, row1):
    pieces = [full[:row0]] if row0 else []
    pieces.append(full[row0:row1] + part)
    if row1 < full.shape[0]:
        pieces.append(full[row1:])
    return pieces[0] if len(pieces) == 1 else jnp.concatenate(pieces, axis=0)


def _sb_visit(tiles, carry, c_slots, qt, diag_start, n_left, left_start):
    for sub in reversed(range(qt // SB_BLK)):
        carry = tiles(diag_start(sub), carry, True, sub * SB_BLK, qt)

    def step(jj, cr):
        return tiles(left_start(jj), cr, False, 0, qt)

    return _sb_sweep(n_left, step, carry, c_slots)


def _sb_sweep(n_tiles, step, carry, c_slots):
    def alive(state):
        jj, carry = state
        c_max = jnp.max(functools.reduce(jnp.maximum, [carry[s] for s in c_slots]))
        return jnp.logical_and(jj < n_tiles, c_max > SB_DEAD)

    def body(state):
        jj, carry = state
        return jj + 1, step(jj, carry)

    return lax.while_loop(alive, body, (jnp.int32(0), carry))[1]


def _row_blocks(n):
    return [slice(r, r + SB_BLK) for r in range(0, n, SB_BLK)]


def _hilo(x):
    hi, lo = _split(x, 2)
    return jnp.concatenate([hi, lo], axis=1)


def _sb_tile(q, k, c, tri_excl, diag):
    blocks = _row_blocks(q.shape[0])
    strict = _sb_mask(SB_BLK) if diag else None
    z = _dot(q, k, 1, 1)
    lbs, pieces = [], []
    for r, rs in enumerate(blocks):
        lb = _log_sigmoid(z[rs])
        l1 = lb - z[rs]
        if diag and r == 0:
            l1 = jnp.where(strict, l1, 0.0)
        lbs.append(lb)
        pieces.append(_hilo(l1))
    sums = _dot(jnp.concatenate(pieces, axis=0), tri_excl)
    a = []
    for r, rs in enumerate(blocks):
        ar = jnp.exp(lbs[r] + sums[rs, :SB_BLK] + c[rs])
        if diag and r == 0:
            ar = jnp.where(strict, ar, 0.0)
        a.append(ar.astype(BF16))
    return lbs, a, sums[:, SB_BLK:]


def _sb_fwd(proj, rows, exchange=None):
    qt = min(SB_QT, rows)
    subs = qt // SB_BLK

    def body(q_ref, k_ref, v_ref, o_ref):
        i = pl.program_id(1)
        tri_excl = _sb_tri(False)
        heads = [slice(SB_DH * hh, SB_DH * (hh + 1)) for hh in range(2)]
        qs = [(q_ref[:, sl] * 0.125).astype(BF16) for sl in heads]

        def tiles(start, carry, diag, row0, row1):
            out = []
            for hh, sl in enumerate(heads):
                o, c = carry[2 * hh], carry[2 * hh + 1]
                k = k_ref[pl.ds(start, SB_BLK), sl].astype(BF16)
                v = v_ref[pl.ds(start, SB_BLK), sl].astype(BF16)
                _, a, dc = _sb_tile(qs[hh][row0:row1], k, c[row0:row1], tri_excl, diag)
                out += [_add_rows(o, _dot(jnp.concatenate(a, axis=0), v), row0, row1), _add_rows(c, dc, row0, row1)]
            return tuple(out)

        carry = (jnp.zeros((qt, SB_DH), F32), jnp.zeros((qt, SB_BLK), F32)) * 2
        carry = _sb_visit(tiles, carry, (1, 3), qt,
                          lambda sub: pl.multiple_of(i * qt + sub * SB_BLK, SB_BLK), i * subs,
                          lambda jj: pl.multiple_of((i * subs - 1 - jj) * SB_BLK, SB_BLK))
        for hh, sl in enumerate(heads):
            o_ref[:, sl] = carry[2 * hh]

    return _hosted_call(
        body, exchange, grid=(SB_HEADS // 2, rows // qt),
        in_specs=[pl.BlockSpec((qt, LANES), lambda h, i: (i, C_QS // LANES + h)),
                  pl.BlockSpec((rows, LANES), lambda h, i: (0, C_KS // LANES + h)),
                  pl.BlockSpec((rows, LANES), lambda h, i: (0, C_VS // LANES + h))],
        out_specs=[pl.BlockSpec((qt, LANES), lambda h, i: (i, h))],
        out_shape=[jax.ShapeDtypeStruct((rows, SB_W), F32)], scratch_shapes=[],
        args=(proj, proj, proj), name="sb_attention_fwd", compiler_params=_cparams("arbitrary", "arbitrary"))


def _sb_bwd(proj, o_s, do_s, rows, exchange=None):
    qt = min(SB_QT, rows)
    subs = qt // SB_BLK

    def body(q_ref, k_ref, v_ref, o_ref, do_ref, dq_ref, dk_ref, dv_ref):
        i = pl.program_id(1)

        @pl.when(i == 0)
        def _():
            dk_ref[...] = jnp.zeros(dk_ref.shape, F32)
            dv_ref[...] = jnp.zeros(dv_ref.shape, F32)

        tri_excl, tri_incl = _sb_tri(False), _sb_tri(True)
        heads = [slice(SB_DH * hh, SB_DH * (hh + 1)) for hh in range(2)]
        qs = [(q_ref[:, sl] * 0.125).astype(BF16) for sl in heads]
        dobs = [do_ref[:, sl].astype(BF16) for sl in heads]
        dsums = [jnp.broadcast_to(jnp.sum(dob.astype(F32) * o_ref[:, sl], axis=1, keepdims=True), (qt, SB_BLK))
                 for dob, sl in zip(dobs, heads)]

        def tiles(start, carry, diag, row0, row1):
            out = []
            strict = _sb_mask(SB_BLK) if diag else None
            for hh, sl in enumerate(heads):
                dq, c, cp = carry[3 * hh:3 * hh + 3]
                q, dob = qs[hh][row0:row1], dobs[hh][row0:row1]
                dsum, cpr = dsums[hh][row0:row1], cp[row0:row1]
                blocks = _row_blocks(q.shape[0])
                k = k_ref[pl.ds(start, SB_BLK), sl].astype(BF16)
                v = v_ref[pl.ds(start, SB_BLK), sl].astype(BF16)
                lbs, a, dc = _sb_tile(q, k, c[row0:row1], tri_excl, diag)
                da = _dot(dob, v, 1, 1)
                ps = [a[r].astype(F32) * da[rs] for r, rs in enumerate(blocks)]
                psums = _dot(jnp.concatenate([_hilo(p) for p in ps], axis=0), tri_incl)
                dzs = []
                for r, rs in enumerate(blocks):
                    left = dsum[rs] - (psums[rs, :SB_BLK] + cpr[rs])
                    dz = ps[r] - jnp.exp(lbs[r]) * (ps[r] + left)
                    if diag and r == 0:
                        dz = jnp.where(strict, dz, 0.0)
                    dzs.append(dz.astype(BF16))
                dzb, ab = jnp.concatenate(dzs, axis=0), jnp.concatenate(a, axis=0)
                dk_ref[pl.ds(start, SB_BLK), sl] += _dot(dzb, q, 0, 0)
                dv_ref[pl.ds(start, SB_BLK), sl] += _dot(ab, dob, 0, 0)
                out += [_add_rows(dq, _dot(dzb, k), row0, row1), _add_rows(c, dc, row0, row1),
                        _add_rows(cp, psums[:, SB_BLK:], row0, row1)]
            return tuple(out)

        zero = jnp.zeros((qt, SB_BLK), F32)
        carry = (jnp.zeros((qt, SB_DH), F32), zero, zero) * 2
        carry = _sb_visit(tiles, carry, (1, 4), qt,
                          lambda sub: pl.multiple_of(i * qt + sub * SB_BLK, SB_BLK), i * subs,
                          lambda jj: pl.multiple_of((i * subs - 1 - jj) * SB_BLK, SB_BLK))
        for hh, sl in enumerate(heads):
            dq_ref[:, sl] = carry[3 * hh] * 0.125

    whole = lambda base: pl.BlockSpec((rows, LANES), functools.partial(lambda h, i, b: (0, b + h), b=base))
    blk = lambda base: pl.BlockSpec((qt, LANES), functools.partial(lambda h, i, b: (i, b + h), b=base))
    return _hosted_call(
        body, exchange, grid=(SB_HEADS // 2, rows // qt),
        in_specs=[blk(C_QS // LANES), whole(C_KS // LANES), whole(C_VS // LANES), blk(0), blk(0)],
        out_specs=[blk(0), whole(0), whole(0)],
        out_shape=[jax.ShapeDtypeStruct((rows, SB_W), F32)] * 3, scratch_shapes=[],
        args=(proj, proj, proj, o_s, do_s), name="sb_attention_bwd",
        compiler_params=_cparams("arbitrary", "arbitrary"))


def _gla_chunk_common(g_all):
    r_i, c_i = _iota2((GLA_CHUNK, GLA_CHUNK), 0), _iota2((GLA_CHUNK, GLA_CHUNK), 1)
    tri = (c_i <= r_i).astype(BF16)
    return _dot_rhs_exact(tri, g_all), r_i, c_i


def _gla_scaled(qh, kh, bh):
    ref = bh[GLA_CHUNK // 2:GLA_CHUNK // 2 + 1, :]
    eq, ek = jnp.exp(bh - ref), jnp.exp(ref - bh)
    mask = _iota2((GLA_CHUNK, GLA_CHUNK), 1) <= _iota2((GLA_CHUNK, GLA_CHUNK), 0)
    return eq, ek, qh * eq, kh * ek, mask


def _gla_fwd(proj, gk, rows, exchange=None):
    n_chunks = rows // GLA_CHUNK
    step_rows = GLA_CHUNK * GLA_STEP_CHUNKS

    def body(q_ref, k_ref, v_ref, g_ref, o_ref, sall_ref, s_scr):
        @pl.when(pl.program_id(0) == 0)
        def _():
            s_scr[...] = jnp.zeros(s_scr.shape, F32)

        ones = jnp.ones((GLA_CHUNK, GLA_DV), BF16)
        states = [s_scr[h] for h in range(GLA_HEADS)]
        for ci in range(GLA_STEP_CHUNKS):
            rs = slice(GLA_CHUNK * ci, GLA_CHUNK * (ci + 1))
            g_all = g_ref[rs, :]
            b_all, _, _ = _gla_chunk_common(g_all)
            for h in range(GLA_HEADS):
                sl = slice(GLA_DK * h, GLA_DK * (h + 1))
                vs = slice(GLA_DV * h, GLA_DV * (h + 1))
                qh, kh, vh = q_ref[rs, sl] * 0.125, k_ref[rs, sl], v_ref[rs, vs]
                bh, gh = b_all[:, sl], g_all[:, sl]
                s = states[h]
                sall_ref[ci, h] = s
                _, _, qs, ks, mask = _gla_scaled(qh, kh, bh)
                a = jnp.where(mask, _dot(qs, ks, 1, 1), 0.0)
                o_ref[rs, vs] = _dot(qh * jnp.exp(bh), s) + _dot(a, vh)
                bl_col = _dot_lhs_exact(gh, ones, 0, 0)
                kd = kh * jnp.exp(bh[GLA_CHUNK - 1:GLA_CHUNK, :] - bh)
                states[h] = jnp.exp(bl_col) * s + _dot(kd, vh, 0, 0)
        for h in range(GLA_HEADS):
            s_scr[h] = states[h]

    c64 = lambda w, base: pl.BlockSpec((step_rows, w), functools.partial(lambda n, b: (n, b), b=base))
    return _hosted_call(
        body, exchange, grid=(rows // step_rows,),
        in_specs=[c64(GLA_QK_W, C_QG // GLA_QK_W), c64(GLA_QK_W, C_KG // GLA_QK_W), c64(GLA_V_W, C_VG // GLA_V_W),
                  c64(GLA_QK_W, 0)],
        out_specs=[c64(GLA_V_W, 0),
                   pl.BlockSpec((GLA_STEP_CHUNKS, GLA_HEADS, GLA_DK, GLA_DV), lambda n: (n, 0, 0, 0))],
        out_shape=[jax.ShapeDtypeStruct((rows, GLA_V_W), F32),
                   jax.ShapeDtypeStruct((n_chunks, GLA_HEADS, GLA_DK, GLA_DV), F32)],
        scratch_shapes=[pltpu.VMEM((GLA_HEADS, GLA_DK, GLA_DV), F32)],
        args=(proj, proj, proj, gk), name="gla_fwd", compiler_params=_cparams("arbitrary"))


def _gla_bwd(proj, gk, do_g, s_all, rows, exchange=None):
    n_chunks = rows // GLA_CHUNK

    def body(q_ref, k_ref, v_ref, g_ref, do_ref, sall_ref, dq_ref, dk_ref, dv_ref, dg_ref, ds_scr):
        @pl.when(pl.program_id(0) == 0)
        def _():
            ds_scr[...] = jnp.zeros(ds_scr.shape, F32)

        ones = jnp.ones((GLA_CHUNK, GLA_DV), BF16)
        ones8 = jnp.ones((8, GLA_DV), F32)
        last_row = _iota2((GLA_CHUNK, GLA_DK), 0) == GLA_CHUNK - 1
        dstates = [ds_scr[h] for h in range(GLA_HEADS)]
        for ci in reversed(range(GLA_STEP_CHUNKS)):
            cs = slice(GLA_CHUNK * ci, GLA_CHUNK * (ci + 1))
            g_all = g_ref[cs, :]
            b_all, r_i, c_i = _gla_chunk_common(g_all)
            triu = (c_i >= r_i).astype(BF16)
            for h in range(GLA_HEADS):
                sl = slice(GLA_DK * h, GLA_DK * (h + 1))
                vs = slice(GLA_DV * h, GLA_DV * (h + 1))
                qh, kh, vh = q_ref[cs, sl] * 0.125, k_ref[cs, sl], v_ref[cs, vs]
                bh, gh = b_all[:, sl], g_all[:, sl]
                doh = do_ref[cs, vs]
                s, ds = sall_ref[ci, h], dstates[h]
                eb = jnp.exp(bh)
                ekd = jnp.exp(bh[GLA_CHUNK - 1:GLA_CHUNK, :] - bh)
                ebl = jnp.exp(_dot_lhs_exact(gh, ones, 0, 0))
                qb, kd = qh * eb, kh * ekd
                dq = _dot(doh, s, 1, 1) * eb
                dk = _dot(vh, ds, 1, 1) * ekd
                dv = _dot(kd, ds)
                dbl = jnp.sum(dk * kh, axis=0, keepdims=True) + _dot3(ones8, ebl * s * ds, 1, 1)[0:1, :]
                eq, ek, qs, ks, mask = _gla_scaled(qh, kh, bh)
                a = jnp.where(mask, _dot(qs, ks, 1, 1), 0.0)
                da = jnp.where(mask, _dot(doh, vh, 1, 1), 0.0)
                dq = dq + _dot(da, ks) * eq
                dk = dk + _dot(da, qs, 0, 0) * ek
                dv = dv + _dot(a, doh, 0, 0)
                db = qh * dq - kh * dk + jnp.where(last_row, dbl, 0.0)
                dq_ref[cs, sl] = dq * 0.125
                dk_ref[cs, sl] = dk
                dv_ref[cs, vs] = dv
                dg_ref[cs, sl] = _dot_rhs_exact(triu, db)
                dstates[h] = _dot(qb, doh, 0, 0) + ebl * ds
        for h in range(GLA_HEADS):
            ds_scr[h] = dstates[h]

    step_rows = GLA_CHUNK * GLA_STEP_CHUNKS
    last = rows // step_rows - 1
    c64 = lambda w, base: pl.BlockSpec((step_rows, w), functools.partial(lambda n, b: (last - n, b), b=base))
    return _hosted_call(
        body, exchange, grid=(rows // step_rows,),
        in_specs=[c64(GLA_QK_W, C_QG // GLA_QK_W), c64(GLA_QK_W, C_KG // GLA_QK_W), c64(GLA_V_W, C_VG // GLA_V_W),
                  c64(GLA_QK_W, 0), c64(GLA_V_W, 0),
                  pl.BlockSpec((GLA_STEP_CHUNKS, GLA_HEADS, GLA_DK, GLA_DV), lambda n: (last - n, 0, 0, 0))],
        out_specs=[c64(GLA_QK_W, 0), c64(GLA_QK_W, 0), c64(GLA_V_W, 0), c64(GLA_QK_W, 0)],
        out_shape=[jax.ShapeDtypeStruct((rows, GLA_QK_W), F32), jax.ShapeDtypeStruct((rows, GLA_QK_W), F32),
                   jax.ShapeDtypeStruct((rows, GLA_V_W), F32), jax.ShapeDtypeStruct((rows, GLA_QK_W), F32)],
        scratch_shapes=[pltpu.VMEM((GLA_HEADS, GLA_DK, GLA_DV), F32)],
        args=(proj, proj, proj, gk, do_g, s_all), name="gla_bwd", compiler_params=_cparams("arbitrary"))


def _mem_kv_fwd(mem, mem_norm_w, w_mkv, mk_norm_w):
    def body(mem_ref, mw_ref, w_ref, kw_ref, memn_ref, kpre_ref, kn_ref, v_ref):
        xb = mem_ref[...]
        r = lax.rsqrt(jnp.mean(xb * xb, axis=-1, keepdims=True) + RMS_EPS)
        mem_n = (xb * r * mw_ref[...]).astype(BF16)
        memn_ref[...] = mem_n
        kv = _dot(mem_n, w_ref[...])
        kpre_ref[...] = kv[:, :D_MODEL]
        v_ref[...] = kv[:, D_MODEL:].astype(BF16)
        for h in range(MEM_HEADS):
            sl = slice(MEM_DH * h, MEM_DH * (h + 1))
            kh = kv[:, sl]
            rk = lax.rsqrt(jnp.mean(kh * kh, axis=-1, keepdims=True) + RMS_EPS)
            kn_ref[:, sl] = (kh * rk * kw_ref[...]).astype(BF16)

    return pl.pallas_call(
        body, name="mem_kv_fwd",
        out_shape=[jax.ShapeDtypeStruct((MEM_LEN, D_MODEL), BF16), jax.ShapeDtypeStruct((MEM_LEN, D_MODEL), F32),
                   jax.ShapeDtypeStruct((MEM_LEN, D_MODEL), BF16), jax.ShapeDtypeStruct((MEM_LEN, D_MODEL), BF16)],
        compiler_params=_cparams(),
    )(mem, mem_norm_w, w_mkv, mk_norm_w)


def _mem_kv_bwd(mem, mem_norm_w, w_mkv, mk_norm_w, mem_n, kpre, dkn, dv):
    def body(mem_ref, mw_ref, w_ref, kw_ref, memn_ref, kpre_ref, dkn_ref, dv_ref, dw_ref, dkw_ref, dmw_ref):
        dkw = jnp.zeros((1, MEM_DH), F32)
        dk_parts = []
        for h in range(MEM_HEADS):
            sl = slice(MEM_DH * h, MEM_DH * (h + 1))
            kh, dkh = kpre_ref[:, sl], dkn_ref[:, sl]
            rk = lax.rsqrt(jnp.mean(kh * kh, axis=-1, keepdims=True) + RMS_EPS)
            xh = kh * rk
            dxh = dkh * kw_ref[...]
            dkw = dkw + jnp.sum(dkh * xh, axis=0, keepdims=True)
            dk_parts.append(rk * (dxh - xh * jnp.mean(dxh * xh, axis=-1, keepdims=True)))
        dkw_ref[...] = dkw
        dkv = jnp.concatenate(dk_parts + [dv_ref[...]], axis=1).astype(BF16)
        dw_ref[...] = _dot(memn_ref[...], dkv, 0, 0)
        dmem_n = _dot(dkv, w_ref[...], 1, 1)
        xb = mem_ref[...]
        r = lax.rsqrt(jnp.mean(xb * xb, axis=-1, keepdims=True) + RMS_EPS)
        dmw_ref[...] = jnp.sum(dmem_n * (xb * r), axis=0, keepdims=True)

    return pl.pallas_call(
        body, name="mem_kv_bwd",
        out_shape=[jax.ShapeDtypeStruct((D_MODEL, 2 * D_MODEL), F32), jax.ShapeDtypeStruct((1, MEM_DH), F32),
                   jax.ShapeDtypeStruct((1, D_MODEL), F32)],
        compiler_params=_cparams(),
    )(mem, mem_norm_w, w_mkv, mk_norm_w, mem_n, kpre, dkn, dv)


def _xattn_head(qh, kn_h, qw):
    rq = lax.rsqrt(jnp.mean(qh * qh, axis=-1, keepdims=True) + RMS_EPS)
    xh = qh * rq
    qn = (xh * qw).astype(BF16)
    s = _dot(qn, kn_h, 1, 1) * (1.0 / 16.0)
    e = jnp.exp(s - jnp.max(s, axis=-1, keepdims=True))
    p = e / jnp.sum(e, axis=-1, keepdims=True)
    return rq, xh, qn, p


def _xattn_fwd(qm, kn, v, mq_norm_w, bm):
    def fn(qb, knb, vb, qw):
        outs = []
        for h in range(MEM_HEADS):
            sl = slice(MEM_DH * h, MEM_DH * (h + 1))
            _, _, _, p = _xattn_head(qb[:, sl], knb[:, sl], qw)
            outs.append(_dot(p, vb[:, sl]))
        return (jnp.concatenate(outs, axis=1),)
    return _rowcall("xattn_fwd", fn, [(qm, D_MODEL, 0)], [kn, v, mq_norm_w], [(D_MODEL, BF16)], [], bm,
                    qm.shape[0])[0]


def _xattn_bwd(qm, kn, v, mq_norm_w, do, bm):
    def fn(qb, dob, knb, vb, qw):
        dq_parts, dkn_parts, dv_parts = [], [], []
        dqw = jnp.zeros((1, MEM_DH), F32)
        for h in range(MEM_HEADS):
            sl = slice(MEM_DH * h, MEM_DH * (h + 1))
            rq, xh, qn, p = _xattn_head(qb[:, sl], knb[:, sl], qw)
            doh = dob[:, sl].astype(BF16)
            dp = _dot(doh, vb[:, sl], 1, 1)
            ds = (p * (dp - jnp.sum(dp * p, axis=-1, keepdims=True)) * (1.0 / 16.0)).astype(BF16)
            dqn = _dot(ds, knb[:, sl])
            dkn_parts.append(_dot(ds, qn, 0, 0))
            dv_parts.append(_dot(p, doh, 0, 0))
            dqw = dqw + jnp.sum(dqn * xh, axis=0, keepdims=True)
            dxh = dqn * qw
            dq_parts.append(rq * (dxh - xh * jnp.mean(dxh * xh, axis=-1, keepdims=True)))
        return (jnp.concatenate(dq_parts, axis=1), jnp.concatenate(dkn_parts, axis=1),
                jnp.concatenate(dv_parts, axis=1), dqw)
    return _rowcall("xattn_bwd", fn, [(qm, D_MODEL, 0), (do, D_MODEL, 0)], [kn, v, mq_norm_w],
                    [(D_MODEL, BF16)], [(MEM_LEN, D_MODEL), (MEM_LEN, D_MODEL), (1, MEM_DH)], bm, qm.shape[0])


FF_BN = 1408
FF_NB = D_FF // FF_BN


def _ffn_up(h3, w_gate_up, rows, bm):
    def body(h_ref, wg_ref, wu_ref, gate_ref, up_ref, act_ref):
        hb = h_ref[...]
        gate = _dot(hb, wg_ref[...])
        up = _dot(hb, wu_ref[...])
        gate_ref[...] = gate.astype(BF16)
        up_ref[...] = up.astype(BF16)
        act_ref[...] = (gate * _sigmoid(gate) * up).astype(BF16)

    out_blk = pl.BlockSpec((bm, FF_BN), lambda i, j: (i, j))
    return pl.pallas_call(
        body, name="ffn_up", grid=(rows // bm, FF_NB),
        in_specs=[pl.BlockSpec((bm, D_MODEL), lambda i, j: (i, 0)),
                  pl.BlockSpec((D_MODEL, FF_BN), lambda i, j: (0, j)),
                  pl.BlockSpec((D_MODEL, FF_BN), lambda i, j: (0, FF_NB + j))],
        out_specs=[out_blk, out_blk, out_blk],
        out_shape=[jax.ShapeDtypeStruct((rows, D_FF), BF16)] * 3,
        compiler_params=_cparams("parallel", "arbitrary"),
    )(h3, w_gate_up, w_gate_up)


def _ffn_act_bwd(dy, w_down, gate, up, rows, bm):
    def body(dy_ref, wd_ref, gate_ref, up_ref, o_ref):
        dact = _dot(dy_ref[...], wd_ref[...], 1, 1)
        g, u = gate_ref[...].astype(F32), up_ref[...].astype(F32)
        sg = _sigmoid(g)
        o_ref[0] = (dact * u * (sg * (1.0 + g * (1.0 - sg)))).astype(BF16)
        o_ref[1] = (dact * (g * sg)).astype(BF16)

    blk = pl.BlockSpec((bm, FF_BN), lambda i, j: (i, j))
    return pl.pallas_call(
        body, name="ffn_act_bwd", grid=(rows // bm, FF_NB),
        in_specs=[pl.BlockSpec((bm, D_MODEL), lambda i, j: (i, 0)),
                  pl.BlockSpec((FF_BN, D_MODEL), lambda i, j: (j, 0)), blk, blk],
        out_specs=pl.BlockSpec((2, bm, FF_BN), lambda i, j: (0, i, j)),
        out_shape=jax.ShapeDtypeStruct((2, rows, D_FF), BF16),
        compiler_params=_cparams("parallel", "arbitrary"),
    )(dy, w_down, gate, up)


ROW_SHARDED = ("w_out", "w_mq", "w_mo", "w_down")
TAIL_ROWS = 16
AG_FIRST = ("w_in", "w_gk_up")
AG_MID = ("w_out", "w_mq", "w_mkv")
AG_LATE = ("w_mo", "w_gate_up", "w_down")
RS_FFN = ("w_gate_up", "w_down")
RS_MID = ("w_out", "w_mq", "w_mkv", "w_mo")
RS_WIN = ("w_in", "w_gk_up")
ADAM_BLOCK = {"w_in": 256, "w_out": 128, "w_mq": 128, "w_mkv": 256, "w_mo": 128, "w_gate_up": 256,
              "w_down": 176, "w_gk_up": 16}


def _full_weight(name, got):
    r, c = SHARD_SHAPE[name]
    if name in ROW_SHARDED:
        return got.reshape(N_DEV * r, c)
    return got.transpose(1, 0, 2).reshape(r, N_DEV * c)


def _tail_rows(rows):
    return jnp.concatenate(rows + [jnp.zeros((TAIL_ROWS - len(rows), 1024), F32)], axis=0)


def _split_for_owners(name, full):
    r, c = SHARD_SHAPE[name]
    if name in ROW_SHARDED:
        return full.reshape(N_DEV, r, c)
    return full.reshape(r, N_DEV, c).transpose(1, 0, 2)


def _repl_row(a):
    flat = a.reshape(-1)
    return jnp.concatenate([flat, jnp.zeros((1024 - flat.shape[0],), flat.dtype)]).reshape(1, 1024)


def _local_step(x, mem, tgt, wf, rp, shards=None, scatter=False):
    rows = x.shape[0]
    bm = min(512, rows)
    bmx = min(256, rows)
    mt = min(MM_TILE, rows)
    kt = min(512, rows)
    w_cat, wgk_pad = wf["w_cat"], wf["wgk_pad"]
    wg_t = jnp.tile(rp["gla_norm_w"], (1, GLA_HEADS))
    ws_t = jnp.tile(rp["sb_norm_w"], (1, SB_HEADS))
    lane = jnp.arange(GLA_V_W)
    grp_g = (lane[:, None] // GLA_DV == lane[None, :] // GLA_DV).astype(BF16)
    grp_s = (lane[:, None] // SB_DH == lane[None, :] // SB_DH).astype(BF16)

    h1 = _rms_fwd("mix_norm_fwd", x, rp["mix_norm_w"], bm)
    proj = _matmul("in_proj", h1, w_cat, "nn", rows, PROJ_W, D_MODEL, F32, mt, 640, D_MODEL)
    gk = _gate_fwd(proj, wgk_pad, rp["b_gk"], bm)
    if shards is None:
        o_g, s_all = _gla_fwd(proj, gk, rows)
        (o_s,) = _sb_fwd(proj, rows)
    else:
        mine = lambda names: [shards[n].astype(BF16) for n in names]
        o_g, s_all, got_mid = _gla_fwd(proj, gk, rows, _Exchange(mine(AG_MID), scatter=False))
        o_s, got_late = _sb_fwd(proj, rows, _Exchange(mine(AG_LATE), scatter=False))
        wf = {**wf, **{n: _full_weight(n, a) for n, a in zip(AG_MID + AG_LATE, got_mid + got_late)}}
    cat = _mix_out_fwd(o_g, proj, o_s, wg_t, ws_t, grp_g, grp_s, bm)
    x1 = _matmul("out_proj", cat, wf["w_out"], "nn", rows, D_MODEL, D_MODEL, F32, mt, MM_TILE, D_MODEL, residual=x)
    h2 = _rms_fwd("xattn_norm_fwd", x1, rp["xattn_norm_w"], bm)
    qm = _matmul("mq_proj", h2, wf["w_mq"], "nn", rows, D_MODEL, D_MODEL, F32, mt, MM_TILE, D_MODEL)
    mem_n, kpre, kn, v_m = _mem_kv_fwd(mem, rp["mem_norm_w"], wf["w_mkv"], rp["mk_norm_w"])
    o_m = _xattn_fwd(qm, kn, v_m, rp["mq_norm_w"], bmx)
    x2 = _matmul("mo_proj", o_m, wf["w_mo"], "nn", rows, D_MODEL, D_MODEL, F32, mt, MM_TILE, D_MODEL, residual=x1)
    h3 = _rms_fwd("ffn_norm_fwd", x2, rp["ffn_norm_w"], bm)
    gate, up, act = _ffn_up(h3, wf["w_gate_up"], rows, bm)
    y = _matmul("ffn_down", act, wf["w_down"], "nn", rows, D_MODEL, D_FF, F32, mt, MM_TILE, FF_BN, residual=x2)
    dy, sq = _loss_kernel(y, tgt, bm)

    g = {}
    dgu = _ffn_act_bwd(dy, wf["w_down"], gate, up, rows, bm)
    g["w_down"] = _matmul("grad_w_down", act, dy, "tn", D_FF, D_MODEL, rows, F32, FF_BN, MM_TILE, kt)
    nkb = FF_NB
    dh3 = _matmul("ffn_up_bwd", dgu, wf["w_gate_up"], "nt", rows, D_MODEL, 2 * D_FF, F32, mt, MM_TILE, FF_BN,
                  a_spec=pl.BlockSpec((None, mt, FF_BN), lambda i, j, kk: (kk // nkb, i, kk % nkb)))
    g["w_gate_up"] = _matmul(
        "grad_w_gate_up", h3, dgu, "tn", D_MODEL, 2 * D_FF, rows, F32, MM_TILE, FF_BN, kt,
        b_spec=pl.BlockSpec((None, kt, FF_BN), lambda i, j, kk: (j // nkb, kk, j % nkb)))
    dx2, g["ffn_norm_w"] = _rms_bwd("ffn_norm_bwd", x2, rp["ffn_norm_w"], dh3, dy, bm)

    do_m = _matmul("mo_proj_bwd", dx2, wf["w_mo"], "nt", rows, D_MODEL, D_MODEL, BF16, mt, MM_TILE, D_MODEL)
    g["w_mo"] = _matmul("grad_w_mo", o_m, dx2, "tn", D_MODEL, D_MODEL, rows, F32, MM_TILE, MM_TILE, kt)
    dqm, dkn, dv_m, g["mq_norm_w"] = _xattn_bwd(qm, kn, v_m, rp["mq_norm_w"], do_m, bmx)
    g["w_mkv"], g["mk_norm_w"], g["mem_norm_w"] = _mem_kv_bwd(
        mem, rp["mem_norm_w"], wf["w_mkv"], rp["mk_norm_w"], mem_n, kpre, dkn, dv_m)
    dh2 = _matmul("mq_proj_bwd", dqm, wf["w_mq"], "nt", rows, D_MODEL, D_MODEL, F32, mt, MM_TILE, D_MODEL)
    g["w_mq"] = _matmul("grad_w_mq", h2, dqm, "tn", D_MODEL, D_MODEL, rows, F32, MM_TILE, MM_TILE, kt)
    dx1, g["xattn_norm_w"] = _rms_bwd("xattn_norm_bwd", x1, rp["xattn_norm_w"], dh2, dx2, bm)

    dcat = _matmul("out_proj_bwd", dx1, wf["w_out"], "nt", rows, D_MODEL, D_MODEL, F32, mt, MM_TILE, D_MODEL)
    g["w_out"] = _matmul("grad_w_out", cat, dx1, "tn", D_MODEL, D_MODEL, rows, F32, MM_TILE, MM_TILE, kt)
    do_g, dgg, do_s, dwg, dws = _mix_out_bwd(dcat, o_g, proj, o_s, wg_t, ws_t, grp_g, grp_s, bm)
    g["gla_norm_w"] = dwg.reshape(GLA_HEADS, GLA_DV).sum(axis=0, keepdims=True)
    g["sb_norm_w"] = dws.reshape(SB_HEADS, SB_DH).sum(axis=0, keepdims=True)
    recv = {}
    to_owners = lambda names: [_split_for_owners(n, g[n].astype(BF16)) for n in names]
    if scatter:
        dq_s, dk_s, dv_s, got = _sb_bwd(proj, o_s, do_s, rows, _Exchange(to_owners(RS_FFN), scatter=True))
        recv.update(zip(RS_FFN, got))
        dq_g, dk_g, dv_g, dgk, got = _gla_bwd(proj, gk, do_g, s_all, rows, _Exchange(to_owners(RS_MID), scatter=True))
        recv.update(zip(RS_MID, got))
    else:
        dq_s, dk_s, dv_s = _sb_bwd(proj, o_s, do_s, rows)
        dq_g, dk_g, dv_g, dgk = _gla_bwd(proj, gk, do_g, s_all, rows)
    dlr, dwgk, g["b_gk"] = _gate_bwd(proj, wgk_pad, rp["b_gk"], dgk, bm)
    g["w_gk_up"] = dwgk[:GATE_RANK]
    dproj = jnp.concatenate([dq_g.astype(BF16), dk_g.astype(BF16), dv_g.astype(BF16), dgg, dq_s.astype(BF16),
                             dk_s.astype(BF16), dv_s.astype(BF16), dlr], axis=1)
    dw_cat = _matmul("grad_w_in", h1, dproj, "tn", D_MODEL, PROJ_W, rows, F32, MM_TILE, 640, kt)
    g["w_in"] = jnp.concatenate([dw_cat[:, :C_QS], dw_cat[:, C_LR:C_LR + GATE_RANK], dw_cat[:, C_QS:C_LR]], axis=1)
    if scatter:
        dh1, got = _matmul("in_proj_bwd", dproj, w_cat, "nt", rows, D_MODEL, PROJ_W, F32, mt, MM_TILE, 640,
                           exchange=_Exchange(to_owners(RS_WIN), scatter=True))
        recv.update(zip(RS_WIN, got))
    else:
        dh1 = _matmul("in_proj_bwd", dproj, w_cat, "nt", rows, D_MODEL, PROJ_W, F32, mt, MM_TILE, 640)
    dx, g["mix_norm_w"] = _rms_bwd("mix_norm_bwd", x, rp["mix_norm_w"], dh1, dx1, bm)
    return sq, dx, g, recv


def _first_weights(got):
    w_in = _full_weight("w_in", got[0])
    lr_end = C_QS + GATE_RANK
    w_cat = jnp.concatenate([w_in[:, :C_QS], w_in[:, lr_end:], w_in[:, C_QS:lr_end],
                             jnp.zeros((D_MODEL, PROJ_W - D_IN), BF16)], axis=1)
    wgk = _full_weight("w_gk_up", got[1])
    return {"w_cat": w_cat, "wgk_pad": jnp.concatenate([wgk, jnp.zeros((LANES - GATE_RANK, GLA_QK_W), BF16)], axis=0)}


def kernel(x, mem, mix_norm_w, w_in, w_gk_up, b_gk, gla_norm_w, sb_norm_w, w_out, xattn_norm_w, mem_norm_w, w_mq, w_mkv, mq_norm_w, mk_norm_w, w_mo, ffn_norm_w, w_gate_up, w_down, loss_target, m_mix_norm_w, m_w_in, m_w_gk_up, m_b_gk, m_gla_norm_w, m_sb_norm_w, m_w_out, m_xattn_norm_w, m_mem_norm_w, m_w_mq, m_w_mkv, m_mq_norm_w, m_mk_norm_w, m_w_mo, m_ffn_norm_w, m_w_gate_up, m_w_down, v_mix_norm_w, v_w_in, v_w_gk_up, v_b_gk, v_gla_norm_w, v_sb_norm_w, v_w_out, v_xattn_norm_w, v_mem_norm_w, v_w_mq, v_w_mkv, v_mq_norm_w, v_mk_norm_w, v_w_mo, v_ffn_norm_w, v_w_gate_up, v_w_down):
    given = dict(locals())
    w = {n: given[n][0] for n in WEIGHTS}
    m = {n: given["m_" + n][0] for n in WEIGHTS}
    v = {n: given["v_" + n][0] for n in WEIGHTS}

    wf = _first_weights(_exchange_call("gather_first_weights", [w[n].astype(BF16) for n in AG_FIRST], scatter=False))
    rp = {n: w[n].reshape(1, -1) for n in REPL}
    sq, dx, g, recv = _local_step(x[0], mem[0], loss_target[0], wf, rp, shards=w, scatter=True)

    loss_row = _repl_row(jnp.sum(sq).reshape(1) * (0.5 / D_MODEL))
    tail = _tail_rows([_repl_row(g[n]) for n in REPL] + [loss_row])
    (recv_tail,) = _exchange_call("scatter_tail_gradients", [jnp.broadcast_to(tail[None], (N_DEV, TAIL_ROWS, 1024))],
                                  scatter=True)
    results = {n: _adamw("sum_adamw_" + n, recv[n], w[n], m[n], v[n], ADAM_BLOCK[n]) for n in SHARD_SHAPE}
    tail_out = _adamw("sum_adamw_tail", recv_tail, *[_tail_rows([_repl_row(t[n]) for n in REPL]) for t in (w, m, v)],
                      TAIL_ROWS)

    def output(kind, name):
        if name in results:
            return results[name][kind][None]
        return tail_out[kind][REPL.index(name), :w[name].shape[-1]].reshape(1, -1)

    loss = tail_out[0][len(REPL), 0]
    outs = [loss, dx[None]]
    for kind in range(4):
        outs += [output(kind, n) for n in WEIGHTS]
    return tuple(outs)
```

```python
import functools
import math

import jax
import jax.numpy as jnp
from jax import lax
from jax.experimental import pallas as pl
from jax.experimental.pallas import tpu as pltpu

F32 = jnp.float32
BF16 = jnp.bfloat16

N_DEV = 8
D_MODEL = 1024
GLA_HEADS = 4
GLA_DK = 64
GLA_DV = 128
GLA_CHUNK = 64
GLA_STEP_CHUNKS = 4
GLA_QK_W = GLA_HEADS * GLA_DK
GLA_V_W = GLA_HEADS * GLA_DV
GATE_RANK = 16
SB_HEADS = 8
SB_DH = 64
SB_W = SB_HEADS * SB_DH
SB_BLK = 128
SB_QT = 1024
SB_DEAD = -104.0
MEM_LEN = 256
MEM_HEADS = 4
MEM_DH = 256
D_FF = 2816
D_IN = 3088
RMS_EPS = 1e-6
LANES = 128

PROJ_W = 3200
C_QG, C_KG, C_VG, C_GG, C_QS, C_KS, C_VS, C_LR = 0, 256, 512, 1024, 1536, 2048, 2560, 3072

ADAM_LR, ADAM_B1, ADAM_B2, ADAM_EPS, ADAM_WD, ADAM_STEP = 0.001, 0.9, 0.999, 1e-08, 0.01, 10

SHARD_SHAPE = {"w_in": (1024, 386), "w_out": (128, 1024), "w_mq": (128, 1024), "w_mkv": (1024, 256),
               "w_mo": (128, 1024), "w_gate_up": (1024, 704), "w_down": (352, 1024), "w_gk_up": (16, 32)}
REPL = ("mix_norm_w", "b_gk", "gla_norm_w", "sb_norm_w", "xattn_norm_w", "mem_norm_w", "mq_norm_w",
        "mk_norm_w", "ffn_norm_w")
WEIGHTS = ("mix_norm_w", "w_in", "w_gk_up", "b_gk", "gla_norm_w", "sb_norm_w", "w_out", "xattn_norm_w",
           "mem_norm_w", "w_mq", "w_mkv", "mq_norm_w", "mk_norm_w", "w_mo", "ffn_norm_w", "w_gate_up", "w_down")
VMEM_LIMIT = 56 * 1024 * 1024
MM_TILE = 1024


def _cparams(*sem):
    return pltpu.CompilerParams(dimension_semantics=sem if sem else None, vmem_limit_bytes=VMEM_LIMIT)


def _dot(a, b, ca=1, cb=0):
    return lax.dot_general(a.astype(BF16), b.astype(BF16), (((ca,), (cb,)), ((), ())),
                           preferred_element_type=F32)


def _split(x, parts):
    out = []
    for _ in range(parts - 1):
        hi = x.astype(BF16)
        out.append(hi)
        x = x - hi.astype(F32)
    out.append(x.astype(BF16))
    return out


def _dot_lhs_exact(x, m, ca=1, cb=0, parts=3):
    acc = None
    for p in _split(x, parts):
        t = _dot(p, m, ca, cb)
        acc = t if acc is None else acc + t
    return acc


def _dot_rhs_exact(m, x, ca=1, cb=0, parts=3):
    acc = None
    for p in _split(x, parts):
        t = _dot(m, p, ca, cb)
        acc = t if acc is None else acc + t
    return acc


def _dot3(a, b, ca=1, cb=0):
    a_hi, a_lo = _split(a, 2)
    b_hi, b_lo = _split(b, 2)
    return _dot(a_hi, b_hi, ca, cb) + (_dot(a_hi, b_lo, ca, cb) + _dot(a_lo, b_hi, ca, cb))


def _log_sigmoid(z):
    return jnp.minimum(z, 0.0) - jnp.log(1.0 + jnp.exp(-jnp.abs(z)))


def _sigmoid(z):
    e = jnp.exp(-jnp.abs(z))
    return jnp.where(z >= 0, 1.0, e) / (1.0 + e)


def _iota2(shape, dim):
    return lax.broadcasted_iota(jnp.int32, shape, dim)


def _rowcall(name, fn, row_ins, full_ins, row_outs, acc_outs, bm, rows):
    n_in = len(row_ins) + len(full_ins)
    n_row = len(row_outs)

    def body(*refs):
        ins, outs = refs[:n_in], refs[n_in:]
        res = fn(*[r[...] for r in ins])
        for r, v in zip(outs[:n_row], res[:n_row]):
            r[...] = v.astype(r.dtype)
        first = pl.program_id(0) == 0
        for r, v in zip(outs[n_row:], res[n_row:]):
            def init(r=r):
                r[...] = jnp.zeros(r.shape, r.dtype)
            pl.when(first)(init)
            r[...] += v

    in_specs = [pl.BlockSpec((bm, w), functools.partial(lambda i, c: (i, c), c=c)) for _, w, c in row_ins]
    in_specs += [pl.BlockSpec(a.shape, lambda i: (0, 0)) for a in full_ins]
    out_specs = [pl.BlockSpec((bm, w), lambda i: (i, 0)) for w, _ in row_outs]
    out_specs += [pl.BlockSpec(s, lambda i: (0, 0)) for s in acc_outs]
    out_shape = [jax.ShapeDtypeStruct((rows, w), dt) for w, dt in row_outs]
    out_shape += [jax.ShapeDtypeStruct(s, F32) for s in acc_outs]
    return pl.pallas_call(
        body, name=name, grid=(rows // bm,), in_specs=in_specs, out_specs=out_specs, out_shape=out_shape,
        compiler_params=_cparams("arbitrary"),
    )(*[a for a, _, _ in row_ins], *full_ins)


def _matmul(name, a, b, mode, m, n, k, out_dtype, bm, bn, bk, residual=None, a_spec=None, b_spec=None,
            exchange=None):
    bm, bn, bk = min(bm, m), min(bn, n), min(bk, k)
    nk = k // bk
    ca, cb = {"nn": (1, 0), "nt": (1, 1), "tn": (0, 0)}[mode]
    if a_spec is None:
        a_spec = (pl.BlockSpec((bk, bm), lambda i, j, kk: (kk, i)) if mode == "tn"
                  else pl.BlockSpec((bm, bk), lambda i, j, kk: (i, kk)))
    if b_spec is None:
        b_spec = (pl.BlockSpec((bn, bk), lambda i, j, kk: (j, kk)) if mode == "nt"
                  else pl.BlockSpec((bk, bn), lambda i, j, kk: (kk, j)))
    has_res = residual is not None

    def body(*refs):
        a_ref, b_ref = refs[0], refs[1]
        res_ref = refs[2] if has_res else None
        o_ref = refs[2 + has_res]
        part = _dot(a_ref[...], b_ref[...], ca, cb)

        def finish(total):
            if has_res:
                total = total + res_ref[...]
            o_ref[...] = total.astype(o_ref.dtype)

        if nk == 1:
            finish(part)
        else:
            acc_ref = refs[3 + has_res]
            kk = pl.program_id(2)

            @pl.when(kk == 0)
            def _():
                acc_ref[...] = part

            @pl.when(kk > 0)
            def _():
                acc_ref[...] += part

            @pl.when(kk == nk - 1)
            def _():
                finish(acc_ref[...])

    in_specs = [a_spec, b_spec]
    args = [a, b]
    if has_res:
        in_specs.append(pl.BlockSpec((bm, bn), lambda i, j, kk: (i, j)))
        args.append(residual)
    sem = ("parallel", "parallel", "arbitrary") if exchange is None else ("arbitrary",) * 3
    res = _hosted_call(
        body, exchange, grid=(m // bm, n // bn, nk), in_specs=in_specs,
        out_specs=[pl.BlockSpec((bm, bn), lambda i, j, kk: (i, j))],
        out_shape=[jax.ShapeDtypeStruct((m, n), out_dtype)],
        scratch_shapes=[pltpu.VMEM((bm, bn), F32)] if nk > 1 else [],
        args=args, name=name, compiler_params=_cparams(*sem))
    return res[0] if exchange is None else res


def _peer(mask):
    x, y, c = lax.axis_index("x"), lax.axis_index("y"), lax.axis_index("c")
    mx, my, mc = (mask >> 2) & 1, (mask >> 1) & 1, mask & 1
    px, py, pc = (1 - x if mx else x), (1 - y if my else y), (1 - c if mc else c)
    return (px, py, pc), 4 * px + 2 * py + pc


def _my_index():
    return 4 * lax.axis_index("x") + 2 * lax.axis_index("y") + lax.axis_index("c")


class _Exchange:
    def __init__(self, srcs, scatter):
        self.srcs, self.scatter, n = list(srcs), scatter, len(srcs)
        self.in_specs = [pl.BlockSpec(memory_space=pl.ANY)] * n
        self.out_specs = [pl.BlockSpec(memory_space=pl.ANY)] * n
        self.out_shapes = [jax.ShapeDtypeStruct((N_DEV,) + s.shape[-2:], s.dtype) for s in self.srcs]
        self.scratch = [pltpu.SemaphoreType.DMA((n * (N_DEV - 1),)), pltpu.SemaphoreType.DMA((n * (N_DEV - 1),)),
                        pltpu.SemaphoreType.DMA((n,))]

    def _copies(self, src_refs, out_refs, sems):
        send_sems, recv_sems, local_sems = sems
        me = _my_index()
        copies = []
        for a, (src_ref, out_ref) in enumerate(zip(src_refs, out_refs)):
            copies.append(pltpu.make_async_copy(src_ref.at[me] if self.scatter else src_ref, out_ref.at[me],
                                                local_sems.at[a]))
            for mask in range(1, N_DEV):
                peer, peer_index = _peer(mask)
                slot = a * (N_DEV - 1) + mask - 1
                copies.append(pltpu.make_async_remote_copy(
                    src_ref=src_ref.at[peer_index] if self.scatter else src_ref, dst_ref=out_ref.at[me],
                    send_sem=send_sems.at[slot], recv_sem=recv_sems.at[slot],
                    device_id=peer, device_id_type=pl.DeviceIdType.MESH))
        return copies

    def start(self, src_refs, out_refs, sems):
        for cp in self._copies(src_refs, out_refs, sems):
            cp.start()

    def wait(self, src_refs, out_refs, sems):
        for cp in self._copies(src_refs, out_refs, sems):
            cp.wait()


def _hosted_call(body, ex, grid, in_specs, out_specs, out_shape, scratch_shapes, args, **kw):
    if ex is None:
        return pl.pallas_call(body, grid=grid, in_specs=in_specs, out_specs=out_specs, out_shape=out_shape,
                              scratch_shapes=scratch_shapes, **kw)(*args)
    n_in, n_out, n_scr, n_x = len(in_specs), len(out_specs), len(scratch_shapes), len(ex.srcs)

    def hosted(*refs):
        ins, src_refs = refs[:n_in], refs[n_in:n_in + n_x]
        refs = refs[n_in + n_x:]
        outs, out_refs = refs[:n_out], refs[n_out:n_out + n_x]
        refs = refs[n_out + n_x:]
        scr, sems = refs[:n_scr], refs[n_scr:]
        ids = [pl.program_id(a) for a in range(len(grid))]
        first = functools.reduce(jnp.logical_and, [p == 0 for p in ids])
        last = functools.reduce(jnp.logical_and, [p == n - 1 for p, n in zip(ids, grid)])

        @pl.when(first)
        def _():
            ex.start(src_refs, out_refs, sems)

        body(*ins, *outs, *scr)

        @pl.when(last)
        def _():
            ex.wait(src_refs, out_refs, sems)

    res = pl.pallas_call(
        hosted, grid=grid, in_specs=list(in_specs) + ex.in_specs, out_specs=list(out_specs) + ex.out_specs,
        out_shape=list(out_shape) + ex.out_shapes, scratch_shapes=list(scratch_shapes) + ex.scratch, **kw,
    )(*args, *ex.srcs)
    return tuple(res[:n_out]) + (list(res[n_out:]),)


def _exchange_call(name, srcs, scatter):
    ex = _Exchange(srcs, scatter)
    n_x = len(ex.srcs)

    def body(*refs):
        src_refs, out_refs, sems = refs[:n_x], refs[n_x:2 * n_x], refs[2 * n_x:]
        ex.start(src_refs, out_refs, sems)
        ex.wait(src_refs, out_refs, sems)

    return list(pl.pallas_call(
        body, name=name, in_specs=ex.in_specs, out_specs=ex.out_specs, out_shape=ex.out_shapes,
        scratch_shapes=ex.scratch, compiler_params=pltpu.CompilerParams(has_side_effects=True),
    )(*ex.srcs))


def _adamw(name, recv, w, m, v, block):
    rows, cols = w.shape
    c1 = 1.0 - ADAM_B1 ** ADAM_STEP
    c2 = 1.0 - ADAM_B2 ** ADAM_STEP

    def body(r_ref, w_ref, m_ref, v_ref, g_out, d_out, m_out, v_out):
        g = r_ref[0].astype(F32)
        for s in range(1, N_DEV):
            g = g + r_ref[s].astype(F32)
        m_new = ADAM_B1 * m_ref[...] + (1.0 - ADAM_B1) * g
        v_new = ADAM_B2 * v_ref[...] + (1.0 - ADAM_B2) * (g * g)
        m_hat = m_new / c1
        v_hat = v_new / c2
        g_out[...] = g
        d_out[...] = -ADAM_LR * (m_hat / (jnp.sqrt(v_hat) + ADAM_EPS) + ADAM_WD * w_ref[...])
        m_out[...] = m_new
        v_out[...] = v_new

    blk = pl.BlockSpec((block, cols), lambda i: (i, 0))
    return pl.pallas_call(
        body, name=name, grid=(rows // block,),
        in_specs=[pl.BlockSpec((N_DEV, block, cols), lambda i: (0, i, 0)), blk, blk, blk],
        out_specs=[blk] * 4, out_shape=[jax.ShapeDtypeStruct((rows, cols), F32)] * 4,
        compiler_params=_cparams("parallel"),
    )(recv, w, m, v)


def _rms_fwd(name, x, w, bm):
    def fn(xb, wb):
        r = lax.rsqrt(jnp.mean(xb * xb, axis=-1, keepdims=True) + RMS_EPS)
        return (xb * r * wb,)
    return _rowcall(name, fn, [(x, D_MODEL, 0)], [w], [(D_MODEL, BF16)], [], bm, x.shape[0])[0]


def _rms_bwd(name, x, w, dh, dres, bm):
    def fn(xb, dhb, drb, wb):
        r = lax.rsqrt(jnp.mean(xb * xb, axis=-1, keepdims=True) + RMS_EPS)
        xh = xb * r
        dxh = dhb.astype(F32) * wb
        dx = drb + r * (dxh - xh * jnp.mean(dxh * xh, axis=-1, keepdims=True))
        return dx, jnp.sum(dhb.astype(F32) * xh, axis=0, keepdims=True)
    return _rowcall(name, fn, [(x, D_MODEL, 0), (dh, D_MODEL, 0), (dres, D_MODEL, 0)], [w],
                    [(D_MODEL, F32)], [(1, D_MODEL)], bm, x.shape[0])


def _gate_fwd(proj, wgk_pad, b_gk, bm):
    def fn(lr, wg, bg):
        z = _dot(lr, wg) + bg
        return (_log_sigmoid(z) * (1.0 / 16.0),)
    return _rowcall("gla_gate_fwd", fn, [(proj, LANES, C_LR // LANES)], [wgk_pad, b_gk],
                    [(GLA_QK_W, F32)], [], bm, proj.shape[0])[0]


def _gate_bwd(proj, wgk_pad, b_gk, dgk, bm):
    def fn(lr, dg, wg, bg):
        z = _dot(lr, wg) + bg
        dz = dg * _sigmoid(-z) * (1.0 / 16.0)
        return _dot(dz, wg, 1, 1), _dot(lr, dz, 0, 0), jnp.sum(dz, axis=0, keepdims=True)
    return _rowcall("gla_gate_bwd", fn, [(proj, LANES, C_LR // LANES), (dgk, GLA_QK_W, 0)], [wgk_pad, b_gk],
                    [(LANES, BF16)], [(LANES, GLA_QK_W), (1, GLA_QK_W)], bm, proj.shape[0])


def _group_mean(x, g, size):
    return _dot_lhs_exact(x, g, parts=2) * (1.0 / size)


def _mix_out_fwd(o_g, proj, o_s, wg_t, ws_t, grp_g, grp_s, bm):
    def fn(og, gg, os_, wg, ws, gmat, smat):
        rg = lax.rsqrt(_group_mean(og * og, gmat, GLA_DV) + RMS_EPS)
        yg = og * rg * wg * (gg * _sigmoid(gg))
        rs = lax.rsqrt(_group_mean(os_ * os_, smat, SB_DH) + RMS_EPS)
        ys = os_ * rs * ws
        return (jnp.concatenate([yg, ys], axis=1),)
    return _rowcall("mix_out_fwd", fn, [(o_g, GLA_V_W, 0), (proj, GLA_V_W, C_GG // GLA_V_W), (o_s, SB_W, 0)],
                    [wg_t, ws_t, grp_g, grp_s], [(D_MODEL, BF16)], [], bm, o_g.shape[0])[0]


def _mix_out_bwd(dcat, o_g, proj, o_s, wg_t, ws_t, grp_g, grp_s, bm):
    def fn(dyg, dys, og, gg, os_, wg, ws, gmat, smat):
        dyg = dyg.astype(F32)
        dys = dys.astype(F32)
        rg = lax.rsqrt(_group_mean(og * og, gmat, GLA_DV) + RMS_EPS)
        xh = og * rg
        sg = _sigmoid(gg)
        silu = gg * sg
        dxh = dyg * wg * silu
        dgg = dyg * xh * wg * (sg * (1.0 + gg * (1.0 - sg)))
        dwg = jnp.sum(dyg * xh * silu, axis=0, keepdims=True)
        dog = rg * (dxh - xh * _group_mean(dxh * xh, gmat, GLA_DV))
        rs = lax.rsqrt(_group_mean(os_ * os_, smat, SB_DH) + RMS_EPS)
        xs = os_ * rs
        dxs = dys * ws
        dws = jnp.sum(dys * xs, axis=0, keepdims=True)
        dos = rs * (dxs - xs * _group_mean(dxs * xs, smat, SB_DH))
        return dog, dgg, dos, dwg, dws
    return _rowcall("mix_out_bwd", fn,
                    [(dcat, GLA_V_W, 0), (dcat, SB_W, 1), (o_g, GLA_V_W, 0), (proj, GLA_V_W, C_GG // GLA_V_W),
                     (o_s, SB_W, 0)],
                    [wg_t, ws_t, grp_g, grp_s], [(GLA_V_W, F32), (GLA_V_W, BF16), (SB_W, F32)],
                    [(1, GLA_V_W), (1, SB_W)], bm, o_g.shape[0])


def _loss_kernel(y, tgt, bm):
    def fn(yb, tb):
        err = yb - tb
        return err * (1.0 / D_MODEL), jnp.sum(err * err, axis=0, keepdims=True)
    return _rowcall("loss_head", fn, [(y, D_MODEL, 0), (tgt, D_MODEL, 0)], [], [(D_MODEL, F32)], [(1, D_MODEL)],
                    bm, y.shape[0])


def _sb_tri(inclusive):
    j, s = _iota2((2 * SB_BLK, 2 * SB_BLK), 0), _iota2((2 * SB_BLK, 2 * SB_BLK), 1)
    j = jnp.where(j >= SB_BLK, j - SB_BLK, j)
    keep = (j >= s) if inclusive else (j > s)
    return ((s >= SB_BLK) | keep).astype(BF16)


def _dot_hilo(x, m2):
    hi, lo = _split(x, 2)
    return _dot(jnp.concatenate([hi, lo], axis=1), m2)


def _sb_mask(n):
    return _iota2((n, SB_BLK), 1) < _iota2((n, SB_BLK), 0)


def _add_rows(full, part, row0, row1):
    pieces = [full[:row0]] if row0 else []
    pieces.append(full[row0:row1] + part)
    if row1 < full.shape[0]:
        pieces.append(full[row1:])
    return pieces[0] if len(pieces) == 1 else jnp.concatenate(pieces, axis=0)


def _sb_visit(tiles, carry, c_slots, qt, diag_start, n_left, left_start):
    for sub in reversed(range(qt // SB_BLK)):
        carry = tiles(diag_start(sub), carry, True, sub * SB_BLK, qt)

    def step(jj, cr):
        return tiles(left_start(jj), cr, False, 0, qt)

    return _sb_sweep(n_left, step, carry, c_slots)


def _sb_sweep(n_tiles, step, carry, c_slots):
    def alive(state):
        jj, carry = state
        c_max = jnp.max(functools.reduce(jnp.maximum, [carry[s] for s in c_slots]))
        return jnp.logical_and(jj < n_tiles, c_max > SB_DEAD)

    def body(state):
        jj, carry = state
        return jj + 1, step(jj, carry)

    return lax.while_loop(alive, body, (jnp.int32(0), carry))[1]


def _row_blocks(n):
    return [slice(r, r + SB_BLK) for r in range(0, n, SB_BLK)]


def _hilo(x):
    hi, lo = _split(x, 2)
    return jnp.concatenate([hi, lo], axis=1)


def _sb_tile(q, k, c, tri_excl, diag):
    blocks = _row_blocks(q.shape[0])
    strict = _sb_mask(SB_BLK) if diag else None
    z = _dot(q, k, 1, 1)
    lbs, pieces = [], []
    for r, rs in enumerate(blocks):
        lb = _log_sigmoid(z[rs])
        l1 = lb - z[rs]
        if diag and r == 0:
            l1 = jnp.where(strict, l1, 0.0)
        lbs.append(lb)
        pieces.append(_hilo(l1))
    sums = _dot(jnp.concatenate(pieces, axis=0), tri_excl)
    a = []
    for r, rs in enumerate(blocks):
        ar = jnp.exp(lbs[r] + sums[rs, :SB_BLK] + c[rs])
        if diag and r == 0:
            ar = jnp.where(strict, ar, 0.0)
        a.append(ar.astype(BF16))
    return lbs, a, sums[:, SB_BLK:]


def _sb_fwd(proj, rows, exchange=None):
    qt = min(SB_QT, rows)
    subs = qt // SB_BLK

    def body(q_ref, k_ref, v_ref, o_ref):
        i = pl.program_id(1)
        tri_excl = _sb_tri(False)
        heads = [slice(SB_DH * hh, SB_DH * (hh + 1)) for hh in range(2)]
        qs = [(q_ref[:, sl] * 0.125).astype(BF16) for sl in heads]

        def tiles(start, carry, diag, row0, row1):
            out = []
            for hh, sl in enumerate(heads):
                o, c = carry[2 * hh], carry[2 * hh + 1]
                k = k_ref[pl.ds(start, SB_BLK), sl].astype(BF16)
                v = v_ref[pl.ds(start, SB_BLK), sl].astype(BF16)
                _, a, dc = _sb_tile(qs[hh][row0:row1], k, c[row0:row1], tri_excl, diag)
                out += [_add_rows(o, _dot(jnp.concatenate(a, axis=0), v), row0, row1), _add_rows(c, dc, row0, row1)]
            return tuple(out)

        carry = (jnp.zeros((qt, SB_DH), F32), jnp.zeros((qt, SB_BLK), F32)) * 2
        carry = _sb_visit(tiles, carry, (1, 3), qt,
                          lambda sub: pl.multiple_of(i * qt + sub * SB_BLK, SB_BLK), i * subs,
                          lambda jj: pl.multiple_of((i * subs - 1 - jj) * SB_BLK, SB_BLK))
        for hh, sl in enumerate(heads):
            o_ref[:, sl] = carry[2 * hh]

    return _hosted_call(
        body, exchange, grid=(SB_HEADS // 2, rows // qt),
        in_specs=[pl.BlockSpec((qt, LANES), lambda h, i: (i, C_QS // LANES + h)),
                  pl.BlockSpec((rows, LANES), lambda h, i: (0, C_KS // LANES + h)),
                  pl.BlockSpec((rows, LANES), lambda h, i: (0, C_VS // LANES + h))],
        out_specs=[pl.BlockSpec((qt, LANES), lambda h, i: (i, h))],
        out_shape=[jax.ShapeDtypeStruct((rows, SB_W), F32)], scratch_shapes=[],
        args=(proj, proj, proj), name="sb_attention_fwd", compiler_params=_cparams("arbitrary", "arbitrary"))


def _sb_bwd(proj, o_s, do_s, rows, exchange=None):
    qt = min(SB_QT, rows)
    subs = qt // SB_BLK

    def body(q_ref, k_ref, v_ref, o_ref, do_ref, dq_ref, dk_ref, dv_ref):
        i = pl.program_id(1)

        @pl.when(i == 0)
        def _():
            dk_ref[...] = jnp.zeros(dk_ref.shape, F32)
            dv_ref[...] = jnp.zeros(dv_ref.shape, F32)

        tri_excl, tri_incl = _sb_tri(False), _sb_tri(True)
        heads = [slice(SB_DH * hh, SB_DH * (hh + 1)) for hh in range(2)]
        qs = [(q_ref[:, sl] * 0.125).astype(BF16) for sl in heads]
        dobs = [do_ref[:, sl].astype(BF16) for sl in heads]
        dsums = [jnp.broadcast_to(jnp.sum(dob.astype(F32) * o_ref[:, sl], axis=1, keepdims=True), (qt, SB_BLK))
                 for dob, sl in zip(dobs, heads)]

        def tiles(start, carry, diag, row0, row1):
            out = []
            strict = _sb_mask(SB_BLK) if diag else None
            for hh, sl in enumerate(heads):
                dq, c, cp = carry[3 * hh:3 * hh + 3]
                q, dob = qs[hh][row0:row1], dobs[hh][row0:row1]
                dsum, cpr = dsums[hh][row0:row1], cp[row0:row1]
                blocks = _row_blocks(q.shape[0])
                k = k_ref[pl.ds(start, SB_BLK), sl].astype(BF16)
                v = v_ref[pl.ds(start, SB_BLK), sl].astype(BF16)
                lbs, a, dc = _sb_tile(q, k, c[row0:row1], tri_excl, diag)
                da = _dot(dob, v, 1, 1)
                ps = [a[r].astype(F32) * da[rs] for r, rs in enumerate(blocks)]
                psums = _dot(jnp.concatenate([_hilo(p) for p in ps], axis=0), tri_incl)
                dzs = []
                for r, rs in enumerate(blocks):
                    left = dsum[rs] - (psums[rs, :SB_BLK] + cpr[rs])
                    dz = ps[r] - jnp.exp(lbs[r]) * (ps[r] + left)
                    if diag and r == 0:
                        dz = jnp.where(strict, dz, 0.0)
                    dzs.append(dz.astype(BF16))
                dzb, ab = jnp.concatenate(dzs, axis=0), jnp.concatenate(a, axis=0)
                dk_ref[pl.ds(start, SB_BLK), sl] += _dot(dzb, q, 0, 0)
                dv_ref[pl.ds(start, SB_BLK), sl] += _dot(ab, dob, 0, 0)
                out += [_add_rows(dq, _dot(dzb, k), row0, row1), _add_rows(c, dc, row0, row1),
                        _add_rows(cp, psums[:, SB_BLK:], row0, row1)]
            return tuple(out)

        zero = jnp.zeros((qt, SB_BLK), F32)
        carry = (jnp.zeros((qt, SB_DH), F32), zero, zero) * 2
        carry = _sb_visit(tiles, carry, (1, 4), qt,
                          lambda sub: pl.multiple_of(i * qt + sub * SB_BLK, SB_BLK), i * subs,
                          lambda jj: pl.multiple_of((i * subs - 1 - jj) * SB_BLK, SB_BLK))
        for hh, sl in enumerate(heads):
            dq_ref[:, sl] = carry[3 * hh] * 0.125

    whole = lambda base: pl.BlockSpec((rows, LANES), functools.partial(lambda h, i, b: (0, b + h), b=base))
    blk = lambda base: pl.BlockSpec((qt, LANES), functools.partial(lambda h, i, b: (i, b + h), b=base))
    return _hosted_call(
        body, exchange, grid=(SB_HEADS // 2, rows // qt),
        in_specs=[blk(C_QS // LANES), whole(C_KS // LANES), whole(C_VS // LANES), blk(0), blk(0)],
        out_specs=[blk(0), whole(0), whole(0)],
        out_shape=[jax.ShapeDtypeStruct((rows, SB_W), F32)] * 3, scratch_shapes=[],
        args=(proj, proj, proj, o_s, do_s), name="sb_attention_bwd",
        compiler_params=_cparams("arbitrary", "arbitrary"))


def _gla_chunk_common(g_all):
    r_i, c_i = _iota2((GLA_CHUNK, GLA_CHUNK), 0), _iota2((GLA_CHUNK, GLA_CHUNK), 1)
    tri = (c_i <= r_i).astype(BF16)
    return _dot_rhs_exact(tri, g_all), r_i, c_i


def _gla_scaled(qh, kh, bh):
    ref = bh[GLA_CHUNK // 2:GLA_CHUNK // 2 + 1, :]
    eq, ek = jnp.exp(bh - ref), jnp.exp(ref - bh)
    mask = _iota2((GLA_CHUNK, GLA_CHUNK), 1) <= _iota2((GLA_CHUNK, GLA_CHUNK), 0)
    return eq, ek, qh * eq, kh * ek, mask


def _gla_fwd(proj, gk, rows, exchange=None):
    n_chunks = rows // GLA_CHUNK
    step_rows = GLA_CHUNK * GLA_STEP_CHUNKS

    def body(q_ref, k_ref, v_ref, g_ref, o_ref, sall_ref, s_scr):
        @pl.when(pl.program_id(0) == 0)
        def _():
            s_scr[...] = jnp.zeros(s_scr.shape, F32)

        ones = jnp.ones((GLA_CHUNK, GLA_DV), BF16)
        states = [s_scr[h] for h in range(GLA_HEADS)]
        for ci in range(GLA_STEP_CHUNKS):
            rs = slice(GLA_CHUNK * ci, GLA_CHUNK * (ci + 1))
            g_all = g_ref[rs, :]
            b_all, _, _ = _gla_chunk_common(g_all)
            for h in range(GLA_HEADS):
                sl = slice(GLA_DK * h, GLA_DK * (h + 1))
                vs = slice(GLA_DV * h, GLA_DV * (h + 1))
                qh, kh, vh = q_ref[rs, sl] * 0.125, k_ref[rs, sl], v_ref[rs, vs]
                bh, gh = b_all[:, sl], g_all[:, sl]
                s = states[h]
                sall_ref[ci, h] = s
                _, _, qs, ks, mask = _gla_scaled(qh, kh, bh)
                a = jnp.where(mask, _dot(qs, ks, 1, 1), 0.0)
                o_ref[rs, vs] = _dot(qh * jnp.exp(bh), s) + _dot(a, vh)
                bl_col = _dot_lhs_exact(gh, ones, 0, 0)
                kd = kh * jnp.exp(bh[GLA_CHUNK - 1:GLA_CHUNK, :] - bh)
                states[h] = jnp.exp(bl_col) * s + _dot(kd, vh, 0, 0)
        for h in range(GLA_HEADS):
            s_scr[h] = states[h]

    c64 = lambda w, base: pl.BlockSpec((step_rows, w), functools.partial(lambda n, b: (n, b), b=base))
    return _hosted_call(
        body, exchange, grid=(rows // step_rows,),
        in_specs=[c64(GLA_QK_W, C_QG // GLA_QK_W), c64(GLA_QK_W, C_KG // GLA_QK_W), c64(GLA_V_W, C_VG // GLA_V_W),
                  c64(GLA_QK_W, 0)],
        out_specs=[c64(GLA_V_W, 0),
                   pl.BlockSpec((GLA_STEP_CHUNKS, GLA_HEADS, GLA_DK, GLA_DV), lambda n: (n, 0, 0, 0))],
        out_shape=[jax.ShapeDtypeStruct((rows, GLA_V_W), F32),
                   jax.ShapeDtypeStruct((n_chunks, GLA_HEADS, GLA_DK, GLA_DV), F32)],
        scratch_shapes=[pltpu.VMEM((GLA_HEADS, GLA_DK, GLA_DV), F32)],
        args=(proj, proj, proj, gk), name="gla_fwd", compiler_params=_cparams("arbitrary"))


def _gla_bwd(proj, gk, do_g, s_all, rows, exchange=None):
    n_chunks = rows // GLA_CHUNK

    def body(q_ref, k_ref, v_ref, g_ref, do_ref, sall_ref, dq_ref, dk_ref, dv_ref, dg_ref, ds_scr):
        @pl.when(pl.program_id(0) == 0)
        def _():
            ds_scr[...] = jnp.zeros(ds_scr.shape, F32)

        ones = jnp.ones((GLA_CHUNK, GLA_DV), BF16)
        ones8 = jnp.ones((8, GLA_DV), F32)
        last_row = _iota2((GLA_CHUNK, GLA_DK), 0) == GLA_CHUNK - 1
        dstates = [ds_scr[h] for h in range(GLA_HEADS)]
        for ci in reversed(range(GLA_STEP_CHUNKS)):
            cs = slice(GLA_CHUNK * ci, GLA_CHUNK * (ci + 1))
            g_all = g_ref[cs, :]
            b_all, r_i, c_i = _gla_chunk_common(g_all)
            triu = (c_i >= r_i).astype(BF16)
            for h in range(GLA_HEADS):
                sl = slice(GLA_DK * h, GLA_DK * (h + 1))
                vs = slice(GLA_DV * h, GLA_DV * (h + 1))
                qh, kh, vh = q_ref[cs, sl] * 0.125, k_ref[cs, sl], v_ref[cs, vs]
                bh, gh = b_all[:, sl], g_all[:, sl]
                doh = do_ref[cs, vs]
                s, ds = sall_ref[ci, h], dstates[h]
                eb = jnp.exp(bh)
                ekd = jnp.exp(bh[GLA_CHUNK - 1:GLA_CHUNK, :] - bh)
                ebl = jnp.exp(_dot_lhs_exact(gh, ones, 0, 0))
                qb, kd = qh * eb, kh * ekd
                dq = _dot(doh, s, 1, 1) * eb
                dk = _dot(vh, ds, 1, 1) * ekd
                dv = _dot(kd, ds)
                dbl = jnp.sum(dk * kh, axis=0, keepdims=True) + _dot3(ones8, ebl * s * ds, 1, 1)[0:1, :]
                eq, ek, qs, ks, mask = _gla_scaled(qh, kh, bh)
                a = jnp.where(mask, _dot(qs, ks, 1, 1), 0.0)
                da = jnp.where(mask, _dot(doh, vh, 1, 1), 0.0)
                dq = dq + _dot(da, ks) * eq
                dk = dk + _dot(da, qs, 0, 0) * ek
                dv = dv + _dot(a, doh, 0, 0)
                db = qh * dq - kh * dk + jnp.where(last_row, dbl, 0.0)
                dq_ref[cs, sl] = dq * 0.125
                dk_ref[cs, sl] = dk
                dv_ref[cs, vs] = dv
                dg_ref[cs, sl] = _dot_rhs_exact(triu, db)
                dstates[h] = _dot(qb, doh, 0, 0) + ebl * ds
        for h in range(GLA_HEADS):
            ds_scr[h] = dstates[h]

    step_rows = GLA_CHUNK * GLA_STEP_CHUNKS
    last = rows // step_rows - 1
    c64 = lambda w, base: pl.BlockSpec((step_rows, w), functools.partial(lambda n, b: (last - n, b), b=base))
    return _hosted_call(
        body, exchange, grid=(rows // step_rows,),
        in_specs=[c64(GLA_QK_W, C_QG // GLA_QK_W), c64(GLA_QK_W, C_KG // GLA_QK_W), c64(GLA_V_W, C_VG // GLA_V_W),
                  c64(GLA_QK_W, 0), c64(GLA_V_W, 0),
                  pl.BlockSpec((GLA_STEP_CHUNKS, GLA_HEADS, GLA_DK, GLA_DV), lambda n: (last - n, 0, 0, 0))],
        out_specs=[c64(GLA_QK_W, 0), c64(GLA_QK_W, 0), c64(GLA_V_W, 0), c64(GLA_QK_W, 0)],
        out_shape=[jax.ShapeDtypeStruct((rows, GLA_QK_W), F32), jax.ShapeDtypeStruct((rows, GLA_QK_W), F32),
                   jax.ShapeDtypeStruct((rows, GLA_V_W), F32), jax.ShapeDtypeStruct((rows, GLA_QK_W), F32)],
        scratch_shapes=[pltpu.VMEM((GLA_HEADS, GLA_DK, GLA_DV), F32)],
        args=(proj, proj, proj, gk, do_g, s_all), name="gla_bwd", compiler_params=_cparams("arbitrary"))


def _mem_kv_fwd(mem, mem_norm_w, w_mkv, mk_norm_w):
    def body(mem_ref, mw_ref, w_ref, kw_ref, memn_ref, kpre_ref, kn_ref, v_ref):
        xb = mem_ref[...]
        r = lax.rsqrt(jnp.mean(xb * xb, axis=-1, keepdims=True) + RMS_EPS)
        mem_n = (xb * r * mw_ref[...]).astype(BF16)
        memn_ref[...] = mem_n
        kv = _dot(mem_n, w_ref[...])
        kpre_ref[...] = kv[:, :D_MODEL]
        v_ref[...] = kv[:, D_MODEL:].astype(BF16)
        for h in range(MEM_HEADS):
            sl = slice(MEM_DH * h, MEM_DH * (h + 1))
            kh = kv[:, sl]
            rk = lax.rsqrt(jnp.mean(kh * kh, axis=-1, keepdims=True) + RMS_EPS)
            kn_ref[:, sl] = (kh * rk * kw_ref[...]).astype(BF16)

    return pl.pallas_call(
        body, name="mem_kv_fwd",
        out_shape=[jax.ShapeDtypeStruct((MEM_LEN, D_MODEL), BF16), jax.ShapeDtypeStruct((MEM_LEN, D_MODEL), F32),
                   jax.ShapeDtypeStruct((MEM_LEN, D_MODEL), BF16), jax.ShapeDtypeStruct((MEM_LEN, D_MODEL), BF16)],
        compiler_params=_cparams(),
    )(mem, mem_norm_w, w_mkv, mk_norm_w)


def _mem_kv_bwd(mem, mem_norm_w, w_mkv, mk_norm_w, mem_n, kpre, dkn, dv):
    def body(mem_ref, mw_ref, w_ref, kw_ref, memn_ref, kpre_ref, dkn_ref, dv_ref, dw_ref, dkw_ref, dmw_ref):
        dkw = jnp.zeros((1, MEM_DH), F32)
        dk_parts = []
        for h in range(MEM_HEADS):
            sl = slice(MEM_DH * h, MEM_DH * (h + 1))
            kh, dkh = kpre_ref[:, sl], dkn_ref[:, sl]
            rk = lax.rsqrt(jnp.mean(kh * kh, axis=-1, keepdims=True) + RMS_EPS)
            xh = kh * rk
            dxh = dkh * kw_ref[...]
            dkw = dkw + jnp.sum(dkh * xh, axis=0, keepdims=True)
            dk_parts.append(rk * (dxh - xh * jnp.mean(dxh * xh, axis=-1, keepdims=True)))
        dkw_ref[...] = dkw
        dkv = jnp.concatenate(dk_parts + [dv_ref[...]], axis=1).astype(BF16)
        dw_ref[...] = _dot(memn_ref[...], dkv, 0, 0)
        dmem_n = _dot(dkv, w_ref[...], 1, 1)
        xb = mem_ref[...]
        r = lax.rsqrt(jnp.mean(xb * xb, axis=-1, keepdims=True) + RMS_EPS)
        dmw_ref[...] = jnp.sum(dmem_n * (xb * r), axis=0, keepdims=True)

    return pl.pallas_call(
        body, name="mem_kv_bwd",
        out_shape=[jax.ShapeDtypeStruct((D_MODEL, 2 * D_MODEL), F32), jax.ShapeDtypeStruct((1, MEM_DH), F32),
                   jax.ShapeDtypeStruct((1, D_MODEL), F32)],
        compiler_params=_cparams(),
    )(mem, mem_norm_w, w_mkv, mk_norm_w, mem_n, kpre, dkn, dv)


def _xattn_head(qh, kn_h, qw):
    rq = lax.rsqrt(jnp.mean(qh * qh, axis=-1, keepdims=True) + RMS_EPS)
    xh = qh * rq
    qn = (xh * qw).astype(BF16)
    s = _dot(qn, kn_h, 1, 1) * (1.0 / 16.0)
    e = jnp.exp(s - jnp.max(s, axis=-1, keepdims=True))
    p = e / jnp.sum(e, axis=-1, keepdims=True)
    return rq, xh, qn, p


def _xattn_fwd(qm, kn, v, mq_norm_w, bm):
    def fn(qb, knb, vb, qw):
        outs = []
        for h in range(MEM_HEADS):
            sl = slice(MEM_DH * h, MEM_DH * (h + 1))
            _, _, _, p = _xattn_head(qb[:, sl], knb[:, sl], qw)
            outs.append(_dot(p, vb[:, sl]))
        return (jnp.concatenate(outs, axis=1),)
    return _rowcall("xattn_fwd", fn, [(qm, D_MODEL, 0)], [kn, v, mq_norm_w], [(D_MODEL, BF16)], [], bm,
                    qm.shape[0])[0]


def _xattn_bwd(qm, kn, v, mq_norm_w, do, bm):
    def fn(qb, dob, knb, vb, qw):
        dq_parts, dkn_parts, dv_parts = [], [], []
        dqw = jnp.zeros((1, MEM_DH), F32)
        for h in range(MEM_HEADS):
            sl = slice(MEM_DH * h, MEM_DH * (h + 1))
            rq, xh, qn, p = _xattn_head(qb[:, sl], knb[:, sl], qw)
            doh = dob[:, sl].astype(BF16)
            dp = _dot(doh, vb[:, sl], 1, 1)
            ds = (p * (dp - jnp.sum(dp * p, axis=-1, keepdims=True)) * (1.0 / 16.0)).astype(BF16)
            dqn = _dot(ds, knb[:, sl])
            dkn_parts.append(_dot(ds, qn, 0, 0))
            dv_parts.append(_dot(p, doh, 0, 0))
            dqw = dqw + jnp.sum(dqn * xh, axis=0, keepdims=True)
            dxh = dqn * qw
            dq_parts.append(rq * (dxh - xh * jnp.mean(dxh * xh, axis=-1, keepdims=True)))
        return (jnp.concatenate(dq_parts, axis=1), jnp.concatenate(dkn_parts, axis=1),
                jnp.concatenate(dv_parts, axis=1), dqw)
    return _rowcall("xattn_bwd", fn, [(qm, D_MODEL, 0), (do, D_MODEL, 0)], [kn, v, mq_norm_w],
                    [(D_MODEL, BF16)], [(MEM_LEN, D_MODEL), (MEM_LEN, D_MODEL), (1, MEM_DH)], bm, qm.shape[0])


FF_BN = 1408
FF_NB = D_FF // FF_BN


def _ffn_up(h3, w_gate_up, rows, bm):
    def body(h_ref, wg_ref, wu_ref, gate_ref, up_ref, act_ref):
        hb = h_ref[...]
        gate = _dot(hb, wg_ref[...])
        up = _dot(hb, wu_ref[...])
        gate_ref[...] = gate.astype(BF16)
        up_ref[...] = up.astype(BF16)
        act_ref[...] = (gate * _sigmoid(gate) * up).astype(BF16)

    out_blk = pl.BlockSpec((bm, FF_BN), lambda i, j: (i, j))
    return pl.pallas_call(
        body, name="ffn_up", grid=(rows // bm, FF_NB),
        in_specs=[pl.BlockSpec((bm, D_MODEL), lambda i, j: (i, 0)),
                  pl.BlockSpec((D_MODEL, FF_BN), lambda i, j: (0, j)),
                  pl.BlockSpec((D_MODEL, FF_BN), lambda i, j: (0, FF_NB + j))],
        out_specs=[out_blk, out_blk, out_blk],
        out_shape=[jax.ShapeDtypeStruct((rows, D_FF), BF16)] * 3,
        compiler_params=_cparams("parallel", "arbitrary"),
    )(h3, w_gate_up, w_gate_up)


def _ffn_act_bwd(dy, w_down, gate, up, rows, bm):
    def body(dy_ref, wd_ref, gate_ref, up_ref, o_ref):
        dact = _dot(dy_ref[...], wd_ref[...], 1, 1)
        g, u = gate_ref[...].astype(F32), up_ref[...].astype(F32)
        sg = _sigmoid(g)
        o_ref[0] = (dact * u * (sg * (1.0 + g * (1.0 - sg)))).astype(BF16)
        o_ref[1] = (dact * (g * sg)).astype(BF16)

    blk = pl.BlockSpec((bm, FF_BN), lambda i, j: (i, j))
    return pl.pallas_call(
        body, name="ffn_act_bwd", grid=(rows // bm, FF_NB),
        in_specs=[pl.BlockSpec((bm, D_MODEL), lambda i, j: (i, 0)),
                  pl.BlockSpec((FF_BN, D_MODEL), lambda i, j: (j, 0)), blk, blk],
        out_specs=pl.BlockSpec((2, bm, FF_BN), lambda i, j: (0, i, j)),
        out_shape=jax.ShapeDtypeStruct((2, rows, D_FF), BF16),
        compiler_params=_cparams("parallel", "arbitrary"),
    )(dy, w_down, gate, up)


ROW_SHARDED = ("w_out", "w_mq", "w_mo", "w_down")
TAIL_ROWS = 16
AG_FIRST = ("w_in", "w_gk_up")
AG_MID = ("w_out", "w_mq", "w_mkv")
AG_LATE = ("w_mo", "w_gate_up", "w_down")
RS_FFN = ("w_gate_up", "w_down")
RS_MID = ("w_out", "w_mq", "w_mkv", "w_mo")
RS_WIN = ("w_in", "w_gk_up")
ADAM_BLOCK = {"w_in": 256, "w_out": 128, "w_mq": 128, "w_mkv": 256, "w_mo": 128, "w_gate_up": 256,
              "w_down": 176, "w_gk_up": 16}


def _full_weight(name, got):
    r, c = SHARD_SHAPE[name]
    if name in ROW_SHARDED:
        return got.reshape(N_DEV * r, c)
    return got.transpose(1, 0, 2).reshape(r, N_DEV * c)


def _tail_rows(rows):
    return jnp.concatenate(rows + [jnp.zeros((TAIL_ROWS - len(rows), 1024), F32)], axis=0)


def _split_for_owners(name, full):
    r, c = SHARD_SHAPE[name]
    if name in ROW_SHARDED:
        return full.reshape(N_DEV, r, c)
    return full.reshape(r, N_DEV, c).transpose(1, 0, 2)


def _repl_row(a):
    flat = a.reshape(-1)
    return jnp.concatenate([flat, jnp.zeros((1024 - flat.shape[0],), flat.dtype)]).reshape(1, 1024)


def _local_step(x, mem, tgt, wf, rp, shards=None, scatter=False):
    rows = x.shape[0]
    bm = min(512, rows)
    bmx = min(256, rows)
    mt = min(MM_TILE, rows)
    kt = min(512, rows)
    w_cat, wgk_pad = wf["w_cat"], wf["wgk_pad"]
    wg_t = jnp.tile(rp["gla_norm_w"], (1, GLA_HEADS))
    ws_t = jnp.tile(rp["sb_norm_w"], (1, SB_HEADS))
    lane = jnp.arange(GLA_V_W)
    grp_g = (lane[:, None] // GLA_DV == lane[None, :] // GLA_DV).astype(BF16)
    grp_s = (lane[:, None] // SB_DH == lane[None, :] // SB_DH).astype(BF16)

    h1 = _rms_fwd("mix_norm_fwd", x, rp["mix_norm_w"], bm)
    proj = _matmul("in_proj", h1, w_cat, "nn", rows, PROJ_W, D_MODEL, F32, mt, 640, D_MODEL)
    gk = _gate_fwd(proj, wgk_pad, rp["b_gk"], bm)
    if shards is None:
        o_g, s_all = _gla_fwd(proj, gk, rows)
        (o_s,) = _sb_fwd(proj, rows)
    else:
        mine = lambda names: [shards[n].astype(BF16) for n in names]
        o_g, s_all, got_mid = _gla_fwd(proj, gk, rows, _Exchange(mine(AG_MID), scatter=False))
        o_s, got_late = _sb_fwd(proj, rows, _Exchange(mine(AG_LATE), scatter=False))
        wf = {**wf, **{n: _full_weight(n, a) for n, a in zip(AG_MID + AG_LATE, got_mid + got_late)}}
    cat = _mix_out_fwd(o_g, proj, o_s, wg_t, ws_t, grp_g, grp_s, bm)
    x1 = _matmul("out_proj", cat, wf["w_out"], "nn", rows, D_MODEL, D_MODEL, F32, mt, MM_TILE, D_MODEL, residual=x)
    h2 = _rms_fwd("xattn_norm_fwd", x1, rp["xattn_norm_w"], bm)
    qm = _matmul("mq_proj", h2, wf["w_mq"], "nn", rows, D_MODEL, D_MODEL, F32, mt, MM_TILE, D_MODEL)
    mem_n, kpre, kn, v_m = _mem_kv_fwd(mem, rp["mem_norm_w"], wf["w_mkv"], rp["mk_norm_w"])
    o_m = _xattn_fwd(qm, kn, v_m, rp["mq_norm_w"], bmx)
    x2 = _matmul("mo_proj", o_m, wf["w_mo"], "nn", rows, D_MODEL, D_MODEL, F32, mt, MM_TILE, D_MODEL, residual=x1)
    h3 = _rms_fwd("ffn_norm_fwd", x2, rp["ffn_norm_w"], bm)
    gate, up, act = _ffn_up(h3, wf["w_gate_up"], rows, mt)
    y = _matmul("ffn_down", act, wf["w_down"], "nn", rows, D_MODEL, D_FF, F32, mt, MM_TILE, FF_BN, residual=x2)
    dy, sq = _loss_kernel(y, tgt, bm)

    g = {}
    dgu = _ffn_act_bwd(dy, wf["w_down"], gate, up, rows, bm)
    g["w_down"] = _matmul("grad_w_down", act, dy, "tn", D_FF, D_MODEL, rows, F32, FF_BN, MM_TILE, kt)
    nkb = FF_NB
    dh3 = _matmul("ffn_up_bwd", dgu, wf["w_gate_up"], "nt", rows, D_MODEL, 2 * D_FF, F32, mt, MM_TILE, FF_BN,
                  a_spec=pl.BlockSpec((None, mt, FF_BN), lambda i, j, kk: (kk // nkb, i, kk % nkb)))
    g["w_gate_up"] = _matmul(
        "grad_w_gate_up", h3, dgu, "tn", D_MODEL, 2 * D_FF, rows, F32, MM_TILE, FF_BN, kt,
        b_spec=pl.BlockSpec((None, kt, FF_BN), lambda i, j, kk: (j // nkb, kk, j % nkb)))
    dx2, g["ffn_norm_w"] = _rms_bwd("ffn_norm_bwd", x2, rp["ffn_norm_w"], dh3, dy, bm)

    do_m = _matmul("mo_proj_bwd", dx2, wf["w_mo"], "nt", rows, D_MODEL, D_MODEL, BF16, mt, MM_TILE, D_MODEL)
    g["w_mo"] = _matmul("grad_w_mo", o_m, dx2, "tn", D_MODEL, D_MODEL, rows, F32, MM_TILE, MM_TILE, kt)
    dqm, dkn, dv_m, g["mq_norm_w"] = _xattn_bwd(qm, kn, v_m, rp["mq_norm_w"], do_m, bmx)
    g["w_mkv"], g["mk_norm_w"], g["mem_norm_w"] = _mem_kv_bwd(
        mem, rp["mem_norm_w"], wf["w_mkv"], rp["mk_norm_w"], mem_n, kpre, dkn, dv_m)
    dh2 = _matmul("mq_proj_bwd", dqm, wf["w_mq"], "nt", rows, D_MODEL, D_MODEL, F32, mt, MM_TILE, D_MODEL)
    g["w_mq"] = _matmul("grad_w_mq", h2, dqm, "tn", D_MODEL, D_MODEL, rows, F32, MM_TILE, MM_TILE, kt)
    dx1, g["xattn_norm_w"] = _rms_bwd("xattn_norm_bwd", x1, rp["xattn_norm_w"], dh2, dx2, bm)

    dcat = _matmul("out_proj_bwd", dx1, wf["w_out"], "nt", rows, D_MODEL, D_MODEL, F32, mt, MM_TILE, D_MODEL)
    g["w_out"] = _matmul("grad_w_out", cat, dx1, "tn", D_MODEL, D_MODEL, rows, F32, MM_TILE, MM_TILE, kt)
    do_g, dgg, do_s, dwg, dws = _mix_out_bwd(dcat, o_g, proj, o_s, wg_t, ws_t, grp_g, grp_s, bm)
    g["gla_norm_w"] = dwg.reshape(GLA_HEADS, GLA_DV).sum(axis=0, keepdims=True)
    g["sb_norm_w"] = dws.reshape(SB_HEADS, SB_DH).sum(axis=0, keepdims=True)
    recv = {}
    to_owners = lambda names: [_split_for_owners(n, g[n].astype(BF16)) for n in names]
    if scatter:
        dq_s, dk_s, dv_s, got = _sb_bwd(proj, o_s, do_s, rows, _Exchange(to_owners(RS_FFN), scatter=True))
        recv.update(zip(RS_FFN, got))
        dq_g, dk_g, dv_g, dgk, got = _gla_bwd(proj, gk, do_g, s_all, rows, _Exchange(to_owners(RS_MID), scatter=True))
        recv.update(zip(RS_MID, got))
    else:
        dq_s, dk_s, dv_s = _sb_bwd(proj, o_s, do_s, rows)
        dq_g, dk_g, dv_g, dgk = _gla_bwd(proj, gk, do_g, s_all, rows)
    dlr, dwgk, g["b_gk"] = _gate_bwd(proj, wgk_pad, rp["b_gk"], dgk, bm)
    g["w_gk_up"] = dwgk[:GATE_RANK]
    dproj = jnp.concatenate([dq_g.astype(BF16), dk_g.astype(BF16), dv_g.astype(BF16), dgg, dq_s.astype(BF16),
                             dk_s.astype(BF16), dv_s.astype(BF16), dlr], axis=1)
    dw_cat = _matmul("grad_w_in", h1, dproj, "tn", D_MODEL, PROJ_W, rows, F32, MM_TILE, 640, kt)
    g["w_in"] = jnp.concatenate([dw_cat[:, :C_QS], dw_cat[:, C_LR:C_LR + GATE_RANK], dw_cat[:, C_QS:C_LR]], axis=1)
    if scatter:
        dh1, got = _matmul("in_proj_bwd", dproj, w_cat, "nt", rows, D_MODEL, PROJ_W, F32, mt, MM_TILE, 640,
                           exchange=_Exchange(to_owners(RS_WIN), scatter=True))
        recv.update(zip(RS_WIN, got))
    else:
        dh1 = _matmul("in_proj_bwd", dproj, w_cat, "nt", rows, D_MODEL, PROJ_W, F32, mt, MM_TILE, 640)
    dx, g["mix_norm_w"] = _rms_bwd("mix_norm_bwd", x, rp["mix_norm_w"], dh1, dx1, bm)
    return sq, dx, g, recv


def _first_weights(got):
    w_in = _full_weight("w_in", got[0])
    lr_end = C_QS + GATE_RANK
    w_cat = jnp.concatenate([w_in[:, :C_QS], w_in[:, lr_end:], w_in[:, C_QS:lr_end],
                             jnp.zeros((D_MODEL, PROJ_W - D_IN), BF16)], axis=1)
    wgk = _full_weight("w_gk_up", got[1])
    return {"w_cat": w_cat, "wgk_pad": jnp.concatenate([wgk, jnp.zeros((LANES - GATE_RANK, GLA_QK_W), BF16)], axis=0)}


def kernel(x, mem, mix_norm_w, w_in, w_gk_up, b_gk, gla_norm_w, sb_norm_w, w_out, xattn_norm_w, mem_norm_w, w_mq, w_mkv, mq_norm_w, mk_norm_w, w_mo, ffn_norm_w, w_gate_up, w_down, loss_target, m_mix_norm_w, m_w_in, m_w_gk_up, m_b_gk, m_gla_norm_w, m_sb_norm_w, m_w_out, m_xattn_norm_w, m_mem_norm_w, m_w_mq, m_w_mkv, m_mq_norm_w, m_mk_norm_w, m_w_mo, m_ffn_norm_w, m_w_gate_up, m_w_down, v_mix_norm_w, v_w_in, v_w_gk_up, v_b_gk, v_gla_norm_w, v_sb_norm_w, v_w_out, v_xattn_norm_w, v_mem_norm_w, v_w_mq, v_w_mkv, v_mq_norm_w, v_mk_norm_w, v_w_mo, v_ffn_norm_w, v_w_gate_up, v_w_down):
    given = dict(locals())
    w = {n: given[n][0] for n in WEIGHTS}
    m = {n: given["m_" + n][0] for n in WEIGHTS}
    v = {n: given["v_" + n][0] for n in WEIGHTS}

    wf = _first_weights(_exchange_call("gather_first_weights", [w[n].astype(BF16) for n in AG_FIRST], scatter=False))
    rp = {n: w[n].reshape(1, -1) for n in REPL}
    sq, dx, g, recv = _local_step(x[0], mem[0], loss_target[0], wf, rp, shards=w, scatter=True)

    loss_row = _repl_row(jnp.sum(sq).reshape(1) * (0.5 / D_MODEL))
    tail = _tail_rows([_repl_row(g[n]) for n in REPL] + [loss_row])
    (recv_tail,) = _exchange_call("scatter_tail_gradients", [jnp.broadcast_to(tail[None], (N_DEV, TAIL_ROWS, 1024))],
                                  scatter=True)
    results = {n: _adamw("sum_adamw_" + n, recv[n], w[n], m[n], v[n], ADAM_BLOCK[n]) for n in SHARD_SHAPE}
    tail_out = _adamw("sum_adamw_tail", recv_tail, *[_tail_rows([_repl_row(t[n]) for n in REPL]) for t in (w, m, v)],
                      TAIL_ROWS)

    def output(kind, name):
        if name in results:
            return results[name][kind][None]
        return tail_out[kind][REPL.index(name), :w[name].shape[-1]].reshape(1, -1)

    loss = tail_out[0][len(REPL), 0]
    outs = [loss, dx[None]]
    for kind in range(4):
        outs += [output(kind, n) for n in WEIGHTS]
    return tuple(outs)
```

```python
import functools
import math

import jax
import jax.numpy as jnp
from jax import lax
from jax.experimental import pallas as pl
from jax.experimental.pallas import tpu as pltpu

F32 = jnp.float32
BF16 = jnp.bfloat16

N_DEV = 8
D_MODEL = 1024
GLA_HEADS = 4
GLA_DK = 64
GLA_DV = 128
GLA_CHUNK = 64
GLA_STEP_CHUNKS = 4
GLA_QK_W = GLA_HEADS * GLA_DK
GLA_V_W = GLA_HEADS * GLA_DV
GATE_RANK = 16
SB_HEADS = 8
SB_DH = 64
SB_W = SB_HEADS * SB_DH
SB_BLK = 128
SB_QT = 1024
SB_LIVE_ROWS = 384
SB_DEAD = -104.0
MEM_LEN = 256
MEM_HEADS = 4
MEM_DH = 256
D_FF = 2816
D_IN = 3088
RMS_EPS = 1e-6
LANES = 128

PROJ_W = 3200
C_QG, C_KG, C_VG, C_GG, C_QS, C_KS, C_VS, C_LR = 0, 256, 512, 1024, 1536, 2048, 2560, 3072

ADAM_LR, ADAM_B1, ADAM_B2, ADAM_EPS, ADAM_WD, ADAM_STEP = 0.001, 0.9, 0.999, 1e-08, 0.01, 10

SHARD_SHAPE = {"w_in": (1024, 386), "w_out": (128, 1024), "w_mq": (128, 1024), "w_mkv": (1024, 256),
               "w_mo": (128, 1024), "w_gate_up": (1024, 704), "w_down": (352, 1024), "w_gk_up": (16, 32)}
REPL = ("mix_norm_w", "b_gk", "gla_norm_w", "sb_norm_w", "xattn_norm_w", "mem_norm_w", "mq_norm_w",
        "mk_norm_w", "ffn_norm_w")
WEIGHTS = ("mix_norm_w", "w_in", "w_gk_up", "b_gk", "gla_norm_w", "sb_norm_w", "w_out", "xattn_norm_w",
           "mem_norm_w", "w_mq", "w_mkv", "mq_norm_w", "mk_norm_w", "w_mo", "ffn_norm_w", "w_gate_up", "w_down")
VMEM_LIMIT = 56 * 1024 * 1024
MM_TILE = 1024


def _cparams(*sem):
    return pltpu.CompilerParams(dimension_semantics=sem if sem else None, vmem_limit_bytes=VMEM_LIMIT)


def _dot(a, b, ca=1, cb=0):
    return lax.dot_general(a.astype(BF16), b.astype(BF16), (((ca,), (cb,)), ((), ())),
                           preferred_element_type=F32)


def _split(x, parts):
    out = []
    for _ in range(parts - 1):
        hi = x.astype(BF16)
        out.append(hi)
        x = x - hi.astype(F32)
    out.append(x.astype(BF16))
    return out


def _dot_lhs_exact(x, m, ca=1, cb=0, parts=3):
    acc = None
    for p in _split(x, parts):
        t = _dot(p, m, ca, cb)
        acc = t if acc is None else acc + t
    return acc


def _dot_rhs_exact(m, x, ca=1, cb=0, parts=3):
    acc = None
    for p in _split(x, parts):
        t = _dot(m, p, ca, cb)
        acc = t if acc is None else acc + t
    return acc


def _dot3(a, b, ca=1, cb=0):
    a_hi, a_lo = _split(a, 2)
    b_hi, b_lo = _split(b, 2)
    return _dot(a_hi, b_hi, ca, cb) + (_dot(a_hi, b_lo, ca, cb) + _dot(a_lo, b_hi, ca, cb))


def _log_sigmoid(z):
    return jnp.minimum(z, 0.0) - jnp.log(1.0 + jnp.exp(-jnp.abs(z)))


def _sigmoid(z):
    e = jnp.exp(-jnp.abs(z))
    return jnp.where(z >= 0, 1.0, e) / (1.0 + e)


def _iota2(shape, dim):
    return lax.broadcasted_iota(jnp.int32, shape, dim)


def _rowcall(name, fn, row_ins, full_ins, row_outs, acc_outs, bm, rows):
    n_in = len(row_ins) + len(full_ins)
    n_row = len(row_outs)

    def body(*refs):
        ins, outs = refs[:n_in], refs[n_in:]
        res = fn(*[r[...] for r in ins])
        for r, v in zip(outs[:n_row], res[:n_row]):
            r[...] = v.astype(r.dtype)
        first = pl.program_id(0) == 0
        for r, v in zip(outs[n_row:], res[n_row:]):
            def init(r=r):
                r[...] = jnp.zeros(r.shape, r.dtype)
            pl.when(first)(init)
            r[...] += v

    in_specs = [pl.BlockSpec((bm, w), functools.partial(lambda i, c: (i, c), c=c)) for _, w, c in row_ins]
    in_specs += [pl.BlockSpec(a.shape, lambda i: (0, 0)) for a in full_ins]
    out_specs = [pl.BlockSpec((bm, w), lambda i: (i, 0)) for w, _ in row_outs]
    out_specs += [pl.BlockSpec(s, lambda i: (0, 0)) for s in acc_outs]
    out_shape = [jax.ShapeDtypeStruct((rows, w), dt) for w, dt in row_outs]
    out_shape += [jax.ShapeDtypeStruct(s, F32) for s in acc_outs]
    return pl.pallas_call(
        body, name=name, grid=(rows // bm,), in_specs=in_specs, out_specs=out_specs, out_shape=out_shape,
        compiler_params=_cparams("arbitrary"),
    )(*[a for a, _, _ in row_ins], *full_ins)


def _matmul(name, a, b, mode, m, n, k, out_dtype, bm, bn, bk, residual=None, a_spec=None, b_spec=None,
            exchange=None):
    bm, bn, bk = min(bm, m), min(bn, n), min(bk, k)
    nk = k // bk
    ca, cb = {"nn": (1, 0), "nt": (1, 1), "tn": (0, 0)}[mode]
    if a_spec is None:
        a_spec = (pl.BlockSpec((bk, bm), lambda i, j, kk: (kk, i)) if mode == "tn"
                  else pl.BlockSpec((bm, bk), lambda i, j, kk: (i, kk)))
    if b_spec is None:
        b_spec = (pl.BlockSpec((bn, bk), lambda i, j, kk: (j, kk)) if mode == "nt"
                  else pl.BlockSpec((bk, bn), lambda i, j, kk: (kk, j)))
    has_res = residual is not None

    def body(*refs):
        a_ref, b_ref = refs[0], refs[1]
        res_ref = refs[2] if has_res else None
        o_ref = refs[2 + has_res]
        part = _dot(a_ref[...], b_ref[...], ca, cb)

        def finish(total):
            if has_res:
                total = total + res_ref[...]
            o_ref[...] = total.astype(o_ref.dtype)

        if nk == 1:
            finish(part)
        else:
            acc_ref = refs[3 + has_res]
            kk = pl.program_id(2)

            @pl.when(kk == 0)
            def _():
                acc_ref[...] = part

            @pl.when(kk > 0)
            def _():
                acc_ref[...] += part

            @pl.when(kk == nk - 1)
            def _():
                finish(acc_ref[...])

    in_specs = [a_spec, b_spec]
    args = [a, b]
    if has_res:
        in_specs.append(pl.BlockSpec((bm, bn), lambda i, j, kk: (i, j)))
        args.append(residual)
    sem = ("parallel", "parallel", "arbitrary") if exchange is None else ("arbitrary",) * 3
    res = _hosted_call(
        body, exchange, grid=(m // bm, n // bn, nk), in_specs=in_specs,
        out_specs=[pl.BlockSpec((bm, bn), lambda i, j, kk: (i, j))],
        out_shape=[jax.ShapeDtypeStruct((m, n), out_dtype)],
        scratch_shapes=[pltpu.VMEM((bm, bn), F32)] if nk > 1 else [],
        args=args, name=name, compiler_params=_cparams(*sem))
    return res[0] if exchange is None else res


def _peer(mask):
    x, y, c = lax.axis_index("x"), lax.axis_index("y"), lax.axis_index("c")
    mx, my, mc = (mask >> 2) & 1, (mask >> 1) & 1, mask & 1
    px, py, pc = (1 - x if mx else x), (1 - y if my else y), (1 - c if mc else c)
    return (px, py, pc), 4 * px + 2 * py + pc


def _my_index():
    return 4 * lax.axis_index("x") + 2 * lax.axis_index("y") + lax.axis_index("c")


class _Exchange:
    def __init__(self, srcs, scatter):
        self.srcs, self.scatter, n = list(srcs), scatter, len(srcs)
        self.in_specs = [pl.BlockSpec(memory_space=pl.ANY)] * n
        self.out_specs = [pl.BlockSpec(memory_space=pl.ANY)] * n
        self.out_shapes = [jax.ShapeDtypeStruct((N_DEV,) + s.shape[-2:], s.dtype) for s in self.srcs]
        self.scratch = [pltpu.SemaphoreType.DMA((n * (N_DEV - 1),)), pltpu.SemaphoreType.DMA((n * (N_DEV - 1),)),
                        pltpu.SemaphoreType.DMA((n,))]

    def _copies(self, src_refs, out_refs, sems):
        send_sems, recv_sems, local_sems = sems
        me = _my_index()
        copies = []
        for a, (src_ref, out_ref) in enumerate(zip(src_refs, out_refs)):
            copies.append(pltpu.make_async_copy(src_ref.at[me] if self.scatter else src_ref, out_ref.at[me],
                                                local_sems.at[a]))
            for mask in range(1, N_DEV):
                peer, peer_index = _peer(mask)
                slot = a * (N_DEV - 1) + mask - 1
                copies.append(pltpu.make_async_remote_copy(
                    src_ref=src_ref.at[peer_index] if self.scatter else src_ref, dst_ref=out_ref.at[me],
                    send_sem=send_sems.at[slot], recv_sem=recv_sems.at[slot],
                    device_id=peer, device_id_type=pl.DeviceIdType.MESH))
        return copies

    def start(self, src_refs, out_refs, sems):
        for cp in self._copies(src_refs, out_refs, sems):
            cp.start()

    def wait(self, src_refs, out_refs, sems):
        for cp in self._copies(src_refs, out_refs, sems):
            cp.wait()


def _hosted_call(body, ex, grid, in_specs, out_specs, out_shape, scratch_shapes, args, **kw):
    if ex is None:
        return pl.pallas_call(body, grid=grid, in_specs=in_specs, out_specs=out_specs, out_shape=out_shape,
                              scratch_shapes=scratch_shapes, **kw)(*args)
    n_in, n_out, n_scr, n_x = len(in_specs), len(out_specs), len(scratch_shapes), len(ex.srcs)

    def hosted(*refs):
        ins, src_refs = refs[:n_in], refs[n_in:n_in + n_x]
        refs = refs[n_in + n_x:]
        outs, out_refs = refs[:n_out], refs[n_out:n_out + n_x]
        refs = refs[n_out + n_x:]
        scr, sems = refs[:n_scr], refs[n_scr:]
        ids = [pl.program_id(a) for a in range(len(grid))]
        first = functools.reduce(jnp.logical_and, [p == 0 for p in ids])
        last = functools.reduce(jnp.logical_and, [p == n - 1 for p, n in zip(ids, grid)])

        @pl.when(first)
        def _():
            ex.start(src_refs, out_refs, sems)

        body(*ins, *outs, *scr)

        @pl.when(last)
        def _():
            ex.wait(src_refs, out_refs, sems)

    res = pl.pallas_call(
        hosted, grid=grid, in_specs=list(in_specs) + ex.in_specs, out_specs=list(out_specs) + ex.out_specs,
        out_shape=list(out_shape) + ex.out_shapes, scratch_shapes=list(scratch_shapes) + ex.scratch, **kw,
    )(*args, *ex.srcs)
    return tuple(res[:n_out]) + (list(res[n_out:]),)


def _exchange_call(name, srcs, scatter):
    ex = _Exchange(srcs, scatter)
    n_x = len(ex.srcs)

    def body(*refs):
        src_refs, out_refs, sems = refs[:n_x], refs[n_x:2 * n_x], refs[2 * n_x:]
        ex.start(src_refs, out_refs, sems)
        ex.wait(src_refs, out_refs, sems)

    return list(pl.pallas_call(
        body, name=name, in_specs=ex.in_specs, out_specs=ex.out_specs, out_shape=ex.out_shapes,
        scratch_shapes=ex.scratch, compiler_params=pltpu.CompilerParams(has_side_effects=True),
    )(*ex.srcs))


def _adamw(name, recv, w, m, v, block):
    rows, cols = w.shape
    c1 = 1.0 - ADAM_B1 ** ADAM_STEP
    c2 = 1.0 - ADAM_B2 ** ADAM_STEP

    def body(r_ref, w_ref, m_ref, v_ref, g_out, d_out, m_out, v_out):
        g = r_ref[0].astype(F32)
        for s in range(1, N_DEV):
            g = g + r_ref[s].astype(F32)
        m_new = ADAM_B1 * m_ref[...] + (1.0 - ADAM_B1) * g
        v_new = ADAM_B2 * v_ref[...] + (1.0 - ADAM_B2) * (g * g)
        m_hat = m_new / c1
        v_hat = v_new / c2
        g_out[...] = g
        d_out[...] = -ADAM_LR * (m_hat / (jnp.sqrt(v_hat) + ADAM_EPS) + ADAM_WD * w_ref[...])
        m_out[...] = m_new
        v_out[...] = v_new

    blk = pl.BlockSpec((block, cols), lambda i: (i, 0))
    return pl.pallas_call(
        body, name=name, grid=(rows // block,),
        in_specs=[pl.BlockSpec((N_DEV, block, cols), lambda i: (0, i, 0)), blk, blk, blk],
        out_specs=[blk] * 4, out_shape=[jax.ShapeDtypeStruct((rows, cols), F32)] * 4,
        compiler_params=_cparams("parallel"),
    )(recv, w, m, v)


def _rms_fwd(name, x, w, bm):
    def fn(xb, wb):
        r = lax.rsqrt(jnp.mean(xb * xb, axis=-1, keepdims=True) + RMS_EPS)
        return (xb * r * wb,)
    return _rowcall(name, fn, [(x, D_MODEL, 0)], [w], [(D_MODEL, BF16)], [], bm, x.shape[0])[0]


def _rms_bwd(name, x, w, dh, dres, bm):
    def fn(xb, dhb, drb, wb):
        r = lax.rsqrt(jnp.mean(xb * xb, axis=-1, keepdims=True) + RMS_EPS)
        xh = xb * r
        dxh = dhb.astype(F32) * wb
        dx = drb + r * (dxh - xh * jnp.mean(dxh * xh, axis=-1, keepdims=True))
        return dx, jnp.sum(dhb.astype(F32) * xh, axis=0, keepdims=True)
    return _rowcall(name, fn, [(x, D_MODEL, 0), (dh, D_MODEL, 0), (dres, D_MODEL, 0)], [w],
                    [(D_MODEL, F32)], [(1, D_MODEL)], bm, x.shape[0])


def _gate_fwd(proj, wgk_pad, b_gk, bm):
    def fn(lr, wg, bg):
        z = _dot(lr, wg) + bg
        return (_log_sigmoid(z) * (1.0 / 16.0),)
    return _rowcall("gla_gate_fwd", fn, [(proj, LANES, C_LR // LANES)], [wgk_pad, b_gk],
                    [(GLA_QK_W, F32)], [], bm, proj.shape[0])[0]


def _gate_bwd(proj, wgk_pad, b_gk, dgk, bm):
    def fn(lr, dg, wg, bg):
        z = _dot(lr, wg) + bg
        dz = dg * _sigmoid(-z) * (1.0 / 16.0)
        return _dot(dz, wg, 1, 1), _dot(lr, dz, 0, 0), jnp.sum(dz, axis=0, keepdims=True)
    return _rowcall("gla_gate_bwd", fn, [(proj, LANES, C_LR // LANES), (dgk, GLA_QK_W, 0)], [wgk_pad, b_gk],
                    [(LANES, BF16)], [(LANES, GLA_QK_W), (1, GLA_QK_W)], bm, proj.shape[0])


def _group_mean(x, g, size):
    return _dot_lhs_exact(x, g, parts=2) * (1.0 / size)


def _mix_out_fwd(o_g, proj, o_s, wg_t, ws_t, grp_g, grp_s, bm):
    def fn(og, gg, os_, wg, ws, gmat, smat):
        rg = lax.rsqrt(_group_mean(og * og, gmat, GLA_DV) + RMS_EPS)
        yg = og * rg * wg * (gg * _sigmoid(gg))
        rs = lax.rsqrt(_group_mean(os_ * os_, smat, SB_DH) + RMS_EPS)
        ys = os_ * rs * ws
        return (jnp.concatenate([yg, ys], axis=1),)
    return _rowcall("mix_out_fwd", fn, [(o_g, GLA_V_W, 0), (proj, GLA_V_W, C_GG // GLA_V_W), (o_s, SB_W, 0)],
                    [wg_t, ws_t, grp_g, grp_s], [(D_MODEL, BF16)], [], bm, o_g.shape[0])[0]


def _mix_out_bwd(dcat, o_g, proj, o_s, wg_t, ws_t, grp_g, grp_s, bm):
    def fn(dyg, dys, og, gg, os_, wg, ws, gmat, smat):
        dyg = dyg.astype(F32)
        dys = dys.astype(F32)
        rg = lax.rsqrt(_group_mean(og * og, gmat, GLA_DV) + RMS_EPS)
        xh = og * rg
        sg = _sigmoid(gg)
        silu = gg * sg
        dxh = dyg * wg * silu
        dgg = dyg * xh * wg * (sg * (1.0 + gg * (1.0 - sg)))
        dwg = jnp.sum(dyg * xh * silu, axis=0, keepdims=True)
        dog = rg * (dxh - xh * _group_mean(dxh * xh, gmat, GLA_DV))
        rs = lax.rsqrt(_group_mean(os_ * os_, smat, SB_DH) + RMS_EPS)
        xs = os_ * rs
        dxs = dys * ws
        dws = jnp.sum(dys * xs, axis=0, keepdims=True)
        dos = rs * (dxs - xs * _group_mean(dxs * xs, smat, SB_DH))
        return dog, dgg, dos, dwg, dws
    return _rowcall("mix_out_bwd", fn,
                    [(dcat, GLA_V_W, 0), (dcat, SB_W, 1), (o_g, GLA_V_W, 0), (proj, GLA_V_W, C_GG // GLA_V_W),
                     (o_s, SB_W, 0)],
                    [wg_t, ws_t, grp_g, grp_s], [(GLA_V_W, F32), (GLA_V_W, BF16), (SB_W, F32)],
                    [(1, GLA_V_W), (1, SB_W)], bm, o_g.shape[0])


def _loss_kernel(y, tgt, bm):
    def fn(yb, tb):
        err = yb - tb
        return err * (1.0 / D_MODEL), jnp.sum(err * err, axis=0, keepdims=True)
    return _rowcall("loss_head", fn, [(y, D_MODEL, 0), (tgt, D_MODEL, 0)], [], [(D_MODEL, F32)], [(1, D_MODEL)],
                    bm, y.shape[0])


def _sb_tri(inclusive):
    j, s = _iota2((2 * SB_BLK, 2 * SB_BLK), 0), _iota2((2 * SB_BLK, 2 * SB_BLK), 1)
    j = jnp.where(j >= SB_BLK, j - SB_BLK, j)
    keep = (j >= s) if inclusive else (j > s)
    return ((s >= SB_BLK) | keep).astype(BF16)


def _dot_hilo(x, m2):
    hi, lo = _split(x, 2)
    return _dot(jnp.concatenate([hi, lo], axis=1), m2)


def _sb_mask(n):
    return _iota2((n, SB_BLK), 1) < _iota2((n, SB_BLK), 0)


def _add_rows(full, part, row0, row1):
    pieces = [full[:row0]] if row0 else []
    pieces.append(full[row0:row1] + part)
    if row1 < full.shape[0]:
        pieces.append(full[row1:])
    return pieces[0] if len(pieces) == 1 else jnp.concatenate(pieces, axis=0)


def _sb_visit(tiles, carry, c_slots, qt, diag_start, n_left, left_start):
    for sub in reversed(range(qt // SB_BLK)):
        carry = tiles(diag_start(sub), carry, True, sub * SB_BLK, qt)

    def sweep(row1):
        def step(jj, cr):
            return tiles(left_start(jj), cr, False, 0, row1)
        return functools.partial(_sb_sweep, n_left, step, c_slots=c_slots)

    if qt <= SB_LIVE_ROWS:
        return sweep(qt)(carry)
    rest_dead = jnp.max(functools.reduce(jnp.maximum, [carry[s][SB_LIVE_ROWS:] for s in c_slots])) <= SB_DEAD
    return lax.cond(rest_dead, sweep(SB_LIVE_ROWS), sweep(qt), carry)


def _sb_sweep(n_tiles, step, carry, c_slots):
    def alive(state):
        jj, carry = state
        c_max = jnp.max(functools.reduce(jnp.maximum, [carry[s] for s in c_slots]))
        return jnp.logical_and(jj < n_tiles, c_max > SB_DEAD)

    def body(state):
        jj, carry = state
        return jj + 1, step(jj, carry)

    return lax.while_loop(alive, body, (jnp.int32(0), carry))[1]


def _row_blocks(n):
    return [slice(r, r + SB_BLK) for r in range(0, n, SB_BLK)]


def _hilo(x):
    hi, lo = _split(x, 2)
    return jnp.concatenate([hi, lo], axis=1)


def _sb_tile(q, k, c, tri_excl, diag):
    blocks = _row_blocks(q.shape[0])
    strict = _sb_mask(SB_BLK) if diag else None
    z = _dot(q, k, 1, 1)
    lbs, pieces = [], []
    for r, rs in enumerate(blocks):
        lb = _log_sigmoid(z[rs])
        l1 = lb - z[rs]
        if diag and r == 0:
            l1 = jnp.where(strict, l1, 0.0)
        lbs.append(lb)
        pieces.append(_hilo(l1))
    sums = _dot(jnp.concatenate(pieces, axis=0), tri_excl)
    a = []
    for r, rs in enumerate(blocks):
        ar = jnp.exp(lbs[r] + sums[rs, :SB_BLK] + c[rs])
        if diag and r == 0:
            ar = jnp.where(strict, ar, 0.0)
        a.append(ar.astype(BF16))
    return lbs, a, sums[:, SB_BLK:]


def _sb_fwd(proj, rows, exchange=None):
    qt = min(SB_QT, rows)
    subs = qt // SB_BLK

    def body(q_ref, k_ref, v_ref, o_ref):
        i = pl.program_id(1)
        tri_excl = _sb_tri(False)
        heads = [slice(SB_DH * hh, SB_DH * (hh + 1)) for hh in range(2)]
        qs = [(q_ref[:, sl] * 0.125).astype(BF16) for sl in heads]

        def tiles(start, carry, diag, row0, row1):
            out = []
            for hh, sl in enumerate(heads):
                o, c = carry[2 * hh], carry[2 * hh + 1]
                k = k_ref[pl.ds(start, SB_BLK), sl].astype(BF16)
                v = v_ref[pl.ds(start, SB_BLK), sl].astype(BF16)
                _, a, dc = _sb_tile(qs[hh][row0:row1], k, c[row0:row1], tri_excl, diag)
                out += [_add_rows(o, _dot(jnp.concatenate(a, axis=0), v), row0, row1), _add_rows(c, dc, row0, row1)]
            return tuple(out)

        carry = (jnp.zeros((qt, SB_DH), F32), jnp.zeros((qt, SB_BLK), F32)) * 2
        carry = _sb_visit(tiles, carry, (1, 3), qt,
                          lambda sub: pl.multiple_of(i * qt + sub * SB_BLK, SB_BLK), i * subs,
                          lambda jj: pl.multiple_of((i * subs - 1 - jj) * SB_BLK, SB_BLK))
        for hh, sl in enumerate(heads):
            o_ref[:, sl] = carry[2 * hh]

    return _hosted_call(
        body, exchange, grid=(SB_HEADS // 2, rows // qt),
        in_specs=[pl.BlockSpec((qt, LANES), lambda h, i: (i, C_QS // LANES + h)),
                  pl.BlockSpec((rows, LANES), lambda h, i: (0, C_KS // LANES + h)),
                  pl.BlockSpec((rows, LANES), lambda h, i: (0, C_VS // LANES + h))],
        out_specs=[pl.BlockSpec((qt, LANES), lambda h, i: (i, h))],
        out_shape=[jax.ShapeDtypeStruct((rows, SB_W), F32)], scratch_shapes=[],
        args=(proj, proj, proj), name="sb_attention_fwd", compiler_params=_cparams("arbitrary", "arbitrary"))


def _sb_bwd(proj, o_s, do_s, rows, exchange=None):
    qt = min(SB_QT, rows)
    subs = qt // SB_BLK

    def body(q_ref, k_ref, v_ref, o_ref, do_ref, dq_ref, dk_ref, dv_ref):
        i = pl.program_id(1)

        @pl.when(i == 0)
        def _():
            dk_ref[...] = jnp.zeros(dk_ref.shape, F32)
            dv_ref[...] = jnp.zeros(dv_ref.shape, F32)

        tri_excl, tri_incl = _sb_tri(False), _sb_tri(True)
        heads = [slice(SB_DH * hh, SB_DH * (hh + 1)) for hh in range(2)]
        qs = [(q_ref[:, sl] * 0.125).astype(BF16) for sl in heads]
        dobs = [do_ref[:, sl].astype(BF16) for sl in heads]
        dsums = [jnp.broadcast_to(jnp.sum(dob.astype(F32) * o_ref[:, sl], axis=1, keepdims=True), (qt, SB_BLK))
                 for dob, sl in zip(dobs, heads)]

        def tiles(start, carry, diag, row0, row1):
            out = []
            strict = _sb_mask(SB_BLK) if diag else None
            for hh, sl in enumerate(heads):
                dq, c, cp = carry[3 * hh:3 * hh + 3]
                q, dob = qs[hh][row0:row1], dobs[hh][row0:row1]
                dsum, cpr = dsums[hh][row0:row1], cp[row0:row1]
                blocks = _row_blocks(q.shape[0])
                k = k_ref[pl.ds(start, SB_BLK), sl].astype(BF16)
                v = v_ref[pl.ds(start, SB_BLK), sl].astype(BF16)
                lbs, a, dc = _sb_tile(q, k, c[row0:row1], tri_excl, diag)
                da = _dot(dob, v, 1, 1)
                ps = [a[r].astype(F32) * da[rs] for r, rs in enumerate(blocks)]
                psums = _dot(jnp.concatenate([_hilo(p) for p in ps], axis=0), tri_incl)
                dzs = []
                for r, rs in enumerate(blocks):
                    left = dsum[rs] - (psums[rs, :SB_BLK] + cpr[rs])
                    dz = ps[r] - jnp.exp(lbs[r]) * (ps[r] + left)
                    if diag and r == 0:
                        dz = jnp.where(strict, dz, 0.0)
                    dzs.append(dz.astype(BF16))
                dzb, ab = jnp.concatenate(dzs, axis=0), jnp.concatenate(a, axis=0)
                dk_ref[pl.ds(start, SB_BLK), sl] += _dot(dzb, q, 0, 0)
                dv_ref[pl.ds(start, SB_BLK), sl] += _dot(ab, dob, 0, 0)
                out += [_add_rows(dq, _dot(dzb, k), row0, row1), _add_rows(c, dc, row0, row1),
                        _add_rows(cp, psums[:, SB_BLK:], row0, row1)]
            return tuple(out)

        zero = jnp.zeros((qt, SB_BLK), F32)
        carry = (jnp.zeros((qt, SB_DH), F32), zero, zero) * 2
        carry = _sb_visit(tiles, carry, (1, 4), qt,
                          lambda sub: pl.multiple_of(i * qt + sub * SB_BLK, SB_BLK), i * subs,
                          lambda jj: pl.multiple_of((i * subs - 1 - jj) * SB_BLK, SB_BLK))
        for hh, sl in enumerate(heads):
            dq_ref[:, sl] = carry[3 * hh] * 0.125

    whole = lambda base: pl.BlockSpec((rows, LANES), functools.partial(lambda h, i, b: (0, b + h), b=base))
    blk = lambda base: pl.BlockSpec((qt, LANES), functools.partial(lambda h, i, b: (i, b + h), b=base))
    return _hosted_call(
        body, exchange, grid=(SB_HEADS // 2, rows // qt),
        in_specs=[blk(C_QS // LANES), whole(C_KS // LANES), whole(C_VS // LANES), blk(0), blk(0)],
        out_specs=[blk(0), whole(0), whole(0)],
        out_shape=[jax.ShapeDtypeStruct((rows, SB_W), F32)] * 3, scratch_shapes=[],
        args=(proj, proj, proj, o_s, do_s), name="sb_attention_bwd",
        compiler_params=_cparams("arbitrary", "arbitrary"))


def _gla_chunk_common(g_all):
    r_i, c_i = _iota2((GLA_CHUNK, GLA_CHUNK), 0), _iota2((GLA_CHUNK, GLA_CHUNK), 1)
    tri = (c_i <= r_i).astype(BF16)
    return _dot_rhs_exact(tri, g_all), r_i, c_i


def _gla_scaled(qh, kh, bh):
    ref = bh[GLA_CHUNK // 2:GLA_CHUNK // 2 + 1, :]
    eq, ek = jnp.exp(bh - ref), jnp.exp(ref - bh)
    mask = _iota2((GLA_CHUNK, GLA_CHUNK), 1) <= _iota2((GLA_CHUNK, GLA_CHUNK), 0)
    return eq, ek, qh * eq, kh * ek, mask


def _gla_fwd(proj, gk, rows, exchange=None):
    n_chunks = rows // GLA_CHUNK
    step_rows = GLA_CHUNK * GLA_STEP_CHUNKS

    def body(q_ref, k_ref, v_ref, g_ref, o_ref, sall_ref, s_scr):
        @pl.when(pl.program_id(0) == 0)
        def _():
            s_scr[...] = jnp.zeros(s_scr.shape, F32)

        ones = jnp.ones((GLA_CHUNK, GLA_DV), BF16)
        states = [s_scr[h] for h in range(GLA_HEADS)]
        for ci in range(GLA_STEP_CHUNKS):
            rs = slice(GLA_CHUNK * ci, GLA_CHUNK * (ci + 1))
            g_all = g_ref[rs, :]
            b_all, _, _ = _gla_chunk_common(g_all)
            for h in range(GLA_HEADS):
                sl = slice(GLA_DK * h, GLA_DK * (h + 1))
                vs = slice(GLA_DV * h, GLA_DV * (h + 1))
                qh, kh, vh = q_ref[rs, sl] * 0.125, k_ref[rs, sl], v_ref[rs, vs]
                bh, gh = b_all[:, sl], g_all[:, sl]
                s = states[h]
                sall_ref[ci, h] = s
                _, _, qs, ks, mask = _gla_scaled(qh, kh, bh)
                a = jnp.where(mask, _dot(qs, ks, 1, 1), 0.0)
                o_ref[rs, vs] = _dot(qh * jnp.exp(bh), s) + _dot(a, vh)
                bl_col = _dot_lhs_exact(gh, ones, 0, 0)
                kd = kh * jnp.exp(bh[GLA_CHUNK - 1:GLA_CHUNK, :] - bh)
                states[h] = jnp.exp(bl_col) * s + _dot(kd, vh, 0, 0)
        for h in range(GLA_HEADS):
            s_scr[h] = states[h]

    c64 = lambda w, base: pl.BlockSpec((step_rows, w), functools.partial(lambda n, b: (n, b), b=base))
    return _hosted_call(
        body, exchange, grid=(rows // step_rows,),
        in_specs=[c64(GLA_QK_W, C_QG // GLA_QK_W), c64(GLA_QK_W, C_KG // GLA_QK_W), c64(GLA_V_W, C_VG // GLA_V_W),
                  c64(GLA_QK_W, 0)],
        out_specs=[c64(GLA_V_W, 0),
                   pl.BlockSpec((GLA_STEP_CHUNKS, GLA_HEADS, GLA_DK, GLA_DV), lambda n: (n, 0, 0, 0))],
        out_shape=[jax.ShapeDtypeStruct((rows, GLA_V_W), F32),
                   jax.ShapeDtypeStruct((n_chunks, GLA_HEADS, GLA_DK, GLA_DV), F32)],
        scratch_shapes=[pltpu.VMEM((GLA_HEADS, GLA_DK, GLA_DV), F32)],
        args=(proj, proj, proj, gk), name="gla_fwd", compiler_params=_cparams("arbitrary"))


def _gla_bwd(proj, gk, do_g, s_all, rows, exchange=None):
    n_chunks = rows // GLA_CHUNK

    def body(q_ref, k_ref, v_ref, g_ref, do_ref, sall_ref, dq_ref, dk_ref, dv_ref, dg_ref, ds_scr):
        @pl.when(pl.program_id(0) == 0)
        def _():
            ds_scr[...] = jnp.zeros(ds_scr.shape, F32)

        ones = jnp.ones((GLA_CHUNK, GLA_DV), BF16)
        ones8 = jnp.ones((8, GLA_DV), F32)
        last_row = _iota2((GLA_CHUNK, GLA_DK), 0) == GLA_CHUNK - 1
        dstates = [ds_scr[h] for h in range(GLA_HEADS)]
        for ci in reversed(range(GLA_STEP_CHUNKS)):
            cs = slice(GLA_CHUNK * ci, GLA_CHUNK * (ci + 1))
            g_all = g_ref[cs, :]
            b_all, r_i, c_i = _gla_chunk_common(g_all)
            triu = (c_i >= r_i).astype(BF16)
            for h in range(GLA_HEADS):
                sl = slice(GLA_DK * h, GLA_DK * (h + 1))
                vs = slice(GLA_DV * h, GLA_DV * (h + 1))
                qh, kh, vh = q_ref[cs, sl] * 0.125, k_ref[cs, sl], v_ref[cs, vs]
                bh, gh = b_all[:, sl], g_all[:, sl]
                doh = do_ref[cs, vs]
                s, ds = sall_ref[ci, h], dstates[h]
                eb = jnp.exp(bh)
                ekd = jnp.exp(bh[GLA_CHUNK - 1:GLA_CHUNK, :] - bh)
                ebl = jnp.exp(_dot_lhs_exact(gh, ones, 0, 0))
                qb, kd = qh * eb, kh * ekd
                dq = _dot(doh, s, 1, 1) * eb
                dk = _dot(vh, ds, 1, 1) * ekd
                dv = _dot(kd, ds)
                dbl = jnp.sum(dk * kh, axis=0, keepdims=True) + _dot3(ones8, ebl * s * ds, 1, 1)[0:1, :]
                eq, ek, qs, ks, mask = _gla_scaled(qh, kh, bh)
                a = jnp.where(mask, _dot(qs, ks, 1, 1), 0.0)
                da = jnp.where(mask, _dot(doh, vh, 1, 1), 0.0)
                dq = dq + _dot(da, ks) * eq
                dk = dk + _dot(da, qs, 0, 0) * ek
                dv = dv + _dot(a, doh, 0, 0)
                db = qh * dq - kh * dk + jnp.where(last_row, dbl, 0.0)
                dq_ref[cs, sl] = dq * 0.125
                dk_ref[cs, sl] = dk
                dv_ref[cs, vs] = dv
                dg_ref[cs, sl] = _dot_rhs_exact(triu, db)
                dstates[h] = _dot(qb, doh, 0, 0) + ebl * ds
        for h in range(GLA_HEADS):
            ds_scr[h] = dstates[h]

    step_rows = GLA_CHUNK * GLA_STEP_CHUNKS
    last = rows // step_rows - 1
    c64 = lambda w, base: pl.BlockSpec((step_rows, w), functools.partial(lambda n, b: (last - n, b), b=base))
    return _hosted_call(
        body, exchange, grid=(rows // step_rows,),
        in_specs=[c64(GLA_QK_W, C_QG // GLA_QK_W), c64(GLA_QK_W, C_KG // GLA_QK_W), c64(GLA_V_W, C_VG // GLA_V_W),
                  c64(GLA_QK_W, 0), c64(GLA_V_W, 0),
                  pl.BlockSpec((GLA_STEP_CHUNKS, GLA_HEADS, GLA_DK, GLA_DV), lambda n: (last - n, 0, 0, 0))],
        out_specs=[c64(GLA_QK_W, 0), c64(GLA_QK_W, 0), c64(GLA_V_W, 0), c64(GLA_QK_W, 0)],
        out_shape=[jax.ShapeDtypeStruct((rows, GLA_QK_W), F32), jax.ShapeDtypeStruct((rows, GLA_QK_W), F32),
                   jax.ShapeDtypeStruct((rows, GLA_V_W), F32), jax.ShapeDtypeStruct((rows, GLA_QK_W), F32)],
        scratch_shapes=[pltpu.VMEM((GLA_HEADS, GLA_DK, GLA_DV), F32)],
        args=(proj, proj, proj, gk, do_g, s_all), name="gla_bwd", compiler_params=_cparams("arbitrary"))


def _mem_kv_fwd(mem, mem_norm_w, w_mkv, mk_norm_w):
    def body(mem_ref, mw_ref, w_ref, kw_ref, memn_ref, kpre_ref, kn_ref, v_ref):
        xb = mem_ref[...]
        r = lax.rsqrt(jnp.mean(xb * xb, axis=-1, keepdims=True) + RMS_EPS)
        mem_n = (xb * r * mw_ref[...]).astype(BF16)
        memn_ref[...] = mem_n
        kv = _dot(mem_n, w_ref[...])
        kpre_ref[...] = kv[:, :D_MODEL]
        v_ref[...] = kv[:, D_MODEL:].astype(BF16)
        for h in range(MEM_HEADS):
            sl = slice(MEM_DH * h, MEM_DH * (h + 1))
            kh = kv[:, sl]
            rk = lax.rsqrt(jnp.mean(kh * kh, axis=-1, keepdims=True) + RMS_EPS)
            kn_ref[:, sl] = (kh * rk * kw_ref[...]).astype(BF16)

    return pl.pallas_call(
        body, name="mem_kv_fwd",
        out_shape=[jax.ShapeDtypeStruct((MEM_LEN, D_MODEL), BF16), jax.ShapeDtypeStruct((MEM_LEN, D_MODEL), F32),
                   jax.ShapeDtypeStruct((MEM_LEN, D_MODEL), BF16), jax.ShapeDtypeStruct((MEM_LEN, D_MODEL), BF16)],
        compiler_params=_cparams(),
    )(mem, mem_norm_w, w_mkv, mk_norm_w)


def _mem_kv_bwd(mem, mem_norm_w, w_mkv, mk_norm_w, mem_n, kpre, dkn, dv):
    def body(mem_ref, mw_ref, w_ref, kw_ref, memn_ref, kpre_ref, dkn_ref, dv_ref, dw_ref, dkw_ref, dmw_ref):
        dkw = jnp.zeros((1, MEM_DH), F32)
        dk_parts = []
        for h in range(MEM_HEADS):
            sl = slice(MEM_DH * h, MEM_DH * (h + 1))
            kh, dkh = kpre_ref[:, sl], dkn_ref[:, sl]
            rk = lax.rsqrt(jnp.mean(kh * kh, axis=-1, keepdims=True) + RMS_EPS)
            xh = kh * rk
            dxh = dkh * kw_ref[...]
            dkw = dkw + jnp.sum(dkh * xh, axis=0, keepdims=True)
            dk_parts.append(rk * (dxh - xh * jnp.mean(dxh * xh, axis=-1, keepdims=True)))
        dkw_ref[...] = dkw
        dkv = jnp.concatenate(dk_parts + [dv_ref[...]], axis=1).astype(BF16)
        dw_ref[...] = _dot(memn_ref[...], dkv, 0, 0)
        dmem_n = _dot(dkv, w_ref[...], 1, 1)
        xb = mem_ref[...]
        r = lax.rsqrt(jnp.mean(xb * xb, axis=-1, keepdims=True) + RMS_EPS)
        dmw_ref[...] = jnp.sum(dmem_n * (xb * r), axis=0, keepdims=True)

    return pl.pallas_call(
        body, name="mem_kv_bwd",
        out_shape=[jax.ShapeDtypeStruct((D_MODEL, 2 * D_MODEL), F32), jax.ShapeDtypeStruct((1, MEM_DH), F32),
                   jax.ShapeDtypeStruct((1, D_MODEL), F32)],
        compiler_params=_cparams(),
    )(mem, mem_norm_w, w_mkv, mk_norm_w, mem_n, kpre, dkn, dv)


def _xattn_head(qh, kn_h, qw):
    rq = lax.rsqrt(jnp.mean(qh * qh, axis=-1, keepdims=True) + RMS_EPS)
    xh = qh * rq
    qn = (xh * qw).astype(BF16)
    s = _dot(qn, kn_h, 1, 1) * (1.0 / 16.0)
    e = jnp.exp(s - jnp.max(s, axis=-1, keepdims=True))
    p = e / jnp.sum(e, axis=-1, keepdims=True)
    return rq, xh, qn, p


def _xattn_fwd(qm, kn, v, mq_norm_w, bm):
    def fn(qb, knb, vb, qw):
        outs = []
        for h in range(MEM_HEADS):
            sl = slice(MEM_DH * h, MEM_DH * (h + 1))
            _, _, _, p = _xattn_head(qb[:, sl], knb[:, sl], qw)
            outs.append(_dot(p, vb[:, sl]))
        return (jnp.concatenate(outs, axis=1),)
    return _rowcall("xattn_fwd", fn, [(qm, D_MODEL, 0)], [kn, v, mq_norm_w], [(D_MODEL, BF16)], [], bm,
                    qm.shape[0])[0]


def _xattn_bwd(qm, kn, v, mq_norm_w, do, bm):
    def fn(qb, dob, knb, vb, qw):
        dq_parts, dkn_parts, dv_parts = [], [], []
        dqw = jnp.zeros((1, MEM_DH), F32)
        for h in range(MEM_HEADS):
            sl = slice(MEM_DH * h, MEM_DH * (h + 1))
            rq, xh, qn, p = _xattn_head(qb[:, sl], knb[:, sl], qw)
            doh = dob[:, sl].astype(BF16)
            dp = _dot(doh, vb[:, sl], 1, 1)
            ds = (p * (dp - jnp.sum(dp * p, axis=-1, keepdims=True)) * (1.0 / 16.0)).astype(BF16)
            dqn = _dot(ds, knb[:, sl])
            dkn_parts.append(_dot(ds, qn, 0, 0))
            dv_parts.append(_dot(p, doh, 0, 0))
            dqw = dqw + jnp.sum(dqn * xh, axis=0, keepdims=True)
            dxh = dqn * qw
            dq_parts.append(rq * (dxh - xh * jnp.mean(dxh * xh, axis=-1, keepdims=True)))
        return (jnp.concatenate(dq_parts, axis=1), jnp.concatenate(dkn_parts, axis=1),
                jnp.concatenate(dv_parts, axis=1), dqw)
    return _rowcall("xattn_bwd", fn, [(qm, D_MODEL, 0), (do, D_MODEL, 0)], [kn, v, mq_norm_w],
                    [(D_MODEL, BF16)], [(MEM_LEN, D_MODEL), (MEM_LEN, D_MODEL), (1, MEM_DH)], bm, qm.shape[0])


FF_BN = 1408
FF_NB = D_FF // FF_BN


def _ffn_up(h3, w_gate_up, rows, bm):
    def body(h_ref, wg_ref, wu_ref, gate_ref, up_ref, act_ref):
        hb = h_ref[...]
        gate = _dot(hb, wg_ref[...])
        up = _dot(hb, wu_ref[...])
        gate_ref[...] = gate.astype(BF16)
        up_ref[...] = up.astype(BF16)
        act_ref[...] = (gate * _sigmoid(gate) * up).astype(BF16)

    out_blk = pl.BlockSpec((bm, FF_BN), lambda i, j: (i, j))
    return pl.pallas_call(
        body, name="ffn_up", grid=(rows // bm, FF_NB),
        in_specs=[pl.BlockSpec((bm, D_MODEL), lambda i, j: (i, 0)),
                  pl.BlockSpec((D_MODEL, FF_BN), lambda i, j: (0, j)),
                  pl.BlockSpec((D_MODEL, FF_BN), lambda i, j: (0, FF_NB + j))],
        out_specs=[out_blk, out_blk, out_blk],
        out_shape=[jax.ShapeDtypeStruct((rows, D_FF), BF16)] * 3,
        compiler_params=_cparams("parallel", "arbitrary"),
    )(h3, w_gate_up, w_gate_up)


def _ffn_act_bwd(dy, w_down, gate, up, rows, bm):
    def body(dy_ref, wd_ref, gate_ref, up_ref, o_ref):
        dact = _dot(dy_ref[...], wd_ref[...], 1, 1)
        g, u = gate_ref[...].astype(F32), up_ref[...].astype(F32)
        sg = _sigmoid(g)
        o_ref[0] = (dact * u * (sg * (1.0 + g * (1.0 - sg)))).astype(BF16)
        o_ref[1] = (dact * (g * sg)).astype(BF16)

    blk = pl.BlockSpec((bm, FF_BN), lambda i, j: (i, j))
    return pl.pallas_call(
        body, name="ffn_act_bwd", grid=(rows // bm, FF_NB),
        in_specs=[pl.BlockSpec((bm, D_MODEL), lambda i, j: (i, 0)),
                  pl.BlockSpec((FF_BN, D_MODEL), lambda i, j: (j, 0)), blk, blk],
        out_specs=pl.BlockSpec((2, bm, FF_BN), lambda i, j: (0, i, j)),
        out_shape=jax.ShapeDtypeStruct((2, rows, D_FF), BF16),
        compiler_params=_cparams("parallel", "arbitrary"),
    )(dy, w_down, gate, up)


ROW_SHARDED = ("w_out", "w_mq", "w_mo", "w_down")
TAIL_ROWS = 16
AG_FIRST = ("w_in", "w_gk_up")
AG_MID = ("w_out", "w_mq", "w_mkv")
AG_LATE = ("w_mo", "w_gate_up", "w_down")
RS_FFN = ("w_gate_up", "w_down")
RS_MID = ("w_out", "w_mq", "w_mkv", "w_mo")
RS_WIN = ("w_in", "w_gk_up")
ADAM_BLOCK = {"w_in": 256, "w_out": 128, "w_mq": 128, "w_mkv": 256, "w_mo": 128, "w_gate_up": 256,
              "w_down": 176, "w_gk_up": 16}


def _full_weight(name, got):
    r, c = SHARD_SHAPE[name]
    if name in ROW_SHARDED:
        return got.reshape(N_DEV * r, c)
    return got.transpose(1, 0, 2).reshape(r, N_DEV * c)


def _tail_rows(rows):
    return jnp.concatenate(rows + [jnp.zeros((TAIL_ROWS - len(rows), 1024), F32)], axis=0)


def _split_for_owners(name, full):
    r, c = SHARD_SHAPE[name]
    if name in ROW_SHARDED:
        return full.reshape(N_DEV, r, c)
    return full.reshape(r, N_DEV, c).transpose(1, 0, 2)


def _repl_row(a):
    flat = a.reshape(-1)
    return jnp.concatenate([flat, jnp.zeros((1024 - flat.shape[0],), flat.dtype)]).reshape(1, 1024)


def _local_step(x, mem, tgt, wf, rp, shards=None, scatter=False):
    rows = x.shape[0]
    bm = min(512, rows)
    bmx = min(256, rows)
    mt = min(MM_TILE, rows)
    kt = min(512, rows)
    w_cat, wgk_pad = wf["w_cat"], wf["wgk_pad"]
    wg_t = jnp.tile(rp["gla_norm_w"], (1, GLA_HEADS))
    ws_t = jnp.tile(rp["sb_norm_w"], (1, SB_HEADS))
    lane = jnp.arange(GLA_V_W)
    grp_g = (lane[:, None] // GLA_DV == lane[None, :] // GLA_DV).astype(BF16)
    grp_s = (lane[:, None] // SB_DH == lane[None, :] // SB_DH).astype(BF16)

    h1 = _rms_fwd("mix_norm_fwd", x, rp["mix_norm_w"], bm)
    proj = _matmul("in_proj", h1, w_cat, "nn", rows, PROJ_W, D_MODEL, F32, mt, 640, D_MODEL)
    gk = _gate_fwd(proj, wgk_pad, rp["b_gk"], bm)
    if shards is None:
        o_g, s_all = _gla_fwd(proj, gk, rows)
        (o_s,) = _sb_fwd(proj, rows)
    else:
        mine = lambda names: [shards[n].astype(BF16) for n in names]
        o_g, s_all, got_mid = _gla_fwd(proj, gk, rows, _Exchange(mine(AG_MID), scatter=False))
        o_s, got_late = _sb_fwd(proj, rows, _Exchange(mine(AG_LATE), scatter=False))
        wf = {**wf, **{n: _full_weight(n, a) for n, a in zip(AG_MID + AG_LATE, got_mid + got_late)}}
    cat = _mix_out_fwd(o_g, proj, o_s, wg_t, ws_t, grp_g, grp_s, bm)
    x1 = _matmul("out_proj", cat, wf["w_out"], "nn", rows, D_MODEL, D_MODEL, F32, mt, MM_TILE, D_MODEL, residual=x)
    h2 = _rms_fwd("xattn_norm_fwd", x1, rp["xattn_norm_w"], bm)
    qm = _matmul("mq_proj", h2, wf["w_mq"], "nn", rows, D_MODEL, D_MODEL, F32, mt, MM_TILE, D_MODEL)
    mem_n, kpre, kn, v_m = _mem_kv_fwd(mem, rp["mem_norm_w"], wf["w_mkv"], rp["mk_norm_w"])
    o_m = _xattn_fwd(qm, kn, v_m, rp["mq_norm_w"], bmx)
    x2 = _matmul("mo_proj", o_m, wf["w_mo"], "nn", rows, D_MODEL, D_MODEL, F32, mt, MM_TILE, D_MODEL, residual=x1)
    h3 = _rms_fwd("ffn_norm_fwd", x2, rp["ffn_norm_w"], bm)
    gate, up, act = _ffn_up(h3, wf["w_gate_up"], rows, mt)
    y = _matmul("ffn_down", act, wf["w_down"], "nn", rows, D_MODEL, D_FF, F32, mt, MM_TILE, FF_BN, residual=x2)
    dy, sq = _loss_kernel(y, tgt, bm)

    g = {}
    dgu = _ffn_act_bwd(dy, wf["w_down"], gate, up, rows, bm)
    g["w_down"] = _matmul("grad_w_down", act, dy, "tn", D_FF, D_MODEL, rows, F32, FF_BN, MM_TILE, kt)
    nkb = FF_NB
    dh3 = _matmul("ffn_up_bwd", dgu, wf["w_gate_up"], "nt", rows, D_MODEL, 2 * D_FF, F32, mt, MM_TILE, FF_BN,
                  a_spec=pl.BlockSpec((None, mt, FF_BN), lambda i, j, kk: (kk // nkb, i, kk % nkb)))
    g["w_gate_up"] = _matmul(
        "grad_w_gate_up", h3, dgu, "tn", D_MODEL, 2 * D_FF, rows, F32, MM_TILE, FF_BN, kt,
        b_spec=pl.BlockSpec((None, kt, FF_BN), lambda i, j, kk: (j // nkb, kk, j % nkb)))
    dx2, g["ffn_norm_w"] = _rms_bwd("ffn_norm_bwd", x2, rp["ffn_norm_w"], dh3, dy, bm)

    do_m = _matmul("mo_proj_bwd", dx2, wf["w_mo"], "nt", rows, D_MODEL, D_MODEL, BF16, mt, MM_TILE, D_MODEL)
    g["w_mo"] = _matmul("grad_w_mo", o_m, dx2, "tn", D_MODEL, D_MODEL, rows, F32, MM_TILE, MM_TILE, kt)
    dqm, dkn, dv_m, g["mq_norm_w"] = _xattn_bwd(qm, kn, v_m, rp["mq_norm_w"], do_m, bmx)
    g["w_mkv"], g["mk_norm_w"], g["mem_norm_w"] = _mem_kv_bwd(
        mem, rp["mem_norm_w"], wf["w_mkv"], rp["mk_norm_w"], mem_n, kpre, dkn, dv_m)
    dh2 = _matmul("mq_proj_bwd", dqm, wf["w_mq"], "nt", rows, D_MODEL, D_MODEL, F32, mt, MM_TILE, D_MODEL)
    g["w_mq"] = _matmul("grad_w_mq", h2, dqm, "tn", D_MODEL, D_MODEL, rows, F32, MM_TILE, MM_TILE, kt)
    dx1, g["xattn_norm_w"] = _rms_bwd("xattn_norm_bwd", x1, rp["xattn_norm_w"], dh2, dx2, bm)

    dcat = _matmul("out_proj_bwd", dx1, wf["w_out"], "nt", rows, D_MODEL, D_MODEL, F32, mt, MM_TILE, D_MODEL)
    g["w_out"] = _matmul("grad_w_out", cat, dx1, "tn", D_MODEL, D_MODEL, rows, F32, MM_TILE, MM_TILE, kt)
    do_g, dgg, do_s, dwg, dws = _mix_out_bwd(dcat, o_g, proj, o_s, wg_t, ws_t, grp_g, grp_s, bm)
    g["gla_norm_w"] = dwg.reshape(GLA_HEADS, GLA_DV).sum(axis=0, keepdims=True)
    g["sb_norm_w"] = dws.reshape(SB_HEADS, SB_DH).sum(axis=0, keepdims=True)
    recv = {}
    to_owners = lambda names: [_split_for_owners(n, g[n].astype(BF16)) for n in names]
    if scatter:
        dq_s, dk_s, dv_s, got = _sb_bwd(proj, o_s, do_s, rows, _Exchange(to_owners(RS_FFN), scatter=True))
        recv.update(zip(RS_FFN, got))
        dq_g, dk_g, dv_g, dgk, got = _gla_bwd(proj, gk, do_g, s_all, rows, _Exchange(to_owners(RS_MID), scatter=True))
        recv.update(zip(RS_MID, got))
    else:
        dq_s, dk_s, dv_s = _sb_bwd(proj, o_s, do_s, rows)
        dq_g, dk_g, dv_g, dgk = _gla_bwd(proj, gk, do_g, s_all, rows)
    dlr, dwgk, g["b_gk"] = _gate_bwd(proj, wgk_pad, rp["b_gk"], dgk, bm)
    g["w_gk_up"] = dwgk[:GATE_RANK]
    dproj = jnp.concatenate([dq_g.astype(BF16), dk_g.astype(BF16), dv_g.astype(BF16), dgg, dq_s.astype(BF16),
                             dk_s.astype(BF16), dv_s.astype(BF16), dlr], axis=1)
    dw_cat = _matmul("grad_w_in", h1, dproj, "tn", D_MODEL, PROJ_W, rows, F32, MM_TILE, 640, kt)
    g["w_in"] = jnp.concatenate([dw_cat[:, :C_QS], dw_cat[:, C_LR:C_LR + GATE_RANK], dw_cat[:, C_QS:C_LR]], axis=1)
    if scatter:
        dh1, got = _matmul("in_proj_bwd", dproj, w_cat, "nt", rows, D_MODEL, PROJ_W, F32, mt, MM_TILE, 640,
                           exchange=_Exchange(to_owners(RS_WIN), scatter=True))
        recv.update(zip(RS_WIN, got))
    else:
        dh1 = _matmul("in_proj_bwd", dproj, w_cat, "nt", rows, D_MODEL, PROJ_W, F32, mt, MM_TILE, 640)
    dx, g["mix_norm_w"] = _rms_bwd("mix_norm_bwd", x, rp["mix_norm_w"], dh1, dx1, bm)
    return sq, dx, g, recv


def _first_weights(got):
    w_in = _full_weight("w_in", got[0])
    lr_end = C_QS + GATE_RANK
    w_cat = jnp.concatenate([w_in[:, :C_QS], w_in[:, lr_end:], w_in[:, C_QS:lr_end],
                             jnp.zeros((D_MODEL, PROJ_W - D_IN), BF16)], axis=1)
    wgk = _full_weight("w_gk_up", got[1])
    return {"w_cat": w_cat, "wgk_pad": jnp.concatenate([wgk, jnp.zeros((LANES - GATE_RANK, GLA_QK_W), BF16)], axis=0)}


def kernel(x, mem, mix_norm_w, w_in, w_gk_up, b_gk, gla_norm_w, sb_norm_w, w_out, xattn_norm_w, mem_norm_w, w_mq, w_mkv, mq_norm_w, mk_norm_w, w_mo, ffn_norm_w, w_gate_up, w_down, loss_target, m_mix_norm_w, m_w_in, m_w_gk_up, m_b_gk, m_gla_norm_w, m_sb_norm_w, m_w_out, m_xattn_norm_w, m_mem_norm_w, m_w_mq, m_w_mkv, m_mq_norm_w, m_mk_norm_w, m_w_mo, m_ffn_norm_w, m_w_gate_up, m_w_down, v_mix_norm_w, v_w_in, v_w_gk_up, v_b_gk, v_gla_norm_w, v_sb_norm_w, v_w_out, v_xattn_norm_w, v_mem_norm_w, v_w_mq, v_w_mkv, v_mq_norm_w, v_mk_norm_w, v_w_mo, v_ffn_norm_w, v_w_gate_up, v_w_down):
    given = dict(locals())
    w = {n: given[n][0] for n in WEIGHTS}
    m = {n: given["m_" + n][0] for n in WEIGHTS}
    v = {n: given["v_" + n][0] for n in WEIGHTS}

    wf = _first_weights(_exchange_call("gather_first_weights", [w[n].astype(BF16) for n in AG_FIRST], scatter=False))
    rp = {n: w[n].reshape(1, -1) for n in REPL}
    sq, dx, g, recv = _local_step(x[0], mem[0], loss_target[0], wf, rp, shards=w, scatter=True)

    loss_row = _repl_row(jnp.sum(sq).reshape(1) * (0.5 / D_MODEL))
    tail = _tail_rows([_repl_row(g[n]) for n in REPL] + [loss_row])
    (recv_tail,) = _exchange_call("scatter_tail_gradients", [jnp.broadcast_to(tail[None], (N_DEV, TAIL_ROWS, 1024))],
                                  scatter=True)
    results = {n: _adamw("sum_adamw_" + n, recv[n], w[n], m[n], v[n], ADAM_BLOCK[n]) for n in SHARD_SHAPE}
    tail_out = _adamw("sum_adamw_tail", recv_tail, *[_tail_rows([_repl_row(t[n]) for n in REPL]) for t in (w, m, v)],
                      TAIL_ROWS)

    def output(kind, name):
        if name in results:
            return results[name][kind][None]
        return tail_out[kind][REPL.index(name), :w[name].shape[-1]].reshape(1, -1)

    loss = tail_out[0][len(REPL), 0]
    outs = [loss, dx[None]]
    for kind in range(4):
        outs += [output(kind, n) for n in WEIGHTS]
    return tuple(outs)
```

```python
import functools
import math

import jax
import jax.numpy as jnp
from jax import lax
from jax.experimental import pallas as pl
from jax.experimental.pallas import tpu as pltpu

F32 = jnp.float32
BF16 = jnp.bfloat16

N_DEV = 8
D_MODEL = 1024
GLA_HEADS = 4
GLA_DK = 64
GLA_DV = 128
GLA_CHUNK = 64
GLA_STEP_CHUNKS = 4
GLA_QK_W = GLA_HEADS * GLA_DK
GLA_V_W = GLA_HEADS * GLA_DV
GATE_RANK = 16
SB_HEADS = 8
SB_DH = 64
SB_W = SB_HEADS * SB_DH
SB_BLK = 128
SB_QT = 1024
SB_LIVE_ROWS = 256
SB_DEAD = -104.0
MEM_LEN = 256
MEM_HEADS = 4
MEM_DH = 256
D_FF = 2816
D_IN = 3088
RMS_EPS = 1e-6
LANES = 128

PROJ_W = 3200
C_QG, C_KG, C_VG, C_GG, C_QS, C_KS, C_VS, C_LR = 0, 256, 512, 1024, 1536, 2048, 2560, 3072

ADAM_LR, ADAM_B1, ADAM_B2, ADAM_EPS, ADAM_WD, ADAM_STEP = 0.001, 0.9, 0.999, 1e-08, 0.01, 10

SHARD_SHAPE = {"w_in": (1024, 386), "w_out": (128, 1024), "w_mq": (128, 1024), "w_mkv": (1024, 256),
               "w_mo": (128, 1024), "w_gate_up": (1024, 704), "w_down": (352, 1024), "w_gk_up": (16, 32)}
REPL = ("mix_norm_w", "b_gk", "gla_norm_w", "sb_norm_w", "xattn_norm_w", "mem_norm_w", "mq_norm_w",
        "mk_norm_w", "ffn_norm_w")
WEIGHTS = ("mix_norm_w", "w_in", "w_gk_up", "b_gk", "gla_norm_w", "sb_norm_w", "w_out", "xattn_norm_w",
           "mem_norm_w", "w_mq", "w_mkv", "mq_norm_w", "mk_norm_w", "w_mo", "ffn_norm_w", "w_gate_up", "w_down")
VMEM_LIMIT = 56 * 1024 * 1024
MM_TILE = 1024


def _cparams(*sem):
    return pltpu.CompilerParams(dimension_semantics=sem if sem else None, vmem_limit_bytes=VMEM_LIMIT)


def _dot(a, b, ca=1, cb=0):
    return lax.dot_general(a.astype(BF16), b.astype(BF16), (((ca,), (cb,)), ((), ())),
                           preferred_element_type=F32)


def _split(x, parts):
    out = []
    for _ in range(parts - 1):
        hi = x.astype(BF16)
        out.append(hi)
        x = x - hi.astype(F32)
    out.append(x.astype(BF16))
    return out


def _dot_lhs_exact(x, m, ca=1, cb=0, parts=3):
    acc = None
    for p in _split(x, parts):
        t = _dot(p, m, ca, cb)
        acc = t if acc is None else acc + t
    return acc


def _dot_rhs_exact(m, x, ca=1, cb=0, parts=3):
    acc = None
    for p in _split(x, parts):
        t = _dot(m, p, ca, cb)
        acc = t if acc is None else acc + t
    return acc


def _dot3(a, b, ca=1, cb=0):
    a_hi, a_lo = _split(a, 2)
    b_hi, b_lo = _split(b, 2)
    return _dot(a_hi, b_hi, ca, cb) + (_dot(a_hi, b_lo, ca, cb) + _dot(a_lo, b_hi, ca, cb))


def _log_sigmoid(z):
    return jnp.minimum(z, 0.0) - jnp.log(1.0 + jnp.exp(-jnp.abs(z)))


def _sigmoid(z):
    e = jnp.exp(-jnp.abs(z))
    return jnp.where(z >= 0, 1.0, e) / (1.0 + e)


def _iota2(shape, dim):
    return lax.broadcasted_iota(jnp.int32, shape, dim)


def _rowcall(name, fn, row_ins, full_ins, row_outs, acc_outs, bm, rows):
    n_in = len(row_ins) + len(full_ins)
    n_row = len(row_outs)

    def body(*refs):
        ins, outs = refs[:n_in], refs[n_in:]
        res = fn(*[r[...] for r in ins])
        for r, v in zip(outs[:n_row], res[:n_row]):
            r[...] = v.astype(r.dtype)
        first = pl.program_id(0) == 0
        for r, v in zip(outs[n_row:], res[n_row:]):
            def init(r=r):
                r[...] = jnp.zeros(r.shape, r.dtype)
            pl.when(first)(init)
            r[...] += v

    in_specs = [pl.BlockSpec((bm, w), functools.partial(lambda i, c: (i, c), c=c)) for _, w, c in row_ins]
    in_specs += [pl.BlockSpec(a.shape, lambda i: (0, 0)) for a in full_ins]
    out_specs = [pl.BlockSpec((bm, w), lambda i: (i, 0)) for w, _ in row_outs]
    out_specs += [pl.BlockSpec(s, lambda i: (0, 0)) for s in acc_outs]
    out_shape = [jax.ShapeDtypeStruct((rows, w), dt) for w, dt in row_outs]
    out_shape += [jax.ShapeDtypeStruct(s, F32) for s in acc_outs]
    return pl.pallas_call(
        body, name=name, grid=(rows // bm,), in_specs=in_specs, out_specs=out_specs, out_shape=out_shape,
        compiler_params=_cparams("arbitrary"),
    )(*[a for a, _, _ in row_ins], *full_ins)


def _matmul(name, a, b, mode, m, n, k, out_dtype, bm, bn, bk, residual=None, a_spec=None, b_spec=None,
            exchange=None):
    bm, bn, bk = min(bm, m), min(bn, n), min(bk, k)
    nk = k // bk
    ca, cb = {"nn": (1, 0), "nt": (1, 1), "tn": (0, 0)}[mode]
    if a_spec is None:
        a_spec = (pl.BlockSpec((bk, bm), lambda i, j, kk: (kk, i)) if mode == "tn"
                  else pl.BlockSpec((bm, bk), lambda i, j, kk: (i, kk)))
    if b_spec is None:
        b_spec = (pl.BlockSpec((bn, bk), lambda i, j, kk: (j, kk)) if mode == "nt"
                  else pl.BlockSpec((bk, bn), lambda i, j, kk: (kk, j)))
    has_res = residual is not None

    def body(*refs):
        a_ref, b_ref = refs[0], refs[1]
        res_ref = refs[2] if has_res else None
        o_ref = refs[2 + has_res]
        part = _dot(a_ref[...], b_ref[...], ca, cb)

        def finish(total):
            if has_res:
                total = total + res_ref[...]
            o_ref[...] = total.astype(o_ref.dtype)

        if nk == 1:
            finish(part)
        else:
            acc_ref = refs[3 + has_res]
            kk = pl.program_id(2)

            @pl.when(kk == 0)
            def _():
                acc_ref[...] = part

            @pl.when(kk > 0)
            def _():
                acc_ref[...] += part

            @pl.when(kk == nk - 1)
            def _():
                finish(acc_ref[...])

    in_specs = [a_spec, b_spec]
    args = [a, b]
    if has_res:
        in_specs.append(pl.BlockSpec((bm, bn), lambda i, j, kk: (i, j)))
        args.append(residual)
    sem = ("parallel", "parallel", "arbitrary") if exchange is None else ("arbitrary",) * 3
    res = _hosted_call(
        body, exchange, grid=(m // bm, n // bn, nk), in_specs=in_specs,
        out_specs=[pl.BlockSpec((bm, bn), lambda i, j, kk: (i, j))],
        out_shape=[jax.ShapeDtypeStruct((m, n), out_dtype)],
        scratch_shapes=[pltpu.VMEM((bm, bn), F32)] if nk > 1 else [],
        args=args, name=name, compiler_params=_cparams(*sem))
    return res[0] if exchange is None else res


def _peer(mask):
    x, y, c = lax.axis_index("x"), lax.axis_index("y"), lax.axis_index("c")
    mx, my, mc = (mask >> 2) & 1, (mask >> 1) & 1, mask & 1
    px, py, pc = (1 - x if mx else x), (1 - y if my else y), (1 - c if mc else c)
    return (px, py, pc), 4 * px + 2 * py + pc


def _my_index():
    return 4 * lax.axis_index("x") + 2 * lax.axis_index("y") + lax.axis_index("c")


class _Exchange:
    def __init__(self, srcs, scatter):
        self.srcs, self.scatter, n = list(srcs), scatter, len(srcs)
        self.in_specs = [pl.BlockSpec(memory_space=pl.ANY)] * n
        self.out_specs = [pl.BlockSpec(memory_space=pl.ANY)] * n
        self.out_shapes = [jax.ShapeDtypeStruct((N_DEV,) + s.shape[-2:], s.dtype) for s in self.srcs]
        self.scratch = [pltpu.SemaphoreType.DMA((n * (N_DEV - 1),)), pltpu.SemaphoreType.DMA((n * (N_DEV - 1),)),
                        pltpu.SemaphoreType.DMA((n,))]

    def _copies(self, src_refs, out_refs, sems):
        send_sems, recv_sems, local_sems = sems
        me = _my_index()
        copies = []
        for a, (src_ref, out_ref) in enumerate(zip(src_refs, out_refs)):
            copies.append(pltpu.make_async_copy(src_ref.at[me] if self.scatter else src_ref, out_ref.at[me],
                                                local_sems.at[a]))
            for mask in range(1, N_DEV):
                peer, peer_index = _peer(mask)
                slot = a * (N_DEV - 1) + mask - 1
                copies.append(pltpu.make_async_remote_copy(
                    src_ref=src_ref.at[peer_index] if self.scatter else src_ref, dst_ref=out_ref.at[me],
                    send_sem=send_sems.at[slot], recv_sem=recv_sems.at[slot],
                    device_id=peer, device_id_type=pl.DeviceIdType.MESH))
        return copies

    def start(self, src_refs, out_refs, sems):
        for cp in self._copies(src_refs, out_refs, sems):
            cp.start()

    def wait(self, src_refs, out_refs, sems):
        for cp in self._copies(src_refs, out_refs, sems):
            cp.wait()


def _hosted_call(body, ex, grid, in_specs, out_specs, out_shape, scratch_shapes, args, **kw):
    if ex is None:
        return pl.pallas_call(body, grid=grid, in_specs=in_specs, out_specs=out_specs, out_shape=out_shape,
                              scratch_shapes=scratch_shapes, **kw)(*args)
    n_in, n_out, n_scr, n_x = len(in_specs), len(out_specs), len(scratch_shapes), len(ex.srcs)

    def hosted(*refs):
        ins, src_refs = refs[:n_in], refs[n_in:n_in + n_x]
        refs = refs[n_in + n_x:]
        outs, out_refs = refs[:n_out], refs[n_out:n_out + n_x]
        refs = refs[n_out + n_x:]
        scr, sems = refs[:n_scr], refs[n_scr:]
        ids = [pl.program_id(a) for a in range(len(grid))]
        first = functools.reduce(jnp.logical_and, [p == 0 for p in ids])
        last = functools.reduce(jnp.logical_and, [p == n - 1 for p, n in zip(ids, grid)])

        @pl.when(first)
        def _():
            ex.start(src_refs, out_refs, sems)

        body(*ins, *outs, *scr)

        @pl.when(last)
        def _():
            ex.wait(src_refs, out_refs, sems)

    res = pl.pallas_call(
        hosted, grid=grid, in_specs=list(in_specs) + ex.in_specs, out_specs=list(out_specs) + ex.out_specs,
        out_shape=list(out_shape) + ex.out_shapes, scratch_shapes=list(scratch_shapes) + ex.scratch, **kw,
    )(*args, *ex.srcs)
    return tuple(res[:n_out]) + (list(res[n_out:]),)


def _exchange_call(name, srcs, scatter):
    ex = _Exchange(srcs, scatter)
    n_x = len(ex.srcs)

    def body(*refs):
        src_refs, out_refs, sems = refs[:n_x], refs[n_x:2 * n_x], refs[2 * n_x:]
        ex.start(src_refs, out_refs, sems)
        ex.wait(src_refs, out_refs, sems)

    return list(pl.pallas_call(
        body, name=name, in_specs=ex.in_specs, out_specs=ex.out_specs, out_shape=ex.out_shapes,
        scratch_shapes=ex.scratch, compiler_params=pltpu.CompilerParams(has_side_effects=True),
    )(*ex.srcs))


def _adamw(name, recv, w, m, v, block):
    rows, cols = w.shape
    c1 = 1.0 - ADAM_B1 ** ADAM_STEP
    c2 = 1.0 - ADAM_B2 ** ADAM_STEP

    def body(r_ref, w_ref, m_ref, v_ref, g_out, d_out, m_out, v_out):
        g = r_ref[0].astype(F32)
        for s in range(1, N_DEV):
            g = g + r_ref[s].astype(F32)
        m_new = ADAM_B1 * m_ref[...] + (1.0 - ADAM_B1) * g
        v_new = ADAM_B2 * v_ref[...] + (1.0 - ADAM_B2) * (g * g)
        m_hat = m_new / c1
        v_hat = v_new / c2
        g_out[...] = g
        d_out[...] = -ADAM_LR * (m_hat / (jnp.sqrt(v_hat) + ADAM_EPS) + ADAM_WD * w_ref[...])
        m_out[...] = m_new
        v_out[...] = v_new

    blk = pl.BlockSpec((block, cols), lambda i: (i, 0))
    return pl.pallas_call(
        body, name=name, grid=(rows // block,),
        in_specs=[pl.BlockSpec((N_DEV, block, cols), lambda i: (0, i, 0)), blk, blk, blk],
        out_specs=[blk] * 4, out_shape=[jax.ShapeDtypeStruct((rows, cols), F32)] * 4,
        compiler_params=_cparams("parallel"),
    )(recv, w, m, v)


def _rms_fwd(name, x, w, bm):
    def fn(xb, wb):
        r = lax.rsqrt(jnp.mean(xb * xb, axis=-1, keepdims=True) + RMS_EPS)
        return (xb * r * wb,)
    return _rowcall(name, fn, [(x, D_MODEL, 0)], [w], [(D_MODEL, BF16)], [], bm, x.shape[0])[0]


def _rms_bwd(name, x, w, dh, dres, bm):
    def fn(xb, dhb, drb, wb):
        r = lax.rsqrt(jnp.mean(xb * xb, axis=-1, keepdims=True) + RMS_EPS)
        xh = xb * r
        dxh = dhb.astype(F32) * wb
        dx = drb + r * (dxh - xh * jnp.mean(dxh * xh, axis=-1, keepdims=True))
        return dx, jnp.sum(dhb.astype(F32) * xh, axis=0, keepdims=True)
    return _rowcall(name, fn, [(x, D_MODEL, 0), (dh, D_MODEL, 0), (dres, D_MODEL, 0)], [w],
                    [(D_MODEL, F32)], [(1, D_MODEL)], bm, x.shape[0])


def _gate_fwd(proj, wgk_pad, b_gk, bm):
    def fn(lr, wg, bg):
        z = _dot(lr, wg) + bg
        return (_log_sigmoid(z) * (1.0 / 16.0),)
    return _rowcall("gla_gate_fwd", fn, [(proj, LANES, C_LR // LANES)], [wgk_pad, b_gk],
                    [(GLA_QK_W, F32)], [], bm, proj.shape[0])[0]


def _gate_bwd(proj, wgk_pad, b_gk, dgk, bm):
    def fn(lr, dg, wg, bg):
        z = _dot(lr, wg) + bg
        dz = dg * _sigmoid(-z) * (1.0 / 16.0)
        return _dot(dz, wg, 1, 1), _dot(lr, dz, 0, 0), jnp.sum(dz, axis=0, keepdims=True)
    return _rowcall("gla_gate_bwd", fn, [(proj, LANES, C_LR // LANES), (dgk, GLA_QK_W, 0)], [wgk_pad, b_gk],
                    [(LANES, BF16)], [(LANES, GLA_QK_W), (1, GLA_QK_W)], bm, proj.shape[0])


def _group_mean(x, g, size):
    return _dot_lhs_exact(x, g, parts=2) * (1.0 / size)


def _mix_out_fwd(o_g, proj, o_s, wg_t, ws_t, grp_g, grp_s, bm):
    def fn(og, gg, os_, wg, ws, gmat, smat):
        rg = lax.rsqrt(_group_mean(og * og, gmat, GLA_DV) + RMS_EPS)
        yg = og * rg * wg * (gg * _sigmoid(gg))
        rs = lax.rsqrt(_group_mean(os_ * os_, smat, SB_DH) + RMS_EPS)
        ys = os_ * rs * ws
        return (jnp.concatenate([yg, ys], axis=1),)
    return _rowcall("mix_out_fwd", fn, [(o_g, GLA_V_W, 0), (proj, GLA_V_W, C_GG // GLA_V_W), (o_s, SB_W, 0)],
                    [wg_t, ws_t, grp_g, grp_s], [(D_MODEL, BF16)], [], bm, o_g.shape[0])[0]


def _mix_out_bwd(dcat, o_g, proj, o_s, wg_t, ws_t, grp_g, grp_s, bm):
    def fn(dyg, dys, og, gg, os_, wg, ws, gmat, smat):
        dyg = dyg.astype(F32)
        dys = dys.astype(F32)
        rg = lax.rsqrt(_group_mean(og * og, gmat, GLA_DV) + RMS_EPS)
        xh = og * rg
        sg = _sigmoid(gg)
        silu = gg * sg
        dxh = dyg * wg * silu
        dgg = dyg * xh * wg * (sg * (1.0 + gg * (1.0 - sg)))
        dwg = jnp.sum(dyg * xh * silu, axis=0, keepdims=True)
        dog = rg * (dxh - xh * _group_mean(dxh * xh, gmat, GLA_DV))
        rs = lax.rsqrt(_group_mean(os_ * os_, smat, SB_DH) + RMS_EPS)
        xs = os_ * rs
        dxs = dys * ws
        dws = jnp.sum(dys * xs, axis=0, keepdims=True)
        dos = rs * (dxs - xs * _group_mean(dxs * xs, smat, SB_DH))
        return dog, dgg, dos, dwg, dws
    return _rowcall("mix_out_bwd", fn,
                    [(dcat, GLA_V_W, 0), (dcat, SB_W, 1), (o_g, GLA_V_W, 0), (proj, GLA_V_W, C_GG // GLA_V_W),
                     (o_s, SB_W, 0)],
                    [wg_t, ws_t, grp_g, grp_s], [(GLA_V_W, F32), (GLA_V_W, BF16), (SB_W, F32)],
                    [(1, GLA_V_W), (1, SB_W)], bm, o_g.shape[0])


def _loss_kernel(y, tgt, bm):
    def fn(yb, tb):
        err = yb - tb
        return err * (1.0 / D_MODEL), jnp.sum(err * err, axis=0, keepdims=True)
    return _rowcall("loss_head", fn, [(y, D_MODEL, 0), (tgt, D_MODEL, 0)], [], [(D_MODEL, F32)], [(1, D_MODEL)],
                    bm, y.shape[0])


def _sb_tri(inclusive):
    j, s = _iota2((2 * SB_BLK, 2 * SB_BLK), 0), _iota2((2 * SB_BLK, 2 * SB_BLK), 1)
    j = jnp.where(j >= SB_BLK, j - SB_BLK, j)
    keep = (j >= s) if inclusive else (j > s)
    return ((s >= SB_BLK) | keep).astype(BF16)


def _dot_hilo(x, m2):
    hi, lo = _split(x, 2)
    return _dot(jnp.concatenate([hi, lo], axis=1), m2)


def _sb_mask(n):
    return _iota2((n, SB_BLK), 1) < _iota2((n, SB_BLK), 0)


def _add_rows(full, part, row0, row1):
    pieces = [full[:row0]] if row0 else []
    pieces.append(full[row0:row1] + part)
    if row1 < full.shape[0]:
        pieces.append(full[row1:])
    return pieces[0] if len(pieces) == 1 else jnp.concatenate(pieces, axis=0)


def _sb_visit(tiles, carry, c_slots, qt, diag_start, n_left, left_start):
    for sub in reversed(range(qt // SB_BLK)):
        carry = tiles(diag_start(sub), carry, True, sub * SB_BLK, qt)

    def sweep(row1):
        def step(jj, cr):
            return tiles(left_start(jj), cr, False, 0, row1)
        return functools.partial(_sb_sweep, n_left, step, c_slots=c_slots)

    if qt <= SB_LIVE_ROWS:
        return sweep(qt)(carry)
    rest_dead = jnp.max(functools.reduce(jnp.maximum, [carry[s][SB_LIVE_ROWS:] for s in c_slots])) <= SB_DEAD
    return lax.cond(rest_dead, sweep(SB_LIVE_ROWS), sweep(qt), carry)


def _sb_sweep(n_tiles, step, carry, c_slots):
    def alive(state):
        jj, carry = state
        c_max = jnp.max(functools.reduce(jnp.maximum, [carry[s] for s in c_slots]))
        return jnp.logical_and(jj < n_tiles, c_max > SB_DEAD)

    def body(state):
        jj, carry = state
        return jj + 1, step(jj, carry)

    return lax.while_loop(alive, body, (jnp.int32(0), carry))[1]


def _row_blocks(n):
    return [slice(r, r + SB_BLK) for r in range(0, n, SB_BLK)]


def _hilo(x):
    hi, lo = _split(x, 2)
    return jnp.concatenate([hi, lo], axis=1)


def _sb_tile(q, k, c, tri_excl, diag):
    blocks = _row_blocks(q.shape[0])
    strict = _sb_mask(SB_BLK) if diag else None
    z = _dot(q, k, 1, 1)
    lbs, pieces = [], []
    for r, rs in enumerate(blocks):
        lb = _log_sigmoid(z[rs])
        l1 = lb - z[rs]
        if diag and r == 0:
            l1 = jnp.where(strict, l1, 0.0)
        lbs.append(lb)
        pieces.append(_hilo(l1))
    sums = _dot(jnp.concatenate(pieces, axis=0), tri_excl)
    a = []
    for r, rs in enumerate(blocks):
        ar = jnp.exp(lbs[r] + sums[rs, :SB_BLK] + c[rs])
        if diag and r == 0:
            ar = jnp.where(strict, ar, 0.0)
        a.append(ar.astype(BF16))
    return lbs, a, sums[:, SB_BLK:]


def _sb_fwd(proj, rows, exchange=None):
    qt = min(SB_QT, rows)
    subs = qt // SB_BLK

    def body(q_ref, k_ref, v_ref, o_ref):
        i = pl.program_id(1)
        tri_excl = _sb_tri(False)
        heads = [slice(SB_DH * hh, SB_DH * (hh + 1)) for hh in range(2)]
        qs = [(q_ref[:, sl] * 0.125).astype(BF16) for sl in heads]

        def tiles(start, carry, diag, row0, row1):
            out = []
            for hh, sl in enumerate(heads):
                o, c = carry[2 * hh], carry[2 * hh + 1]
                k = k_ref[pl.ds(start, SB_BLK), sl].astype(BF16)
                v = v_ref[pl.ds(start, SB_BLK), sl].astype(BF16)
                _, a, dc = _sb_tile(qs[hh][row0:row1], k, c[row0:row1], tri_excl, diag)
                out += [_add_rows(o, _dot(jnp.concatenate(a, axis=0), v), row0, row1), _add_rows(c, dc, row0, row1)]
            return tuple(out)

        carry = (jnp.zeros((qt, SB_DH), F32), jnp.zeros((qt, SB_BLK), F32)) * 2
        carry = _sb_visit(tiles, carry, (1, 3), qt,
                          lambda sub: pl.multiple_of(i * qt + sub * SB_BLK, SB_BLK), i * subs,
                          lambda jj: pl.multiple_of((i * subs - 1 - jj) * SB_BLK, SB_BLK))
        for hh, sl in enumerate(heads):
            o_ref[:, sl] = carry[2 * hh]

    return _hosted_call(
        body, exchange, grid=(SB_HEADS // 2, rows // qt),
        in_specs=[pl.BlockSpec((qt, LANES), lambda h, i: (i, C_QS // LANES + h)),
                  pl.BlockSpec((rows, LANES), lambda h, i: (0, C_KS // LANES + h)),
                  pl.BlockSpec((rows, LANES), lambda h, i: (0, C_VS // LANES + h))],
        out_specs=[pl.BlockSpec((qt, LANES), lambda h, i: (i, h))],
        out_shape=[jax.ShapeDtypeStruct((rows, SB_W), F32)], scratch_shapes=[],
        args=(proj, proj, proj), name="sb_attention_fwd", compiler_params=_cparams("arbitrary", "arbitrary"))


def _sb_bwd(proj, o_s, do_s, rows, exchange=None):
    qt = min(SB_QT, rows)
    subs = qt // SB_BLK

    def body(q_ref, k_ref, v_ref, o_ref, do_ref, dq_ref, dk_ref, dv_ref):
        i = pl.program_id(1)

        @pl.when(i == 0)
        def _():
            dk_ref[...] = jnp.zeros(dk_ref.shape, F32)
            dv_ref[...] = jnp.zeros(dv_ref.shape, F32)

        tri_excl, tri_incl = _sb_tri(False), _sb_tri(True)
        heads = [slice(SB_DH * hh, SB_DH * (hh + 1)) for hh in range(2)]
        qs = [(q_ref[:, sl] * 0.125).astype(BF16) for sl in heads]
        dobs = [do_ref[:, sl].astype(BF16) for sl in heads]
        dsums = [jnp.broadcast_to(jnp.sum(dob.astype(F32) * o_ref[:, sl], axis=1, keepdims=True), (qt, SB_BLK))
                 for dob, sl in zip(dobs, heads)]

        def tiles(start, carry, diag, row0, row1):
            out = []
            strict = _sb_mask(SB_BLK) if diag else None
            for hh, sl in enumerate(heads):
                dq, c, cp = carry[3 * hh:3 * hh + 3]
                q, dob = qs[hh][row0:row1], dobs[hh][row0:row1]
                dsum, cpr = dsums[hh][row0:row1], cp[row0:row1]
                blocks = _row_blocks(q.shape[0])
                k = k_ref[pl.ds(start, SB_BLK), sl].astype(BF16)
                v = v_ref[pl.ds(start, SB_BLK), sl].astype(BF16)
                lbs, a, dc = _sb_tile(q, k, c[row0:row1], tri_excl, diag)
                da = _dot(dob, v, 1, 1)
                ps = [a[r].astype(F32) * da[rs] for r, rs in enumerate(blocks)]
                psums = _dot(jnp.concatenate([_hilo(p) for p in ps], axis=0), tri_incl)
                dzs = []
                for r, rs in enumerate(blocks):
                    left = dsum[rs] - (psums[rs, :SB_BLK] + cpr[rs])
                    dz = ps[r] - jnp.exp(lbs[r]) * (ps[r] + left)
                    if diag and r == 0:
                        dz = jnp.where(strict, dz, 0.0)
                    dzs.append(dz.astype(BF16))
                dzb, ab = jnp.concatenate(dzs, axis=0), jnp.concatenate(a, axis=0)
                dk_ref[pl.ds(start, SB_BLK), sl] += _dot(dzb, q, 0, 0)
                dv_ref[pl.ds(start, SB_BLK), sl] += _dot(ab, dob, 0, 0)
                out += [_add_rows(dq, _dot(dzb, k), row0, row1), _add_rows(c, dc, row0, row1),
                        _add_rows(cp, psums[:, SB_BLK:], row0, row1)]
            return tuple(out)

        zero = jnp.zeros((qt, SB_BLK), F32)
        carry = (jnp.zeros((qt, SB_DH), F32), zero, zero) * 2
        carry = _sb_visit(tiles, carry, (1, 4), qt,
                          lambda sub: pl.multiple_of(i * qt + sub * SB_BLK, SB_BLK), i * subs,
                          lambda jj: pl.multiple_of((i * subs - 1 - jj) * SB_BLK, SB_BLK))
        for hh, sl in enumerate(heads):
            dq_ref[:, sl] = carry[3 * hh] * 0.125

    whole = lambda base: pl.BlockSpec((rows, LANES), functools.partial(lambda h, i, b: (0, b + h), b=base))
    blk = lambda base: pl.BlockSpec((qt, LANES), functools.partial(lambda h, i, b: (i, b + h), b=base))
    return _hosted_call(
        body, exchange, grid=(SB_HEADS // 2, rows // qt),
        in_specs=[blk(C_QS // LANES), whole(C_KS // LANES), whole(C_VS // LANES), blk(0), blk(0)],
        out_specs=[blk(0), whole(0), whole(0)],
        out_shape=[jax.ShapeDtypeStruct((rows, SB_W), F32)] * 3, scratch_shapes=[],
        args=(proj, proj, proj, o_s, do_s), name="sb_attention_bwd",
        compiler_params=_cparams("arbitrary", "arbitrary"))


def _gla_chunk_common(g_all):
    r_i, c_i = _iota2((GLA_CHUNK, GLA_CHUNK), 0), _iota2((GLA_CHUNK, GLA_CHUNK), 1)
    tri = (c_i <= r_i).astype(BF16)
    return _dot_rhs_exact(tri, g_all), r_i, c_i


def _gla_scaled(qh, kh, bh):
    ref = bh[GLA_CHUNK // 2:GLA_CHUNK // 2 + 1, :]
    eq, ek = jnp.exp(bh - ref), jnp.exp(ref - bh)
    mask = _iota2((GLA_CHUNK, GLA_CHUNK), 1) <= _iota2((GLA_CHUNK, GLA_CHUNK), 0)
    return eq, ek, qh * eq, kh * ek, mask


def _gla_fwd(proj, gk, rows, exchange=None):
    n_chunks = rows // GLA_CHUNK
    step_rows = GLA_CHUNK * GLA_STEP_CHUNKS

    def body(q_ref, k_ref, v_ref, g_ref, o_ref, sall_ref, s_scr):
        @pl.when(pl.program_id(0) == 0)
        def _():
            s_scr[...] = jnp.zeros(s_scr.shape, F32)

        ones = jnp.ones((GLA_CHUNK, GLA_DV), BF16)
        states = [s_scr[h] for h in range(GLA_HEADS)]
        for ci in range(GLA_STEP_CHUNKS):
            rs = slice(GLA_CHUNK * ci, GLA_CHUNK * (ci + 1))
            g_all = g_ref[rs, :]
            b_all, _, _ = _gla_chunk_common(g_all)
            for h in range(GLA_HEADS):
                sl = slice(GLA_DK * h, GLA_DK * (h + 1))
                vs = slice(GLA_DV * h, GLA_DV * (h + 1))
                qh, kh, vh = q_ref[rs, sl] * 0.125, k_ref[rs, sl], v_ref[rs, vs]
                bh, gh = b_all[:, sl], g_all[:, sl]
                s = states[h]
                sall_ref[ci, h] = s
                _, _, qs, ks, mask = _gla_scaled(qh, kh, bh)
                a = jnp.where(mask, _dot(qs, ks, 1, 1), 0.0)
                o_ref[rs, vs] = _dot(qh * jnp.exp(bh), s) + _dot(a, vh)
                bl_col = _dot_lhs_exact(gh, ones, 0, 0)
                kd = kh * jnp.exp(bh[GLA_CHUNK - 1:GLA_CHUNK, :] - bh)
                states[h] = jnp.exp(bl_col) * s + _dot(kd, vh, 0, 0)
        for h in range(GLA_HEADS):
            s_scr[h] = states[h]

    c64 = lambda w, base: pl.BlockSpec((step_rows, w), functools.partial(lambda n, b: (n, b), b=base))
    return _hosted_call(
        body, exchange, grid=(rows // step_rows,),
        in_specs=[c64(GLA_QK_W, C_QG // GLA_QK_W), c64(GLA_QK_W, C_KG // GLA_QK_W), c64(GLA_V_W, C_VG // GLA_V_W),
                  c64(GLA_QK_W, 0)],
        out_specs=[c64(GLA_V_W, 0),
                   pl.BlockSpec((GLA_STEP_CHUNKS, GLA_HEADS, GLA_DK, GLA_DV), lambda n: (n, 0, 0, 0))],
        out_shape=[jax.ShapeDtypeStruct((rows, GLA_V_W), F32),
                   jax.ShapeDtypeStruct((n_chunks, GLA_HEADS, GLA_DK, GLA_DV), F32)],
        scratch_shapes=[pltpu.VMEM((GLA_HEADS, GLA_DK, GLA_DV), F32)],
        args=(proj, proj, proj, gk), name="gla_fwd", compiler_params=_cparams("arbitrary"))


def _gla_bwd(proj, gk, do_g, s_all, rows, exchange=None):
    n_chunks = rows // GLA_CHUNK

    def body(q_ref, k_ref, v_ref, g_ref, do_ref, sall_ref, dq_ref, dk_ref, dv_ref, dg_ref, ds_scr):
        @pl.when(pl.program_id(0) == 0)
        def _():
            ds_scr[...] = jnp.zeros(ds_scr.shape, F32)

        ones = jnp.ones((GLA_CHUNK, GLA_DV), BF16)
        ones8 = jnp.ones((8, GLA_DV), F32)
        last_row = _iota2((GLA_CHUNK, GLA_DK), 0) == GLA_CHUNK - 1
        dstates = [ds_scr[h] for h in range(GLA_HEADS)]
        for ci in reversed(range(GLA_STEP_CHUNKS)):
            cs = slice(GLA_CHUNK * ci, GLA_CHUNK * (ci + 1))
            g_all = g_ref[cs, :]
            b_all, r_i, c_i = _gla_chunk_common(g_all)
            triu = (c_i >= r_i).astype(BF16)
            for h in range(GLA_HEADS):
                sl = slice(GLA_DK * h, GLA_DK * (h + 1))
                vs = slice(GLA_DV * h, GLA_DV * (h + 1))
                qh, kh, vh = q_ref[cs, sl] * 0.125, k_ref[cs, sl], v_ref[cs, vs]
                bh, gh = b_all[:, sl], g_all[:, sl]
                doh = do_ref[cs, vs]
                s, ds = sall_ref[ci, h], dstates[h]
                eb = jnp.exp(bh)
                ekd = jnp.exp(bh[GLA_CHUNK - 1:GLA_CHUNK, :] - bh)
                ebl = jnp.exp(_dot_lhs_exact(gh, ones, 0, 0))
                qb, kd = qh * eb, kh * ekd
                dq = _dot(doh, s, 1, 1) * eb
                dk = _dot(vh, ds, 1, 1) * ekd
                dv = _dot(kd, ds)
                dbl = jnp.sum(dk * kh, axis=0, keepdims=True) + _dot3(ones8, ebl * s * ds, 1, 1)[0:1, :]
                eq, ek, qs, ks, mask = _gla_scaled(qh, kh, bh)
                a = jnp.where(mask, _dot(qs, ks, 1, 1), 0.0)
                da = jnp.where(mask, _dot(doh, vh, 1, 1), 0.0)
                dq = dq + _dot(da, ks) * eq
                dk = dk + _dot(da, qs, 0, 0) * ek
                dv = dv + _dot(a, doh, 0, 0)
                db = qh * dq - kh * dk + jnp.where(last_row, dbl, 0.0)
                dq_ref[cs, sl] = dq * 0.125
                dk_ref[cs, sl] = dk
                dv_ref[cs, vs] = dv
                dg_ref[cs, sl] = _dot_rhs_exact(triu, db)
                dstates[h] = _dot(qb, doh, 0, 0) + ebl * ds
        for h in range(GLA_HEADS):
            ds_scr[h] = dstates[h]

    step_rows = GLA_CHUNK * GLA_STEP_CHUNKS
    last = rows // step_rows - 1
    c64 = lambda w, base: pl.BlockSpec((step_rows, w), functools.partial(lambda n, b: (last - n, b), b=base))
    return _hosted_call(
        body, exchange, grid=(rows // step_rows,),
        in_specs=[c64(GLA_QK_W, C_QG // GLA_QK_W), c64(GLA_QK_W, C_KG // GLA_QK_W), c64(GLA_V_W, C_VG // GLA_V_W),
                  c64(GLA_QK_W, 0), c64(GLA_V_W, 0),
                  pl.BlockSpec((GLA_STEP_CHUNKS, GLA_HEADS, GLA_DK, GLA_DV), lambda n: (last - n, 0, 0, 0))],
        out_specs=[c64(GLA_QK_W, 0), c64(GLA_QK_W, 0), c64(GLA_V_W, 0), c64(GLA_QK_W, 0)],
        out_shape=[jax.ShapeDtypeStruct((rows, GLA_QK_W), F32), jax.ShapeDtypeStruct((rows, GLA_QK_W), F32),
                   jax.ShapeDtypeStruct((rows, GLA_V_W), F32), jax.ShapeDtypeStruct((rows, GLA_QK_W), F32)],
        scratch_shapes=[pltpu.VMEM((GLA_HEADS, GLA_DK, GLA_DV), F32)],
        args=(proj, proj, proj, gk, do_g, s_all), name="gla_bwd", compiler_params=_cparams("arbitrary"))


def _mem_kv_fwd(mem, mem_norm_w, w_mkv, mk_norm_w):
    def body(mem_ref, mw_ref, w_ref, kw_ref, memn_ref, kpre_ref, kn_ref, v_ref):
        xb = mem_ref[...]
        r = lax.rsqrt(jnp.mean(xb * xb, axis=-1, keepdims=True) + RMS_EPS)
        mem_n = (xb * r * mw_ref[...]).astype(BF16)
        memn_ref[...] = mem_n
        kv = _dot(mem_n, w_ref[...])
        kpre_ref[...] = kv[:, :D_MODEL]
        v_ref[...] = kv[:, D_MODEL:].astype(BF16)
        for h in range(MEM_HEADS):
            sl = slice(MEM_DH * h, MEM_DH * (h + 1))
            kh = kv[:, sl]
            rk = lax.rsqrt(jnp.mean(kh * kh, axis=-1, keepdims=True) + RMS_EPS)
            kn_ref[:, sl] = (kh * rk * kw_ref[...]).astype(BF16)

    return pl.pallas_call(
        body, name="mem_kv_fwd",
        out_shape=[jax.ShapeDtypeStruct((MEM_LEN, D_MODEL), BF16), jax.ShapeDtypeStruct((MEM_LEN, D_MODEL), F32),
                   jax.ShapeDtypeStruct((MEM_LEN, D_MODEL), BF16), jax.ShapeDtypeStruct((MEM_LEN, D_MODEL), BF16)],
        compiler_params=_cparams(),
    )(mem, mem_norm_w, w_mkv, mk_norm_w)


def _mem_kv_bwd(mem, mem_norm_w, w_mkv, mk_norm_w, mem_n, kpre, dkn, dv):
    def body(mem_ref, mw_ref, w_ref, kw_ref, memn_ref, kpre_ref, dkn_ref, dv_ref, dw_ref, dkw_ref, dmw_ref):
        dkw = jnp.zeros((1, MEM_DH), F32)
        dk_parts = []
        for h in range(MEM_HEADS):
            sl = slice(MEM_DH * h, MEM_DH * (h + 1))
            kh, dkh = kpre_ref[:, sl], dkn_ref[:, sl]
            rk = lax.rsqrt(jnp.mean(kh * kh, axis=-1, keepdims=True) + RMS_EPS)
            xh = kh * rk
            dxh = dkh * kw_ref[...]
            dkw = dkw + jnp.sum(dkh * xh, axis=0, keepdims=True)
            dk_parts.append(rk * (dxh - xh * jnp.mean(dxh * xh, axis=-1, keepdims=True)))
        dkw_ref[...] = dkw
        dkv = jnp.concatenate(dk_parts + [dv_ref[...]], axis=1).astype(BF16)
        dw_ref[...] = _dot(memn_ref[...], dkv, 0, 0)
        dmem_n = _dot(dkv, w_ref[...], 1, 1)
        xb = mem_ref[...]
        r = lax.rsqrt(jnp.mean(xb * xb, axis=-1, keepdims=True) + RMS_EPS)
        dmw_ref[...] = jnp.sum(dmem_n * (xb * r), axis=0, keepdims=True)

    return pl.pallas_call(
        body, name="mem_kv_bwd",
        out_shape=[jax.ShapeDtypeStruct((D_MODEL, 2 * D_MODEL), F32), jax.ShapeDtypeStruct((1, MEM_DH), F32),
                   jax.ShapeDtypeStruct((1, D_MODEL), F32)],
        compiler_params=_cparams(),
    )(mem, mem_norm_w, w_mkv, mk_norm_w, mem_n, kpre, dkn, dv)


def _xattn_head(qh, kn_h, qw):
    rq = lax.rsqrt(jnp.mean(qh * qh, axis=-1, keepdims=True) + RMS_EPS)
    xh = qh * rq
    qn = (xh * qw).astype(BF16)
    s = _dot(qn, kn_h, 1, 1) * (1.0 / 16.0)
    e = jnp.exp(s - jnp.max(s, axis=-1, keepdims=True))
    p = e / jnp.sum(e, axis=-1, keepdims=True)
    return rq, xh, qn, p


def _xattn_fwd(qm, kn, v, mq_norm_w, bm):
    def fn(qb, knb, vb, qw):
        outs = []
        for h in range(MEM_HEADS):
            sl = slice(MEM_DH * h, MEM_DH * (h + 1))
            _, _, _, p = _xattn_head(qb[:, sl], knb[:, sl], qw)
            outs.append(_dot(p, vb[:, sl]))
        return (jnp.concatenate(outs, axis=1),)
    return _rowcall("xattn_fwd", fn, [(qm, D_MODEL, 0)], [kn, v, mq_norm_w], [(D_MODEL, BF16)], [], bm,
                    qm.shape[0])[0]


def _xattn_bwd(qm, kn, v, mq_norm_w, do, bm):
    def fn(qb, dob, knb, vb, qw):
        dq_parts, dkn_parts, dv_parts = [], [], []
        dqw = jnp.zeros((1, MEM_DH), F32)
        for h in range(MEM_HEADS):
            sl = slice(MEM_DH * h, MEM_DH * (h + 1))
            rq, xh, qn, p = _xattn_head(qb[:, sl], knb[:, sl], qw)
            doh = dob[:, sl].astype(BF16)
            dp = _dot(doh, vb[:, sl], 1, 1)
            ds = (p * (dp - jnp.sum(dp * p, axis=-1, keepdims=True)) * (1.0 / 16.0)).astype(BF16)
            dqn = _dot(ds, knb[:, sl])
            dkn_parts.append(_dot(ds, qn, 0, 0))
            dv_parts.append(_dot(p, doh, 0, 0))
            dqw = dqw + jnp.sum(dqn * xh, axis=0, keepdims=True)
            dxh = dqn * qw
            dq_parts.append(rq * (dxh - xh * jnp.mean(dxh * xh, axis=-1, keepdims=True)))
        return (jnp.concatenate(dq_parts, axis=1), jnp.concatenate(dkn_parts, axis=1),
                jnp.concatenate(dv_parts, axis=1), dqw)
    return _rowcall("xattn_bwd", fn, [(qm, D_MODEL, 0), (do, D_MODEL, 0)], [kn, v, mq_norm_w],
                    [(D_MODEL, BF16)], [(MEM_LEN, D_MODEL), (MEM_LEN, D_MODEL), (1, MEM_DH)], bm, qm.shape[0])


FF_BN = 1408
FF_NB = D_FF // FF_BN


def _ffn_up(h3, w_gate_up, rows, bm):
    def body(h_ref, wg_ref, wu_ref, gate_ref, up_ref, act_ref):
        hb = h_ref[...]
        gate = _dot(hb, wg_ref[...])
        up = _dot(hb, wu_ref[...])
        gate_ref[...] = gate.astype(BF16)
        up_ref[...] = up.astype(BF16)
        act_ref[...] = (gate * _sigmoid(gate) * up).astype(BF16)

    out_blk = pl.BlockSpec((bm, FF_BN), lambda i, j: (i, j))
    return pl.pallas_call(
        body, name="ffn_up", grid=(rows // bm, FF_NB),
        in_specs=[pl.BlockSpec((bm, D_MODEL), lambda i, j: (i, 0)),
                  pl.BlockSpec((D_MODEL, FF_BN), lambda i, j: (0, j)),
                  pl.BlockSpec((D_MODEL, FF_BN), lambda i, j: (0, FF_NB + j))],
        out_specs=[out_blk, out_blk, out_blk],
        out_shape=[jax.ShapeDtypeStruct((rows, D_FF), BF16)] * 3,
        compiler_params=_cparams("parallel", "arbitrary"),
    )(h3, w_gate_up, w_gate_up)


def _ffn_act_bwd(dy, w_down, gate, up, rows, bm):
    def body(dy_ref, wd_ref, gate_ref, up_ref, o_ref):
        dact = _dot(dy_ref[...], wd_ref[...], 1, 1)
        g, u = gate_ref[...].astype(F32), up_ref[...].astype(F32)
        sg = _sigmoid(g)
        o_ref[0] = (dact * u * (sg * (1.0 + g * (1.0 - sg)))).astype(BF16)
        o_ref[1] = (dact * (g * sg)).astype(BF16)

    blk = pl.BlockSpec((bm, FF_BN), lambda i, j: (i, j))
    return pl.pallas_call(
        body, name="ffn_act_bwd", grid=(rows // bm, FF_NB),
        in_specs=[pl.BlockSpec((bm, D_MODEL), lambda i, j: (i, 0)),
                  pl.BlockSpec((FF_BN, D_MODEL), lambda i, j: (j, 0)), blk, blk],
        out_specs=pl.BlockSpec((2, bm, FF_BN), lambda i, j: (0, i, j)),
        out_shape=jax.ShapeDtypeStruct((2, rows, D_FF), BF16),
        compiler_params=_cparams("parallel", "arbitrary"),
    )(dy, w_down, gate, up)


ROW_SHARDED = ("w_out", "w_mq", "w_mo", "w_down")
TAIL_ROWS = 16
AG_FIRST = ("w_in", "w_gk_up")
AG_MID = ("w_out", "w_mq", "w_mkv")
AG_LATE = ("w_mo", "w_gate_up", "w_down")
RS_FFN = ("w_gate_up", "w_down")
RS_MID = ("w_out", "w_mq", "w_mkv", "w_mo")
RS_WIN = ("w_in", "w_gk_up")
ADAM_BLOCK = {"w_in": 256, "w_out": 128, "w_mq": 128, "w_mkv": 256, "w_mo": 128, "w_gate_up": 256,
              "w_down": 176, "w_gk_up": 16}


def _full_weight(name, got):
    r, c = SHARD_SHAPE[name]
    if name in ROW_SHARDED:
        return got.reshape(N_DEV * r, c)
    return got.transpose(1, 0, 2).reshape(r, N_DEV * c)


def _tail_rows(rows):
    return jnp.concatenate(rows + [jnp.zeros((TAIL_ROWS - len(rows), 1024), F32)], axis=0)


def _split_for_owners(name, full):
    r, c = SHARD_SHAPE[name]
    if name in ROW_SHARDED:
        return full.reshape(N_DEV, r, c)
    return full.reshape(r, N_DEV, c).transpose(1, 0, 2)


def _repl_row(a):
    flat = a.reshape(-1)
    return jnp.concatenate([flat, jnp.zeros((1024 - flat.shape[0],), flat.dtype)]).reshape(1, 1024)


def _local_step(x, mem, tgt, wf, rp, shards=None, scatter=False):
    rows = x.shape[0]
    bm = min(512, rows)
    bmx = min(256, rows)
    mt = min(MM_TILE, rows)
    kt = min(512, rows)
    w_cat, wgk_pad = wf["w_cat"], wf["wgk_pad"]
    wg_t = jnp.tile(rp["gla_norm_w"], (1, GLA_HEADS))
    ws_t = jnp.tile(rp["sb_norm_w"], (1, SB_HEADS))
    lane = jnp.arange(GLA_V_W)
    grp_g = (lane[:, None] // GLA_DV == lane[None, :] // GLA_DV).astype(BF16)
    grp_s = (lane[:, None] // SB_DH == lane[None, :] // SB_DH).astype(BF16)

    h1 = _rms_fwd("mix_norm_fwd", x, rp["mix_norm_w"], bm)
    proj = _matmul("in_proj", h1, w_cat, "nn", rows, PROJ_W, D_MODEL, F32, mt, 640, D_MODEL)
    gk = _gate_fwd(proj, wgk_pad, rp["b_gk"], bm)
    if shards is None:
        o_g, s_all = _gla_fwd(proj, gk, rows)
        (o_s,) = _sb_fwd(proj, rows)
    else:
        mine = lambda names: [shards[n].astype(BF16) for n in names]
        o_g, s_all, got_mid = _gla_fwd(proj, gk, rows, _Exchange(mine(AG_MID), scatter=False))
        o_s, got_late = _sb_fwd(proj, rows, _Exchange(mine(AG_LATE), scatter=False))
        wf = {**wf, **{n: _full_weight(n, a) for n, a in zip(AG_MID + AG_LATE, got_mid + got_late)}}
    cat = _mix_out_fwd(o_g, proj, o_s, wg_t, ws_t, grp_g, grp_s, bm)
    x1 = _matmul("out_proj", cat, wf["w_out"], "nn", rows, D_MODEL, D_MODEL, F32, mt, MM_TILE, D_MODEL, residual=x)
    h2 = _rms_fwd("xattn_norm_fwd", x1, rp["xattn_norm_w"], bm)
    qm = _matmul("mq_proj", h2, wf["w_mq"], "nn", rows, D_MODEL, D_MODEL, F32, mt, MM_TILE, D_MODEL)
    mem_n, kpre, kn, v_m = _mem_kv_fwd(mem, rp["mem_norm_w"], wf["w_mkv"], rp["mk_norm_w"])
    o_m = _xattn_fwd(qm, kn, v_m, rp["mq_norm_w"], bmx)
    x2 = _matmul("mo_proj", o_m, wf["w_mo"], "nn", rows, D_MODEL, D_MODEL, F32, mt, MM_TILE, D_MODEL, residual=x1)
    h3 = _rms_fwd("ffn_norm_fwd", x2, rp["ffn_norm_w"], bm)
    gate, up, act = _ffn_up(h3, wf["w_gate_up"], rows, mt)
    y = _matmul("ffn_down", act, wf["w_down"], "nn", rows, D_MODEL, D_FF, F32, mt, MM_TILE, FF_BN, residual=x2)
    dy, sq = _loss_kernel(y, tgt, bm)

    g = {}
    dgu = _ffn_act_bwd(dy, wf["w_down"], gate, up, rows, bm)
    g["w_down"] = _matmul("grad_w_down", act, dy, "tn", D_FF, D_MODEL, rows, F32, FF_BN, MM_TILE, kt)
    nkb = FF_NB
    dh3 = _matmul("ffn_up_bwd", dgu, wf["w_gate_up"], "nt", rows, D_MODEL, 2 * D_FF, F32, mt, MM_TILE, FF_BN,
                  a_spec=pl.BlockSpec((None, mt, FF_BN), lambda i, j, kk: (kk // nkb, i, kk % nkb)))
    g["w_gate_up"] = _matmul(
        "grad_w_gate_up", h3, dgu, "tn", D_MODEL, 2 * D_FF, rows, F32, MM_TILE, FF_BN, kt,
        b_spec=pl.BlockSpec((None, kt, FF_BN), lambda i, j, kk: (j // nkb, kk, j % nkb)))
    dx2, g["ffn_norm_w"] = _rms_bwd("ffn_norm_bwd", x2, rp["ffn_norm_w"], dh3, dy, bm)

    do_m = _matmul("mo_proj_bwd", dx2, wf["w_mo"], "nt", rows, D_MODEL, D_MODEL, BF16, mt, MM_TILE, D_MODEL)
    g["w_mo"] = _matmul("grad_w_mo", o_m, dx2, "tn", D_MODEL, D_MODEL, rows, F32, MM_TILE, MM_TILE, kt)
    dqm, dkn, dv_m, g["mq_norm_w"] = _xattn_bwd(qm, kn, v_m, rp["mq_norm_w"], do_m, bmx)
    g["w_mkv"], g["mk_norm_w"], g["mem_norm_w"] = _mem_kv_bwd(
        mem, rp["mem_norm_w"], wf["w_mkv"], rp["mk_norm_w"], mem_n, kpre, dkn, dv_m)
    dh2 = _matmul("mq_proj_bwd", dqm, wf["w_mq"], "nt", rows, D_MODEL, D_MODEL, F32, mt, MM_TILE, D_MODEL)
    g["w_mq"] = _matmul("grad_w_mq", h2, dqm, "tn", D_MODEL, D_MODEL, rows, F32, MM_TILE, MM_TILE, kt)
    dx1, g["xattn_norm_w"] = _rms_bwd("xattn_norm_bwd", x1, rp["xattn_norm_w"], dh2, dx2, bm)

    dcat = _matmul("out_proj_bwd", dx1, wf["w_out"], "nt", rows, D_MODEL, D_MODEL, F32, mt, MM_TILE, D_MODEL)
    g["w_out"] = _matmul("grad_w_out", cat, dx1, "tn", D_MODEL, D_MODEL, rows, F32, MM_TILE, MM_TILE, kt)
    do_g, dgg, do_s, dwg, dws = _mix_out_bwd(dcat, o_g, proj, o_s, wg_t, ws_t, grp_g, grp_s, bm)
    g["gla_norm_w"] = dwg.reshape(GLA_HEADS, GLA_DV).sum(axis=0, keepdims=True)
    g["sb_norm_w"] = dws.reshape(SB_HEADS, SB_DH).sum(axis=0, keepdims=True)
    recv = {}
    to_owners = lambda names: [_split_for_owners(n, g[n].astype(BF16)) for n in names]
    if scatter:
        dq_s, dk_s, dv_s, got = _sb_bwd(proj, o_s, do_s, rows, _Exchange(to_owners(RS_FFN), scatter=True))
        recv.update(zip(RS_FFN, got))
        dq_g, dk_g, dv_g, dgk, got = _gla_bwd(proj, gk, do_g, s_all, rows, _Exchange(to_owners(RS_MID), scatter=True))
        recv.update(zip(RS_MID, got))
    else:
        dq_s, dk_s, dv_s = _sb_bwd(proj, o_s, do_s, rows)
        dq_g, dk_g, dv_g, dgk = _gla_bwd(proj, gk, do_g, s_all, rows)
    dlr, dwgk, g["b_gk"] = _gate_bwd(proj, wgk_pad, rp["b_gk"], dgk, bm)
    g["w_gk_up"] = dwgk[:GATE_RANK]
    dproj = jnp.concatenate([dq_g.astype(BF16), dk_g.astype(BF16), dv_g.astype(BF16), dgg, dq_s.astype(BF16),
                             dk_s.astype(BF16), dv_s.astype(BF16), dlr], axis=1)
    dw_cat = _matmul("grad_w_in", h1, dproj, "tn", D_MODEL, PROJ_W, rows, F32, MM_TILE, 640, kt)
    g["w_in"] = jnp.concatenate([dw_cat[:, :C_QS], dw_cat[:, C_LR:C_LR + GATE_RANK], dw_cat[:, C_QS:C_LR]], axis=1)
    if scatter:
        dh1, got = _matmul("in_proj_bwd", dproj, w_cat, "nt", rows, D_MODEL, PROJ_W, F32, mt, MM_TILE, 640,
                           exchange=_Exchange(to_owners(RS_WIN), scatter=True))
        recv.update(zip(RS_WIN, got))
    else:
        dh1 = _matmul("in_proj_bwd", dproj, w_cat, "nt", rows, D_MODEL, PROJ_W, F32, mt, MM_TILE, 640)
    dx, g["mix_norm_w"] = _rms_bwd("mix_norm_bwd", x, rp["mix_norm_w"], dh1, dx1, bm)
    return sq, dx, g, recv


def _first_weights(got):
    w_in = _full_weight("w_in", got[0])
    lr_end = C_QS + GATE_RANK
    w_cat = jnp.concatenate([w_in[:, :C_QS], w_in[:, lr_end:], w_in[:, C_QS:lr_end],
                             jnp.zeros((D_MODEL, PROJ_W - D_IN), BF16)], axis=1)
    wgk = _full_weight("w_gk_up", got[1])
    return {"w_cat": w_cat, "wgk_pad": jnp.concatenate([wgk, jnp.zeros((LANES - GATE_RANK, GLA_QK_W), BF16)], axis=0)}


def kernel(x, mem, mix_norm_w, w_in, w_gk_up, b_gk, gla_norm_w, sb_norm_w, w_out, xattn_norm_w, mem_norm_w, w_mq, w_mkv, mq_norm_w, mk_norm_w, w_mo, ffn_norm_w, w_gate_up, w_down, loss_target, m_mix_norm_w, m_w_in, m_w_gk_up, m_b_gk, m_gla_norm_w, m_sb_norm_w, m_w_out, m_xattn_norm_w, m_mem_norm_w, m_w_mq, m_w_mkv, m_mq_norm_w, m_mk_norm_w, m_w_mo, m_ffn_norm_w, m_w_gate_up, m_w_down, v_mix_norm_w, v_w_in, v_w_gk_up, v_b_gk, v_gla_norm_w, v_sb_norm_w, v_w_out, v_xattn_norm_w, v_mem_norm_w, v_w_mq, v_w_mkv, v_mq_norm_w, v_mk_norm_w, v_w_mo, v_ffn_norm_w, v_w_gate_up, v_w_down):
    given = dict(locals())
    w = {n: given[n][0] for n in WEIGHTS}
    m = {n: given["m_" + n][0] for n in WEIGHTS}
    v = {n: given["v_" + n][0] for n in WEIGHTS}

    wf = _first_weights(_exchange_call("gather_first_weights", [w[n].astype(BF16) for n in AG_FIRST], scatter=False))
    rp = {n: w[n].reshape(1, -1) for n in REPL}
    sq, dx, g, recv = _local_step(x[0], mem[0], loss_target[0], wf, rp, shards=w, scatter=True)

    loss_row = _repl_row(jnp.sum(sq).reshape(1) * (0.5 / D_MODEL))
    tail = _tail_rows([_repl_row(g[n]) for n in REPL] + [loss_row])
    (recv_tail,) = _exchange_call("scatter_tail_gradients", [jnp.broadcast_to(tail[None], (N_DEV, TAIL_ROWS, 1024))],
                                  scatter=True)
    results = {n: _adamw("sum_adamw_" + n, recv[n], w[n], m[n], v[n], ADAM_BLOCK[n]) for n in SHARD_SHAPE}
    tail_out = _adamw("sum_adamw_tail", recv_tail, *[_tail_rows([_repl_row(t[n]) for n in REPL]) for t in (w, m, v)],
                      TAIL_ROWS)

    def output(kind, name):
        if name in results:
            return results[name][kind][None]
        return tail_out[kind][REPL.index(name), :w[name].shape[-1]].reshape(1, -1)

    loss = tail_out[0][len(REPL), 0]
    outs = [loss, dx[None]]
    for kind in range(4):
        outs += [output(kind, n) for n in WEIGHTS]
    return tuple(outs)
```
